```python
import math
import jax
import jax.numpy as jnp
from jax import lax
import numpy as np

D_MODEL = 1024
BATCH = 32
SEQ = 256
DEPTH = 2
DEC_BATCH = 8
DEC_SEQ = 2048
PAST_LEN = 256

GRID_W = 64
EPS = 1e-6
N_EVEN = (DEPTH + 1) // 2
N_ODD = DEPTH // 2

SSD_HEADS = 8
SSD_HEAD_DIM = 64
SSD_WIDTH = SSD_HEADS * SSD_HEAD_DIM
SSD_GROUPS = 2
SSD_HPG = SSD_HEADS // SSD_GROUPS
SSD_STATE = 64
SSD_GN = SSD_GROUPS * SSD_STATE
SSD_XBC = SSD_WIDTH + 2 * SSD_GN
SSD_CHUNK = 128
CONV_W = 5

NA_HEADS = 8
NA_HEAD_DIM = 64
NA_WIDTH = NA_HEADS * NA_HEAD_DIM
NA_ROWS = 8
NA_COLS = 16
NA_SCALE = NA_HEAD_DIM ** -0.5
CTX_BLOCK = 128

O_XBC = SSD_WIDTH
O_DT = O_XBC + SSD_XBC
O_Q = O_DT + SSD_HEADS
O_K = O_Q + NA_WIDTH
O_V = O_K + NA_WIDTH
EVEN_IN = O_V + NA_WIDTH
EVEN_MIX = SSD_WIDTH + NA_WIDTH

S5_GROUP_CH = 16
S5_GROUPS = 32
S5_WIDTH = S5_GROUPS * S5_GROUP_CH
S5_STATE = 64

FNET_GROUP_CH = 64
FNET_GROUPS = 8
FNET_WIDTH = FNET_GROUPS * FNET_GROUP_CH
ODD_IN = S5_WIDTH + FNET_WIDTH

FFN_HIDDEN = -(-8 * D_MODEL // (3 * 256)) * 256

kernel_name = 'hybrid_ssd_na_s5_fnet_prefix_step'


def rmsnorm(x, g):
    xf = x.astype(jnp.float32)
    y = xf * lax.rsqrt(jnp.mean(xf * xf, axis=-1, keepdims=True) + EPS)
    return (y * g.astype(jnp.float32)).astype(x.dtype)


def adaln(cond, w, b):
    m = jax.nn.silu(cond) @ w + b
    return jnp.split(m, 6, axis=-1)


def modulate(h, shift, scale):
    return h * (1 + scale[:, None]) + shift[:, None]


def swiglu(h, w1, w3, w2):
    return (jax.nn.silu(h @ w1) * (h @ w3)) @ w2


def dwconv(x, w, bias):
    ch = x.shape[-1]
    y = lax.conv_general_dilated(
        x, w[:, None, :].astype(x.dtype), window_strides=(1,),
        padding=[(CONV_W // 2, CONV_W // 2)],
        dimension_numbers=('NWC', 'WIO', 'NWC'), feature_group_count=ch)
    return y + bias.astype(x.dtype)


def ssd_scan(x, dt, a, bmat, cmat, s0):
    f32 = jnp.float32
    b, L, H, P = x.shape
    N = bmat.shape[-1]
    Q = SSD_CHUNK
    nc = L // Q
    xdt = (x.astype(f32) * dt[..., None]).reshape(b, nc, Q, H, P)
    bc = bmat.astype(f32).reshape(b, nc, Q, H, N)
    cc = cmat.astype(f32).reshape(b, nc, Q, H, N)
    cum = jnp.cumsum((dt * a).reshape(b, nc, Q, H), axis=2)
    causal = jnp.tril(jnp.ones((Q, Q), dtype=bool))[None, None, :, :, None]
    seg = cum[:, :, :, None, :] - cum[:, :, None, :, :]
    decay = jnp.exp(jnp.where(causal, seg, -jnp.inf))
    cb = jnp.einsum('bcihn,bcjhn->bcijh', cc, bc)
    y_diag = jnp.einsum('bcijh,bcjhp->bcihp', cb * decay, xdt)
    to_end = jnp.exp(cum[:, :, -1:, :] - cum)
    chunk_states = jnp.einsum('bcjhn,bcjh,bcjhp->bchpn', bc, to_end, xdt)
    chunk_decay = jnp.exp(cum[:, :, -1, :])

    def step(s, inp):
        cs, cd = inp
        return s * cd[:, :, None, None] + cs, s

    s_final, s_in = lax.scan(
        step, s0.astype(f32),
        (jnp.moveaxis(chunk_states, 1, 0), jnp.moveaxis(chunk_decay, 1, 0)))
    s_in = jnp.moveaxis(s_in, 0, 1)
    y_off = jnp.einsum('bcihn,bchpn,bcih->bcihp', cc, s_in, jnp.exp(cum))
    return (y_diag + y_off).reshape(b, L, H, P), s_final


def ssd_mixer(z, xbc, dt_raw, conv_w, conv_b, a_log, dt_bias, d_skip, norm_g, s0):
    f32 = jnp.float32
    b, L, _ = z.shape
    xbc = jax.nn.silu(dwconv(xbc, conv_w, conv_b))
    xs = xbc[..., :SSD_WIDTH].reshape(b, L, SSD_HEADS, SSD_HEAD_DIM)
    bm = jnp.repeat(
        xbc[..., SSD_WIDTH:SSD_WIDTH + SSD_GN].reshape(b, L, SSD_GROUPS, SSD_STATE),
        SSD_HPG, axis=2)
    cm = jnp.repeat(
        xbc[..., SSD_WIDTH + SSD_GN:].reshape(b, L, SSD_GROUPS, SSD_STATE),
        SSD_HPG, axis=2)
    y = d_skip.astype(f32)[:, None] * xs.astype(f32)
    finals = []
    for d in range(2):
        dt = jax.nn.softplus(dt_raw.astype(f32) + dt_bias[d].astype(f32))
        a = -jnp.exp(a_log[d].astype(f32))
        if d == 0:
            yd, sf = ssd_scan(xs, dt, a, bm, cm, s0[:, 0])
        else:
            yd, sf = ssd_scan(
                jnp.flip(xs, 1), jnp.flip(dt, 1), a,
                jnp.flip(bm, 1), jnp.flip(cm, 1), s0[:, 1])
            yd = jnp.flip(yd, 1)
        y = y + yd
        finals.append(sf)
    y = y.reshape(b, L, SSD_WIDTH) * jax.nn.silu(z.astype(f32))
    y = rmsnorm(y, norm_g)
    return y.astype(z.dtype), jnp.stack(finals, axis=1)


def ctx_attention(q, k, v):
    b, Lc, H, Dh = q.shape
    nb = Lc // CTX_BLOCK
    qb = jnp.moveaxis(q.reshape(b, nb, CTX_BLOCK, H, Dh), 1, 0)

    def blk(qi):
        s = jnp.einsum('bqhd,bkhd->bhqk', qi, k).astype(jnp.float32) * NA_SCALE
        p = jax.nn.softmax(s, axis=-1).astype(v.dtype)
        return jnp.einsum('bhqk,bkhd->bqhd', p, v)

    o = lax.map(blk, qb)
    return jnp.moveaxis(o, 0, 1).reshape(b, Lc, H * Dh)


def latent_na(q, k, v, k_ctx, v_ctx, rpb):
    b, L, H, Dh = q.shape
    rows = L // GRID_W
    kr = min(NA_ROWS, rows)
    qg = q.reshape(b, rows, GRID_W, H, Dh)
    kg = k.reshape(b, rows, GRID_W, H, Dh)
    vg = v.reshape(b, rows, GRID_W, H, Dh)
    col = jnp.arange(GRID_W)
    cs = jnp.clip(col - NA_COLS // 2, 0, GRID_W - NA_COLS)
    col_ok = (col[None, :] >= cs[:, None]) & (col[None, :] < cs[:, None] + NA_COLS)
    dc_idx = jnp.clip(col[None, :] - col[:, None] + NA_COLS - 1, 0, 2 * NA_COLS - 2)
    rpb32 = rpb.astype(jnp.float32)

    def row_block(r):
        rs = jnp.clip(r - kr // 2, 0, rows - kr)
        qr = lax.dynamic_index_in_dim(qg, r, axis=1, keepdims=False)
        kb = lax.dynamic_slice_in_dim(kg, rs, kr, axis=1)
        vb = lax.dynamic_slice_in_dim(vg, rs, kr, axis=1)
        dr_idx = rs + jnp.arange(kr) - r + NA_ROWS - 1
        bias = rpb32[:, dr_idx][:, :, dc_idx]
        s_loc = (jnp.einsum('bqhd,bikhd->bhqik', qr, kb).astype(jnp.float32) * NA_SCALE
                 + jnp.transpose(bias, (0, 2, 1, 3))[None])
        s_loc = jnp.where(col_ok[None, None, :, None, :], s_loc, -jnp.inf)
        s_loc = s_loc.reshape(b, H, GRID_W, kr * GRID_W)
        s_ctx = jnp.einsum('bqhd,bkhd->bhqk', qr, k_ctx).astype(jnp.float32) * NA_SCALE
        p = jax.nn.softmax(jnp.concatenate([s_loc, s_ctx], axis=-1), axis=-1).astype(v.dtype)
        p_loc = p[..., :kr * GRID_W].reshape(b, H, GRID_W, kr, GRID_W)
        return (jnp.einsum('bhqik,bikhd->bqhd', p_loc, vb)
                + jnp.einsum('bhqk,bkhd->bqhd', p[..., kr * GRID_W:], v_ctx))

    o = lax.map(row_block, jnp.arange(rows))
    return jnp.moveaxis(o, 0, 1).reshape(b, L, H * Dh)


def even_project(h, w_in):
    b, L, _ = h.shape
    p = h @ w_in
    heads = lambda t: t.reshape(b, L, NA_HEADS, NA_HEAD_DIM)
    return (p[..., :O_XBC], p[..., O_XBC:O_DT], p[..., O_DT:O_Q],
            heads(p[..., O_Q:O_K]), heads(p[..., O_K:O_V]), heads(p[..., O_V:]))


def even_context(h, w_in, conv_w, conv_b, a_log, dt_bias, d_skip, ssd_g, w_out):
    z, xbc, dt_raw, q, k, v = even_project(h, w_in)
    s0 = jnp.zeros((h.shape[0], 2, SSD_HEADS, SSD_HEAD_DIM, SSD_STATE), jnp.float32)
    y_ssd, st = ssd_mixer(z, xbc, dt_raw, conv_w, conv_b, a_log, dt_bias, d_skip, ssd_g, s0)
    o_na = ctx_attention(q, k, v)
    return jnp.concatenate([y_ssd, o_na.astype(y_ssd.dtype)], axis=-1) @ w_out, k, v, st


def even_latent(h, w_in, conv_w, conv_b, a_log, dt_bias, d_skip, ssd_g, rpb, w_out,
                k_ctx, v_ctx, s0):
    z, xbc, dt_raw, q, k, v = even_project(h, w_in)
    y_ssd, _ = ssd_mixer(z, xbc, dt_raw, conv_w, conv_b, a_log, dt_bias, d_skip, ssd_g, s0)
    o_na = latent_na(q, k, v, k_ctx.astype(k.dtype), v_ctx.astype(v.dtype), rpb)
    return jnp.concatenate([y_ssd, o_na.astype(y_ssd.dtype)], axis=-1) @ w_out


def s5_discretize(lam_re, lam_im, log_step):
    step = jnp.exp(log_step)[:, None]
    mag = jnp.exp(lam_re * step)
    a_re = mag * jnp.cos(lam_im * step)
    a_im = mag * jnp.sin(lam_im * step)
    den = lam_re * lam_re + lam_im * lam_im
    k_re = ((a_re - 1) * lam_re + a_im * lam_im) / den
    k_im = (a_im * lam_re - (a_re - 1) * lam_im) / den
    return a_re, a_im, k_re, k_im


def s5_scan(a_re, a_im, bu_re, bu_im, s0_re, s0_im):
    bu_re = bu_re.at[:, 0].add(a_re * s0_re - a_im * s0_im)
    bu_im = bu_im.at[:, 0].add(a_re * s0_im + a_im * s0_re)
    ar = jnp.broadcast_to(a_re, bu_re.shape)
    ai = jnp.broadcast_to(a_im, bu_im.shape)

    def combine(e1, e2):
        a1r, a1i, b1r, b1i = e1
        a2r, a2i, b2r, b2i = e2
        return (a1r * a2r - a1i * a2i, a1r * a2i + a1i * a2r,
                a2r * b1r - a2i * b1i + b2r, a2r * b1i + a2i * b1r + b2i)

    _, _, xr, xi = lax.associative_scan(combine, (ar, ai, bu_re, bu_im), axis=1)
    return xr, xi


def s5_mixer(u, lam_re, lam_im, log_step, b_re, b_im, c_re, c_im, d_skip, glu_w, glu_b, s0):
    f32 = jnp.float32
    b, L, _ = u.shape
    uf = u.astype(f32)
    ug = uf.reshape(b, L, S5_GROUPS, S5_GROUP_CH)
    br, bi, cr, ci = (t.astype(f32) for t in (b_re, b_im, c_re, c_im))
    s0 = s0.astype(f32)
    y = uf * d_skip.astype(f32)
    finals = []
    for d in range(2):
        a_re, a_im, k_re, k_im = s5_discretize(
            lam_re[d].astype(f32), lam_im[d].astype(f32), log_step[d].astype(f32))
        bb_re = k_re[..., None] * br - k_im[..., None] * bi
        bb_im = k_re[..., None] * bi + k_im[..., None] * br
        ud = ug if d == 0 else jnp.flip(ug, axis=1)
        bu_re = jnp.einsum('blgc,gpc->blgp', ud, bb_re)
        bu_im = jnp.einsum('blgc,gpc->blgp', ud, bb_im)
        xr, xi = s5_scan(a_re, a_im, bu_re, bu_im, s0[:, d, ..., 0], s0[:, d, ..., 1])
        finals.append(jnp.stack([xr[:, -1], xi[:, -1]], axis=-1))
        if d == 1:
            xr, xi = jnp.flip(xr, axis=1), jnp.flip(xi, axis=1)
        y = y + (jnp.einsum('gcp,blgp->blgc', cr, xr)
                 - jnp.einsum('gcp,blgp->blgc', ci, xi)).reshape(b, L, S5_WIDTH)
    g = jax.nn.gelu(y)
    out = g * jax.nn.sigmoid(g @ glu_w.astype(f32) + glu_b.astype(f32))
    return out.astype(u.dtype), jnp.stack(finals, axis=1)


def fnet_mixer(u):
    b, L, _ = u.shape
    ug = u.astype(jnp.float32).reshape(b, L, FNET_GROUPS, FNET_GROUP_CH)
    f = jnp.fft.fftn(ug, axes=(1, 3), norm='ortho').real
    return f.reshape(b, L, FNET_WIDTH).astype(u.dtype)


def odd_mixer(h, w_in, lam_re, lam_im, log_step, b_re, b_im, c_re, c_im, d_skip, glu_w,
              glu_b, w_out, s0):
    p = h @ w_in
    y_s5, st = s5_mixer(p[..., :S5_WIDTH], lam_re, lam_im, log_step, b_re, b_im, c_re,
                        c_im, d_skip, glu_w, glu_b, s0)
    y_f = fnet_mixer(p[..., S5_WIDTH:])
    return jnp.concatenate([y_s5, y_f], axis=-1) @ w_out, st


def setup_inputs(seed: int = 0) -> dict:
    key = jax.random.key(seed)
    ks = iter(jax.random.split(key, 64))
    f32 = jnp.float32
    nrm = lambda shape, s: jax.random.normal(next(ks), shape, f32) * s
    D = D_MODEL
    F = FFN_HIDDEN
    E = N_EVEN
    O = N_ODD
    dt0 = jnp.exp(jax.random.uniform(next(ks), (E, 2, SSD_HEADS), f32,
                                     minval=math.log(1e-3), maxval=math.log(1e-1)))
    dt_bias = dt0 + jnp.log(-jnp.expm1(-dt0))
    a_log = jnp.log(jax.random.uniform(next(ks), (E, 2, SSD_HEADS), f32, minval=1.0, maxval=16.0))
    n = jnp.arange(S5_STATE, dtype=f32)
    lam_re = -0.5 + nrm((O, 2, S5_GROUPS, S5_STATE), 0.01)
    lam_im = math.pi * n + nrm((O, 2, S5_GROUPS, S5_STATE), 0.01)
    log_step = jax.random.uniform(next(ks), (O, 2, S5_GROUPS), f32,
                                  minval=math.log(1e-3), maxval=math.log(1e-1))
    return {
        'x_prompt': nrm((BATCH, SEQ, D), 1.0),
        'x_sample': nrm((DEC_BATCH, DEC_SEQ, D), 1.0),
        'cache_na_k': nrm((DEC_BATCH, E, PAST_LEN, NA_HEADS, NA_HEAD_DIM), 1.0),
        'cache_na_v': nrm((DEC_BATCH, E, PAST_LEN, NA_HEADS, NA_HEAD_DIM), 1.0),
        'state_ssd': nrm((DEC_BATCH, E, 2, SSD_HEADS, SSD_HEAD_DIM, SSD_STATE), 0.5),
        'state_s5': nrm((DEC_BATCH, O, 2, S5_GROUPS, S5_STATE, 2), 0.5),
        'c': nrm((DEC_BATCH, D), 1.0),
        'c_ctx': nrm((D,), 1.0),
        'mod_w': nrm((DEPTH, D, 6 * D), 0.5 * D ** -0.5),
        'mod_b': nrm((DEPTH, 6 * D), 0.02),
        'norm1_g': 1.0 + nrm((DEPTH, D), 0.02),
        'norm2_g': 1.0 + nrm((DEPTH, D), 0.02),
        'ffn_w1': nrm((DEPTH, D, F), D ** -0.5),
        'ffn_w3': nrm((DEPTH, D, F), D ** -0.5),
        'ffn_w2': nrm((DEPTH, F, D), F ** -0.5),
        'final_g': 1.0 + nrm((D,), 0.02),
        'ev_w_in': nrm((E, D, EVEN_IN), D ** -0.5),
        'ev_conv_w': nrm((E, CONV_W, SSD_XBC), CONV_W ** -0.5),
        'ev_conv_b': nrm((E, SSD_XBC), 0.02),
        'ev_a_log': a_log,
        'ev_dt_bias': dt_bias,
        'ev_d_skip': 1.0 + nrm((E, SSD_HEADS), 0.1),
        'ev_ssd_norm_g': 1.0 + nrm((E, SSD_WIDTH), 0.02),
        'ev_rpb': nrm((E, NA_HEADS, 2 * NA_ROWS - 1, 2 * NA_COLS - 1), 0.1),
        'ev_w_out': nrm((E, EVEN_MIX, D), EVEN_MIX ** -0.5),
        'od_w_in': nrm((O, D, ODD_IN), D ** -0.5),
        'od_lam_re': lam_re,
        'od_lam_im': lam_im,
        'od_log_step': log_step,
        'od_b_re': nrm((O, S5_GROUPS, S5_STATE, S5_GROUP_CH), (2 * S5_GROUP_CH) ** -0.5),
        'od_b_im': nrm((O, S5_GROUPS, S5_STATE, S5_GROUP_CH), (2 * S5_GROUP_CH) ** -0.5),
        'od_c_re': nrm((O, S5_GROUPS, S5_GROUP_CH, S5_STATE), S5_STATE ** -0.5),
        'od_c_im': nrm((O, S5_GROUPS, S5_GROUP_CH, S5_STATE), S5_STATE ** -0.5),
        'od_d_skip': nrm((O, S5_WIDTH), 0.5),
        'od_glu_w': nrm((O, S5_WIDTH, S5_WIDTH), S5_WIDTH ** -0.5),
        'od_glu_b': nrm((O, S5_WIDTH), 0.02),
        'od_w_out': nrm((O, ODD_IN, D), ODD_IN ** -0.5),
    }


def reference(x_prompt, x_sample, cache_na_k, cache_na_v, state_ssd, state_s5, c, c_ctx,
              mod_w, mod_b, norm1_g, norm2_g, ffn_w1, ffn_w3, ffn_w2, final_g,
              ev_w_in, ev_conv_w, ev_conv_b, ev_a_log, ev_dt_bias, ev_d_skip, ev_ssd_norm_g,
              ev_rpb, ev_w_out, od_w_in, od_lam_re, od_lam_im, od_log_step, od_b_re, od_b_im,
              od_c_re, od_c_im, od_d_skip, od_glu_w, od_glu_b, od_w_out):
    xc = x_prompt
    xl = x_sample
    new_k, new_v, new_ssd, new_s5 = [], [], [], []
    for i in range(DEPTH):
        j = i // 2
        mc = adaln(c_ctx[None], mod_w[i], mod_b[i])
        ml = adaln(c, mod_w[i], mod_b[i])
        hc = modulate(rmsnorm(xc, norm1_g[i]), mc[0], mc[1])
        hl = modulate(rmsnorm(xl, norm1_g[i]), ml[0], ml[1])
        if i % 2 == 0:
            oc, kc, vc, sc = even_context(
                hc, ev_w_in[j], ev_conv_w[j], ev_conv_b[j], ev_a_log[j], ev_dt_bias[j],
                ev_d_skip[j], ev_ssd_norm_g[j], ev_w_out[j])
            ol = even_latent(
                hl, ev_w_in[j], ev_conv_w[j], ev_conv_b[j], ev_a_log[j], ev_dt_bias[j],
                ev_d_skip[j], ev_ssd_norm_g[j], ev_rpb[j], ev_w_out[j],
                cache_na_k[:, j], cache_na_v[:, j], state_ssd[:, j])
            new_k.append(kc)
            new_v.append(vc)
            new_ssd.append(sc)
        else:
            zero = jnp.zeros((xc.shape[0], 2, S5_GROUPS, S5_STATE, 2), jnp.float32)
            oc, sc = odd_mixer(
                hc, od_w_in[j], od_lam_re[j], od_lam_im[j], od_log_step[j], od_b_re[j],
                od_b_im[j], od_c_re[j], od_c_im[j], od_d_skip[j], od_glu_w[j], od_glu_b[j],
                od_w_out[j], zero)
            ol, _ = odd_mixer(
                hl, od_w_in[j], od_lam_re[j], od_lam_im[j], od_log_step[j], od_b_re[j],
                od_b_im[j], od_c_re[j], od_c_im[j], od_d_skip[j], od_glu_w[j], od_glu_b[j],
                od_w_out[j], state_s5[:, j])
            new_s5.append(sc)
        xc = xc + mc[2][:, None] * oc.astype(xc.dtype)
        xl = xl + ml[2][:, None] * ol.astype(xl.dtype)
        hc = modulate(rmsnorm(xc, norm2_g[i]), mc[3], mc[4])
        hl = modulate(rmsnorm(xl, norm2_g[i]), ml[3], ml[4])
        xc = xc + mc[5][:, None] * swiglu(hc, ffn_w1[i], ffn_w3[i], ffn_w2[i])
        xl = xl + ml[5][:, None] * swiglu(hl, ffn_w1[i], ffn_w3[i], ffn_w2[i])
    y_prompt = rmsnorm(xc, final_g)
    y_sample = rmsnorm(xl, final_g)
    new_cache_na_k = jnp.stack(new_k, axis=1)
    new_cache_na_v = jnp.stack(new_v, axis=1)
    new_state_ssd = jnp.stack(new_ssd, axis=1)
    new_state_s5 = jnp.stack(new_s5, axis=1)
    return (y_prompt, y_sample, new_cache_na_k, new_cache_na_v, new_state_ssd, new_state_s5)
```

```python
import functools
import math

import jax
import jax.numpy as jnp
from jax import lax
from jax.experimental import pallas as pl
from jax.experimental.pallas import tpu as pltpu

F32 = jnp.float32
BF16 = jnp.bfloat16
HIGHEST = lax.Precision.HIGHEST

D_MODEL = 1024
DEPTH = 2
EPS = 1e-6
GRID_W = 64
FFN_HIDDEN = 2816
FFN_CHUNK = 1408

SSD_HEADS = 8
SSD_HEAD_DIM = 64
SSD_WIDTH = 512
SSD_GROUPS = 2
SSD_HPG = SSD_HEADS // SSD_GROUPS
SSD_STATE = 64
SSD_GN = SSD_GROUPS * SSD_STATE
SSD_XBC = SSD_WIDTH + 2 * SSD_GN
SSD_CHUNK = 128
CONV_W = 5
CONV_HALO = 8

NA_HEADS = 8
NA_HEAD_DIM = 64
NA_WIDTH = 512
NA_ROWS = 8
NA_COLS = 16
NA_SCALE = NA_HEAD_DIM ** -0.5
NA_QROWS = 4
NA_WIN = NA_ROWS + NA_QROWS
NEG_BIG = -1e30

O_XBC = SSD_WIDTH
O_DT = O_XBC + SSD_XBC
O_Q = O_DT + SSD_HEADS
O_K = O_Q + NA_WIDTH
O_V = O_K + NA_WIDTH
LANES = 128

S5_GROUP_CH = 16
S5_GROUPS = 32
S5_WIDTH = 512
S5_STATE = 64
S5_Q = 16
S5_CW = S5_Q * S5_GROUP_CH
S5_LOG_MAX = 7

FNET_GROUP_CH = 64
FNET_GROUPS = 8
FNET_WIDTH = 512

MOD_ROWS = 16
VMEM_LIMIT = 56 * 1024 * 1024


def _cparams(*sem):
    return pltpu.CompilerParams(dimension_semantics=sem, vmem_limit_bytes=VMEM_LIMIT)


def _const_spec(shape):
    nd = len(shape)
    return pl.BlockSpec(shape, lambda *_: (0,) * nd, pipeline_mode=pl.Buffered(1))


def _sigmoid(x):
    return 1.0 / (1.0 + jnp.exp(-x))


def _silu(x):
    return x * _sigmoid(x)


def _softplus(x):
    return jnp.maximum(x, 0.0) + jnp.log(1.0 + jnp.exp(-jnp.abs(x)))


def _gelu_tanh(x):
    return 0.5 * x * (1.0 + jnp.tanh(math.sqrt(2.0 / math.pi) * (x + 0.044715 * (x * x * x))))


def _rmsnorm(x, g):
    return x * lax.rsqrt(jnp.mean(x * x, axis=-1, keepdims=True) + EPS) * g


def _bdot(a, b):
    return jnp.dot(a.astype(BF16), b.astype(BF16), preferred_element_type=F32)


def _bdot_nt(a, b):
    return lax.dot_general(a.astype(BF16), b.astype(BF16), (((1,), (1,)), ((), ())),
                           preferred_element_type=F32)


def _bdot_tn(a, b):
    return lax.dot_general(a.astype(BF16), b.astype(BF16), (((0,), (0,)), ((), ())),
                           preferred_element_type=F32)


def _adaln_body(c_ref, w_ref, b_ref, o_ref):
    s = _silu(c_ref[...])
    o_ref[0] = jnp.dot(s, w_ref[0], precision=HIGHEST, preferred_element_type=F32) + b_ref[0]


def _adaln(cond, mod_w, mod_b):
    n = 6 * D_MODEL
    tn = 1536
    out = pl.pallas_call(
        _adaln_body,
        grid=(DEPTH, n // tn),
        in_specs=[pl.BlockSpec((MOD_ROWS, D_MODEL), lambda i, j: (0, 0)),
                  pl.BlockSpec((1, D_MODEL, tn), lambda i, j: (i, 0, j)),
                  pl.BlockSpec((1, 1, tn), lambda i, j: (i, 0, j))],
        out_specs=pl.BlockSpec((1, MOD_ROWS, tn), lambda i, j: (i, 0, j)),
        out_shape=jax.ShapeDtypeStruct((DEPTH, MOD_ROWS, n), F32),
        compiler_params=_cparams("arbitrary", "arbitrary"),
        name="adaln",
    )(cond, mod_w, mod_b.reshape(DEPTH, 1, n))
    return out.reshape(DEPTH, MOD_ROWS, 6, D_MODEL)


def _mod_spec(mrow0):
    return pl.BlockSpec((1, 6, D_MODEL), lambda b, t: (b + mrow0, 0, 0))


def _inproj_body(x_ref, mod_ref, g_ref, w_ref, *o_refs, splits):
    h = _rmsnorm(x_ref[0], g_ref[...]) * (1.0 + mod_ref[0, 1:2, :]) + mod_ref[0, 0:1, :]
    acc = jnp.dot(h.astype(BF16), w_ref[...], preferred_element_type=F32)
    off = 0
    for o_ref, n in zip(o_refs, splits):
        o_ref[0] = acc[:, off:off + n]
        off += n


def _inproj(x, mod, mrow0, g, w, splits, tm):
    b, l, _ = x.shape
    n = w.shape[1]
    return pl.pallas_call(
        functools.partial(_inproj_body, splits=splits),
        grid=(b, l // tm),
        in_specs=[pl.BlockSpec((1, tm, D_MODEL), lambda i, t: (i, t, 0)),
                  _mod_spec(mrow0),
                  _const_spec((1, D_MODEL)),
                  _const_spec((D_MODEL, n))],
        out_specs=[pl.BlockSpec((1, tm, s), lambda i, t: (i, t, 0)) for s in splits],
        out_shape=[jax.ShapeDtypeStruct((b, l, s), F32) for s in splits],
        compiler_params=_cparams("arbitrary", "arbitrary"),
        name="inproj",
    )(x, mod, g.reshape(1, D_MODEL), w)


def _outproj_body(x_ref, mod_ref, ya_ref, yb_ref, wa_ref, wb_ref, *rest, glu):
    ya = ya_ref[0]
    if glu:
        gw_ref, gb_ref, o_ref = rest
        ya = ya * _sigmoid(_bdot(ya, gw_ref[...]) + gb_ref[...])
    else:
        (o_ref,) = rest
    o = _bdot(ya, wa_ref[...]) + _bdot(yb_ref[0], wb_ref[...])
    o_ref[0] = x_ref[0] + mod_ref[0, 2:3, :] * o


def _outproj(x, mod, mrow0, ya, yb, wa, wb, glu_w, glu_b, tm):
    b, l, _ = x.shape
    wa_n, wb_n = ya.shape[-1], yb.shape[-1]
    glu = glu_w is not None
    in_specs = [pl.BlockSpec((1, tm, D_MODEL), lambda i, t: (i, t, 0)),
                _mod_spec(mrow0),
                pl.BlockSpec((1, tm, wa_n), lambda i, t: (i, t, 0)),
                pl.BlockSpec((1, tm, wb_n), lambda i, t: (i, t, 0)),
                _const_spec((wa_n, D_MODEL)),
                _const_spec((wb_n, D_MODEL))]
    args = [x, mod, ya, yb, wa, wb]
    if glu:
        in_specs += [_const_spec((wa_n, wa_n)), _const_spec((1, wa_n))]
        args += [glu_w, glu_b.reshape(1, wa_n)]
    return pl.pallas_call(
        functools.partial(_outproj_body, glu=glu),
        grid=(b, l // tm),
        in_specs=in_specs,
        out_specs=pl.BlockSpec((1, tm, D_MODEL), lambda i, t: (i, t, 0)),
        out_shape=jax.ShapeDtypeStruct((b, l, D_MODEL), F32),
        compiler_params=_cparams("arbitrary", "arbitrary"),
        name="outproj",
    )(*args)


def _ffn_body(x_ref, mod_ref, g_ref, w1_ref, w3_ref, w2_ref, *rest, final):
    x = x_ref[0]
    h = (_rmsnorm(x, g_ref[...]) * (1.0 + mod_ref[0, 4:5, :]) + mod_ref[0, 3:4, :]).astype(BF16)
    acc = jnp.zeros(x.shape, F32)
    for c in range(FFN_HIDDEN // FFN_CHUNK):
        cols = slice(c * FFN_CHUNK, (c + 1) * FFN_CHUNK)
        a = jnp.dot(h, w1_ref[:, cols], preferred_element_type=F32)
        u = _silu(a) * jnp.dot(h, w3_ref[:, cols], preferred_element_type=F32)
        acc = acc + jnp.dot(u.astype(BF16), w2_ref[cols, :], preferred_element_type=F32)
    y = x + mod_ref[0, 5:6, :] * acc
    if final:
        fg_ref, o_ref = rest
        y = _rmsnorm(y, fg_ref[...])
    else:
        (o_ref,) = rest
    o_ref[0] = y


def _ffn(x, mod, mrow0, g, w1, w3, w2, final_g, tm):
    b, l, _ = x.shape
    final = final_g is not None
    in_specs = [pl.BlockSpec((1, tm, D_MODEL), lambda i, t: (i, t, 0)),
                _mod_spec(mrow0),
                _const_spec((1, D_MODEL)),
                _const_spec((D_MODEL, FFN_HIDDEN)),
                _const_spec((D_MODEL, FFN_HIDDEN)),
                _const_spec((FFN_HIDDEN, D_MODEL))]
    args = [x, mod, g.reshape(1, D_MODEL), w1, w3, w2]
    if final:
        in_specs.append(_const_spec((1, D_MODEL)))
        args.append(final_g.reshape(1, D_MODEL))
    return pl.pallas_call(
        functools.partial(_ffn_body, final=final),
        grid=(b, l // tm),
        in_specs=in_specs,
        out_specs=pl.BlockSpec((1, tm, D_MODEL), lambda i, t: (i, t, 0)),
        out_shape=jax.ShapeDtypeStruct((b, l, D_MODEL), F32),
        compiler_params=_cparams("arbitrary", "arbitrary"),
        name="ffn",
    )(*args)


def _ssd_body(z_ref, xbc_ref, dt_ref, cw_ref, cb_ref, hp_ref, dsk_ref, ng_ref, *rest,
              seq, has_s0, want_final):
    rest = list(rest)
    s0_ref = rest.pop(0) if has_s0 else None
    y_ref = rest.pop(0)
    fin_ref = rest.pop(0) if want_final else None
    st_s, cd_s, scur, ych = rest
    q = SSD_CHUNK
    nc = seq // q
    row = lax.broadcasted_iota(jnp.int32, (q, q), 0)
    col = lax.broadcasted_iota(jnp.int32, (q, q), 1)
    lower = row >= col
    upper = col >= row
    tri_l = lower.astype(F32)
    tri_u = upper.astype(F32)
    a_f = -jnp.exp(hp_ref[0:1, :])
    a_b = -jnp.exp(hp_ref[1:2, :])
    lane = lax.broadcasted_iota(jnp.int32, (1, LANES), 1)
    a_f = jnp.where(lane < SSD_HEADS, a_f, 0.0)
    a_b = jnp.where(lane < SSD_HEADS, a_b, 0.0)
    bias_f = hp_ref[2:3, :]
    bias_b = hp_ref[3:4, :]

    def chunk_pre(c):
        r0 = pl.multiple_of(c * q, q)
        lo = pl.multiple_of(jnp.maximum(r0 - CONV_HALO, 0), CONV_HALO)
        hi = pl.multiple_of(jnp.minimum(r0 + q, seq - CONV_HALO), CONV_HALO)
        prev = jnp.where(c > 0, xbc_ref[0, pl.ds(lo, CONV_HALO), :], 0.0)
        nxt = jnp.where(c < nc - 1, xbc_ref[0, pl.ds(hi, CONV_HALO), :], 0.0)
        win = jnp.concatenate([prev, xbc_ref[0, pl.ds(r0, q), :], nxt], axis=0)
        acc = cb_ref[...] + cw_ref[0:1, :] * win[CONV_HALO - 2:CONV_HALO - 2 + q]
        for k in range(1, CONV_W):
            o = CONV_HALO - CONV_W // 2 + k
            acc = acc + cw_ref[k:k + 1, :] * win[o:o + q]
        xc = _silu(acc)
        dtr = dt_ref[0, pl.ds(r0, q), :]
        dt_f = _softplus(dtr + bias_f)
        dt_b = _softplus(dtr + bias_b)
        cum_f = jnp.dot(tri_l, dt_f * a_f, precision=HIGHEST, preferred_element_type=F32)
        cum_b = jnp.dot(tri_u, dt_b * a_b, precision=HIGHEST, preferred_element_type=F32)
        return r0, xc, dt_f, dt_b, cum_f, cum_b

    def pass_a(c, carry):
        _, xc, dt_f, dt_b, cum_f, cum_b = chunk_pre(c)
        end_f = cum_f[q - 1:q, :]
        end_b = cum_b[0:1, :]
        w_f = jnp.exp(end_f - cum_f) * dt_f
        w_b = jnp.exp(end_b - cum_b) * dt_b
        cd_s[0, c] = jnp.broadcast_to(jnp.exp(end_f), (SSD_STATE, LANES))
        cd_s[1, c] = jnp.broadcast_to(jnp.exp(end_b), (SSD_STATE, LANES))
        for h in range(SSD_HEADS):
            g = h // SSD_HPG
            xh = xc[:, h * SSD_HEAD_DIM:(h + 1) * SSD_HEAD_DIM]
            bg = xc[:, SSD_WIDTH + g * SSD_STATE:SSD_WIDTH + (g + 1) * SSD_STATE]
            st_s[0, c, h] = _bdot_tn(bg, xh * w_f[:, h:h + 1])
            st_s[1, c, h] = _bdot_tn(bg, xh * w_b[:, h:h + 1])
        return carry

    lax.fori_loop(0, nc, pass_a, 0)

    if has_s0:
        scur[...] = s0_ref[0]
    else:
        scur[...] = jnp.zeros(scur.shape, F32)

    def pass_b(c, carry):
        cr = nc - 1 - c
        cd_f = cd_s[0, c]
        cd_b = cd_s[1, cr]
        for h in range(SSD_HEADS):
            s_in = scur[0, h]
            scur[0, h] = s_in * cd_f[:, h:h + 1] + st_s[0, c, h]
            st_s[0, c, h] = s_in
            s_in = scur[1, h]
            scur[1, h] = s_in * cd_b[:, h:h + 1] + st_s[1, cr, h]
            st_s[1, cr, h] = s_in
        return carry

    lax.fori_loop(0, nc, pass_b, 0)
    if want_final:
        fin_ref[0] = scur[...]

    def pass_c(c, carry):
        r0, xc, dt_f, dt_b, cum_f, cum_b = chunk_pre(c)
        cum_ft, cum_bt, dt_ft, dt_bt = cum_f.T, cum_b.T, dt_f.T, dt_b.T
        e_f = jnp.exp(cum_f)
        e_b = jnp.exp(cum_b)
        cbs = []
        for g in range(SSD_GROUPS):
            bg = xc[:, SSD_WIDTH + g * SSD_STATE:SSD_WIDTH + (g + 1) * SSD_STATE]
            cg = xc[:, SSD_WIDTH + SSD_GN + g * SSD_STATE:SSD_WIDTH + SSD_GN + (g + 1) * SSD_STATE]
            cbs.append((cg, _bdot_nt(cg, bg)))
        for h in range(SSD_HEADS):
            cg, cb = cbs[h // SSD_HPG]
            xh = xc[:, h * SSD_HEAD_DIM:(h + 1) * SSD_HEAD_DIM]
            seg_f = cum_f[:, h:h + 1] - cum_ft[h:h + 1, :]
            seg_b = cum_b[:, h:h + 1] - cum_bt[h:h + 1, :]
            m_f = jnp.exp(jnp.where(lower, seg_f, NEG_BIG)) * dt_ft[h:h + 1, :]
            m_b = jnp.exp(jnp.where(upper, seg_b, NEG_BIG)) * dt_bt[h:h + 1, :]
            y = _bdot(cb * (m_f + m_b), xh)
            c_off = jnp.concatenate([cg * e_f[:, h:h + 1], cg * e_b[:, h:h + 1]], axis=1)
            s_in = jnp.concatenate([st_s[0, c, h], st_s[1, c, h]], axis=0)
            y = y + _bdot(c_off, s_in)
            ych[:, h * SSD_HEAD_DIM:(h + 1) * SSD_HEAD_DIM] = y
        yf = (ych[...] + dsk_ref[...] * xc[:, :SSD_WIDTH]) * _silu(z_ref[0, pl.ds(r0, q), :])
        y_ref[0, pl.ds(r0, q), :] = _rmsnorm(yf, ng_ref[...])
        return carry

    lax.fori_loop(0, nc, pass_c, 0)


def _ssd(z, xbc, dtp, conv_w, conv_b, a_log, dt_bias, d_skip, norm_g, s0t, want_final):
    b, l, _ = z.shape
    nc = l // SSD_CHUNK
    has_s0 = s0t is not None
    cw = jnp.zeros((8, SSD_XBC), F32).at[:CONV_W].set(conv_w)
    hp = jnp.zeros((8, LANES), F32)
    hp = hp.at[0:2, :SSD_HEADS].set(a_log).at[2:4, :SSD_HEADS].set(dt_bias)
    dsk = jnp.repeat(d_skip, SSD_HEAD_DIM).reshape(1, SSD_WIDTH)
    seq_spec = lambda n: pl.BlockSpec((1, l, n), lambda i: (i, 0, 0))
    st_spec = pl.BlockSpec((1, 2, SSD_HEADS, SSD_STATE, SSD_HEAD_DIM), lambda i: (i, 0, 0, 0, 0))
    in_specs = [seq_spec(SSD_WIDTH), seq_spec(SSD_XBC), seq_spec(LANES),
                _const_spec((8, SSD_XBC)), _const_spec((1, SSD_XBC)), _const_spec((8, LANES)),
                _const_spec((1, SSD_WIDTH)), _const_spec((1, SSD_WIDTH))]
    args = [z, xbc, dtp, cw, conv_b.reshape(1, SSD_XBC), hp, dsk, norm_g.reshape(1, SSD_WIDTH)]
    if has_s0:
        in_specs.append(st_spec)
        args.append(s0t)
    out_specs = [seq_spec(SSD_WIDTH)]
    out_shape = [jax.ShapeDtypeStruct((b, l, SSD_WIDTH), F32)]
    if want_final:
        out_specs.append(st_spec)
        out_shape.append(jax.ShapeDtypeStruct((b, 2, SSD_HEADS, SSD_STATE, SSD_HEAD_DIM), F32))
    return pl.pallas_call(
        functools.partial(_ssd_body, seq=l, has_s0=has_s0, want_final=want_final),
        grid=(b,),
        in_specs=in_specs,
        out_specs=out_specs,
        out_shape=out_shape,
        scratch_shapes=[pltpu.VMEM((2, nc, SSD_HEADS, SSD_STATE, SSD_HEAD_DIM), F32),
                        pltpu.VMEM((2, nc, SSD_STATE, LANES), F32),
                        pltpu.VMEM((2, SSD_HEADS, SSD_STATE, SSD_HEAD_DIM), F32),
                        pltpu.VMEM((SSD_CHUNK, SSD_WIDTH), F32)],
        compiler_params=_cparams("arbitrary"),
        name="ssd",
    )(*args)


def _softmax_pv(parts):
    m = parts[0][0].max(axis=-1, keepdims=True)
    for s, _ in parts[1:]:
        m = jnp.maximum(m, s.max(axis=-1, keepdims=True))
    den = 0.0
    out = 0.0
    for s, v in parts:
        p = jnp.exp(s - m)
        den = den + p.sum(axis=-1, keepdims=True)
        out = out + _bdot(p, v)
    return out / den


def _ctx_attn_body(q_ref, k_ref, v_ref, o_ref):
    for h in range(NA_HEADS):
        hs = slice(h * NA_HEAD_DIM, (h + 1) * NA_HEAD_DIM)
        s = _bdot_nt(q_ref[0, :, hs], k_ref[0, :, hs]) * NA_SCALE
        o_ref[0, :, hs] = _softmax_pv([(s, v_ref[0, :, hs])])


def _ctx_attn(q, k, v):
    b, l, _ = q.shape
    spec = pl.BlockSpec((1, l, NA_WIDTH), lambda i: (i, 0, 0))
    return pl.pallas_call(
        _ctx_attn_body,
        grid=(b,),
        in_specs=[spec, spec, spec],
        out_specs=spec,
        out_shape=jax.ShapeDtypeStruct((b, l, NA_WIDTH), F32),
        compiler_params=_cparams("arbitrary"),
        name="ctx_attn",
    )(q, k, v)


def _na_bias_tables(rpb, rows):
    kr = min(NA_ROWS, rows)
    qi = jnp.arange(NA_QROWS)[:, None, None, None]
    qc = jnp.arange(GRID_W)[None, :, None, None]
    wi = jnp.arange(NA_WIN)[None, None, :, None]
    kc = jnp.arange(GRID_W)[None, None, None, :]
    shape = (NA_QROWS, GRID_W, NA_WIN, GRID_W)
    tables = []
    for r0 in (0, NA_QROWS, rows - NA_QROWS):
        ws = min(max(r0 - kr // 2, 0), rows - NA_WIN)
        r = r0 + qi
        krow = ws + wi
        rs = jnp.clip(r - kr // 2, 0, rows - kr)
        cs = jnp.clip(qc - NA_COLS // 2, 0, GRID_W - NA_COLS)
        ok = (krow >= rs) & (krow < rs + kr) & (kc >= cs) & (kc < cs + NA_COLS)
        dr = jnp.broadcast_to(jnp.clip(krow - r + NA_ROWS - 1, 0, 2 * NA_ROWS - 2), shape)
        dc = jnp.broadcast_to(jnp.clip(kc - qc + NA_COLS - 1, 0, 2 * NA_COLS - 2), shape)
        bias = rpb.astype(F32)[:, dr, dc]
        tables.append(jnp.where(jnp.broadcast_to(ok, shape)[None], bias, NEG_BIG).reshape(
            NA_HEADS, NA_QROWS * GRID_W, NA_WIN * GRID_W))
    return jnp.stack(tables)


def _na_body(q_ref, k_ref, v_ref, kc_ref, vc_ref, bias_ref, o_ref, *, rows):
    rb = pl.program_id(1)
    ws = jnp.clip(rb * NA_QROWS - NA_ROWS // 2, 0, rows - NA_WIN)
    k0 = pl.multiple_of(ws * GRID_W, NA_QROWS * GRID_W)
    nk = NA_WIN * GRID_W
    for h in range(NA_HEADS):
        hs = slice(h * NA_HEAD_DIM, (h + 1) * NA_HEAD_DIM)
        qh = q_ref[0, :, hs]
        s_loc = _bdot_nt(qh, k_ref[0, pl.ds(k0, nk), hs]) * NA_SCALE + bias_ref[0, h]
        s_ctx = _bdot_nt(qh, kc_ref[0, :, hs]) * NA_SCALE
        o_ref[0, :, hs] = _softmax_pv([(s_loc, v_ref[0, pl.ds(k0, nk), hs]), (s_ctx, vc_ref[0, :, hs])])


def _latent_na(q, k, v, k_ctx, v_ctx, rpb):
    b, l, _ = q.shape
    lc = k_ctx.shape[1]
    rows = l // GRID_W
    assert rows >= NA_WIN and rows % NA_QROWS == 0 and NA_QROWS * 2 <= NA_ROWS
    nrb = rows // NA_QROWS
    tq = NA_QROWS * GRID_W
    bias = _na_bias_tables(rpb, rows)
    variant = lambda i, r: (jnp.where(r == 0, 0, jnp.where(r == nrb - 1, 2, 1)), 0, 0, 0)
    return pl.pallas_call(
        functools.partial(_na_body, rows=rows),
        grid=(b, nrb),
        in_specs=[pl.BlockSpec((1, tq, NA_WIDTH), lambda i, r: (i, r, 0)),
                  pl.BlockSpec((1, l, NA_WIDTH), lambda i, r: (i, 0, 0)),
                  pl.BlockSpec((1, l, NA_WIDTH), lambda i, r: (i, 0, 0)),
                  pl.BlockSpec((1, lc, NA_WIDTH), lambda i, r: (i, 0, 0)),
                  pl.BlockSpec((1, lc, NA_WIDTH), lambda i, r: (i, 0, 0)),
                  pl.BlockSpec((1, NA_HEADS, tq, NA_WIN * GRID_W), variant)],
        out_specs=pl.BlockSpec((1, tq, NA_WIDTH), lambda i, r: (i, r, 0)),
        out_shape=jax.ShapeDtypeStruct((b, l, NA_WIDTH), F32),
        compiler_params=_cparams("arbitrary", "arbitrary"),
        name="latent_na",
    )(q, k, v, k_ctx, v_ctx, bias)


def _s5_prep_body(lre_ref, lim_ref, lst_ref, btre_ref, btim_ref, cre_ref, cim_ref,
                  wt_ref, kt_ref, et_ref, pw_ref):
    lam_re, lam_im = lre_ref[0], lim_ref[0]
    step = jnp.exp(lst_ref[0])
    mag = jnp.exp(lam_re * step)
    a_re = mag * jnp.cos(lam_im * step)
    a_im = mag * jnp.sin(lam_im * step)
    den = lam_re * lam_re + lam_im * lam_im
    k_re = ((a_re - 1.0) * lam_re + a_im * lam_im) / den
    k_im = (a_im * lam_re - (a_re - 1.0) * lam_im) / den
    bt_re, bt_im = btre_ref[0], btim_ref[0]
    bb_re = k_re * bt_re - k_im * bt_im
    bb_im = k_re * bt_im + k_im * bt_re
    c_re, c_im = cre_ref[0], cim_ref[0]
    p_re = jnp.ones_like(a_re)
    p_im = jnp.zeros_like(a_re)
    for t in range(S5_Q):
        rows = slice(t * S5_GROUP_CH, (t + 1) * S5_GROUP_CH)
        wt_ref[0, rows, 0:S5_STATE] = p_re * bb_re - p_im * bb_im
        wt_ref[0, rows, S5_STATE:2 * S5_STATE] = p_re * bb_im + p_im * bb_re
        p_re, p_im = p_re * a_re - p_im * a_im, p_re * a_im + p_im * a_re
        et_ref[0, rows, 0:S5_STATE] = c_re * p_re - c_im * p_im
        et_ref[0, rows, S5_STATE:2 * S5_STATE] = -(c_re * p_im + c_im * p_re)
    cc = jnp.concatenate([c_re, -c_im], axis=1)
    kt_ref[0] = lax.dot_general(wt_ref[0], cc, (((1,), (1,)), ((), ())), precision=HIGHEST,
                                preferred_element_type=F32)
    pw_ref[0] = jnp.zeros(pw_ref.shape[1:], F32)
    for k in range(S5_LOG_MAX):
        pw_ref[0, 2 * k:2 * k + 1, :] = jnp.concatenate([p_re, p_re], axis=1)
        pw_ref[0, 2 * k + 1:2 * k + 2, :] = jnp.concatenate([-p_im, p_im], axis=1)
        p_re, p_im = p_re * p_re - p_im * p_im, 2.0 * p_re * p_im


def _s5_tables(lam_re, lam_im, log_step, b_re, b_im, c_re, c_im):
    g, p, ch, q = S5_GROUPS, S5_STATE, S5_GROUP_CH, S5_Q
    n = 2 * g
    row = lambda a: a.astype(F32).reshape(n, 1, p)
    lst = jnp.broadcast_to(log_step.astype(F32)[:, :, None], (2, g, p)).reshape(n, 1, p)
    bt_re = jnp.swapaxes(b_re.astype(F32), 1, 2)
    bt_im = jnp.swapaxes(b_im.astype(F32), 1, 2)
    dspec = pl.BlockSpec((1, 1, p), lambda i: (i, 0, 0))
    gspec = pl.BlockSpec((1, ch, p), lambda i: (i % g, 0, 0))
    ospec = lambda a, bb: pl.BlockSpec((1, a, bb), lambda i: (i, 0, 0))
    wt, kt, et, pw = pl.pallas_call(
        _s5_prep_body,
        grid=(n,),
        in_specs=[dspec, dspec, dspec, gspec, gspec, gspec, gspec],
        out_specs=[ospec(S5_CW, 2 * p), ospec(S5_CW, ch), ospec(S5_CW, 2 * p), ospec(16, 2 * p)],
        out_shape=[jax.ShapeDtypeStruct((n, S5_CW, 2 * p), F32),
                   jax.ShapeDtypeStruct((n, S5_CW, ch), F32),
                   jax.ShapeDtypeStruct((n, S5_CW, 2 * p), F32),
                   jax.ShapeDtypeStruct((n, 16, 2 * p), F32)],
        compiler_params=_cparams("arbitrary"),
        name="s5_prep",
    )(row(lam_re), row(lam_im), lst, bt_re, bt_im, c_re.astype(F32), c_im.astype(F32))
    kt = kt.reshape(2, g, q, ch, ch)
    ti = jnp.arange(q)[:, None]
    to = jnp.arange(q)[None, :]
    t_f = jnp.where((to >= ti)[None, :, :, None, None], kt[0][:, jnp.clip(to - ti, 0, q - 1)], 0.0)
    t_b = jnp.where((ti >= to)[None, :, :, None, None], kt[1][:, jnp.clip(ti - to, 0, q - 1)], 0.0)
    t_all = jnp.transpose(t_f + t_b, (0, 1, 3, 2, 4)).reshape(g, S5_CW, S5_CW)
    wt = wt.reshape(2, g, q, ch, 2 * p)
    g_fb = jnp.concatenate([jnp.flip(wt[0], axis=1), wt[1]], axis=-1).reshape(g, S5_CW, 4 * p)
    et = et.reshape(2, g, q, ch, 2 * p)
    e_f = jnp.swapaxes(et[0].reshape(g, S5_CW, 2 * p), 1, 2)
    e_b = jnp.swapaxes(jnp.flip(et[1], axis=1).reshape(g, S5_CW, 2 * p), 1, 2)
    e_fb = jnp.concatenate([e_f, e_b], axis=1)
    return t_all.astype(BF16), g_fb.astype(BF16), e_fb.astype(BF16), pw.reshape(2, g, 16, 2 * p)


def _cmul(a_full, a_sgn, s):
    return a_full * s + a_sgn * pltpu.roll(s, S5_STATE, axis=1)


def _s5_body(u_ref, t_ref, g_ref, e_ref, pw_ref, dsk_ref, *rest, nb, nc, has_s0, want_final):
    rest = list(rest)
    s0_ref = rest.pop(0) if has_s0 else None
    o_ref = rest.pop(0)
    fin_ref = rest.pop(0) if want_final else None
    xs_f, xs_b = rest
    m = nb * nc
    w = 2 * S5_STATE
    u = u_ref[0]
    ub = u.astype(BF16)
    z = jnp.dot(ub, g_ref[0], preferred_element_type=F32)
    x_f, x_b = z[:, :w], z[:, w:]
    cidx = lax.broadcasted_iota(jnp.int32, (m, w), 0) & (nc - 1)
    first = cidx == 0
    last = cidx == nc - 1
    if has_s0:
        rep = lambda a: jnp.broadcast_to(a[:, None, :], (nb, nc, w)).reshape(m, w)
        s0_f, s0_b = rep(s0_ref[0, :, :w]), rep(s0_ref[0, :, w:])
        x_f = x_f + jnp.where(first, _cmul(pw_ref[0, 0, 0:1, :], pw_ref[0, 0, 1:2, :], s0_f), 0.0)
        x_b = x_b + jnp.where(last, _cmul(pw_ref[1, 0, 0:1, :], pw_ref[1, 0, 1:2, :], s0_b), 0.0)
    for k in range(nc.bit_length() - 1):
        sh = 1 << k
        prev = pltpu.roll(x_f, sh, axis=0)
        x_f = x_f + jnp.where(cidx >= sh, _cmul(pw_ref[0, 0, 2 * k:2 * k + 1, :],
                                                 pw_ref[0, 0, 2 * k + 1:2 * k + 2, :], prev), 0.0)
        nxt = pltpu.roll(x_b, m - sh, axis=0)
        x_b = x_b + jnp.where(cidx < nc - sh, _cmul(pw_ref[1, 0, 2 * k:2 * k + 1, :],
                                                     pw_ref[1, 0, 2 * k + 1:2 * k + 2, :], nxt), 0.0)
    in_f = pltpu.roll(x_f, 1, axis=0)
    in_b = pltpu.roll(x_b, m - 1, axis=0)
    if has_s0:
        in_f = jnp.where(first, s0_f, in_f)
        in_b = jnp.where(last, s0_b, in_b)
    else:
        in_f = jnp.where(first, 0.0, in_f)
        in_b = jnp.where(last, 0.0, in_b)
    s_in = jnp.concatenate([in_f, in_b], axis=1).astype(BF16)
    y = (jnp.dot(ub, t_ref[0], preferred_element_type=F32)
         + jnp.dot(s_in, e_ref[0], preferred_element_type=F32) + u * dsk_ref[0])
    o_ref[0] = _gelu_tanh(y)
    if want_final:
        xs_f[...] = x_f
        xs_b[...] = x_b
        fin_ref[0, :, :w] = xs_f[pl.ds(nc - 1, nb, stride=nc), :]
        fin_ref[0, :, w:] = xs_b[pl.ds(0, nb, stride=nc), :]


def _s5(u, tables, d_skip, s0, want_final):
    t_all, g_fb, e_fb, pows = tables
    b, l, _ = u.shape
    g, ch, q = S5_GROUPS, S5_GROUP_CH, S5_Q
    nc = l // q
    m = b * nc
    has_s0 = s0 is not None
    uc = jnp.transpose(u.reshape(b, nc, q, g, ch), (3, 0, 1, 2, 4)).reshape(g, m, S5_CW)
    dsk = jnp.tile(d_skip.astype(F32).reshape(g, 1, ch), (1, q, 1)).reshape(g, 1, S5_CW)
    gspec = lambda a, bb: pl.BlockSpec((1, a, bb), lambda i: (i, 0, 0))
    in_specs = [gspec(m, S5_CW), gspec(S5_CW, S5_CW), gspec(S5_CW, S5_CW), gspec(S5_CW, S5_CW),
                pl.BlockSpec((2, 1, 16, 2 * S5_STATE), lambda i: (0, i, 0, 0)), gspec(1, S5_CW)]
    args = [uc, t_all, g_fb, e_fb, pows, dsk]
    if has_s0:
        in_specs.append(gspec(b, S5_CW))
        args.append(s0)
    out_specs = [gspec(m, S5_CW)]
    out_shape = [jax.ShapeDtypeStruct((g, m, S5_CW), F32)]
    if want_final:
        out_specs.append(gspec(b, S5_CW))
        out_shape.append(jax.ShapeDtypeStruct((g, b, S5_CW), F32))
    res = pl.pallas_call(
        functools.partial(_s5_body, nb=b, nc=nc, has_s0=has_s0, want_final=want_final),
        grid=(g,),
        in_specs=in_specs,
        out_specs=out_specs,
        out_shape=out_shape,
        scratch_shapes=[pltpu.VMEM((m, 2 * S5_STATE), F32), pltpu.VMEM((m, 2 * S5_STATE), F32)],
        compiler_params=_cparams("arbitrary"),
        name="s5",
    )(*args)
    y = jnp.transpose(res[0].reshape(g, b, nc, q, ch), (1, 2, 3, 0, 4)).reshape(b, l, S5_WIDTH)
    return y, (res[1] if want_final else None)


def _dft_cos_sin(n):
    jk = (jnp.arange(n, dtype=jnp.int32)[:, None] * jnp.arange(n, dtype=jnp.int32)[None, :]) % n
    ang = jk.astype(F32) * (2.0 * math.pi / n)
    return jnp.cos(ang), jnp.sin(ang)


def _fnet_body(u_ref, cs_ref, dl_ref, o_ref, xs, *, seq):
    @pl.when(pl.program_id(1) == 0)
    def _():
        xcs = jnp.dot(u_ref[0].astype(BF16), cs_ref[...], preferred_element_type=F32)
        xs[0:seq, :] = xcs[:, :FNET_WIDTH].astype(BF16)
        xs[seq:2 * seq, :] = xcs[:, FNET_WIDTH:].astype(BF16)

    scale = 1.0 / math.sqrt(seq * FNET_GROUP_CH)
    o_ref[0] = jnp.dot(dl_ref[...], xs[...], preferred_element_type=F32) * scale


def _fnet(u):
    b, l, _ = u.shape
    tr = min(l, 512)
    cc, sc = _dft_cos_sin(FNET_GROUP_CH)
    eye = jnp.eye(FNET_GROUPS, dtype=F32)
    cs = jnp.concatenate([jnp.kron(eye, cc), jnp.kron(eye, sc)], axis=1).astype(BF16)
    cl, sl = _dft_cos_sin(l)
    dl = jnp.concatenate([cl, -sl], axis=1).astype(BF16)
    return pl.pallas_call(
        functools.partial(_fnet_body, seq=l),
        grid=(b, l // tr),
        in_specs=[pl.BlockSpec((1, l, FNET_WIDTH), lambda i, t: (i, 0, 0)),
                  _const_spec((FNET_WIDTH, 2 * FNET_WIDTH)),
                  pl.BlockSpec((tr, 2 * l), lambda i, t: (t, 0))],
        out_specs=pl.BlockSpec((1, tr, FNET_WIDTH), lambda i, t: (i, t, 0)),
        out_shape=jax.ShapeDtypeStruct((b, l, FNET_WIDTH), F32),
        scratch_shapes=[pltpu.VMEM((2 * l, FNET_WIDTH), BF16)],
        compiler_params=_cparams("arbitrary", "arbitrary"),
        name="fnet",
    )(u, cs, dl)


def _even_w_in(w):
    pad = jnp.zeros((D_MODEL, LANES - SSD_HEADS), w.dtype)
    return jnp.concatenate([w[:, :O_DT], w[:, O_Q:], w[:, O_DT:O_Q], pad], axis=1).astype(BF16)


EVEN_SPLITS = (SSD_WIDTH, SSD_XBC, NA_WIDTH, NA_WIDTH, NA_WIDTH, LANES)


def kernel(x_prompt, x_sample, cache_na_k, cache_na_v, state_ssd, state_s5, c, c_ctx, mod_w, mod_b, norm1_g, norm2_g, ffn_w1, ffn_w3, ffn_w2, final_g, ev_w_in, ev_conv_w, ev_conv_b, ev_a_log, ev_dt_bias, ev_d_skip, ev_ssd_norm_g, ev_rpb, ev_w_out, od_w_in, od_lam_re, od_lam_im, od_log_step, od_b_re, od_b_im, od_c_re, od_c_im, od_d_skip, od_glu_w, od_glu_b, od_w_out):
    bc, lc, _ = x_prompt.shape
    bl, ll, _ = x_sample.shape
    cond = jnp.zeros((MOD_ROWS, D_MODEL), F32).at[0].set(c_ctx).at[1:1 + bl].set(c)
    mods = _adaln(cond, mod_w, mod_b)
    xc = x_prompt.reshape(1, bc * lc, D_MODEL)
    xl = x_sample
    tm = 512
    new_k, new_v, new_ssd, new_s5 = [], [], [], []
    for i in range(DEPTH):
        j = i // 2
        mod = mods[i]
        last = i == DEPTH - 1
        if i % 2 == 0:
            w_in = _even_w_in(ev_w_in[j])
            w_out = ev_w_out[j].astype(BF16)
            ssd_args = (ev_conv_w[j], ev_conv_b[j], ev_a_log[j], ev_dt_bias[j], ev_d_skip[j],
                        ev_ssd_norm_g[j])
            z, xbc, q, k, v, dtp = _inproj(xc, mod, 0, norm1_g[i], w_in, EVEN_SPLITS, tm)
            seqs = lambda a: a.reshape(bc, lc, a.shape[-1])
            y_ssd, fin = _ssd(seqs(z), seqs(xbc), seqs(dtp), *ssd_args, None, True)
            o_na = _ctx_attn(seqs(q), seqs(k), seqs(v))
            flat = lambda a: a.reshape(1, bc * lc, a.shape[-1])
            xc = _outproj(xc, mod, 0, flat(y_ssd), flat(o_na), w_out[:SSD_WIDTH], w_out[SSD_WIDTH:],
                          None, None, tm)
            new_k.append(k.reshape(bc, lc, NA_HEADS, NA_HEAD_DIM))
            new_v.append(v.reshape(bc, lc, NA_HEADS, NA_HEAD_DIM))
            new_ssd.append(jnp.swapaxes(fin, -1, -2))

            z, xbc, q, k, v, dtp = _inproj(xl, mod, 1, norm1_g[i], w_in, EVEN_SPLITS, tm)
            s0t = jnp.swapaxes(state_ssd[:, j].astype(F32), -1, -2)
            (y_ssd,) = _ssd(z, xbc, dtp, *ssd_args, s0t, False)
            pl_ = cache_na_k.shape[2]
            o_na = _latent_na(q, k, v, cache_na_k[:, j].reshape(bl, pl_, NA_WIDTH),
                              cache_na_v[:, j].reshape(bl, pl_, NA_WIDTH), ev_rpb[j])
            xl = _outproj(xl, mod, 1, y_ssd, o_na, w_out[:SSD_WIDTH], w_out[SSD_WIDTH:], None, None, tm)
        else:
            w_in = od_w_in[j].astype(BF16)
            w_out = od_w_out[j].astype(BF16)
            glu_w = od_glu_w[j].astype(BF16)
            tables = _s5_tables(od_lam_re[j], od_lam_im[j], od_log_step[j], od_b_re[j], od_b_im[j],
                                od_c_re[j], od_c_im[j])
            splits = (S5_WIDTH, FNET_WIDTH)
            u_s, u_f = _inproj(xc, mod, 0, norm1_g[i], w_in, splits, tm)
            seqs = lambda a: a.reshape(bc, lc, a.shape[-1])
            g_s, fin = _s5(seqs(u_s), tables, od_d_skip[j], None, True)
            y_f = _fnet(seqs(u_f))
            flat = lambda a: a.reshape(1, bc * lc, a.shape[-1])
            xc = _outproj(xc, mod, 0, flat(g_s), flat(y_f), w_out[:S5_WIDTH], w_out[S5_WIDTH:],
                          glu_w, od_glu_b[j], tm)
            fin = fin.reshape(S5_GROUPS, bc, 2, 2, S5_STATE)
            new_s5.append(jnp.transpose(fin, (1, 2, 0, 4, 3)))

            u_s, u_f = _inproj(xl, mod, 1, norm1_g[i], w_in, splits, tm)
            s0 = jnp.transpose(state_s5[:, j].astype(F32), (2, 0, 1, 4, 3)).reshape(
                S5_GROUPS, bl, 4 * S5_STATE)
            g_s, _ = _s5(u_s, tables, od_d_skip[j], s0, False)
            y_f = _fnet(u_f)
            xl = _outproj(xl, mod, 1, g_s, y_f, w_out[:S5_WIDTH], w_out[S5_WIDTH:], glu_w,
                          od_glu_b[j], tm)
        w1, w3, w2 = ffn_w1[i].astype(BF16), ffn_w3[i].astype(BF16), ffn_w2[i].astype(BF16)
        fg = final_g if last else None
        xc = _ffn(xc, mod, 0, norm2_g[i], w1, w3, w2, fg, tm)
        xl = _ffn(xl, mod, 1, norm2_g[i], w1, w3, w2, fg, tm)
    return (xc.reshape(bc, lc, D_MODEL), xl,
            jnp.stack(new_k, axis=1), jnp.stack(new_v, axis=1),
            jnp.stack(new_ssd, axis=1), jnp.stack(new_s5, axis=1))
```

```python
import functools
import math

import numpy as np
import jax
import jax.numpy as jnp
from jax import lax
from jax.experimental import pallas as pl
from jax.experimental.pallas import tpu as pltpu

F32 = jnp.float32
BF16 = jnp.bfloat16
HIGHEST = lax.Precision.HIGHEST

D_MODEL = 1024
DEPTH = 2
EPS = 1e-6
GRID_W = 64
FFN_HIDDEN = 2816
FFN_CHUNK = 1408

SSD_HEADS = 8
SSD_HEAD_DIM = 64
SSD_WIDTH = 512
SSD_GROUPS = 2
SSD_HPG = SSD_HEADS // SSD_GROUPS
SSD_STATE = 64
SSD_GN = SSD_GROUPS * SSD_STATE
SSD_XBC = SSD_WIDTH + 2 * SSD_GN
SSD_CHUNK = 128
CONV_W = 5
CONV_HALO = 8

NA_HEADS = 8
NA_HEAD_DIM = 64
NA_WIDTH = 512
NA_ROWS = 8
NA_COLS = 16
NA_SCALE = NA_HEAD_DIM ** -0.5
NA_QROWS = 4
NA_WIN = NA_ROWS + NA_QROWS
NEG_BIG = -1e30

O_XBC = SSD_WIDTH
O_DT = O_XBC + SSD_XBC
O_Q = O_DT + SSD_HEADS
O_K = O_Q + NA_WIDTH
O_V = O_K + NA_WIDTH
LANES = 128

S5_GROUP_CH = 16
S5_GROUPS = 32
S5_WIDTH = 512
S5_STATE = 64
S5_Q = 16
S5_CW = S5_Q * S5_GROUP_CH
S5_LOG_MAX = 7

FNET_GROUP_CH = 64
FNET_GROUPS = 8
FNET_WIDTH = 512

MOD_ROWS = 16
VMEM_LIMIT = 56 * 1024 * 1024


def _cparams(*sem):
    return pltpu.CompilerParams(dimension_semantics=sem, vmem_limit_bytes=VMEM_LIMIT)


def _const_spec(shape):
    nd = len(shape)
    return pl.BlockSpec(shape, lambda *_: (0,) * nd, pipeline_mode=pl.Buffered(1))


def _sigmoid(x):
    return 1.0 / (1.0 + jnp.exp(-x))


def _silu(x):
    return x * _sigmoid(x)


def _softplus(x):
    return jnp.maximum(x, 0.0) + jnp.log(1.0 + jnp.exp(-jnp.abs(x)))


def _gelu_tanh(x):
    return 0.5 * x * (1.0 + jnp.tanh(math.sqrt(2.0 / math.pi) * (x + 0.044715 * (x * x * x))))


def _rmsnorm(x, g):
    return x * lax.rsqrt(jnp.mean(x * x, axis=-1, keepdims=True) + EPS) * g


def _bdot(a, b):
    return jnp.dot(a.astype(BF16), b.astype(BF16), preferred_element_type=F32)


def _bdot_nt(a, b):
    return lax.dot_general(a.astype(BF16), b.astype(BF16), (((1,), (1,)), ((), ())),
                           preferred_element_type=F32)


def _bdot_tn(a, b):
    return lax.dot_general(a.astype(BF16), b.astype(BF16), (((0,), (0,)), ((), ())),
                           preferred_element_type=F32)


def _adaln_body(c_ref, w_ref, b_ref, o_ref):
    s = _silu(c_ref[...])
    o_ref[0] = jnp.dot(s, w_ref[0], precision=HIGHEST, preferred_element_type=F32) + b_ref[0]


def _adaln(cond, mod_w, mod_b):
    n = 6 * D_MODEL
    tn = 1536
    out = pl.pallas_call(
        _adaln_body,
        grid=(DEPTH, n // tn),
        in_specs=[pl.BlockSpec((MOD_ROWS, D_MODEL), lambda i, j: (0, 0)),
                  pl.BlockSpec((1, D_MODEL, tn), lambda i, j: (i, 0, j)),
                  pl.BlockSpec((1, 1, tn), lambda i, j: (i, 0, j))],
        out_specs=pl.BlockSpec((1, MOD_ROWS, tn), lambda i, j: (i, 0, j)),
        out_shape=jax.ShapeDtypeStruct((DEPTH, MOD_ROWS, n), F32),
        compiler_params=_cparams("arbitrary", "arbitrary"),
        name="adaln",
    )(cond, mod_w, mod_b.reshape(DEPTH, 1, n))
    return out.reshape(DEPTH, MOD_ROWS, 6, D_MODEL)


def _mod_spec(mrow0):
    return pl.BlockSpec((1, 6, D_MODEL), lambda b, t: (b + mrow0, 0, 0))


def _inproj_body(x_ref, mod_ref, g_ref, w_ref, *o_refs, splits):
    h = _rmsnorm(x_ref[0], g_ref[...]) * (1.0 + mod_ref[0, 1:2, :]) + mod_ref[0, 0:1, :]
    acc = jnp.dot(h.astype(BF16), w_ref[...], preferred_element_type=F32)
    off = 0
    for o_ref, n in zip(o_refs, splits):
        o_ref[0] = acc[:, off:off + n]
        off += n


def _inproj(x, mod, mrow0, g, w, splits, tm):
    b, l, _ = x.shape
    n = w.shape[1]
    return pl.pallas_call(
        functools.partial(_inproj_body, splits=splits),
        grid=(b, l // tm),
        in_specs=[pl.BlockSpec((1, tm, D_MODEL), lambda i, t: (i, t, 0)),
                  _mod_spec(mrow0),
                  _const_spec((1, D_MODEL)),
                  _const_spec((D_MODEL, n))],
        out_specs=[pl.BlockSpec((1, tm, s), lambda i, t: (i, t, 0)) for s in splits],
        out_shape=[jax.ShapeDtypeStruct((b, l, s), F32) for s in splits],
        compiler_params=_cparams("arbitrary", "arbitrary"),
        name="inproj",
    )(x, mod, g.reshape(1, D_MODEL), w)


def _outproj_body(x_ref, mod_ref, ya_ref, yb_ref, wa_ref, wb_ref, *rest, glu):
    ya = ya_ref[0]
    if glu:
        gw_ref, gb_ref, o_ref = rest
        ya = ya * _sigmoid(_bdot(ya, gw_ref[...]) + gb_ref[...])
    else:
        (o_ref,) = rest
    o = _bdot(ya, wa_ref[...]) + _bdot(yb_ref[0], wb_ref[...])
    o_ref[0] = x_ref[0] + mod_ref[0, 2:3, :] * o


def _outproj(x, mod, mrow0, ya, yb, wa, wb, glu_w, glu_b, tm):
    b, l, _ = x.shape
    wa_n, wb_n = ya.shape[-1], yb.shape[-1]
    glu = glu_w is not None
    in_specs = [pl.BlockSpec((1, tm, D_MODEL), lambda i, t: (i, t, 0)),
                _mod_spec(mrow0),
                pl.BlockSpec((1, tm, wa_n), lambda i, t: (i, t, 0)),
                pl.BlockSpec((1, tm, wb_n), lambda i, t: (i, t, 0)),
                _const_spec((wa_n, D_MODEL)),
                _const_spec((wb_n, D_MODEL))]
    args = [x, mod, ya, yb, wa, wb]
    if glu:
        in_specs += [_const_spec((wa_n, wa_n)), _const_spec((1, wa_n))]
        args += [glu_w, glu_b.reshape(1, wa_n)]
    return pl.pallas_call(
        functools.partial(_outproj_body, glu=glu),
        grid=(b, l // tm),
        in_specs=in_specs,
        out_specs=pl.BlockSpec((1, tm, D_MODEL), lambda i, t: (i, t, 0)),
        out_shape=jax.ShapeDtypeStruct((b, l, D_MODEL), F32),
        compiler_params=_cparams("arbitrary", "arbitrary"),
        name="outproj",
    )(*args)


def _ffn_body(x_ref, mod_ref, g_ref, w1_ref, w3_ref, w2_ref, *rest, final):
    x = x_ref[0]
    h = (_rmsnorm(x, g_ref[...]) * (1.0 + mod_ref[0, 4:5, :]) + mod_ref[0, 3:4, :]).astype(BF16)
    acc = jnp.zeros(x.shape, F32)
    for c in range(FFN_HIDDEN // FFN_CHUNK):
        cols = slice(c * FFN_CHUNK, (c + 1) * FFN_CHUNK)
        a = jnp.dot(h, w1_ref[:, cols], preferred_element_type=F32)
        u = _silu(a) * jnp.dot(h, w3_ref[:, cols], preferred_element_type=F32)
        acc = acc + jnp.dot(u.astype(BF16), w2_ref[cols, :], preferred_element_type=F32)
    y = x + mod_ref[0, 5:6, :] * acc
    if final:
        fg_ref, o_ref = rest
        y = _rmsnorm(y, fg_ref[...])
    else:
        (o_ref,) = rest
    o_ref[0] = y


def _ffn(x, mod, mrow0, g, w1, w3, w2, final_g, tm):
    b, l, _ = x.shape
    final = final_g is not None
    in_specs = [pl.BlockSpec((1, tm, D_MODEL), lambda i, t: (i, t, 0)),
                _mod_spec(mrow0),
                _const_spec((1, D_MODEL)),
                _const_spec((D_MODEL, FFN_HIDDEN)),
                _const_spec((D_MODEL, FFN_HIDDEN)),
                _const_spec((FFN_HIDDEN, D_MODEL))]
    args = [x, mod, g.reshape(1, D_MODEL), w1, w3, w2]
    if final:
        in_specs.append(_const_spec((1, D_MODEL)))
        args.append(final_g.reshape(1, D_MODEL))
    return pl.pallas_call(
        functools.partial(_ffn_body, final=final),
        grid=(b, l // tm),
        in_specs=in_specs,
        out_specs=pl.BlockSpec((1, tm, D_MODEL), lambda i, t: (i, t, 0)),
        out_shape=jax.ShapeDtypeStruct((b, l, D_MODEL), F32),
        compiler_params=_cparams("arbitrary", "arbitrary"),
        name="ffn",
    )(*args)


def _ssd_body(z_ref, xbc_ref, dt_ref, cw_ref, cb_ref, hp_ref, dsk_ref, ng_ref, *rest,
              seq, has_s0, want_final):
    rest = list(rest)
    s0_ref = rest.pop(0) if has_s0 else None
    y_ref = rest.pop(0)
    fin_ref = rest.pop(0) if want_final else None
    st_s, cd_s, scur, ych = rest
    q = SSD_CHUNK
    nc = seq // q
    row = lax.broadcasted_iota(jnp.int32, (q, q), 0)
    col = lax.broadcasted_iota(jnp.int32, (q, q), 1)
    lower = row >= col
    upper = col >= row
    tri_l = lower.astype(F32)
    tri_u = upper.astype(F32)
    a_f = -jnp.exp(hp_ref[0:1, :])
    a_b = -jnp.exp(hp_ref[1:2, :])
    lane = lax.broadcasted_iota(jnp.int32, (1, LANES), 1)
    a_f = jnp.where(lane < SSD_HEADS, a_f, 0.0)
    a_b = jnp.where(lane < SSD_HEADS, a_b, 0.0)
    bias_f = hp_ref[2:3, :]
    bias_b = hp_ref[3:4, :]

    def chunk_pre(c):
        r0 = pl.multiple_of(c * q, q)
        lo = pl.multiple_of(jnp.maximum(r0 - CONV_HALO, 0), CONV_HALO)
        hi = pl.multiple_of(jnp.minimum(r0 + q, seq - CONV_HALO), CONV_HALO)
        prev = jnp.where(c > 0, xbc_ref[0, pl.ds(lo, CONV_HALO), :], 0.0)
        nxt = jnp.where(c < nc - 1, xbc_ref[0, pl.ds(hi, CONV_HALO), :], 0.0)
        win = jnp.concatenate([prev, xbc_ref[0, pl.ds(r0, q), :], nxt], axis=0)
        acc = cb_ref[...] + cw_ref[0:1, :] * win[CONV_HALO - 2:CONV_HALO - 2 + q]
        for k in range(1, CONV_W):
            o = CONV_HALO - CONV_W // 2 + k
            acc = acc + cw_ref[k:k + 1, :] * win[o:o + q]
        xc = _silu(acc)
        dtr = dt_ref[0, pl.ds(r0, q), :]
        dt_f = _softplus(dtr + bias_f)
        dt_b = _softplus(dtr + bias_b)
        cum_f = jnp.dot(tri_l, dt_f * a_f, precision=HIGHEST, preferred_element_type=F32)
        cum_b = jnp.dot(tri_u, dt_b * a_b, precision=HIGHEST, preferred_element_type=F32)
        return r0, xc, dt_f, dt_b, cum_f, cum_b

    def pass_a(c, carry):
        _, xc, dt_f, dt_b, cum_f, cum_b = chunk_pre(c)
        end_f = cum_f[q - 1:q, :]
        end_b = cum_b[0:1, :]
        w_f = jnp.exp(end_f - cum_f) * dt_f
        w_b = jnp.exp(end_b - cum_b) * dt_b
        cd_s[0, c] = jnp.broadcast_to(jnp.exp(end_f), (SSD_STATE, LANES))
        cd_s[1, c] = jnp.broadcast_to(jnp.exp(end_b), (SSD_STATE, LANES))
        for h in range(SSD_HEADS):
            g = h // SSD_HPG
            xh = xc[:, h * SSD_HEAD_DIM:(h + 1) * SSD_HEAD_DIM]
            bg = xc[:, SSD_WIDTH + g * SSD_STATE:SSD_WIDTH + (g + 1) * SSD_STATE]
            st_s[0, c, h] = _bdot_tn(bg, xh * w_f[:, h:h + 1])
            st_s[1, c, h] = _bdot_tn(bg, xh * w_b[:, h:h + 1])
        return carry

    lax.fori_loop(0, nc, pass_a, 0)

    if has_s0:
        scur[...] = s0_ref[0]
    else:
        scur[...] = jnp.zeros(scur.shape, F32)

    def pass_b(c, carry):
        cr = nc - 1 - c
        cd_f = cd_s[0, c]
        cd_b = cd_s[1, cr]
        for h in range(SSD_HEADS):
            s_in = scur[0, h]
            scur[0, h] = s_in * cd_f[:, h:h + 1] + st_s[0, c, h]
            st_s[0, c, h] = s_in
            s_in = scur[1, h]
            scur[1, h] = s_in * cd_b[:, h:h + 1] + st_s[1, cr, h]
            st_s[1, cr, h] = s_in
        return carry

    lax.fori_loop(0, nc, pass_b, 0)
    if want_final:
        fin_ref[0] = scur[...]

    def pass_c(c, carry):
        r0, xc, dt_f, dt_b, cum_f, cum_b = chunk_pre(c)
        cum_ft, cum_bt, dt_ft, dt_bt = cum_f.T, cum_b.T, dt_f.T, dt_b.T
        e_f = jnp.exp(cum_f)
        e_b = jnp.exp(cum_b)
        cbs = []
        for g in range(SSD_GROUPS):
            bg = xc[:, SSD_WIDTH + g * SSD_STATE:SSD_WIDTH + (g + 1) * SSD_STATE]
            cg = xc[:, SSD_WIDTH + SSD_GN + g * SSD_STATE:SSD_WIDTH + SSD_GN + (g + 1) * SSD_STATE]
            cbs.append((cg, _bdot_nt(cg, bg)))
        for h in range(SSD_HEADS):
            cg, cb = cbs[h // SSD_HPG]
            xh = xc[:, h * SSD_HEAD_DIM:(h + 1) * SSD_HEAD_DIM]
            seg_f = cum_f[:, h:h + 1] - cum_ft[h:h + 1, :]
            seg_b = cum_b[:, h:h + 1] - cum_bt[h:h + 1, :]
            m_f = jnp.exp(jnp.where(lower, seg_f, NEG_BIG)) * dt_ft[h:h + 1, :]
            m_b = jnp.exp(jnp.where(upper, seg_b, NEG_BIG)) * dt_bt[h:h + 1, :]
            y = _bdot(cb * (m_f + m_b), xh)
            c_off = jnp.concatenate([cg * e_f[:, h:h + 1], cg * e_b[:, h:h + 1]], axis=1)
            s_in = jnp.concatenate([st_s[0, c, h], st_s[1, c, h]], axis=0)
            y = y + _bdot(c_off, s_in)
            ych[:, h * SSD_HEAD_DIM:(h + 1) * SSD_HEAD_DIM] = y
        yf = (ych[...] + dsk_ref[...] * xc[:, :SSD_WIDTH]) * _silu(z_ref[0, pl.ds(r0, q), :])
        y_ref[0, pl.ds(r0, q), :] = _rmsnorm(yf, ng_ref[...])
        return carry

    lax.fori_loop(0, nc, pass_c, 0)


def _ssd(z, xbc, dtp, conv_w, conv_b, a_log, dt_bias, d_skip, norm_g, s0t, want_final):
    b, l, _ = z.shape
    nc = l // SSD_CHUNK
    has_s0 = s0t is not None
    cw = jnp.zeros((8, SSD_XBC), F32).at[:CONV_W].set(conv_w)
    hp = jnp.zeros((8, LANES), F32)
    hp = hp.at[0:2, :SSD_HEADS].set(a_log).at[2:4, :SSD_HEADS].set(dt_bias)
    dsk = jnp.repeat(d_skip, SSD_HEAD_DIM).reshape(1, SSD_WIDTH)
    seq_spec = lambda n: pl.BlockSpec((1, l, n), lambda i: (i, 0, 0))
    st_spec = pl.BlockSpec((1, 2, SSD_HEADS, SSD_STATE, SSD_HEAD_DIM), lambda i: (i, 0, 0, 0, 0))
    in_specs = [seq_spec(SSD_WIDTH), seq_spec(SSD_XBC), seq_spec(LANES),
                _const_spec((8, SSD_XBC)), _const_spec((1, SSD_XBC)), _const_spec((8, LANES)),
                _const_spec((1, SSD_WIDTH)), _const_spec((1, SSD_WIDTH))]
    args = [z, xbc, dtp, cw, conv_b.reshape(1, SSD_XBC), hp, dsk, norm_g.reshape(1, SSD_WIDTH)]
    if has_s0:
        in_specs.append(st_spec)
        args.append(s0t)
    out_specs = [seq_spec(SSD_WIDTH)]
    out_shape = [jax.ShapeDtypeStruct((b, l, SSD_WIDTH), F32)]
    if want_final:
        out_specs.append(st_spec)
        out_shape.append(jax.ShapeDtypeStruct((b, 2, SSD_HEADS, SSD_STATE, SSD_HEAD_DIM), F32))
    return pl.pallas_call(
        functools.partial(_ssd_body, seq=l, has_s0=has_s0, want_final=want_final),
        grid=(b,),
        in_specs=in_specs,
        out_specs=out_specs,
        out_shape=out_shape,
        scratch_shapes=[pltpu.VMEM((2, nc, SSD_HEADS, SSD_STATE, SSD_HEAD_DIM), F32),
                        pltpu.VMEM((2, nc, SSD_STATE, LANES), F32),
                        pltpu.VMEM((2, SSD_HEADS, SSD_STATE, SSD_HEAD_DIM), F32),
                        pltpu.VMEM((SSD_CHUNK, SSD_WIDTH), F32)],
        compiler_params=_cparams("arbitrary"),
        name="ssd",
    )(*args)


def _softmax_pv(parts):
    m = parts[0][0].max(axis=-1, keepdims=True)
    for s, _ in parts[1:]:
        m = jnp.maximum(m, s.max(axis=-1, keepdims=True))
    den = 0.0
    out = 0.0
    for s, v in parts:
        p = jnp.exp(s - m)
        den = den + p.sum(axis=-1, keepdims=True)
        out = out + _bdot(p, v)
    return out / den


def _ctx_attn_body(q_ref, k_ref, v_ref, o_ref):
    for h in range(NA_HEADS):
        hs = slice(h * NA_HEAD_DIM, (h + 1) * NA_HEAD_DIM)
        s = _bdot_nt(q_ref[0, :, hs], k_ref[0, :, hs]) * NA_SCALE
        o_ref[0, :, hs] = _softmax_pv([(s, v_ref[0, :, hs])])


def _ctx_attn(q, k, v):
    b, l, _ = q.shape
    spec = pl.BlockSpec((1, l, NA_WIDTH), lambda i: (i, 0, 0))
    return pl.pallas_call(
        _ctx_attn_body,
        grid=(b,),
        in_specs=[spec, spec, spec],
        out_specs=spec,
        out_shape=jax.ShapeDtypeStruct((b, l, NA_WIDTH), F32),
        compiler_params=_cparams("arbitrary"),
        name="ctx_attn",
    )(q, k, v)


def _na_bias_tables(rpb, rows):
    kr = min(NA_ROWS, rows)
    qi = np.arange(NA_QROWS)[:, None]
    wi = np.arange(NA_WIN)[None, :]
    qc = np.arange(GRID_W)[:, None]
    kc = np.arange(GRID_W)[None, :]
    cs = np.clip(qc - NA_COLS // 2, 0, GRID_W - NA_COLS)
    col_ok = (kc >= cs) & (kc < cs + NA_COLS)
    dc = np.clip(kc - qc + NA_COLS - 1, 0, 2 * NA_COLS - 2)
    col_sel = (dc[None] == np.arange(2 * NA_COLS - 1)[:, None, None]).astype(np.float32)
    row_sel, row_ok = [], []
    for r0 in (0, NA_QROWS, rows - NA_QROWS):
        ws = min(max(r0 - kr // 2, 0), rows - NA_WIN)
        r = r0 + qi
        krow = ws + wi
        rs = np.clip(r - kr // 2, 0, rows - kr)
        row_ok.append((krow >= rs) & (krow < rs + kr))
        dr = np.clip(krow - r + NA_ROWS - 1, 0, 2 * NA_ROWS - 2)
        row_sel.append((dr[..., None] == np.arange(2 * NA_ROWS - 1)).astype(np.float32))
    ok = np.stack(row_ok)[:, None, :, None, :, None] & col_ok[None, None, None, :, None, :]
    by_col = jnp.einsum('hab,bqk->haqk', rpb.astype(F32), jnp.asarray(col_sel), precision=HIGHEST)
    bias = jnp.einsum('viwa,haqk->vhiqwk', jnp.asarray(np.stack(row_sel)), by_col, precision=HIGHEST)
    return jnp.where(jnp.asarray(ok), bias, NEG_BIG).reshape(
        3, NA_HEADS, NA_QROWS * GRID_W, NA_WIN * GRID_W)


def _na_body(q_ref, k_ref, v_ref, kc_ref, vc_ref, bias_ref, o_ref, *, rows):
    rb = pl.program_id(1)
    ws = jnp.clip(rb * NA_QROWS - NA_ROWS // 2, 0, rows - NA_WIN)
    k0 = pl.multiple_of(ws * GRID_W, NA_QROWS * GRID_W)
    nk = NA_WIN * GRID_W
    for h in range(NA_HEADS):
        hs = slice(h * NA_HEAD_DIM, (h + 1) * NA_HEAD_DIM)
        qh = q_ref[0, :, hs]
        s_loc = _bdot_nt(qh, k_ref[0, pl.ds(k0, nk), hs]) * NA_SCALE + bias_ref[0, h]
        s_ctx = _bdot_nt(qh, kc_ref[0, :, hs]) * NA_SCALE
        o_ref[0, :, hs] = _softmax_pv([(s_loc, v_ref[0, pl.ds(k0, nk), hs]), (s_ctx, vc_ref[0, :, hs])])


def _latent_na(q, k, v, k_ctx, v_ctx, rpb):
    b, l, _ = q.shape
    lc = k_ctx.shape[1]
    rows = l // GRID_W
    assert rows >= NA_WIN and rows % NA_QROWS == 0 and NA_QROWS * 2 <= NA_ROWS
    nrb = rows // NA_QROWS
    tq = NA_QROWS * GRID_W
    bias = _na_bias_tables(rpb, rows)
    variant = lambda i, r: (jnp.where(r == 0, 0, jnp.where(r == nrb - 1, 2, 1)), 0, 0, 0)
    return pl.pallas_call(
        functools.partial(_na_body, rows=rows),
        grid=(b, nrb),
        in_specs=[pl.BlockSpec((1, tq, NA_WIDTH), lambda i, r: (i, r, 0)),
                  pl.BlockSpec((1, l, NA_WIDTH), lambda i, r: (i, 0, 0)),
                  pl.BlockSpec((1, l, NA_WIDTH), lambda i, r: (i, 0, 0)),
                  pl.BlockSpec((1, lc, NA_WIDTH), lambda i, r: (i, 0, 0)),
                  pl.BlockSpec((1, lc, NA_WIDTH), lambda i, r: (i, 0, 0)),
                  pl.BlockSpec((1, NA_HEADS, tq, NA_WIN * GRID_W), variant)],
        out_specs=pl.BlockSpec((1, tq, NA_WIDTH), lambda i, r: (i, r, 0)),
        out_shape=jax.ShapeDtypeStruct((b, l, NA_WIDTH), F32),
        compiler_params=_cparams("arbitrary", "arbitrary"),
        name="latent_na",
    )(q, k, v, k_ctx, v_ctx, bias)


def _s5_prep_body(lre_ref, lim_ref, lst_ref, btre_ref, btim_ref, cre_ref, cim_ref,
                  wt_ref, kt_ref, et_ref, pw_ref):
    lam_re, lam_im = lre_ref[0], lim_ref[0]
    step = jnp.exp(lst_ref[0])
    mag = jnp.exp(lam_re * step)
    a_re = mag * jnp.cos(lam_im * step)
    a_im = mag * jnp.sin(lam_im * step)
    den = lam_re * lam_re + lam_im * lam_im
    k_re = ((a_re - 1.0) * lam_re + a_im * lam_im) / den
    k_im = (a_im * lam_re - (a_re - 1.0) * lam_im) / den
    bt_re, bt_im = btre_ref[0], btim_ref[0]
    bb_re = k_re * bt_re - k_im * bt_im
    bb_im = k_re * bt_im + k_im * bt_re
    c_re, c_im = cre_ref[0], cim_ref[0]
    p_re = jnp.ones_like(a_re)
    p_im = jnp.zeros_like(a_re)
    for t in range(S5_Q):
        rows = slice(t * S5_GROUP_CH, (t + 1) * S5_GROUP_CH)
        wt_ref[0, rows, 0:S5_STATE] = p_re * bb_re - p_im * bb_im
        wt_ref[0, rows, S5_STATE:2 * S5_STATE] = p_re * bb_im + p_im * bb_re
        p_re, p_im = p_re * a_re - p_im * a_im, p_re * a_im + p_im * a_re
        et_ref[0, rows, 0:S5_STATE] = c_re * p_re - c_im * p_im
        et_ref[0, rows, S5_STATE:2 * S5_STATE] = -(c_re * p_im + c_im * p_re)
    cc = jnp.concatenate([c_re, -c_im], axis=1)
    kt_ref[0] = lax.dot_general(wt_ref[0], cc, (((1,), (1,)), ((), ())), precision=HIGHEST,
                                preferred_element_type=F32)
    pw_ref[0] = jnp.zeros(pw_ref.shape[1:], F32)
    for k in range(S5_LOG_MAX):
        pw_ref[0, 2 * k:2 * k + 1, :] = jnp.concatenate([p_re, p_re], axis=1)
        pw_ref[0, 2 * k + 1:2 * k + 2, :] = jnp.concatenate([-p_im, p_im], axis=1)
        p_re, p_im = p_re * p_re - p_im * p_im, 2.0 * p_re * p_im


def _s5_tables(lam_re, lam_im, log_step, b_re, b_im, c_re, c_im):
    g, p, ch, q = S5_GROUPS, S5_STATE, S5_GROUP_CH, S5_Q
    n = 2 * g
    row = lambda a: a.astype(F32).reshape(n, 1, p)
    lst = jnp.broadcast_to(log_step.astype(F32)[:, :, None], (2, g, p)).reshape(n, 1, p)
    bt_re = jnp.swapaxes(b_re.astype(F32), 1, 2)
    bt_im = jnp.swapaxes(b_im.astype(F32), 1, 2)
    dspec = pl.BlockSpec((1, 1, p), lambda i: (i, 0, 0))
    gspec = pl.BlockSpec((1, ch, p), lambda i: (i % g, 0, 0))
    ospec = lambda a, bb: pl.BlockSpec((1, a, bb), lambda i: (i, 0, 0))
    wt, kt, et, pw = pl.pallas_call(
        _s5_prep_body,
        grid=(n,),
        in_specs=[dspec, dspec, dspec, gspec, gspec, gspec, gspec],
        out_specs=[ospec(S5_CW, 2 * p), ospec(S5_CW, ch), ospec(S5_CW, 2 * p), ospec(16, 2 * p)],
        out_shape=[jax.ShapeDtypeStruct((n, S5_CW, 2 * p), F32),
                   jax.ShapeDtypeStruct((n, S5_CW, ch), F32),
                   jax.ShapeDtypeStruct((n, S5_CW, 2 * p), F32),
                   jax.ShapeDtypeStruct((n, 16, 2 * p), F32)],
        compiler_params=_cparams("arbitrary"),
        name="s5_prep",
    )(row(lam_re), row(lam_im), lst, bt_re, bt_im, c_re.astype(F32), c_im.astype(F32))
    kt = kt.reshape(2, g, q, ch, ch)
    ti = jnp.arange(q)[:, None]
    to = jnp.arange(q)[None, :]
    t_f = jnp.where((to >= ti)[None, :, :, None, None], kt[0][:, jnp.clip(to - ti, 0, q - 1)], 0.0)
    t_b = jnp.where((ti >= to)[None, :, :, None, None], kt[1][:, jnp.clip(ti - to, 0, q - 1)], 0.0)
    t_all = jnp.transpose(t_f + t_b, (0, 1, 3, 2, 4)).reshape(g, S5_CW, S5_CW)
    wt = wt.reshape(2, g, q, ch, 2 * p)
    g_fb = jnp.concatenate([jnp.flip(wt[0], axis=1), wt[1]], axis=-1).reshape(g, S5_CW, 4 * p)
    et = et.reshape(2, g, q, ch, 2 * p)
    e_f = jnp.swapaxes(et[0].reshape(g, S5_CW, 2 * p), 1, 2)
    e_b = jnp.swapaxes(jnp.flip(et[1], axis=1).reshape(g, S5_CW, 2 * p), 1, 2)
    e_fb = jnp.concatenate([e_f, e_b], axis=1)
    return t_all.astype(BF16), g_fb.astype(BF16), e_fb.astype(BF16), pw.reshape(2, g, 16, 2 * p)


def _cmul(a_full, a_sgn, s):
    return a_full * s + a_sgn * pltpu.roll(s, S5_STATE, axis=1)


def _s5_body(u_ref, t_ref, g_ref, e_ref, pw_ref, dsk_ref, *rest, nb, nc, has_s0, want_final):
    rest = list(rest)
    s0_ref = rest.pop(0) if has_s0 else None
    o_ref = rest.pop(0)
    fin_ref = rest.pop(0) if want_final else None
    xs_f, xs_b = rest
    m = nb * nc
    w = 2 * S5_STATE
    u = u_ref[0]
    ub = u.astype(BF16)
    z = jnp.dot(ub, g_ref[0], preferred_element_type=F32)
    x_f, x_b = z[:, :w], z[:, w:]
    cidx = lax.broadcasted_iota(jnp.int32, (m, w), 0) & (nc - 1)
    first = cidx == 0
    last = cidx == nc - 1
    if has_s0:
        rep = lambda a: jnp.broadcast_to(a[:, None, :], (nb, nc, w)).reshape(m, w)
        s0_f, s0_b = rep(s0_ref[0, :, :w]), rep(s0_ref[0, :, w:])
        x_f = x_f + jnp.where(first, _cmul(pw_ref[0, 0, 0:1, :], pw_ref[0, 0, 1:2, :], s0_f), 0.0)
        x_b = x_b + jnp.where(last, _cmul(pw_ref[1, 0, 0:1, :], pw_ref[1, 0, 1:2, :], s0_b), 0.0)
    for k in range(nc.bit_length() - 1):
        sh = 1 << k
        prev = pltpu.roll(x_f, sh, axis=0)
        x_f = x_f + jnp.where(cidx >= sh, _cmul(pw_ref[0, 0, 2 * k:2 * k + 1, :],
                                                 pw_ref[0, 0, 2 * k + 1:2 * k + 2, :], prev), 0.0)
        nxt = pltpu.roll(x_b, m - sh, axis=0)
        x_b = x_b + jnp.where(cidx < nc - sh, _cmul(pw_ref[1, 0, 2 * k:2 * k + 1, :],
                                                     pw_ref[1, 0, 2 * k + 1:2 * k + 2, :], nxt), 0.0)
    in_f = pltpu.roll(x_f, 1, axis=0)
    in_b = pltpu.roll(x_b, m - 1, axis=0)
    if has_s0:
        in_f = jnp.where(first, s0_f, in_f)
        in_b = jnp.where(last, s0_b, in_b)
    else:
        in_f = jnp.where(first, 0.0, in_f)
        in_b = jnp.where(last, 0.0, in_b)
    s_in = jnp.concatenate([in_f, in_b], axis=1).astype(BF16)
    y = (jnp.dot(ub, t_ref[0], preferred_element_type=F32)
         + jnp.dot(s_in, e_ref[0], preferred_element_type=F32) + u * dsk_ref[0])
    o_ref[0] = _gelu_tanh(y)
    if want_final:
        xs_f[...] = x_f
        xs_b[...] = x_b
        fin_ref[0, :, :w] = xs_f[pl.ds(nc - 1, nb, stride=nc), :]
        fin_ref[0, :, w:] = xs_b[pl.ds(0, nb, stride=nc), :]


def _s5(u, tables, d_skip, s0, want_final):
    t_all, g_fb, e_fb, pows = tables
    b, l, _ = u.shape
    g, ch, q = S5_GROUPS, S5_GROUP_CH, S5_Q
    nc = l // q
    m = b * nc
    has_s0 = s0 is not None
    uc = jnp.transpose(u.reshape(b, nc, q, g, ch), (3, 0, 1, 2, 4)).reshape(g, m, S5_CW)
    dsk = jnp.tile(d_skip.astype(F32).reshape(g, 1, ch), (1, q, 1)).reshape(g, 1, S5_CW)
    gspec = lambda a, bb: pl.BlockSpec((1, a, bb), lambda i: (i, 0, 0))
    in_specs = [gspec(m, S5_CW), gspec(S5_CW, S5_CW), gspec(S5_CW, S5_CW), gspec(S5_CW, S5_CW),
                pl.BlockSpec((2, 1, 16, 2 * S5_STATE), lambda i: (0, i, 0, 0)), gspec(1, S5_CW)]
    args = [uc, t_all, g_fb, e_fb, pows, dsk]
    if has_s0:
        in_specs.append(gspec(b, S5_CW))
        args.append(s0)
    out_specs = [gspec(m, S5_CW)]
    out_shape = [jax.ShapeDtypeStruct((g, m, S5_CW), F32)]
    if want_final:
        out_specs.append(gspec(b, S5_CW))
        out_shape.append(jax.ShapeDtypeStruct((g, b, S5_CW), F32))
    res = pl.pallas_call(
        functools.partial(_s5_body, nb=b, nc=nc, has_s0=has_s0, want_final=want_final),
        grid=(g,),
        in_specs=in_specs,
        out_specs=out_specs,
        out_shape=out_shape,
        scratch_shapes=[pltpu.VMEM((m, 2 * S5_STATE), F32), pltpu.VMEM((m, 2 * S5_STATE), F32)],
        compiler_params=_cparams("arbitrary"),
        name="s5",
    )(*args)
    y = jnp.transpose(res[0].reshape(g, b, nc, q, ch), (1, 2, 3, 0, 4)).reshape(b, l, S5_WIDTH)
    return y, (res[1] if want_final else None)


def _dft_cos_sin(n):
    jk = (jnp.arange(n, dtype=jnp.int32)[:, None] * jnp.arange(n, dtype=jnp.int32)[None, :]) % n
    ang = jk.astype(F32) * (2.0 * math.pi / n)
    return jnp.cos(ang), jnp.sin(ang)


def _fnet_body(u_ref, cs_ref, dl_ref, o_ref, xs, *, seq):
    @pl.when(pl.program_id(1) == 0)
    def _():
        xcs = jnp.dot(u_ref[0].astype(BF16), cs_ref[...], preferred_element_type=F32)
        xs[0:seq, :] = xcs[:, :FNET_WIDTH].astype(BF16)
        xs[seq:2 * seq, :] = xcs[:, FNET_WIDTH:].astype(BF16)

    scale = 1.0 / math.sqrt(seq * FNET_GROUP_CH)
    o_ref[0] = jnp.dot(dl_ref[...], xs[...], preferred_element_type=F32) * scale


def _fnet(u):
    b, l, _ = u.shape
    tr = min(l, 512)
    cc, sc = _dft_cos_sin(FNET_GROUP_CH)
    eye = jnp.eye(FNET_GROUPS, dtype=F32)
    cs = jnp.concatenate([jnp.kron(eye, cc), jnp.kron(eye, sc)], axis=1).astype(BF16)
    cl, sl = _dft_cos_sin(l)
    dl = jnp.concatenate([cl, -sl], axis=1).astype(BF16)
    return pl.pallas_call(
        functools.partial(_fnet_body, seq=l),
        grid=(b, l // tr),
        in_specs=[pl.BlockSpec((1, l, FNET_WIDTH), lambda i, t: (i, 0, 0)),
                  _const_spec((FNET_WIDTH, 2 * FNET_WIDTH)),
                  pl.BlockSpec((tr, 2 * l), lambda i, t: (t, 0))],
        out_specs=pl.BlockSpec((1, tr, FNET_WIDTH), lambda i, t: (i, t, 0)),
        out_shape=jax.ShapeDtypeStruct((b, l, FNET_WIDTH), F32),
        scratch_shapes=[pltpu.VMEM((2 * l, FNET_WIDTH), BF16)],
        compiler_params=_cparams("arbitrary", "arbitrary"),
        name="fnet",
    )(u, cs, dl)


def _even_w_in(w):
    pad = jnp.zeros((D_MODEL, LANES - SSD_HEADS), w.dtype)
    return jnp.concatenate([w[:, :O_DT], w[:, O_Q:], w[:, O_DT:O_Q], pad], axis=1).astype(BF16)


EVEN_SPLITS = (SSD_WIDTH, SSD_XBC, NA_WIDTH, NA_WIDTH, NA_WIDTH, LANES)


def kernel(x_prompt, x_sample, cache_na_k, cache_na_v, state_ssd, state_s5, c, c_ctx, mod_w, mod_b, norm1_g, norm2_g, ffn_w1, ffn_w3, ffn_w2, final_g, ev_w_in, ev_conv_w, ev_conv_b, ev_a_log, ev_dt_bias, ev_d_skip, ev_ssd_norm_g, ev_rpb, ev_w_out, od_w_in, od_lam_re, od_lam_im, od_log_step, od_b_re, od_b_im, od_c_re, od_c_im, od_d_skip, od_glu_w, od_glu_b, od_w_out):
    bc, lc, _ = x_prompt.shape
    bl, ll, _ = x_sample.shape
    cond = jnp.zeros((MOD_ROWS, D_MODEL), F32).at[0].set(c_ctx).at[1:1 + bl].set(c)
    mods = _adaln(cond, mod_w, mod_b)
    xc = x_prompt.reshape(1, bc * lc, D_MODEL)
    xl = x_sample
    tm = 512
    new_k, new_v, new_ssd, new_s5 = [], [], [], []
    for i in range(DEPTH):
        j = i // 2
        mod = mods[i]
        last = i == DEPTH - 1
        if i % 2 == 0:
            w_in = _even_w_in(ev_w_in[j])
            w_out = ev_w_out[j].astype(BF16)
            ssd_args = (ev_conv_w[j], ev_conv_b[j], ev_a_log[j], ev_dt_bias[j], ev_d_skip[j],
                        ev_ssd_norm_g[j])
            z, xbc, q, k, v, dtp = _inproj(xc, mod, 0, norm1_g[i], w_in, EVEN_SPLITS, tm)
            seqs = lambda a: a.reshape(bc, lc, a.shape[-1])
            y_ssd, fin = _ssd(seqs(z), seqs(xbc), seqs(dtp), *ssd_args, None, True)
            o_na = _ctx_attn(seqs(q), seqs(k), seqs(v))
            flat = lambda a: a.reshape(1, bc * lc, a.shape[-1])
            xc = _outproj(xc, mod, 0, flat(y_ssd), flat(o_na), w_out[:SSD_WIDTH], w_out[SSD_WIDTH:],
                          None, None, tm)
            new_k.append(k.reshape(bc, lc, NA_HEADS, NA_HEAD_DIM))
            new_v.append(v.reshape(bc, lc, NA_HEADS, NA_HEAD_DIM))
            new_ssd.append(jnp.swapaxes(fin, -1, -2))

            z, xbc, q, k, v, dtp = _inproj(xl, mod, 1, norm1_g[i], w_in, EVEN_SPLITS, tm)
            s0t = jnp.swapaxes(state_ssd[:, j].astype(F32), -1, -2)
            (y_ssd,) = _ssd(z, xbc, dtp, *ssd_args, s0t, False)
            pl_ = cache_na_k.shape[2]
            o_na = _latent_na(q, k, v, cache_na_k[:, j].reshape(bl, pl_, NA_WIDTH),
                              cache_na_v[:, j].reshape(bl, pl_, NA_WIDTH), ev_rpb[j])
            xl = _outproj(xl, mod, 1, y_ssd, o_na, w_out[:SSD_WIDTH], w_out[SSD_WIDTH:], None, None, tm)
        else:
            w_in = od_w_in[j].astype(BF16)
            w_out = od_w_out[j].astype(BF16)
            glu_w = od_glu_w[j].astype(BF16)
            tables = _s5_tables(od_lam_re[j], od_lam_im[j], od_log_step[j], od_b_re[j], od_b_im[j],
                                od_c_re[j], od_c_im[j])
            splits = (S5_WIDTH, FNET_WIDTH)
            u_s, u_f = _inproj(xc, mod, 0, norm1_g[i], w_in, splits, tm)
            seqs = lambda a: a.reshape(bc, lc, a.shape[-1])
            g_s, fin = _s5(seqs(u_s), tables, od_d_skip[j], None, True)
            y_f = _fnet(seqs(u_f))
            flat = lambda a: a.reshape(1, bc * lc, a.shape[-1])
            xc = _outproj(xc, mod, 0, flat(g_s), flat(y_f), w_out[:S5_WIDTH], w_out[S5_WIDTH:],
                          glu_w, od_glu_b[j], tm)
            fin = fin.reshape(S5_GROUPS, bc, 2, 2, S5_STATE)
            new_s5.append(jnp.transpose(fin, (1, 2, 0, 4, 3)))

            u_s, u_f = _inproj(xl, mod, 1, norm1_g[i], w_in, splits, tm)
            s0 = jnp.transpose(state_s5[:, j].astype(F32), (2, 0, 1, 4, 3)).reshape(
                S5_GROUPS, bl, 4 * S5_STATE)
            g_s, _ = _s5(u_s, tables, od_d_skip[j], s0, False)
            y_f = _fnet(u_f)
            xl = _outproj(xl, mod, 1, g_s, y_f, w_out[:S5_WIDTH], w_out[S5_WIDTH:], glu_w,
                          od_glu_b[j], tm)
        w1, w3, w2 = ffn_w1[i].astype(BF16), ffn_w3[i].astype(BF16), ffn_w2[i].astype(BF16)
        fg = final_g if last else None
        xc = _ffn(xc, mod, 0, norm2_g[i], w1, w3, w2, fg, tm)
        xl = _ffn(xl, mod, 1, norm2_g[i], w1, w3, w2, fg, tm)
    return (xc.reshape(bc, lc, D_MODEL), xl,
            jnp.stack(new_k, axis=1), jnp.stack(new_v, axis=1),
            jnp.stack(new_ssd, axis=1), jnp.stack(new_s5, axis=1))
```

```python
import functools
import math

import numpy as np
import jax
import jax.numpy as jnp
from jax import lax
from jax.experimental import pallas as pl
from jax.experimental.pallas import tpu as pltpu

F32 = jnp.float32
BF16 = jnp.bfloat16
HIGHEST = lax.Precision.HIGHEST

D_MODEL = 1024
DEPTH = 2
EPS = 1e-6
GRID_W = 64
FFN_HIDDEN = 2816
FFN_CHUNK = 1408

SSD_HEADS = 8
SSD_HEAD_DIM = 64
SSD_WIDTH = 512
SSD_GROUPS = 2
SSD_HPG = SSD_HEADS // SSD_GROUPS
SSD_STATE = 64
SSD_GN = SSD_GROUPS * SSD_STATE
SSD_XBC = SSD_WIDTH + 2 * SSD_GN
SSD_CHUNK = 128
CONV_W = 5
CONV_HALO = 8

NA_HEADS = 8
NA_HEAD_DIM = 64
NA_WIDTH = 512
NA_ROWS = 8
NA_COLS = 16
NA_SCALE = NA_HEAD_DIM ** -0.5
NA_QROWS = 4
NA_WIN = NA_ROWS + NA_QROWS
NEG_BIG = -1e30

O_XBC = SSD_WIDTH
O_DT = O_XBC + SSD_XBC
O_Q = O_DT + SSD_HEADS
O_K = O_Q + NA_WIDTH
O_V = O_K + NA_WIDTH
LANES = 128

S5_GROUP_CH = 16
S5_GROUPS = 32
S5_WIDTH = 512
S5_STATE = 64
S5_Q = 16
S5_CW = S5_Q * S5_GROUP_CH
S5_LOG_MAX = 7
S5_LG = LANES // S5_GROUP_CH

FNET_GROUP_CH = 64
FNET_GROUPS = 8
FNET_WIDTH = 512

MOD_ROWS = 16
VMEM_LIMIT = 56 * 1024 * 1024


def _cparams(*sem):
    return pltpu.CompilerParams(dimension_semantics=sem, vmem_limit_bytes=VMEM_LIMIT)


def _const_spec(shape):
    nd = len(shape)
    return pl.BlockSpec(shape, lambda *_: (0,) * nd, pipeline_mode=pl.Buffered(1))


def _sigmoid(x):
    return 1.0 / (1.0 + jnp.exp(-x))


def _silu(x):
    return x * _sigmoid(x)


def _softplus(x):
    return jnp.maximum(x, 0.0) + jnp.log(1.0 + jnp.exp(-jnp.abs(x)))


def _gelu_tanh(x):
    return 0.5 * x * (1.0 + jnp.tanh(math.sqrt(2.0 / math.pi) * (x + 0.044715 * (x * x * x))))


def _rmsnorm(x, g):
    return x * lax.rsqrt(jnp.mean(x * x, axis=-1, keepdims=True) + EPS) * g


def _bdot(a, b):
    return jnp.dot(a.astype(BF16), b.astype(BF16), preferred_element_type=F32)


def _bdot_nt(a, b):
    return lax.dot_general(a.astype(BF16), b.astype(BF16), (((1,), (1,)), ((), ())),
                           preferred_element_type=F32)


def _bdot_tn(a, b):
    return lax.dot_general(a.astype(BF16), b.astype(BF16), (((0,), (0,)), ((), ())),
                           preferred_element_type=F32)


def _adaln_body(c_ref, w_ref, b_ref, o_ref):
    s = _silu(c_ref[...])
    o_ref[0] = jnp.dot(s, w_ref[0], precision=HIGHEST, preferred_element_type=F32) + b_ref[0]


def _adaln(cond, mod_w, mod_b):
    n = 6 * D_MODEL
    tn = 1536
    out = pl.pallas_call(
        _adaln_body,
        grid=(DEPTH, n // tn),
        in_specs=[pl.BlockSpec((MOD_ROWS, D_MODEL), lambda i, j: (0, 0)),
                  pl.BlockSpec((1, D_MODEL, tn), lambda i, j: (i, 0, j)),
                  pl.BlockSpec((1, 1, tn), lambda i, j: (i, 0, j))],
        out_specs=pl.BlockSpec((1, MOD_ROWS, tn), lambda i, j: (i, 0, j)),
        out_shape=jax.ShapeDtypeStruct((DEPTH, MOD_ROWS, n), F32),
        compiler_params=_cparams("arbitrary", "arbitrary"),
        name="adaln",
    )(cond, mod_w, mod_b.reshape(DEPTH, 1, n))
    return out.reshape(DEPTH, MOD_ROWS, 6, D_MODEL)


def _mod_spec(mrow0):
    return pl.BlockSpec((1, 6, D_MODEL), lambda b, t: (b + mrow0, 0, 0))


def _inproj_body(x_ref, mod_ref, g_ref, w_ref, *o_refs, splits):
    h = _rmsnorm(x_ref[0], g_ref[...]) * (1.0 + mod_ref[0, 1:2, :]) + mod_ref[0, 0:1, :]
    acc = jnp.dot(h.astype(BF16), w_ref[...], preferred_element_type=F32)
    off = 0
    for o_ref, n in zip(o_refs, splits):
        o_ref[0] = acc[:, off:off + n]
        off += n


def _inproj(x, mod, mrow0, g, w, splits, tm):
    b, l, _ = x.shape
    n = w.shape[1]
    return pl.pallas_call(
        functools.partial(_inproj_body, splits=splits),
        grid=(b, l // tm),
        in_specs=[pl.BlockSpec((1, tm, D_MODEL), lambda i, t: (i, t, 0)),
                  _mod_spec(mrow0),
                  _const_spec((1, D_MODEL)),
                  _const_spec((D_MODEL, n))],
        out_specs=[pl.BlockSpec((1, tm, s), lambda i, t: (i, t, 0)) for s in splits],
        out_shape=[jax.ShapeDtypeStruct((b, l, s), F32) for s in splits],
        compiler_params=_cparams("arbitrary", "arbitrary"),
        name="inproj",
    )(x, mod, g.reshape(1, D_MODEL), w)


def _outproj_body(x_ref, mod_ref, ya_ref, yb_ref, wa_ref, wb_ref, *rest, glu):
    ya = ya_ref[0]
    if glu:
        gw_ref, gb_ref, o_ref = rest
        ya = ya * _sigmoid(_bdot(ya, gw_ref[...]) + gb_ref[...])
    else:
        (o_ref,) = rest
    o = _bdot(ya, wa_ref[...]) + _bdot(yb_ref[0], wb_ref[...])
    o_ref[0] = x_ref[0] + mod_ref[0, 2:3, :] * o


def _outproj(x, mod, mrow0, ya, yb, wa, wb, glu_w, glu_b, tm):
    b, l, _ = x.shape
    wa_n, wb_n = ya.shape[-1], yb.shape[-1]
    glu = glu_w is not None
    in_specs = [pl.BlockSpec((1, tm, D_MODEL), lambda i, t: (i, t, 0)),
                _mod_spec(mrow0),
                pl.BlockSpec((1, tm, wa_n), lambda i, t: (i, t, 0)),
                pl.BlockSpec((1, tm, wb_n), lambda i, t: (i, t, 0)),
                _const_spec((wa_n, D_MODEL)),
                _const_spec((wb_n, D_MODEL))]
    args = [x, mod, ya, yb, wa, wb]
    if glu:
        in_specs += [_const_spec((wa_n, wa_n)), _const_spec((1, wa_n))]
        args += [glu_w, glu_b.reshape(1, wa_n)]
    return pl.pallas_call(
        functools.partial(_outproj_body, glu=glu),
        grid=(b, l // tm),
        in_specs=in_specs,
        out_specs=pl.BlockSpec((1, tm, D_MODEL), lambda i, t: (i, t, 0)),
        out_shape=jax.ShapeDtypeStruct((b, l, D_MODEL), F32),
        compiler_params=_cparams("arbitrary", "arbitrary"),
        name="outproj",
    )(*args)


def _ffn_body(x_ref, mod_ref, g_ref, w1_ref, w3_ref, w2_ref, *rest, final):
    x = x_ref[0]
    h = (_rmsnorm(x, g_ref[...]) * (1.0 + mod_ref[0, 4:5, :]) + mod_ref[0, 3:4, :]).astype(BF16)
    acc = jnp.zeros(x.shape, F32)
    for c in range(FFN_HIDDEN // FFN_CHUNK):
        cols = slice(c * FFN_CHUNK, (c + 1) * FFN_CHUNK)
        a = jnp.dot(h, w1_ref[:, cols], preferred_element_type=F32)
        u = _silu(a) * jnp.dot(h, w3_ref[:, cols], preferred_element_type=F32)
        acc = acc + jnp.dot(u.astype(BF16), w2_ref[cols, :], preferred_element_type=F32)
    y = x + mod_ref[0, 5:6, :] * acc
    if final:
        fg_ref, o_ref = rest
        y = _rmsnorm(y, fg_ref[...])
    else:
        (o_ref,) = rest
    o_ref[0] = y


def _ffn(x, mod, mrow0, g, w1, w3, w2, final_g, tm):
    b, l, _ = x.shape
    final = final_g is not None
    in_specs = [pl.BlockSpec((1, tm, D_MODEL), lambda i, t: (i, t, 0)),
                _mod_spec(mrow0),
                _const_spec((1, D_MODEL)),
                _const_spec((D_MODEL, FFN_HIDDEN)),
                _const_spec((D_MODEL, FFN_HIDDEN)),
                _const_spec((FFN_HIDDEN, D_MODEL))]
    args = [x, mod, g.reshape(1, D_MODEL), w1, w3, w2]
    if final:
        in_specs.append(_const_spec((1, D_MODEL)))
        args.append(final_g.reshape(1, D_MODEL))
    return pl.pallas_call(
        functools.partial(_ffn_body, final=final),
        grid=(b, l // tm),
        in_specs=in_specs,
        out_specs=pl.BlockSpec((1, tm, D_MODEL), lambda i, t: (i, t, 0)),
        out_shape=jax.ShapeDtypeStruct((b, l, D_MODEL), F32),
        compiler_params=_cparams("arbitrary", "arbitrary"),
        name="ffn",
    )(*args)


def _ssd_body(z_ref, xbc_ref, dt_ref, cw_ref, cb_ref, hp_ref, dsk_ref, ng_ref, *rest,
              seq, has_s0, want_final):
    rest = list(rest)
    s0_ref = rest.pop(0) if has_s0 else None
    y_ref = rest.pop(0)
    fin_ref = rest.pop(0) if want_final else None
    st_s, cd_s, scur, ych, xc_s, cum_s, tr_s = rest
    q = SSD_CHUNK
    nc = seq // q
    gw = SSD_HPG * SSD_HEAD_DIM
    expand = (lax.broadcasted_iota(jnp.int32, (LANES, SSD_WIDTH), 1) // SSD_HEAD_DIM
              == lax.broadcasted_iota(jnp.int32, (LANES, SSD_WIDTH), 0)).astype(BF16)

    def per_head(v):
        hi = v.astype(BF16)
        lo = (v - hi.astype(F32)).astype(BF16)
        return (jnp.dot(hi, expand, preferred_element_type=F32)
                + jnp.dot(lo, expand, preferred_element_type=F32))
    row = lax.broadcasted_iota(jnp.int32, (q, q), 0)
    col = lax.broadcasted_iota(jnp.int32, (q, q), 1)
    lower = row >= col
    upper = col >= row
    tri_l = lower.astype(F32)
    tri_u = upper.astype(F32)
    a_f = -jnp.exp(hp_ref[0:1, :])
    a_b = -jnp.exp(hp_ref[1:2, :])
    lane = lax.broadcasted_iota(jnp.int32, (1, LANES), 1)
    a_f = jnp.where(lane < SSD_HEADS, a_f, 0.0)
    a_b = jnp.where(lane < SSD_HEADS, a_b, 0.0)
    bias_f = hp_ref[2:3, :]
    bias_b = hp_ref[3:4, :]

    def chunk_pre(c):
        r0 = pl.multiple_of(c * q, q)
        lo = pl.multiple_of(jnp.maximum(r0 - CONV_HALO, 0), CONV_HALO)
        hi = pl.multiple_of(jnp.minimum(r0 + q, seq - CONV_HALO), CONV_HALO)
        prev = jnp.where(c > 0, xbc_ref[0, pl.ds(lo, CONV_HALO), :], 0.0)
        nxt = jnp.where(c < nc - 1, xbc_ref[0, pl.ds(hi, CONV_HALO), :], 0.0)
        win = jnp.concatenate([prev, xbc_ref[0, pl.ds(r0, q), :], nxt], axis=0)
        acc = cb_ref[...] + cw_ref[0:1, :] * win[CONV_HALO - 2:CONV_HALO - 2 + q]
        for k in range(1, CONV_W):
            o = CONV_HALO - CONV_W // 2 + k
            acc = acc + cw_ref[k:k + 1, :] * win[o:o + q]
        xc = _silu(acc)
        dtr = dt_ref[0, pl.ds(r0, q), :]
        dt_f = _softplus(dtr + bias_f)
        dt_b = _softplus(dtr + bias_b)
        cum_f = jnp.dot(tri_l, dt_f * a_f, precision=HIGHEST, preferred_element_type=F32)
        cum_b = jnp.dot(tri_u, dt_b * a_b, precision=HIGHEST, preferred_element_type=F32)
        return r0, xc, dt_f, dt_b, cum_f, cum_b

    def pass_a(c, carry):
        r0, xc, dt_f, dt_b, cum_f, cum_b = chunk_pre(c)
        xc_s[pl.ds(r0, q), :] = xc
        cum_s[0, pl.ds(r0, q), :] = cum_f
        cum_s[1, pl.ds(r0, q), :] = cum_b
        for i, v in enumerate((cum_f, cum_b, dt_f, dt_b)):
            tr_s[c, i] = v.T[0:SSD_HEADS, :]
        end_f = cum_f[q - 1:q, :]
        end_b = cum_b[0:1, :]
        xw_f = xc[:, :SSD_WIDTH] * per_head(jnp.exp(end_f - cum_f) * dt_f)
        xw_b = xc[:, :SSD_WIDTH] * per_head(jnp.exp(end_b - cum_b) * dt_b)
        cd_s[0, c] = per_head(jnp.broadcast_to(jnp.exp(end_f), (8, LANES)))
        cd_s[1, c] = per_head(jnp.broadcast_to(jnp.exp(end_b), (8, LANES)))
        for g in range(SSD_GROUPS):
            bg = xc[:, SSD_WIDTH + g * SSD_STATE:SSD_WIDTH + (g + 1) * SSD_STATE]
            st_s[0, c, g] = _bdot_tn(bg, xw_f[:, g * gw:(g + 1) * gw])
            st_s[1, c, g] = _bdot_tn(bg, xw_b[:, g * gw:(g + 1) * gw])
        return carry

    lax.fori_loop(0, nc, pass_a, 0)

    if has_s0:
        scur[...] = s0_ref[0]
    else:
        scur[...] = jnp.zeros(scur.shape, F32)

    def pass_b(c, carry):
        cr = nc - 1 - c
        for g in range(SSD_GROUPS):
            s_in = scur[0, g]
            scur[0, g] = s_in * cd_s[0, c, 0:1, g * gw:(g + 1) * gw] + st_s[0, c, g]
            st_s[0, c, g] = s_in
            s_in = scur[1, g]
            scur[1, g] = s_in * cd_s[1, cr, 0:1, g * gw:(g + 1) * gw] + st_s[1, cr, g]
            st_s[1, cr, g] = s_in
        return carry

    lax.fori_loop(0, nc, pass_b, 0)
    if want_final:
        fin_ref[0] = scur[...]

    def pass_c(c, carry):
        r0 = pl.multiple_of(c * q, q)
        xc = xc_s[pl.ds(r0, q), :]
        cum_f = cum_s[0, pl.ds(r0, q), :]
        cum_b = cum_s[1, pl.ds(r0, q), :]
        cum_ft, cum_bt, dt_ft, dt_bt = tr_s[c, 0], tr_s[c, 1], tr_s[c, 2], tr_s[c, 3]
        ex_f = per_head(jnp.exp(cum_f))
        ex_b = per_head(jnp.exp(cum_b))
        y_off = []
        cbs = []
        for g in range(SSD_GROUPS):
            gs = slice(g * gw, (g + 1) * gw)
            bg = xc[:, SSD_WIDTH + g * SSD_STATE:SSD_WIDTH + (g + 1) * SSD_STATE]
            cg = xc[:, SSD_WIDTH + SSD_GN + g * SSD_STATE:SSD_WIDTH + SSD_GN + (g + 1) * SSD_STATE]
            cbs.append(_bdot_nt(cg, bg))
            y_off.append(ex_f[:, gs] * _bdot(cg, st_s[0, c, g]) + ex_b[:, gs] * _bdot(cg, st_s[1, c, g]))
        for h in range(SSD_HEADS):
            xh = xc[:, h * SSD_HEAD_DIM:(h + 1) * SSD_HEAD_DIM]
            seg_f = cum_f[:, h:h + 1] - cum_ft[h:h + 1, :]
            seg_b = cum_b[:, h:h + 1] - cum_bt[h:h + 1, :]
            m_f = jnp.exp(jnp.where(lower, seg_f, NEG_BIG)) * dt_ft[h:h + 1, :]
            m_b = jnp.exp(jnp.where(upper, seg_b, NEG_BIG)) * dt_bt[h:h + 1, :]
            ych[:, h * SSD_HEAD_DIM:(h + 1) * SSD_HEAD_DIM] = _bdot(cbs[h // SSD_HPG] * (m_f + m_b), xh)
        yf = ych[...] + jnp.concatenate(y_off, axis=1) + dsk_ref[...] * xc[:, :SSD_WIDTH]
        yf = yf * _silu(z_ref[0, pl.ds(r0, q), :])
        y_ref[0, pl.ds(r0, q), :] = _rmsnorm(yf, ng_ref[...])
        return carry

    lax.fori_loop(0, nc, pass_c, 0)


def _ssd_state_to_kernel(s):
    b = s.shape[0]
    s = s.reshape(b, 2, SSD_GROUPS, SSD_HPG, SSD_HEAD_DIM, SSD_STATE)
    return jnp.transpose(s, (0, 1, 2, 5, 3, 4)).reshape(b, 2, SSD_GROUPS, SSD_STATE, SSD_HPG * SSD_HEAD_DIM)


def _ssd_state_from_kernel(s):
    b = s.shape[0]
    s = s.reshape(b, 2, SSD_GROUPS, SSD_STATE, SSD_HPG, SSD_HEAD_DIM)
    return jnp.transpose(s, (0, 1, 2, 4, 5, 3)).reshape(b, 2, SSD_HEADS, SSD_HEAD_DIM, SSD_STATE)


def _ssd(z, xbc, dtp, conv_w, conv_b, a_log, dt_bias, d_skip, norm_g, s0t, want_final):
    b, l, _ = z.shape
    nc = l // SSD_CHUNK
    gw = SSD_HPG * SSD_HEAD_DIM
    has_s0 = s0t is not None
    cw = jnp.zeros((8, SSD_XBC), F32).at[:CONV_W].set(conv_w)
    hp = jnp.zeros((8, LANES), F32)
    hp = hp.at[0:2, :SSD_HEADS].set(a_log).at[2:4, :SSD_HEADS].set(dt_bias)
    dsk = jnp.repeat(d_skip, SSD_HEAD_DIM).reshape(1, SSD_WIDTH)
    seq_spec = lambda n: pl.BlockSpec((1, l, n), lambda i: (i, 0, 0))
    st_spec = pl.BlockSpec((1, 2, SSD_GROUPS, SSD_STATE, gw), lambda i: (i, 0, 0, 0, 0))
    in_specs = [seq_spec(SSD_WIDTH), seq_spec(SSD_XBC), seq_spec(LANES),
                _const_spec((8, SSD_XBC)), _const_spec((1, SSD_XBC)), _const_spec((8, LANES)),
                _const_spec((1, SSD_WIDTH)), _const_spec((1, SSD_WIDTH))]
    args = [z, xbc, dtp, cw, conv_b.reshape(1, SSD_XBC), hp, dsk, norm_g.reshape(1, SSD_WIDTH)]
    if has_s0:
        in_specs.append(st_spec)
        args.append(s0t)
    out_specs = [seq_spec(SSD_WIDTH)]
    out_shape = [jax.ShapeDtypeStruct((b, l, SSD_WIDTH), F32)]
    if want_final:
        out_specs.append(st_spec)
        out_shape.append(jax.ShapeDtypeStruct((b, 2, SSD_GROUPS, SSD_STATE, gw), F32))
    return pl.pallas_call(
        functools.partial(_ssd_body, seq=l, has_s0=has_s0, want_final=want_final),
        grid=(b,),
        in_specs=in_specs,
        out_specs=out_specs,
        out_shape=out_shape,
        scratch_shapes=[pltpu.VMEM((2, nc, SSD_GROUPS, SSD_STATE, gw), F32),
                        pltpu.VMEM((2, nc, 8, SSD_WIDTH), F32),
                        pltpu.VMEM((2, SSD_GROUPS, SSD_STATE, gw), F32),
                        pltpu.VMEM((SSD_CHUNK, SSD_WIDTH), F32),
                        pltpu.VMEM((l, SSD_XBC), F32),
                        pltpu.VMEM((2, l, LANES), F32),
                        pltpu.VMEM((nc, 4, SSD_HEADS, LANES), F32)],
        compiler_params=_cparams("arbitrary"),
        name="ssd",
    )(*args)


def _softmax_pv(parts):
    m = parts[0][0].max(axis=-1, keepdims=True)
    for s, _ in parts[1:]:
        m = jnp.maximum(m, s.max(axis=-1, keepdims=True))
    den = 0.0
    out = 0.0
    for s, v in parts:
        p = jnp.exp(s - m)
        den = den + p.sum(axis=-1, keepdims=True)
        out = out + _bdot(p, v)
    return out / den


def _ctx_attn_body(q_ref, k_ref, v_ref, o_ref):
    for h in range(NA_HEADS):
        hs = slice(h * NA_HEAD_DIM, (h + 1) * NA_HEAD_DIM)
        s = _bdot_nt(q_ref[0, :, hs], k_ref[0, :, hs]) * NA_SCALE
        o_ref[0, :, hs] = _softmax_pv([(s, v_ref[0, :, hs])])


def _ctx_attn(q, k, v):
    b, l, _ = q.shape
    spec = pl.BlockSpec((1, l, NA_WIDTH), lambda i: (i, 0, 0))
    return pl.pallas_call(
        _ctx_attn_body,
        grid=(b,),
        in_specs=[spec, spec, spec],
        out_specs=spec,
        out_shape=jax.ShapeDtypeStruct((b, l, NA_WIDTH), F32),
        compiler_params=_cparams("arbitrary"),
        name="ctx_attn",
    )(q, k, v)


def _na_bias_tables(rpb, rows):
    kr = min(NA_ROWS, rows)
    qi = np.arange(NA_QROWS)[:, None]
    wi = np.arange(NA_WIN)[None, :]
    qc = np.arange(GRID_W)[:, None]
    kc = np.arange(GRID_W)[None, :]
    cs = np.clip(qc - NA_COLS // 2, 0, GRID_W - NA_COLS)
    col_ok = (kc >= cs) & (kc < cs + NA_COLS)
    dc = np.clip(kc - qc + NA_COLS - 1, 0, 2 * NA_COLS - 2)
    col_sel = (dc[None] == np.arange(2 * NA_COLS - 1)[:, None, None]).astype(np.float32)
    row_sel, row_ok = [], []
    for r0 in (0, NA_QROWS, rows - NA_QROWS):
        ws = min(max(r0 - kr // 2, 0), rows - NA_WIN)
        r = r0 + qi
        krow = ws + wi
        rs = np.clip(r - kr // 2, 0, rows - kr)
        row_ok.append((krow >= rs) & (krow < rs + kr))
        dr = np.clip(krow - r + NA_ROWS - 1, 0, 2 * NA_ROWS - 2)
        row_sel.append((dr[..., None] == np.arange(2 * NA_ROWS - 1)).astype(np.float32))
    ok = np.stack(row_ok)[:, None, :, None, :, None] & col_ok[None, None, None, :, None, :]
    by_col = jnp.einsum('hab,bqk->haqk', rpb.astype(F32), jnp.asarray(col_sel), precision=HIGHEST)
    bias = jnp.einsum('viwa,haqk->vhiqwk', jnp.asarray(np.stack(row_sel)), by_col, precision=HIGHEST)
    return jnp.where(jnp.asarray(ok), bias, NEG_BIG).reshape(
        3, NA_HEADS, NA_QROWS * GRID_W, NA_WIN * GRID_W)


def _na_body(q_ref, k_ref, v_ref, kc_ref, vc_ref, bias_ref, o_ref, *, rows):
    rb = pl.program_id(1)
    ws = jnp.clip(rb * NA_QROWS - NA_ROWS // 2, 0, rows - NA_WIN)
    k0 = pl.multiple_of(ws * GRID_W, NA_QROWS * GRID_W)
    nk = NA_WIN * GRID_W
    for h in range(NA_HEADS):
        hs = slice(h * NA_HEAD_DIM, (h + 1) * NA_HEAD_DIM)
        qh = q_ref[0, :, hs]
        s_loc = _bdot_nt(qh, k_ref[0, pl.ds(k0, nk), hs]) * NA_SCALE + bias_ref[0, h]
        s_ctx = _bdot_nt(qh, kc_ref[0, :, hs]) * NA_SCALE
        o_ref[0, :, hs] = _softmax_pv([(s_loc, v_ref[0, pl.ds(k0, nk), hs]), (s_ctx, vc_ref[0, :, hs])])


def _latent_na(q, k, v, k_ctx, v_ctx, rpb):
    b, l, _ = q.shape
    lc = k_ctx.shape[1]
    rows = l // GRID_W
    assert rows >= NA_WIN and rows % NA_QROWS == 0 and NA_QROWS * 2 <= NA_ROWS
    nrb = rows // NA_QROWS
    tq = NA_QROWS * GRID_W
    bias = _na_bias_tables(rpb, rows)
    variant = lambda i, r: (jnp.where(r == 0, 0, jnp.where(r == nrb - 1, 2, 1)), 0, 0, 0)
    return pl.pallas_call(
        functools.partial(_na_body, rows=rows),
        grid=(b, nrb),
        in_specs=[pl.BlockSpec((1, tq, NA_WIDTH), lambda i, r: (i, r, 0)),
                  pl.BlockSpec((1, l, NA_WIDTH), lambda i, r: (i, 0, 0)),
                  pl.BlockSpec((1, l, NA_WIDTH), lambda i, r: (i, 0, 0)),
                  pl.BlockSpec((1, lc, NA_WIDTH), lambda i, r: (i, 0, 0)),
                  pl.BlockSpec((1, lc, NA_WIDTH), lambda i, r: (i, 0, 0)),
                  pl.BlockSpec((1, NA_HEADS, tq, NA_WIN * GRID_W), variant)],
        out_specs=pl.BlockSpec((1, tq, NA_WIDTH), lambda i, r: (i, r, 0)),
        out_shape=jax.ShapeDtypeStruct((b, l, NA_WIDTH), F32),
        compiler_params=_cparams("arbitrary", "arbitrary"),
        name="latent_na",
    )(q, k, v, k_ctx, v_ctx, bias)


def _s5_prep_body(lre_ref, lim_ref, lst_ref, btre_ref, btim_ref, cre_ref, cim_ref,
                  wt_ref, kt_ref, et_ref, pw_ref):
    lam_re, lam_im = lre_ref[0], lim_ref[0]
    step = jnp.exp(lst_ref[0])
    mag = jnp.exp(lam_re * step)
    a_re = mag * jnp.cos(lam_im * step)
    a_im = mag * jnp.sin(lam_im * step)
    den = lam_re * lam_re + lam_im * lam_im
    k_re = ((a_re - 1.0) * lam_re + a_im * lam_im) / den
    k_im = (a_im * lam_re - (a_re - 1.0) * lam_im) / den
    bt_re, bt_im = btre_ref[0], btim_ref[0]
    bb_re = k_re * bt_re - k_im * bt_im
    bb_im = k_re * bt_im + k_im * bt_re
    c_re, c_im = cre_ref[0], cim_ref[0]
    p_re = jnp.ones_like(a_re)
    p_im = jnp.zeros_like(a_re)
    for t in range(S5_Q):
        rows = slice(t * S5_GROUP_CH, (t + 1) * S5_GROUP_CH)
        wt_ref[0, rows, 0:S5_STATE] = p_re * bb_re - p_im * bb_im
        wt_ref[0, rows, S5_STATE:2 * S5_STATE] = p_re * bb_im + p_im * bb_re
        p_re, p_im = p_re * a_re - p_im * a_im, p_re * a_im + p_im * a_re
        et_ref[0, rows, 0:S5_STATE] = c_re * p_re - c_im * p_im
        et_ref[0, rows, S5_STATE:2 * S5_STATE] = -(c_re * p_im + c_im * p_re)
    cc = jnp.concatenate([c_re, -c_im], axis=1)
    kt_ref[0] = lax.dot_general(wt_ref[0], cc, (((1,), (1,)), ((), ())), precision=HIGHEST,
                                preferred_element_type=F32)
    pw_ref[0] = jnp.zeros(pw_ref.shape[1:], F32)
    for k in range(S5_LOG_MAX):
        pw_ref[0, 2 * k:2 * k + 1, :] = jnp.concatenate([p_re, p_re], axis=1)
        pw_ref[0, 2 * k + 1:2 * k + 2, :] = jnp.concatenate([-p_im, p_im], axis=1)
        p_re, p_im = p_re * p_re - p_im * p_im, 2.0 * p_re * p_im


def _s5_tables(lam_re, lam_im, log_step, b_re, b_im, c_re, c_im):
    g, p, ch, q = S5_GROUPS, S5_STATE, S5_GROUP_CH, S5_Q
    n = 2 * g
    row = lambda a: a.astype(F32).reshape(n, 1, p)
    lst = jnp.broadcast_to(log_step.astype(F32)[:, :, None], (2, g, p)).reshape(n, 1, p)
    bt_re = jnp.swapaxes(b_re.astype(F32), 1, 2)
    bt_im = jnp.swapaxes(b_im.astype(F32), 1, 2)
    dspec = pl.BlockSpec((1, 1, p), lambda i: (i, 0, 0))
    gspec = pl.BlockSpec((1, ch, p), lambda i: (i % g, 0, 0))
    ospec = lambda a, bb: pl.BlockSpec((1, a, bb), lambda i: (i, 0, 0))
    wt, kt, et, pw = pl.pallas_call(
        _s5_prep_body,
        grid=(n,),
        in_specs=[dspec, dspec, dspec, gspec, gspec, gspec, gspec],
        out_specs=[ospec(S5_CW, 2 * p), ospec(S5_CW, ch), ospec(S5_CW, 2 * p), ospec(16, 2 * p)],
        out_shape=[jax.ShapeDtypeStruct((n, S5_CW, 2 * p), F32),
                   jax.ShapeDtypeStruct((n, S5_CW, ch), F32),
                   jax.ShapeDtypeStruct((n, S5_CW, 2 * p), F32),
                   jax.ShapeDtypeStruct((n, 16, 2 * p), F32)],
        compiler_params=_cparams("arbitrary"),
        name="s5_prep",
    )(row(lam_re), row(lam_im), lst, bt_re, bt_im, c_re.astype(F32), c_im.astype(F32))
    kt = kt.reshape(2, g, q, ch, ch)
    ti = jnp.arange(q)[:, None]
    to = jnp.arange(q)[None, :]
    t_f = jnp.where((to >= ti)[None, :, :, None, None], kt[0][:, jnp.clip(to - ti, 0, q - 1)], 0.0)
    t_b = jnp.where((ti >= to)[None, :, :, None, None], kt[1][:, jnp.clip(ti - to, 0, q - 1)], 0.0)
    t_all = jnp.transpose(t_f + t_b, (0, 1, 3, 2, 4)).reshape(g, S5_CW, S5_CW)
    wt = wt.reshape(2, g, q, ch, 2 * p)
    g_fb = jnp.concatenate([jnp.flip(wt[0], axis=1), wt[1]], axis=-1).reshape(g, S5_CW, 4 * p)
    et = et.reshape(2, g, q, ch, 2 * p)
    e_f = jnp.swapaxes(et[0].reshape(g, S5_CW, 2 * p), 1, 2)
    e_b = jnp.swapaxes(jnp.flip(et[1], axis=1).reshape(g, S5_CW, 2 * p), 1, 2)
    e_fb = jnp.concatenate([e_f, e_b], axis=1)
    return t_all.astype(BF16), g_fb.astype(BF16), e_fb.astype(BF16), pw.reshape(2, g, 16, 2 * p)


def _cmul(a_full, a_sgn, s):
    return a_full * s + a_sgn * pltpu.roll(s, S5_STATE, axis=1)


def _s5_body(u_ref, t_ref, g_ref, e_ref, pw_ref, dsk_ref, *rest, nb, nc, has_s0, want_final):
    rest = list(rest)
    s0_ref = rest.pop(0) if has_s0 else None
    o_ref = rest.pop(0)
    fin_ref = rest.pop(0) if want_final else None
    ys, xs_f, xs_b = rest
    m = nb * nc
    w = 2 * S5_STATE
    lane_blk = lax.broadcasted_iota(jnp.int32, (m, LANES), 1) // S5_GROUP_CH
    cidx = lax.broadcasted_iota(jnp.int32, (m, w), 0) & (nc - 1)
    first = cidx == 0
    last = cidx == nc - 1

    def tokens(t):
        return u_ref[:, pl.ds(t, nc, stride=S5_Q), :].reshape(m, LANES)

    def regroup(load, src_blk, dst_blk):
        acc = None
        for k in range(S5_LG):
            sh = (S5_GROUP_CH * (dst_blk(k) - src_blk(k))) % LANES
            r = load(k)
            if sh:
                r = pltpu.roll(r, sh, axis=1)
            acc = r if acc is None else jnp.where(lane_blk == dst_blk(k), r, acc)
        return acc

    for gg in range(S5_LG):
        ub = jnp.concatenate(
            [regroup(lambda tt, tb=tb: tokens(tb * S5_LG + tt), lambda tt: gg, lambda tt: tt)
             for tb in range(S5_Q // S5_LG)], axis=1).astype(BF16)
        z = jnp.dot(ub, g_ref[gg], preferred_element_type=F32)
        x_f, x_b = z[:, :w], z[:, w:]
        pw = lambda d, r: pw_ref[d, gg, r:r + 1, :]
        if has_s0:
            rep = lambda a: jnp.broadcast_to(a[:, None, :], (nb, nc, w)).reshape(m, w)
            s0_f, s0_b = rep(s0_ref[gg, :, :w]), rep(s0_ref[gg, :, w:])
            x_f = x_f + jnp.where(first, _cmul(pw(0, 0), pw(0, 1), s0_f), 0.0)
            x_b = x_b + jnp.where(last, _cmul(pw(1, 0), pw(1, 1), s0_b), 0.0)
        for k in range(nc.bit_length() - 1):
            sh = 1 << k
            prev = pltpu.roll(x_f, sh, axis=0)
            x_f = x_f + jnp.where(cidx >= sh, _cmul(pw(0, 2 * k), pw(0, 2 * k + 1), prev), 0.0)
            nxt = pltpu.roll(x_b, m - sh, axis=0)
            x_b = x_b + jnp.where(cidx < nc - sh, _cmul(pw(1, 2 * k), pw(1, 2 * k + 1), nxt), 0.0)
        in_f = pltpu.roll(x_f, 1, axis=0)
        in_b = pltpu.roll(x_b, m - 1, axis=0)
        if has_s0:
            in_f = jnp.where(first, s0_f, in_f)
            in_b = jnp.where(last, s0_b, in_b)
        else:
            in_f = jnp.where(first, 0.0, in_f)
            in_b = jnp.where(last, 0.0, in_b)
        s_in = jnp.concatenate([in_f, in_b], axis=1).astype(BF16)
        ys[gg] = (jnp.dot(ub, t_ref[gg], preferred_element_type=F32)
                  + jnp.dot(s_in, e_ref[gg], preferred_element_type=F32))
        if want_final:
            xs_f[...] = x_f
            xs_b[...] = x_b
            fin_ref[gg, :, :w] = xs_f[pl.ds(nc - 1, nb, stride=nc), :]
            fin_ref[gg, :, w:] = xs_b[pl.ds(0, nb, stride=nc), :]

    for t in range(S5_Q):
        tb, tt = divmod(t, S5_LG)
        y = regroup(lambda k: ys[k, :, tb * LANES:(tb + 1) * LANES], lambda k: tt, lambda k: k)
        y = y + tokens(t) * dsk_ref[0]
        o_ref[:, pl.ds(t, nc, stride=S5_Q), :] = _gelu_tanh(y).reshape(nb, nc, LANES)


def _s5(u, tables, d_skip, s0, want_final):
    t_all, g_fb, e_fb, pows = tables
    b, l, _ = u.shape
    nc = l // S5_Q
    m = b * nc
    has_s0 = s0 is not None
    seq_spec = pl.BlockSpec((b, l, LANES), lambda i: (0, 0, i))
    tab_spec = pl.BlockSpec((S5_LG, S5_CW, S5_CW), lambda i: (i, 0, 0))
    st_spec = pl.BlockSpec((S5_LG, b, S5_CW), lambda i: (i, 0, 0))
    in_specs = [seq_spec, tab_spec, tab_spec, tab_spec,
                pl.BlockSpec((2, S5_LG, 16, 2 * S5_STATE), lambda i: (0, i, 0, 0)),
                pl.BlockSpec((1, 1, LANES), lambda i: (i, 0, 0))]
    args = [u, t_all, g_fb, e_fb, pows, d_skip.astype(F32).reshape(S5_WIDTH // LANES, 1, LANES)]
    if has_s0:
        in_specs.append(st_spec)
        args.append(s0)
    out_specs = [seq_spec]
    out_shape = [jax.ShapeDtypeStruct((b, l, S5_WIDTH), F32)]
    if want_final:
        out_specs.append(st_spec)
        out_shape.append(jax.ShapeDtypeStruct((S5_GROUPS, b, S5_CW), F32))
    res = pl.pallas_call(
        functools.partial(_s5_body, nb=b, nc=nc, has_s0=has_s0, want_final=want_final),
        grid=(S5_WIDTH // LANES,),
        in_specs=in_specs,
        out_specs=out_specs,
        out_shape=out_shape,
        scratch_shapes=[pltpu.VMEM((S5_LG, m, S5_CW), F32),
                        pltpu.VMEM((m, 2 * S5_STATE), F32), pltpu.VMEM((m, 2 * S5_STATE), F32)],
        compiler_params=_cparams("arbitrary"),
        name="s5",
    )(*args)
    return res[0], (res[1] if want_final else None)


def _dft_cos_sin(n):
    jk = (jnp.arange(n, dtype=jnp.int32)[:, None] * jnp.arange(n, dtype=jnp.int32)[None, :]) % n
    ang = jk.astype(F32) * (2.0 * math.pi / n)
    return jnp.cos(ang), jnp.sin(ang)


def _fnet_body(u_ref, cs_ref, dl_ref, o_ref, xs, *, seq):
    @pl.when(pl.program_id(1) == 0)
    def _():
        xcs = jnp.dot(u_ref[0].astype(BF16), cs_ref[...], preferred_element_type=F32)
        xs[0:seq, :] = xcs[:, :FNET_WIDTH].astype(BF16)
        xs[seq:2 * seq, :] = xcs[:, FNET_WIDTH:].astype(BF16)

    scale = 1.0 / math.sqrt(seq * FNET_GROUP_CH)
    o_ref[0] = jnp.dot(dl_ref[...], xs[...], preferred_element_type=F32) * scale


def _fnet(u):
    b, l, _ = u.shape
    tr = min(l, 512)
    cc, sc = _dft_cos_sin(FNET_GROUP_CH)
    eye = jnp.eye(FNET_GROUPS, dtype=F32)
    cs = jnp.concatenate([jnp.kron(eye, cc), jnp.kron(eye, sc)], axis=1).astype(BF16)
    cl, sl = _dft_cos_sin(l)
    dl = jnp.concatenate([cl, -sl], axis=1).astype(BF16)
    return pl.pallas_call(
        functools.partial(_fnet_body, seq=l),
        grid=(b, l // tr),
        in_specs=[pl.BlockSpec((1, l, FNET_WIDTH), lambda i, t: (i, 0, 0)),
                  _const_spec((FNET_WIDTH, 2 * FNET_WIDTH)),
                  pl.BlockSpec((tr, 2 * l), lambda i, t: (t, 0))],
        out_specs=pl.BlockSpec((1, tr, FNET_WIDTH), lambda i, t: (i, t, 0)),
        out_shape=jax.ShapeDtypeStruct((b, l, FNET_WIDTH), F32),
        scratch_shapes=[pltpu.VMEM((2 * l, FNET_WIDTH), BF16)],
        compiler_params=_cparams("arbitrary", "arbitrary"),
        name="fnet",
    )(u, cs, dl)


def _even_w_in(w):
    pad = jnp.zeros((D_MODEL, LANES - SSD_HEADS), w.dtype)
    return jnp.concatenate([w[:, :O_DT], w[:, O_Q:], w[:, O_DT:O_Q], pad], axis=1).astype(BF16)


EVEN_SPLITS = (SSD_WIDTH, SSD_XBC, NA_WIDTH, NA_WIDTH, NA_WIDTH, LANES)


def kernel(x_prompt, x_sample, cache_na_k, cache_na_v, state_ssd, state_s5, c, c_ctx, mod_w, mod_b, norm1_g, norm2_g, ffn_w1, ffn_w3, ffn_w2, final_g, ev_w_in, ev_conv_w, ev_conv_b, ev_a_log, ev_dt_bias, ev_d_skip, ev_ssd_norm_g, ev_rpb, ev_w_out, od_w_in, od_lam_re, od_lam_im, od_log_step, od_b_re, od_b_im, od_c_re, od_c_im, od_d_skip, od_glu_w, od_glu_b, od_w_out):
    bc, lc, _ = x_prompt.shape
    bl, ll, _ = x_sample.shape
    cond = jnp.zeros((MOD_ROWS, D_MODEL), F32).at[0].set(c_ctx).at[1:1 + bl].set(c)
    mods = _adaln(cond, mod_w, mod_b)
    xc = x_prompt.reshape(1, bc * lc, D_MODEL)
    xl = x_sample
    tm = 512
    new_k, new_v, new_ssd, new_s5 = [], [], [], []
    for i in range(DEPTH):
        j = i // 2
        mod = mods[i]
        last = i == DEPTH - 1
        if i % 2 == 0:
            w_in = _even_w_in(ev_w_in[j])
            w_out = ev_w_out[j].astype(BF16)
            ssd_args = (ev_conv_w[j], ev_conv_b[j], ev_a_log[j], ev_dt_bias[j], ev_d_skip[j],
                        ev_ssd_norm_g[j])
            z, xbc, q, k, v, dtp = _inproj(xc, mod, 0, norm1_g[i], w_in, EVEN_SPLITS, tm)
            seqs = lambda a: a.reshape(bc, lc, a.shape[-1])
            y_ssd, fin = _ssd(seqs(z), seqs(xbc), seqs(dtp), *ssd_args, None, True)
            o_na = _ctx_attn(seqs(q), seqs(k), seqs(v))
            flat = lambda a: a.reshape(1, bc * lc, a.shape[-1])
            xc = _outproj(xc, mod, 0, flat(y_ssd), flat(o_na), w_out[:SSD_WIDTH], w_out[SSD_WIDTH:],
                          None, None, tm)
            new_k.append(k.reshape(bc, lc, NA_HEADS, NA_HEAD_DIM))
            new_v.append(v.reshape(bc, lc, NA_HEADS, NA_HEAD_DIM))
            new_ssd.append(_ssd_state_from_kernel(fin))

            z, xbc, q, k, v, dtp = _inproj(xl, mod, 1, norm1_g[i], w_in, EVEN_SPLITS, tm)
            s0t = _ssd_state_to_kernel(state_ssd[:, j].astype(F32))
            (y_ssd,) = _ssd(z, xbc, dtp, *ssd_args, s0t, False)
            pl_ = cache_na_k.shape[2]
            o_na = _latent_na(q, k, v, cache_na_k[:, j].reshape(bl, pl_, NA_WIDTH),
                              cache_na_v[:, j].reshape(bl, pl_, NA_WIDTH), ev_rpb[j])
            xl = _outproj(xl, mod, 1, y_ssd, o_na, w_out[:SSD_WIDTH], w_out[SSD_WIDTH:], None, None, tm)
        else:
            w_in = od_w_in[j].astype(BF16)
            w_out = od_w_out[j].astype(BF16)
            glu_w = od_glu_w[j].astype(BF16)
            tables = _s5_tables(od_lam_re[j], od_lam_im[j], od_log_step[j], od_b_re[j], od_b_im[j],
                                od_c_re[j], od_c_im[j])
            splits = (S5_WIDTH, FNET_WIDTH)
            u_s, u_f = _inproj(xc, mod, 0, norm1_g[i], w_in, splits, tm)
            seqs = lambda a: a.reshape(bc, lc, a.shape[-1])
            g_s, fin = _s5(seqs(u_s), tables, od_d_skip[j], None, True)
            y_f = _fnet(seqs(u_f))
            flat = lambda a: a.reshape(1, bc * lc, a.shape[-1])
            xc = _outproj(xc, mod, 0, flat(g_s), flat(y_f), w_out[:S5_WIDTH], w_out[S5_WIDTH:],
                          glu_w, od_glu_b[j], tm)
            fin = fin.reshape(S5_GROUPS, bc, 2, 2, S5_STATE)
            new_s5.append(jnp.transpose(fin, (1, 2, 0, 4, 3)))

            u_s, u_f = _inproj(xl, mod, 1, norm1_g[i], w_in, splits, tm)
            s0 = jnp.transpose(state_s5[:, j].astype(F32), (2, 0, 1, 4, 3)).reshape(
                S5_GROUPS, bl, 4 * S5_STATE)
            g_s, _ = _s5(u_s, tables, od_d_skip[j], s0, False)
            y_f = _fnet(u_f)
            xl = _outproj(xl, mod, 1, g_s, y_f, w_out[:S5_WIDTH], w_out[S5_WIDTH:], glu_w,
                          od_glu_b[j], tm)
        w1, w3, w2 = ffn_w1[i].astype(BF16), ffn_w3[i].astype(BF16), ffn_w2[i].astype(BF16)
        fg = final_g if last else None
        xc = _ffn(xc, mod, 0, norm2_g[i], w1, w3, w2, fg, tm)
        xl = _ffn(xl, mod, 1, norm2_g[i], w1, w3, w2, fg, tm)
    return (xc.reshape(bc, lc, D_MODEL), xl,
            jnp.stack(new_k, axis=1), jnp.stack(new_v, axis=1),
            jnp.stack(new_ssd, axis=1), jnp.stack(new_s5, axis=1))
```

```python
import functools
import math

import numpy as np
import jax
import jax.numpy as jnp
from jax import lax
from jax.experimental import pallas as pl
from jax.experimental.pallas import tpu as pltpu

F32 = jnp.float32
BF16 = jnp.bfloat16
HIGHEST = lax.Precision.HIGHEST

D_MODEL = 1024
DEPTH = 2
EPS = 1e-6
GRID_W = 64
FFN_HIDDEN = 2816
FFN_CHUNK = 1408

SSD_HEADS = 8
SSD_HEAD_DIM = 64
SSD_WIDTH = 512
SSD_GROUPS = 2
SSD_HPG = SSD_HEADS // SSD_GROUPS
SSD_STATE = 64
SSD_GN = SSD_GROUPS * SSD_STATE
SSD_XBC = SSD_WIDTH + 2 * SSD_GN
SSD_CHUNK = 128
CONV_W = 5
CONV_HALO = 8

NA_HEADS = 8
NA_HEAD_DIM = 64
NA_WIDTH = 512
NA_ROWS = 8
NA_COLS = 16
NA_SCALE = NA_HEAD_DIM ** -0.5
NA_QROWS = 4
NA_WIN = NA_ROWS + NA_QROWS
NEG_BIG = -1e30

O_XBC = SSD_WIDTH
O_DT = O_XBC + SSD_XBC
O_Q = O_DT + SSD_HEADS
O_K = O_Q + NA_WIDTH
O_V = O_K + NA_WIDTH
LANES = 128

S5_GROUP_CH = 16
S5_GROUPS = 32
S5_WIDTH = 512
S5_STATE = 64
S5_Q = 16
S5_CW = S5_Q * S5_GROUP_CH
S5_LOG_MAX = 7
S5_LG = LANES // S5_GROUP_CH
S5_ROWS = 512

FNET_GROUP_CH = 64
FNET_GROUPS = 8
FNET_WIDTH = 512
DFT_MINOR = 64

MOD_ROWS = 16
VMEM_LIMIT = 56 * 1024 * 1024


def _cparams(*sem):
    return pltpu.CompilerParams(dimension_semantics=sem, vmem_limit_bytes=VMEM_LIMIT)


def _const_spec(shape):
    nd = len(shape)
    return pl.BlockSpec(shape, lambda *_: (0,) * nd, pipeline_mode=pl.Buffered(1))


def _sigmoid(x):
    return 1.0 / (1.0 + jnp.exp(-x))


def _silu(x):
    return x * _sigmoid(x)


def _softplus(x):
    return jnp.maximum(x, 0.0) + jnp.log(1.0 + jnp.exp(-jnp.abs(x)))


def _gelu_tanh(x):
    return 0.5 * x * (1.0 + jnp.tanh(math.sqrt(2.0 / math.pi) * (x + 0.044715 * (x * x * x))))


def _rmsnorm(x, g):
    return x * lax.rsqrt(jnp.mean(x * x, axis=-1, keepdims=True) + EPS) * g


def _bdot(a, b):
    return jnp.dot(a.astype(BF16), b.astype(BF16), preferred_element_type=F32)


def _bdot_nt(a, b):
    return lax.dot_general(a.astype(BF16), b.astype(BF16), (((1,), (1,)), ((), ())),
                           preferred_element_type=F32)


def _bdot_tn(a, b):
    return lax.dot_general(a.astype(BF16), b.astype(BF16), (((0,), (0,)), ((), ())),
                           preferred_element_type=F32)


def _adaln_body(c_ref, w_ref, b_ref, o_ref):
    s = _silu(c_ref[...])
    o_ref[0] = jnp.dot(s, w_ref[0], precision=HIGHEST, preferred_element_type=F32) + b_ref[0]


def _adaln(cond, mod_w, mod_b):
    n = 6 * D_MODEL
    tn = 1536
    out = pl.pallas_call(
        _adaln_body,
        grid=(DEPTH, n // tn),
        in_specs=[pl.BlockSpec((MOD_ROWS, D_MODEL), lambda i, j: (0, 0)),
                  pl.BlockSpec((1, D_MODEL, tn), lambda i, j: (i, 0, j)),
                  pl.BlockSpec((1, 1, tn), lambda i, j: (i, 0, j))],
        out_specs=pl.BlockSpec((1, MOD_ROWS, tn), lambda i, j: (i, 0, j)),
        out_shape=jax.ShapeDtypeStruct((DEPTH, MOD_ROWS, n), F32),
        compiler_params=_cparams("arbitrary", "arbitrary"),
        name="adaln",
    )(cond, mod_w, mod_b.reshape(DEPTH, 1, n))
    return out.reshape(DEPTH, MOD_ROWS, 6, D_MODEL)


def _mod_spec(mrow0):
    return pl.BlockSpec((1, 6, D_MODEL), lambda b, t: (b + mrow0, 0, 0))


def _inproj_body(x_ref, mod_ref, g_ref, w_ref, *o_refs, splits):
    h = _rmsnorm(x_ref[0], g_ref[...]) * (1.0 + mod_ref[0, 1:2, :]) + mod_ref[0, 0:1, :]
    acc = jnp.dot(h.astype(BF16), w_ref[...], preferred_element_type=F32)
    off = 0
    for o_ref, n in zip(o_refs, splits):
        o_ref[0] = acc[:, off:off + n]
        off += n


def _inproj(x, mod, mrow0, g, w, splits, tm):
    b, l, _ = x.shape
    n = w.shape[1]
    return pl.pallas_call(
        functools.partial(_inproj_body, splits=splits),
        grid=(b, l // tm),
        in_specs=[pl.BlockSpec((1, tm, D_MODEL), lambda i, t: (i, t, 0)),
                  _mod_spec(mrow0),
                  _const_spec((1, D_MODEL)),
                  _const_spec((D_MODEL, n))],
        out_specs=[pl.BlockSpec((1, tm, s), lambda i, t: (i, t, 0)) for s in splits],
        out_shape=[jax.ShapeDtypeStruct((b, l, s), F32) for s in splits],
        compiler_params=_cparams("arbitrary", "arbitrary"),
        name="inproj",
    )(x, mod, g.reshape(1, D_MODEL), w)


def _ffn_body(x_ref, mod_ref, ya_ref, yb_ref, wa_ref, wb_ref, g_ref, w1_ref, w3_ref, w2_ref, *rest,
              glu, final):
    rest = list(rest)
    ya = ya_ref[0]
    if glu:
        gw_ref, gb_ref = rest.pop(0), rest.pop(0)
        ya = ya * _sigmoid(_bdot(ya, gw_ref[...]) + gb_ref[...])
    mix = _bdot(ya, wa_ref[...]) + _bdot(yb_ref[0], wb_ref[...])
    x = x_ref[0] + mod_ref[0, 2:3, :] * mix
    h = (_rmsnorm(x, g_ref[...]) * (1.0 + mod_ref[0, 4:5, :]) + mod_ref[0, 3:4, :]).astype(BF16)
    acc = jnp.zeros(x.shape, F32)
    for c in range(FFN_HIDDEN // FFN_CHUNK):
        cols = slice(c * FFN_CHUNK, (c + 1) * FFN_CHUNK)
        a = jnp.dot(h, w1_ref[:, cols], preferred_element_type=F32)
        u = _silu(a) * jnp.dot(h, w3_ref[:, cols], preferred_element_type=F32)
        acc = acc + jnp.dot(u.astype(BF16), w2_ref[cols, :], preferred_element_type=F32)
    y = x + mod_ref[0, 5:6, :] * acc
    if final:
        fg_ref, o_ref = rest
        y = _rmsnorm(y, fg_ref[...])
    else:
        (o_ref,) = rest
    o_ref[0] = y


def _ffn(x, mod, mrow0, ya, yb, wa, wb, glu_w, glu_b, g, w1, w3, w2, final_g, tm):
    b, l, _ = x.shape
    wa_n, wb_n = ya.shape[-1], yb.shape[-1]
    glu = glu_w is not None
    final = final_g is not None
    row_spec = lambda n: pl.BlockSpec((1, tm, n), lambda i, t: (i, t, 0))
    in_specs = [row_spec(D_MODEL), _mod_spec(mrow0), row_spec(wa_n), row_spec(wb_n),
                _const_spec((wa_n, D_MODEL)), _const_spec((wb_n, D_MODEL)),
                _const_spec((1, D_MODEL)),
                _const_spec((D_MODEL, FFN_HIDDEN)),
                _const_spec((D_MODEL, FFN_HIDDEN)),
                _const_spec((FFN_HIDDEN, D_MODEL))]
    args = [x, mod, ya, yb, wa, wb, g.reshape(1, D_MODEL), w1, w3, w2]
    if glu:
        in_specs += [_const_spec((wa_n, wa_n)), _const_spec((1, wa_n))]
        args += [glu_w, glu_b.reshape(1, wa_n)]
    if final:
        in_specs.append(_const_spec((1, D_MODEL)))
        args.append(final_g.reshape(1, D_MODEL))
    return pl.pallas_call(
        functools.partial(_ffn_body, glu=glu, final=final),
        grid=(b, l // tm),
        in_specs=in_specs,
        out_specs=pl.BlockSpec((1, tm, D_MODEL), lambda i, t: (i, t, 0)),
        out_shape=jax.ShapeDtypeStruct((b, l, D_MODEL), F32),
        compiler_params=_cparams("arbitrary", "arbitrary"),
        name="ffn",
    )(*args)


def _ssd_body(z_ref, xbc_ref, dt_ref, cw_ref, cb_ref, hp_ref, dsk_ref, ng_ref, *rest,
              seq, has_s0, want_final):
    rest = list(rest)
    s0_ref = rest.pop(0) if has_s0 else None
    y_ref = rest.pop(0)
    fin_ref = rest.pop(0) if want_final else None
    st_s, cd_s, scur, ych, xc_s, cum_s, tr_s = rest
    q = SSD_CHUNK
    nc = seq // q
    gw = SSD_HPG * SSD_HEAD_DIM
    expand = (lax.broadcasted_iota(jnp.int32, (LANES, SSD_WIDTH), 1) // SSD_HEAD_DIM
              == lax.broadcasted_iota(jnp.int32, (LANES, SSD_WIDTH), 0)).astype(BF16)

    def per_head(v):
        hi = v.astype(BF16)
        lo = (v - hi.astype(F32)).astype(BF16)
        return (jnp.dot(hi, expand, preferred_element_type=F32)
                + jnp.dot(lo, expand, preferred_element_type=F32))
    row = lax.broadcasted_iota(jnp.int32, (q, q), 0)
    col = lax.broadcasted_iota(jnp.int32, (q, q), 1)
    lower = row >= col
    upper = col >= row
    tri_l = lower.astype(F32)
    tri_u = upper.astype(F32)
    a_f = -jnp.exp(hp_ref[0:1, :])
    a_b = -jnp.exp(hp_ref[1:2, :])
    lane = lax.broadcasted_iota(jnp.int32, (1, LANES), 1)
    a_f = jnp.where(lane < SSD_HEADS, a_f, 0.0)
    a_b = jnp.where(lane < SSD_HEADS, a_b, 0.0)
    bias_f = hp_ref[2:3, :]
    bias_b = hp_ref[3:4, :]

    def chunk_pre(c):
        r0 = pl.multiple_of(c * q, q)
        lo = pl.multiple_of(jnp.maximum(r0 - CONV_HALO, 0), CONV_HALO)
        hi = pl.multiple_of(jnp.minimum(r0 + q, seq - CONV_HALO), CONV_HALO)
        prev = jnp.where(c > 0, xbc_ref[0, pl.ds(lo, CONV_HALO), :], 0.0)
        nxt = jnp.where(c < nc - 1, xbc_ref[0, pl.ds(hi, CONV_HALO), :], 0.0)
        win = jnp.concatenate([prev, xbc_ref[0, pl.ds(r0, q), :], nxt], axis=0)
        acc = cb_ref[...] + cw_ref[0:1, :] * win[CONV_HALO - 2:CONV_HALO - 2 + q]
        for k in range(1, CONV_W):
            o = CONV_HALO - CONV_W // 2 + k
            acc = acc + cw_ref[k:k + 1, :] * win[o:o + q]
        xc = _silu(acc)
        dtr = dt_ref[0, pl.ds(r0, q), :]
        dt_f = _softplus(dtr + bias_f)
        dt_b = _softplus(dtr + bias_b)
        cum_f = jnp.dot(tri_l, dt_f * a_f, precision=HIGHEST, preferred_element_type=F32)
        cum_b = jnp.dot(tri_u, dt_b * a_b, precision=HIGHEST, preferred_element_type=F32)
        return r0, xc, dt_f, dt_b, cum_f, cum_b

    def pass_a(c, carry):
        r0, xc, dt_f, dt_b, cum_f, cum_b = chunk_pre(c)
        xc_s[pl.ds(r0, q), :] = xc
        cum_s[0, pl.ds(r0, q), :] = cum_f
        cum_s[1, pl.ds(r0, q), :] = cum_b
        for i, v in enumerate((cum_f, cum_b, dt_f, dt_b)):
            tr_s[c, i] = v.T[0:SSD_HEADS, :]
        end_f = cum_f[q - 1:q, :]
        end_b = cum_b[0:1, :]
        xw_f = xc[:, :SSD_WIDTH] * per_head(jnp.exp(end_f - cum_f) * dt_f)
        xw_b = xc[:, :SSD_WIDTH] * per_head(jnp.exp(end_b - cum_b) * dt_b)
        cd_s[0, c] = per_head(jnp.broadcast_to(jnp.exp(end_f), (8, LANES)))
        cd_s[1, c] = per_head(jnp.broadcast_to(jnp.exp(end_b), (8, LANES)))
        for g in range(SSD_GROUPS):
            bg = xc[:, SSD_WIDTH + g * SSD_STATE:SSD_WIDTH + (g + 1) * SSD_STATE]
            st_s[0, c, g] = _bdot_tn(bg, xw_f[:, g * gw:(g + 1) * gw])
            st_s[1, c, g] = _bdot_tn(bg, xw_b[:, g * gw:(g + 1) * gw])
        return carry

    lax.fori_loop(0, nc, pass_a, 0)

    if has_s0:
        scur[...] = s0_ref[0]
    else:
        scur[...] = jnp.zeros(scur.shape, F32)

    def pass_b(c, carry):
        cr = nc - 1 - c
        for g in range(SSD_GROUPS):
            s_in = scur[0, g]
            scur[0, g] = s_in * cd_s[0, c, 0:1, g * gw:(g + 1) * gw] + st_s[0, c, g]
            st_s[0, c, g] = s_in
            s_in = scur[1, g]
            scur[1, g] = s_in * cd_s[1, cr, 0:1, g * gw:(g + 1) * gw] + st_s[1, cr, g]
            st_s[1, cr, g] = s_in
        return carry

    lax.fori_loop(0, nc, pass_b, 0)
    if want_final:
        fin_ref[0] = scur[...]

    def pass_c(c, carry):
        r0 = pl.multiple_of(c * q, q)
        xc = xc_s[pl.ds(r0, q), :]
        cum_f = cum_s[0, pl.ds(r0, q), :]
        cum_b = cum_s[1, pl.ds(r0, q), :]
        cum_ft, cum_bt, dt_ft, dt_bt = tr_s[c, 0], tr_s[c, 1], tr_s[c, 2], tr_s[c, 3]
        ex_f = per_head(jnp.exp(cum_f))
        ex_b = per_head(jnp.exp(cum_b))
        y_off = []
        cbs = []
        for g in range(SSD_GROUPS):
            gs = slice(g * gw, (g + 1) * gw)
            bg = xc[:, SSD_WIDTH + g * SSD_STATE:SSD_WIDTH + (g + 1) * SSD_STATE]
            cg = xc[:, SSD_WIDTH + SSD_GN + g * SSD_STATE:SSD_WIDTH + SSD_GN + (g + 1) * SSD_STATE]
            cbs.append(_bdot_nt(cg, bg))
            y_off.append(ex_f[:, gs] * _bdot(cg, st_s[0, c, g]) + ex_b[:, gs] * _bdot(cg, st_s[1, c, g]))
        for h in range(SSD_HEADS):
            xh = xc[:, h * SSD_HEAD_DIM:(h + 1) * SSD_HEAD_DIM]
            seg_f = cum_f[:, h:h + 1] - cum_ft[h:h + 1, :]
            seg_b = cum_b[:, h:h + 1] - cum_bt[h:h + 1, :]
            m_f = jnp.exp(jnp.where(lower, seg_f, NEG_BIG)) * dt_ft[h:h + 1, :]
            m_b = jnp.exp(jnp.where(upper, seg_b, NEG_BIG)) * dt_bt[h:h + 1, :]
            ych[:, h * SSD_HEAD_DIM:(h + 1) * SSD_HEAD_DIM] = _bdot(cbs[h // SSD_HPG] * (m_f + m_b), xh)
        yf = ych[...] + jnp.concatenate(y_off, axis=1) + dsk_ref[...] * xc[:, :SSD_WIDTH]
        yf = yf * _silu(z_ref[0, pl.ds(r0, q), :])
        y_ref[0, pl.ds(r0, q), :] = _rmsnorm(yf, ng_ref[...])
        return carry

    lax.fori_loop(0, nc, pass_c, 0)


def _ssd_state_to_kernel(s):
    b = s.shape[0]
    s = s.reshape(b, 2, SSD_GROUPS, SSD_HPG, SSD_HEAD_DIM, SSD_STATE)
    return jnp.transpose(s, (0, 1, 2, 5, 3, 4)).reshape(b, 2, SSD_GROUPS, SSD_STATE, SSD_HPG * SSD_HEAD_DIM)


def _ssd_state_from_kernel(s):
    b = s.shape[0]
    s = s.reshape(b, 2, SSD_GROUPS, SSD_STATE, SSD_HPG, SSD_HEAD_DIM)
    return jnp.transpose(s, (0, 1, 2, 4, 5, 3)).reshape(b, 2, SSD_HEADS, SSD_HEAD_DIM, SSD_STATE)


def _ssd(z, xbc, dtp, conv_w, conv_b, a_log, dt_bias, d_skip, norm_g, s0t, want_final):
    b, l, _ = z.shape
    nc = l // SSD_CHUNK
    gw = SSD_HPG * SSD_HEAD_DIM
    has_s0 = s0t is not None
    cw = jnp.zeros((8, SSD_XBC), F32).at[:CONV_W].set(conv_w)
    hp = jnp.zeros((8, LANES), F32)
    hp = hp.at[0:2, :SSD_HEADS].set(a_log).at[2:4, :SSD_HEADS].set(dt_bias)
    dsk = jnp.repeat(d_skip, SSD_HEAD_DIM).reshape(1, SSD_WIDTH)
    seq_spec = lambda n: pl.BlockSpec((1, l, n), lambda i: (i, 0, 0))
    st_spec = pl.BlockSpec((1, 2, SSD_GROUPS, SSD_STATE, gw), lambda i: (i, 0, 0, 0, 0))
    in_specs = [seq_spec(SSD_WIDTH), seq_spec(SSD_XBC), seq_spec(LANES),
                _const_spec((8, SSD_XBC)), _const_spec((1, SSD_XBC)), _const_spec((8, LANES)),
                _const_spec((1, SSD_WIDTH)), _const_spec((1, SSD_WIDTH))]
    args = [z, xbc, dtp, cw, conv_b.reshape(1, SSD_XBC), hp, dsk, norm_g.reshape(1, SSD_WIDTH)]
    if has_s0:
        in_specs.append(st_spec)
        args.append(s0t)
    out_specs = [seq_spec(SSD_WIDTH)]
    out_shape = [jax.ShapeDtypeStruct((b, l, SSD_WIDTH), F32)]
    if want_final:
        out_specs.append(st_spec)
        out_shape.append(jax.ShapeDtypeStruct((b, 2, SSD_GROUPS, SSD_STATE, gw), F32))
    return pl.pallas_call(
        functools.partial(_ssd_body, seq=l, has_s0=has_s0, want_final=want_final),
        grid=(b,),
        in_specs=in_specs,
        out_specs=out_specs,
        out_shape=out_shape,
        scratch_shapes=[pltpu.VMEM((2, nc, SSD_GROUPS, SSD_STATE, gw), F32),
                        pltpu.VMEM((2, nc, 8, SSD_WIDTH), F32),
                        pltpu.VMEM((2, SSD_GROUPS, SSD_STATE, gw), F32),
                        pltpu.VMEM((SSD_CHUNK, SSD_WIDTH), F32),
                        pltpu.VMEM((l, SSD_XBC), F32),
                        pltpu.VMEM((2, l, LANES), F32),
                        pltpu.VMEM((nc, 4, SSD_HEADS, LANES), F32)],
        compiler_params=_cparams("arbitrary"),
        name="ssd",
    )(*args)


def _softmax_pv(parts):
    m = parts[0][0].max(axis=-1, keepdims=True)
    for s, _ in parts[1:]:
        m = jnp.maximum(m, s.max(axis=-1, keepdims=True))
    den = 0.0
    out = 0.0
    for s, v in parts:
        p = jnp.exp(s - m)
        den = den + p.sum(axis=-1, keepdims=True)
        out = out + _bdot(p, v)
    return out / den


def _ctx_attn_body(q_ref, k_ref, v_ref, o_ref):
    for h in range(NA_HEADS):
        hs = slice(h * NA_HEAD_DIM, (h + 1) * NA_HEAD_DIM)
        s = _bdot_nt(q_ref[0, :, hs], k_ref[0, :, hs]) * NA_SCALE
        o_ref[0, :, hs] = _softmax_pv([(s, v_ref[0, :, hs])])


def _ctx_attn(q, k, v):
    b, l, _ = q.shape
    spec = pl.BlockSpec((1, l, NA_WIDTH), lambda i: (i, 0, 0))
    return pl.pallas_call(
        _ctx_attn_body,
        grid=(b,),
        in_specs=[spec, spec, spec],
        out_specs=spec,
        out_shape=jax.ShapeDtypeStruct((b, l, NA_WIDTH), F32),
        compiler_params=_cparams("arbitrary"),
        name="ctx_attn",
    )(q, k, v)


def _na_bias_tables(rpb, rows):
    kr = min(NA_ROWS, rows)
    qi = np.arange(NA_QROWS)[:, None]
    wi = np.arange(NA_WIN)[None, :]
    qc = np.arange(GRID_W)[:, None]
    kc = np.arange(GRID_W)[None, :]
    cs = np.clip(qc - NA_COLS // 2, 0, GRID_W - NA_COLS)
    col_ok = (kc >= cs) & (kc < cs + NA_COLS)
    dc = np.clip(kc - qc + NA_COLS - 1, 0, 2 * NA_COLS - 2)
    col_sel = (dc[None] == np.arange(2 * NA_COLS - 1)[:, None, None]).astype(np.float32)
    row_sel, row_ok = [], []
    for r0 in (0, NA_QROWS, rows - NA_QROWS):
        ws = min(max(r0 - kr // 2, 0), rows - NA_WIN)
        r = r0 + qi
        krow = ws + wi
        rs = np.clip(r - kr // 2, 0, rows - kr)
        row_ok.append((krow >= rs) & (krow < rs + kr))
        dr = np.clip(krow - r + NA_ROWS - 1, 0, 2 * NA_ROWS - 2)
        row_sel.append((dr[..., None] == np.arange(2 * NA_ROWS - 1)).astype(np.float32))
    ok = np.stack(row_ok)[:, None, :, None, :, None] & col_ok[None, None, None, :, None, :]
    by_col = jnp.einsum('hab,bqk->haqk', rpb.astype(F32), jnp.asarray(col_sel), precision=HIGHEST)
    bias = jnp.einsum('viwa,haqk->vhiqwk', jnp.asarray(np.stack(row_sel)), by_col, precision=HIGHEST)
    return jnp.where(jnp.asarray(ok), bias, NEG_BIG).reshape(
        3, NA_HEADS, NA_QROWS * GRID_W, NA_WIN * GRID_W)


def _na_body(q_ref, k_ref, v_ref, kc_ref, vc_ref, bias_ref, o_ref, *, rows):
    rb = pl.program_id(1)
    ws = jnp.clip(rb * NA_QROWS - NA_ROWS // 2, 0, rows - NA_WIN)
    k0 = pl.multiple_of(ws * GRID_W, NA_QROWS * GRID_W)
    nk = NA_WIN * GRID_W
    for h in range(NA_HEADS):
        hs = slice(h * NA_HEAD_DIM, (h + 1) * NA_HEAD_DIM)
        qh = q_ref[0, :, hs]
        s_loc = _bdot_nt(qh, k_ref[0, pl.ds(k0, nk), hs]) * NA_SCALE + bias_ref[0, h]
        s_ctx = _bdot_nt(qh, kc_ref[0, :, hs]) * NA_SCALE
        o_ref[0, :, hs] = _softmax_pv([(s_loc, v_ref[0, pl.ds(k0, nk), hs]), (s_ctx, vc_ref[0, :, hs])])


def _latent_na(q, k, v, k_ctx, v_ctx, rpb):
    b, l, _ = q.shape
    lc = k_ctx.shape[1]
    rows = l // GRID_W
    assert rows >= NA_WIN and rows % NA_QROWS == 0 and NA_QROWS * 2 <= NA_ROWS
    nrb = rows // NA_QROWS
    tq = NA_QROWS * GRID_W
    bias = _na_bias_tables(rpb, rows)
    variant = lambda i, r: (jnp.where(r == 0, 0, jnp.where(r == nrb - 1, 2, 1)), 0, 0, 0)
    return pl.pallas_call(
        functools.partial(_na_body, rows=rows),
        grid=(b, nrb),
        in_specs=[pl.BlockSpec((1, tq, NA_WIDTH), lambda i, r: (i, r, 0)),
                  pl.BlockSpec((1, l, NA_WIDTH), lambda i, r: (i, 0, 0)),
                  pl.BlockSpec((1, l, NA_WIDTH), lambda i, r: (i, 0, 0)),
                  pl.BlockSpec((1, lc, NA_WIDTH), lambda i, r: (i, 0, 0)),
                  pl.BlockSpec((1, lc, NA_WIDTH), lambda i, r: (i, 0, 0)),
                  pl.BlockSpec((1, NA_HEADS, tq, NA_WIN * GRID_W), variant)],
        out_specs=pl.BlockSpec((1, tq, NA_WIDTH), lambda i, r: (i, r, 0)),
        out_shape=jax.ShapeDtypeStruct((b, l, NA_WIDTH), F32),
        compiler_params=_cparams("arbitrary", "arbitrary"),
        name="latent_na",
    )(q, k, v, k_ctx, v_ctx, bias)


def _s5_prep_body(lre_ref, lim_ref, lst_ref, btre_ref, btim_ref, cre_ref, cim_ref,
                  wt_ref, kt_ref, et_ref, pw_ref):
    lam_re, lam_im = lre_ref[0], lim_ref[0]
    step = jnp.exp(lst_ref[0])
    mag = jnp.exp(lam_re * step)
    a_re = mag * jnp.cos(lam_im * step)
    a_im = mag * jnp.sin(lam_im * step)
    den = lam_re * lam_re + lam_im * lam_im
    k_re = ((a_re - 1.0) * lam_re + a_im * lam_im) / den
    k_im = (a_im * lam_re - (a_re - 1.0) * lam_im) / den
    bt_re, bt_im = btre_ref[0], btim_ref[0]
    bb_re = k_re * bt_re - k_im * bt_im
    bb_im = k_re * bt_im + k_im * bt_re
    c_re, c_im = cre_ref[0], cim_ref[0]
    p_re = jnp.ones_like(a_re)
    p_im = jnp.zeros_like(a_re)
    for t in range(S5_Q):
        rows = slice(t * S5_GROUP_CH, (t + 1) * S5_GROUP_CH)
        wt_ref[0, rows, 0:S5_STATE] = p_re * bb_re - p_im * bb_im
        wt_ref[0, rows, S5_STATE:2 * S5_STATE] = p_re * bb_im + p_im * bb_re
        p_re, p_im = p_re * a_re - p_im * a_im, p_re * a_im + p_im * a_re
        et_ref[0, rows, 0:S5_STATE] = c_re * p_re - c_im * p_im
        et_ref[0, rows, S5_STATE:2 * S5_STATE] = -(c_re * p_im + c_im * p_re)
    cc = jnp.concatenate([c_re, -c_im], axis=1)
    kt_ref[0] = lax.dot_general(wt_ref[0], cc, (((1,), (1,)), ((), ())), precision=HIGHEST,
                                preferred_element_type=F32)
    pw_ref[0] = jnp.zeros(pw_ref.shape[1:], F32)
    for k in range(S5_LOG_MAX):
        pw_ref[0, 2 * k:2 * k + 1, :] = jnp.concatenate([p_re, p_re], axis=1)
        pw_ref[0, 2 * k + 1:2 * k + 2, :] = jnp.concatenate([-p_im, p_im], axis=1)
        p_re, p_im = p_re * p_re - p_im * p_im, 2.0 * p_re * p_im


def _s5_tables(lam_re, lam_im, log_step, b_re, b_im, c_re, c_im):
    g, p, ch, q = S5_GROUPS, S5_STATE, S5_GROUP_CH, S5_Q
    n = 2 * g
    row = lambda a: a.astype(F32).reshape(n, 1, p)
    lst = jnp.broadcast_to(log_step.astype(F32)[:, :, None], (2, g, p)).reshape(n, 1, p)
    bt_re = jnp.swapaxes(b_re.astype(F32), 1, 2)
    bt_im = jnp.swapaxes(b_im.astype(F32), 1, 2)
    dspec = pl.BlockSpec((1, 1, p), lambda i: (i, 0, 0))
    gspec = pl.BlockSpec((1, ch, p), lambda i: (i % g, 0, 0))
    ospec = lambda a, bb: pl.BlockSpec((1, a, bb), lambda i: (i, 0, 0))
    wt, kt, et, pw = pl.pallas_call(
        _s5_prep_body,
        grid=(n,),
        in_specs=[dspec, dspec, dspec, gspec, gspec, gspec, gspec],
        out_specs=[ospec(S5_CW, 2 * p), ospec(S5_CW, ch), ospec(S5_CW, 2 * p), ospec(16, 2 * p)],
        out_shape=[jax.ShapeDtypeStruct((n, S5_CW, 2 * p), F32),
                   jax.ShapeDtypeStruct((n, S5_CW, ch), F32),
                   jax.ShapeDtypeStruct((n, S5_CW, 2 * p), F32),
                   jax.ShapeDtypeStruct((n, 16, 2 * p), F32)],
        compiler_params=_cparams("arbitrary"),
        name="s5_prep",
    )(row(lam_re), row(lam_im), lst, bt_re, bt_im, c_re.astype(F32), c_im.astype(F32))
    kt = kt.reshape(2, g, q, ch, ch)
    ti = jnp.arange(q)[:, None]
    to = jnp.arange(q)[None, :]
    t_f = jnp.where((to >= ti)[None, :, :, None, None], kt[0][:, jnp.clip(to - ti, 0, q - 1)], 0.0)
    t_b = jnp.where((ti >= to)[None, :, :, None, None], kt[1][:, jnp.clip(ti - to, 0, q - 1)], 0.0)
    t_all = jnp.transpose(t_f + t_b, (0, 1, 3, 2, 4)).reshape(g, S5_CW, S5_CW)
    wt = wt.reshape(2, g, q, ch, 2 * p)
    g_fb = jnp.concatenate([jnp.flip(wt[0], axis=1), wt[1]], axis=-1).reshape(g, S5_CW, 4 * p)
    et = et.reshape(2, g, q, ch, 2 * p)
    e_f = jnp.swapaxes(et[0].reshape(g, S5_CW, 2 * p), 1, 2)
    e_b = jnp.swapaxes(jnp.flip(et[1], axis=1).reshape(g, S5_CW, 2 * p), 1, 2)
    e_fb = jnp.concatenate([e_f, e_b], axis=1)
    return t_all.astype(BF16), g_fb.astype(BF16), e_fb.astype(BF16), pw.reshape(2, g, 16, 2 * p)


def _cmul(a_full, a_sgn, s):
    return a_full * s + a_sgn * pltpu.roll(s, S5_STATE, axis=1)


def _s5_body(u_ref, perm_ref, t_ref, g_ref, e_ref, pw_ref, dsk_ref, *rest, nb, nc, has_s0,
             want_final):
    rest = list(rest)
    s0_ref = rest.pop(0) if has_s0 else None
    o_ref = rest.pop(0)
    fin_ref = rest.pop(0) if want_final else None
    ys, xs_f, xs_b, tok_s = rest
    m = nb * nc
    w = 2 * S5_STATE
    lane_blk = lax.broadcasted_iota(jnp.int32, (m, LANES), 1) // S5_GROUP_CH
    row_blk = (lax.broadcasted_iota(jnp.int32, (2 * LANES, S5_CW), 0) % LANES) // S5_GROUP_CH
    cidx = lax.broadcasted_iota(jnp.int32, (m, w), 0) & (nc - 1)
    first = cidx == 0
    last = cidx == nc - 1

    def tokens(t):
        return u_ref[:, pl.ds(t, nc, stride=S5_Q), :].reshape(m, LANES)

    for t in range(S5_Q):
        tok_s[t // 2, :, (t % 2) * LANES:(t % 2 + 1) * LANES] = tokens(t).astype(BF16)

    for gg in range(S5_LG):
        own = row_blk == gg
        parts = [None, None]
        for t2 in range(S5_Q // 2):
            p = jnp.where(own, perm_ref[2 * t2 - gg + S5_LG - 1], jnp.zeros((), BF16))
            d = jnp.dot(tok_s[t2], p, preferred_element_type=F32)
            parts[t2 % 2] = d if parts[t2 % 2] is None else parts[t2 % 2] + d
        ub = (parts[0] + parts[1]).astype(BF16)
        z = jnp.dot(ub, g_ref[gg], preferred_element_type=F32)
        x_f, x_b = z[:, :w], z[:, w:]
        pw = lambda d, r: pw_ref[d, gg, r:r + 1, :]
        if has_s0:
            rep = lambda a: jnp.broadcast_to(a[:, None, :], (nb, nc, w)).reshape(m, w)
            s0_f, s0_b = rep(s0_ref[0, gg, :, :w]), rep(s0_ref[0, gg, :, w:])
            x_f = x_f + jnp.where(first, _cmul(pw(0, 0), pw(0, 1), s0_f), 0.0)
            x_b = x_b + jnp.where(last, _cmul(pw(1, 0), pw(1, 1), s0_b), 0.0)
        for k in range(nc.bit_length() - 1):
            sh = 1 << k
            prev = pltpu.roll(x_f, sh, axis=0)
            x_f = x_f + jnp.where(cidx >= sh, _cmul(pw(0, 2 * k), pw(0, 2 * k + 1), prev), 0.0)
            nxt = pltpu.roll(x_b, m - sh, axis=0)
            x_b = x_b + jnp.where(cidx < nc - sh, _cmul(pw(1, 2 * k), pw(1, 2 * k + 1), nxt), 0.0)
        in_f = pltpu.roll(x_f, 1, axis=0)
        in_b = pltpu.roll(x_b, m - 1, axis=0)
        if has_s0:
            in_f = jnp.where(first, s0_f, in_f)
            in_b = jnp.where(last, s0_b, in_b)
        else:
            in_f = jnp.where(first, 0.0, in_f)
            in_b = jnp.where(last, 0.0, in_b)
        s_in = jnp.concatenate([in_f, in_b], axis=1).astype(BF16)
        ys[gg] = (jnp.dot(ub, t_ref[gg], preferred_element_type=F32)
                  + jnp.dot(s_in, e_ref[gg], preferred_element_type=F32))
        if want_final:
            xs_f[...] = x_f
            xs_b[...] = x_b
            fin_ref[0, gg, :, :w] = xs_f[pl.ds(nc - 1, nb, stride=nc), :]
            fin_ref[0, gg, :, w:] = xs_b[pl.ds(0, nb, stride=nc), :]

    for t in range(S5_Q):
        tb, tt = divmod(t, S5_LG)
        y = None
        for k in range(S5_LG):
            sh = (S5_GROUP_CH * (k - tt)) % LANES
            r = ys[k, :, tb * LANES:(tb + 1) * LANES]
            if sh:
                r = pltpu.roll(r, sh, axis=1)
            y = r if y is None else jnp.where(lane_blk == k, r, y)
        y = y + tokens(t) * dsk_ref[0]
        o_ref[:, pl.ds(t, nc, stride=S5_Q), :] = _gelu_tanh(y).reshape(nb, nc, LANES)


def _s5(u, tables, d_skip, s0, want_final):
    t_all, g_fb, e_fb, pows = tables
    b, l, _ = u.shape
    nc = l // S5_Q
    bb = min(b, max(1, S5_ROWS // nc))
    nbb = b // bb
    m = bb * nc
    has_s0 = s0 is not None
    seq_spec = pl.BlockSpec((bb, l, LANES), lambda i, j: (j, 0, i))
    tab_spec = pl.BlockSpec((S5_LG, S5_CW, S5_CW), lambda i, j: (i, 0, 0),
                            pipeline_mode=pl.Buffered(1))
    st_spec = pl.BlockSpec((1, S5_LG, bb, S5_CW), lambda i, j: (j, i, 0, 0))
    st_blocks = lambda a: jnp.transpose(a.reshape(S5_GROUPS, nbb, bb, S5_CW), (1, 0, 2, 3))
    r = np.arange(LANES)[:, None]
    cc = np.arange(S5_CW)[None, :]
    shift = lambda d: (cc - r == S5_GROUP_CH * d).astype(np.float32)
    perm = np.stack([np.concatenate([shift(d), shift(d + 1)], axis=0)
                     for d in range(1 - S5_LG, S5_Q - 1)])
    in_specs = [seq_spec, _const_spec(perm.shape), tab_spec, tab_spec, tab_spec,
                pl.BlockSpec((2, S5_LG, 16, 2 * S5_STATE), lambda i, j: (0, i, 0, 0)),
                pl.BlockSpec((1, 1, LANES), lambda i, j: (i, 0, 0))]
    args = [u, jnp.asarray(perm, BF16), t_all, g_fb, e_fb, pows,
            d_skip.astype(F32).reshape(S5_WIDTH // LANES, 1, LANES)]
    if has_s0:
        in_specs.append(st_spec)
        args.append(st_blocks(s0))
    out_specs = [seq_spec]
    out_shape = [jax.ShapeDtypeStruct((b, l, S5_WIDTH), F32)]
    if want_final:
        out_specs.append(st_spec)
        out_shape.append(jax.ShapeDtypeStruct((nbb, S5_GROUPS, bb, S5_CW), F32))
    res = pl.pallas_call(
        functools.partial(_s5_body, nb=bb, nc=nc, has_s0=has_s0, want_final=want_final),
        grid=(S5_WIDTH // LANES, nbb),
        in_specs=in_specs,
        out_specs=out_specs,
        out_shape=out_shape,
        scratch_shapes=[pltpu.VMEM((S5_LG, m, S5_CW), F32),
                        pltpu.VMEM((m, 2 * S5_STATE), F32), pltpu.VMEM((m, 2 * S5_STATE), F32),
                        pltpu.VMEM((S5_Q // 2, m, 2 * LANES), BF16)],
        compiler_params=_cparams("arbitrary", "arbitrary"),
        name="s5",
    )(*args)
    fin = None
    if want_final:
        fin = jnp.transpose(res[1], (1, 0, 2, 3)).reshape(S5_GROUPS, b, S5_CW)
    return res[0], fin


def _dft_cos_sin(n, rows=None, row_step=1):
    rows = n if rows is None else rows
    j = jnp.arange(rows, dtype=jnp.int32)[:, None] * row_step
    jk = (j * jnp.arange(n, dtype=jnp.int32)[None, :]) % n
    ang = jk.astype(F32) * (2.0 * math.pi / n)
    return jnp.cos(ang), jnp.sin(ang)


def _dft_cos_sin_split(n, minor):
    c1, s1 = _dft_cos_sin(n, n // minor, minor)
    c2, s2 = _dft_cos_sin(n, minor)
    cos = c1[:, None, :] * c2[None, :, :] - s1[:, None, :] * s2[None, :, :]
    sin = s1[:, None, :] * c2[None, :, :] + c1[:, None, :] * s2[None, :, :]
    return cos.reshape(n, n), sin.reshape(n, n)


def _fnet_body(u_ref, cs_ref, dl_ref, o_ref, xs, *, seq):
    @pl.when(pl.program_id(1) == 0)
    def _():
        xcs = jnp.dot(u_ref[0].astype(BF16), cs_ref[...], preferred_element_type=F32)
        xs[0:seq, :] = xcs[:, :FNET_WIDTH].astype(BF16)
        xs[seq:2 * seq, :] = xcs[:, FNET_WIDTH:].astype(BF16)

    scale = 1.0 / math.sqrt(seq * FNET_GROUP_CH)
    o_ref[0] = jnp.dot(dl_ref[...], xs[...], preferred_element_type=F32) * scale


def _fnet(u):
    b, l, _ = u.shape
    tr = min(l, 512)
    cc, sc = _dft_cos_sin(FNET_GROUP_CH)
    eye = jnp.eye(FNET_GROUPS, dtype=F32)
    cs = jnp.concatenate([jnp.kron(eye, cc), jnp.kron(eye, sc)], axis=1).astype(BF16)
    cl, sl = _dft_cos_sin_split(l, DFT_MINOR) if l > 4 * DFT_MINOR else _dft_cos_sin(l)
    dl = jnp.concatenate([cl, -sl], axis=1).astype(BF16)
    return pl.pallas_call(
        functools.partial(_fnet_body, seq=l),
        grid=(b, l // tr),
        in_specs=[pl.BlockSpec((1, l, FNET_WIDTH), lambda i, t: (i, 0, 0)),
                  _const_spec((FNET_WIDTH, 2 * FNET_WIDTH)),
                  pl.BlockSpec((tr, 2 * l), lambda i, t: (t, 0))],
        out_specs=pl.BlockSpec((1, tr, FNET_WIDTH), lambda i, t: (i, t, 0)),
        out_shape=jax.ShapeDtypeStruct((b, l, FNET_WIDTH), F32),
        scratch_shapes=[pltpu.VMEM((2 * l, FNET_WIDTH), BF16)],
        compiler_params=_cparams("arbitrary", "arbitrary"),
        name="fnet",
    )(u, cs, dl)


def _even_w_in(w):
    pad = jnp.zeros((D_MODEL, LANES - SSD_HEADS), w.dtype)
    return jnp.concatenate([w[:, :O_DT], w[:, O_Q:], w[:, O_DT:O_Q], pad], axis=1).astype(BF16)


EVEN_SPLITS = (SSD_WIDTH, SSD_XBC, NA_WIDTH, NA_WIDTH, NA_WIDTH, LANES)


def kernel(x_prompt, x_sample, cache_na_k, cache_na_v, state_ssd, state_s5, c, c_ctx, mod_w, mod_b, norm1_g, norm2_g, ffn_w1, ffn_w3, ffn_w2, final_g, ev_w_in, ev_conv_w, ev_conv_b, ev_a_log, ev_dt_bias, ev_d_skip, ev_ssd_norm_g, ev_rpb, ev_w_out, od_w_in, od_lam_re, od_lam_im, od_log_step, od_b_re, od_b_im, od_c_re, od_c_im, od_d_skip, od_glu_w, od_glu_b, od_w_out):
    bc, lc, _ = x_prompt.shape
    bl, ll, _ = x_sample.shape
    cond = jnp.zeros((MOD_ROWS, D_MODEL), F32).at[0].set(c_ctx).at[1:1 + bl].set(c)
    mods = _adaln(cond, mod_w, mod_b)
    xc = x_prompt.reshape(1, bc * lc, D_MODEL)
    xl = x_sample
    tm = 512
    new_k, new_v, new_ssd, new_s5 = [], [], [], []
    for i in range(DEPTH):
        j = i // 2
        mod = mods[i]
        last = i == DEPTH - 1
        if i % 2 == 0:
            w_in = _even_w_in(ev_w_in[j])
            w_out = ev_w_out[j].astype(BF16)
            ssd_args = (ev_conv_w[j], ev_conv_b[j], ev_a_log[j], ev_dt_bias[j], ev_d_skip[j],
                        ev_ssd_norm_g[j])
            z, xbc, q, k, v, dtp = _inproj(xc, mod, 0, norm1_g[i], w_in, EVEN_SPLITS, tm)
            seqs = lambda a: a.reshape(bc, lc, a.shape[-1])
            y_ssd, fin = _ssd(seqs(z), seqs(xbc), seqs(dtp), *ssd_args, None, True)
            o_na = _ctx_attn(seqs(q), seqs(k), seqs(v))
            flat = lambda a: a.reshape(1, bc * lc, a.shape[-1])
            mix_c = (flat(y_ssd), flat(o_na))
            new_k.append(k.reshape(bc, lc, NA_HEADS, NA_HEAD_DIM))
            new_v.append(v.reshape(bc, lc, NA_HEADS, NA_HEAD_DIM))
            new_ssd.append(_ssd_state_from_kernel(fin))

            z, xbc, q, k, v, dtp = _inproj(xl, mod, 1, norm1_g[i], w_in, EVEN_SPLITS, tm)
            s0t = _ssd_state_to_kernel(state_ssd[:, j].astype(F32))
            (y_ssd,) = _ssd(z, xbc, dtp, *ssd_args, s0t, False)
            pl_ = cache_na_k.shape[2]
            o_na = _latent_na(q, k, v, cache_na_k[:, j].reshape(bl, pl_, NA_WIDTH),
                              cache_na_v[:, j].reshape(bl, pl_, NA_WIDTH), ev_rpb[j])
            mix_l = (y_ssd, o_na)
            glu_w = glu_b = None
        else:
            w_in = od_w_in[j].astype(BF16)
            w_out = od_w_out[j].astype(BF16)
            glu_w, glu_b = od_glu_w[j].astype(BF16), od_glu_b[j]
            tables = _s5_tables(od_lam_re[j], od_lam_im[j], od_log_step[j], od_b_re[j], od_b_im[j],
                                od_c_re[j], od_c_im[j])
            splits = (S5_WIDTH, FNET_WIDTH)
            u_s, u_f = _inproj(xc, mod, 0, norm1_g[i], w_in, splits, tm)
            seqs = lambda a: a.reshape(bc, lc, a.shape[-1])
            g_s, fin = _s5(seqs(u_s), tables, od_d_skip[j], None, True)
            y_f = _fnet(seqs(u_f))
            flat = lambda a: a.reshape(1, bc * lc, a.shape[-1])
            mix_c = (flat(g_s), flat(y_f))
            fin = fin.reshape(S5_GROUPS, bc, 2, 2, S5_STATE)
            new_s5.append(jnp.transpose(fin, (1, 2, 0, 4, 3)))

            u_s, u_f = _inproj(xl, mod, 1, norm1_g[i], w_in, splits, tm)
            s0 = jnp.transpose(state_s5[:, j].astype(F32), (2, 0, 1, 4, 3)).reshape(
                S5_GROUPS, bl, 4 * S5_STATE)
            g_s, _ = _s5(u_s, tables, od_d_skip[j], s0, False)
            y_f = _fnet(u_f)
            mix_l = (g_s, y_f)
        w_o = (w_out[:SSD_WIDTH], w_out[SSD_WIDTH:])
        w1, w3, w2 = ffn_w1[i].astype(BF16), ffn_w3[i].astype(BF16), ffn_w2[i].astype(BF16)
        fg = final_g if last else None
        xc = _ffn(xc, mod, 0, *mix_c, *w_o, glu_w, glu_b, norm2_g[i], w1, w3, w2, fg, tm)
        xl = _ffn(xl, mod, 1, *mix_l, *w_o, glu_w, glu_b, norm2_g[i], w1, w3, w2, fg, tm)
    return (xc.reshape(bc, lc, D_MODEL), xl,
            jnp.stack(new_k, axis=1), jnp.stack(new_v, axis=1),
            jnp.stack(new_ssd, axis=1), jnp.stack(new_s5, axis=1))
```

```python
import functools
import math

import numpy as np
import jax
import jax.numpy as jnp
from jax import lax
from jax.experimental import pallas as pl
from jax.experimental.pallas import tpu as pltpu

F32 = jnp.float32
BF16 = jnp.bfloat16
HIGHEST = lax.Precision.HIGHEST

D_MODEL = 1024
DEPTH = 2
EPS = 1e-6
GRID_W = 64
FFN_HIDDEN = 2816
FFN_CHUNK = 1408

SSD_HEADS = 8
SSD_HEAD_DIM = 64
SSD_WIDTH = 512
SSD_GROUPS = 2
SSD_HPG = SSD_HEADS // SSD_GROUPS
SSD_STATE = 64
SSD_GN = SSD_GROUPS * SSD_STATE
SSD_XBC = SSD_WIDTH + 2 * SSD_GN
SSD_CHUNK = 128
CONV_W = 5
CONV_HALO = 8

NA_HEADS = 8
NA_HEAD_DIM = 64
NA_WIDTH = 512
NA_ROWS = 8
NA_COLS = 16
NA_SCALE = NA_HEAD_DIM ** -0.5
NA_QROWS = 4
NA_WIN = NA_ROWS + NA_QROWS
NEG_BIG = -1e30

O_XBC = SSD_WIDTH
O_DT = O_XBC + SSD_XBC
O_Q = O_DT + SSD_HEADS
O_K = O_Q + NA_WIDTH
O_V = O_K + NA_WIDTH
LANES = 128

S5_GROUP_CH = 16
S5_GROUPS = 32
S5_WIDTH = 512
S5_STATE = 64
S5_Q = 16
S5_CW = S5_Q * S5_GROUP_CH
S5_LOG_MAX = 7
S5_LG = LANES // S5_GROUP_CH
S5_ROWS = 512

FNET_GROUP_CH = 64
FNET_GROUPS = 8
FNET_WIDTH = 512
DFT_MINOR = 64

MOD_ROWS = 16
VMEM_LIMIT = 56 * 1024 * 1024


def _cparams(*sem):
    return pltpu.CompilerParams(dimension_semantics=sem, vmem_limit_bytes=VMEM_LIMIT)


def _const_spec(shape):
    nd = len(shape)
    return pl.BlockSpec(shape, lambda *_: (0,) * nd, pipeline_mode=pl.Buffered(1))


def _sigmoid(x):
    return 1.0 / (1.0 + jnp.exp(-x))


def _silu(x):
    return x * _sigmoid(x)


def _softplus(x):
    return jnp.maximum(x, 0.0) + jnp.log(1.0 + jnp.exp(-jnp.abs(x)))


def _gelu_tanh(x):
    return 0.5 * x * (1.0 + jnp.tanh(math.sqrt(2.0 / math.pi) * (x + 0.044715 * (x * x * x))))


def _rmsnorm(x, g):
    return x * lax.rsqrt(jnp.mean(x * x, axis=-1, keepdims=True) + EPS) * g


def _bdot(a, b):
    return jnp.dot(a.astype(BF16), b.astype(BF16), preferred_element_type=F32)


def _bdot_nt(a, b):
    return lax.dot_general(a.astype(BF16), b.astype(BF16), (((1,), (1,)), ((), ())),
                           preferred_element_type=F32)


def _bdot_tn(a, b):
    return lax.dot_general(a.astype(BF16), b.astype(BF16), (((0,), (0,)), ((), ())),
                           preferred_element_type=F32)


def _adaln_body(c_ref, w_ref, b_ref, o_ref):
    s = _silu(c_ref[...])
    o_ref[0] = jnp.dot(s, w_ref[0], precision=HIGHEST, preferred_element_type=F32) + b_ref[0]


def _adaln(cond, mod_w, mod_b):
    n = 6 * D_MODEL
    tn = 1536
    out = pl.pallas_call(
        _adaln_body,
        grid=(DEPTH, n // tn),
        in_specs=[pl.BlockSpec((MOD_ROWS, D_MODEL), lambda i, j: (0, 0)),
                  pl.BlockSpec((1, D_MODEL, tn), lambda i, j: (i, 0, j)),
                  pl.BlockSpec((1, 1, tn), lambda i, j: (i, 0, j))],
        out_specs=pl.BlockSpec((1, MOD_ROWS, tn), lambda i, j: (i, 0, j)),
        out_shape=jax.ShapeDtypeStruct((DEPTH, MOD_ROWS, n), F32),
        compiler_params=_cparams("arbitrary", "arbitrary"),
        name="adaln",
    )(cond, mod_w, mod_b.reshape(DEPTH, 1, n))
    return out.reshape(DEPTH, MOD_ROWS, 6, D_MODEL)


def _mod_spec(mrow0):
    return pl.BlockSpec((1, 6, D_MODEL), lambda b, t: (b + mrow0, 0, 0))


def _inproj_body(x_ref, mod_ref, g_ref, w_ref, *o_refs, splits):
    h = _rmsnorm(x_ref[0], g_ref[...]) * (1.0 + mod_ref[0, 1:2, :]) + mod_ref[0, 0:1, :]
    acc = jnp.dot(h.astype(BF16), w_ref[...], preferred_element_type=F32)
    off = 0
    for o_ref, n in zip(o_refs, splits):
        o_ref[0] = acc[:, off:off + n]
        off += n


def _inproj(x, mod, mrow0, g, w, splits, tm):
    b, l, _ = x.shape
    n = w.shape[1]
    return pl.pallas_call(
        functools.partial(_inproj_body, splits=splits),
        grid=(b, l // tm),
        in_specs=[pl.BlockSpec((1, tm, D_MODEL), lambda i, t: (i, t, 0)),
                  _mod_spec(mrow0),
                  _const_spec((1, D_MODEL)),
                  _const_spec((D_MODEL, n))],
        out_specs=[pl.BlockSpec((1, tm, s), lambda i, t: (i, t, 0)) for s in splits],
        out_shape=[jax.ShapeDtypeStruct((b, l, s), F32) for s in splits],
        compiler_params=_cparams("arbitrary", "arbitrary"),
        name="inproj",
    )(x, mod, g.reshape(1, D_MODEL), w)


def _ffn_body(x_ref, mod_ref, ya_ref, yb_ref, wa_ref, wb_ref, g_ref, w1_ref, w3_ref, w2_ref, *rest,
              glu, final):
    rest = list(rest)
    ya = ya_ref[0]
    if glu:
        gw_ref, gb_ref = rest.pop(0), rest.pop(0)
        ya = ya * _sigmoid(_bdot(ya, gw_ref[...]) + gb_ref[...])
    mix = _bdot(ya, wa_ref[...]) + _bdot(yb_ref[0], wb_ref[...])
    x = x_ref[0] + mod_ref[0, 2:3, :] * mix
    h = (_rmsnorm(x, g_ref[...]) * (1.0 + mod_ref[0, 4:5, :]) + mod_ref[0, 3:4, :]).astype(BF16)
    acc = jnp.zeros(x.shape, F32)
    for c in range(FFN_HIDDEN // FFN_CHUNK):
        cols = slice(c * FFN_CHUNK, (c + 1) * FFN_CHUNK)
        a = jnp.dot(h, w1_ref[:, cols], preferred_element_type=F32)
        u = _silu(a) * jnp.dot(h, w3_ref[:, cols], preferred_element_type=F32)
        acc = acc + jnp.dot(u.astype(BF16), w2_ref[cols, :], preferred_element_type=F32)
    y = x + mod_ref[0, 5:6, :] * acc
    if final:
        fg_ref, o_ref = rest
        y = _rmsnorm(y, fg_ref[...])
    else:
        (o_ref,) = rest
    o_ref[0] = y


def _ffn(x, mod, mrow0, ya, yb, wa, wb, glu_w, glu_b, g, w1, w3, w2, final_g, tm):
    b, l, _ = x.shape
    wa_n, wb_n = ya.shape[-1], yb.shape[-1]
    glu = glu_w is not None
    final = final_g is not None
    row_spec = lambda n: pl.BlockSpec((1, tm, n), lambda i, t: (i, t, 0))
    in_specs = [row_spec(D_MODEL), _mod_spec(mrow0), row_spec(wa_n), row_spec(wb_n),
                _const_spec((wa_n, D_MODEL)), _const_spec((wb_n, D_MODEL)),
                _const_spec((1, D_MODEL)),
                _const_spec((D_MODEL, FFN_HIDDEN)),
                _const_spec((D_MODEL, FFN_HIDDEN)),
                _const_spec((FFN_HIDDEN, D_MODEL))]
    args = [x, mod, ya, yb, wa, wb, g.reshape(1, D_MODEL), w1, w3, w2]
    if glu:
        in_specs += [_const_spec((wa_n, wa_n)), _const_spec((1, wa_n))]
        args += [glu_w, glu_b.reshape(1, wa_n)]
    if final:
        in_specs.append(_const_spec((1, D_MODEL)))
        args.append(final_g.reshape(1, D_MODEL))
    return pl.pallas_call(
        functools.partial(_ffn_body, glu=glu, final=final),
        grid=(b, l // tm),
        in_specs=in_specs,
        out_specs=pl.BlockSpec((1, tm, D_MODEL), lambda i, t: (i, t, 0)),
        out_shape=jax.ShapeDtypeStruct((b, l, D_MODEL), F32),
        compiler_params=_cparams("arbitrary", "arbitrary"),
        name="ffn",
    )(*args)


def _ssd_body(z_ref, xbc_ref, dt_ref, cw_ref, cb_ref, hp_ref, dsk_ref, ng_ref, *rest,
              seq, has_s0, want_final):
    rest = list(rest)
    s0_ref = rest.pop(0) if has_s0 else None
    y_ref = rest.pop(0)
    fin_ref = rest.pop(0) if want_final else None
    st_s, cd_s, scur, ych, xc_s, cum_s, tr_s = rest
    q = SSD_CHUNK
    nc = seq // q
    gw = SSD_HPG * SSD_HEAD_DIM
    expand = (lax.broadcasted_iota(jnp.int32, (LANES, SSD_WIDTH), 1) // SSD_HEAD_DIM
              == lax.broadcasted_iota(jnp.int32, (LANES, SSD_WIDTH), 0)).astype(BF16)

    def per_head(v):
        hi = v.astype(BF16)
        lo = (v - hi.astype(F32)).astype(BF16)
        return (jnp.dot(hi, expand, preferred_element_type=F32)
                + jnp.dot(lo, expand, preferred_element_type=F32))
    row = lax.broadcasted_iota(jnp.int32, (q, q), 0)
    col = lax.broadcasted_iota(jnp.int32, (q, q), 1)
    lower = row >= col
    upper = col >= row
    tri_l = lower.astype(F32)
    tri_u = upper.astype(F32)
    a_f = -jnp.exp(hp_ref[0:1, :])
    a_b = -jnp.exp(hp_ref[1:2, :])
    lane = lax.broadcasted_iota(jnp.int32, (1, LANES), 1)
    a_f = jnp.where(lane < SSD_HEADS, a_f, 0.0)
    a_b = jnp.where(lane < SSD_HEADS, a_b, 0.0)
    bias_f = hp_ref[2:3, :]
    bias_b = hp_ref[3:4, :]

    def chunk_pre(c):
        r0 = pl.multiple_of(c * q, q)
        lo = pl.multiple_of(jnp.maximum(r0 - CONV_HALO, 0), CONV_HALO)
        hi = pl.multiple_of(jnp.minimum(r0 + q, seq - CONV_HALO), CONV_HALO)
        prev = jnp.where(c > 0, xbc_ref[0, pl.ds(lo, CONV_HALO), :], 0.0)
        nxt = jnp.where(c < nc - 1, xbc_ref[0, pl.ds(hi, CONV_HALO), :], 0.0)
        win = jnp.concatenate([prev, xbc_ref[0, pl.ds(r0, q), :], nxt], axis=0)
        acc = cb_ref[...] + cw_ref[0:1, :] * win[CONV_HALO - 2:CONV_HALO - 2 + q]
        for k in range(1, CONV_W):
            o = CONV_HALO - CONV_W // 2 + k
            acc = acc + cw_ref[k:k + 1, :] * win[o:o + q]
        xc = _silu(acc)
        dtr = dt_ref[0, pl.ds(r0, q), :]
        dt_f = _softplus(dtr + bias_f)
        dt_b = _softplus(dtr + bias_b)
        cum_f = jnp.dot(tri_l, dt_f * a_f, precision=HIGHEST, preferred_element_type=F32)
        cum_b = jnp.dot(tri_u, dt_b * a_b, precision=HIGHEST, preferred_element_type=F32)
        return r0, xc, dt_f, dt_b, cum_f, cum_b

    def pass_a(c, carry):
        r0, xc, dt_f, dt_b, cum_f, cum_b = chunk_pre(c)
        xc_s[pl.ds(r0, q), :] = xc
        cum_s[0, pl.ds(r0, q), :] = cum_f
        cum_s[1, pl.ds(r0, q), :] = cum_b
        for i, v in enumerate((cum_f, cum_b, dt_f, dt_b)):
            tr_s[c, i] = v.T[0:SSD_HEADS, :]
        end_f = cum_f[q - 1:q, :]
        end_b = cum_b[0:1, :]
        xw_f = xc[:, :SSD_WIDTH] * per_head(jnp.exp(end_f - cum_f) * dt_f)
        xw_b = xc[:, :SSD_WIDTH] * per_head(jnp.exp(end_b - cum_b) * dt_b)
        cd_s[0, c] = per_head(jnp.broadcast_to(jnp.exp(end_f), (8, LANES)))
        cd_s[1, c] = per_head(jnp.broadcast_to(jnp.exp(end_b), (8, LANES)))
        for g in range(SSD_GROUPS):
            bg = xc[:, SSD_WIDTH + g * SSD_STATE:SSD_WIDTH + (g + 1) * SSD_STATE]
            st_s[0, c, g] = _bdot_tn(bg, xw_f[:, g * gw:(g + 1) * gw])
            st_s[1, c, g] = _bdot_tn(bg, xw_b[:, g * gw:(g + 1) * gw])
        return carry

    lax.fori_loop(0, nc, pass_a, 0)

    if has_s0:
        scur[...] = s0_ref[0]
    else:
        scur[...] = jnp.zeros(scur.shape, F32)

    def pass_b(c, carry):
        cr = nc - 1 - c
        for g in range(SSD_GROUPS):
            s_in = scur[0, g]
            scur[0, g] = s_in * cd_s[0, c, 0:1, g * gw:(g + 1) * gw] + st_s[0, c, g]
            st_s[0, c, g] = s_in
            s_in = scur[1, g]
            scur[1, g] = s_in * cd_s[1, cr, 0:1, g * gw:(g + 1) * gw] + st_s[1, cr, g]
            st_s[1, cr, g] = s_in
        return carry

    lax.fori_loop(0, nc, pass_b, 0)
    if want_final:
        fin_ref[0] = scur[...]

    def pass_c(c, carry):
        r0 = pl.multiple_of(c * q, q)
        xc = xc_s[pl.ds(r0, q), :]
        cum_f = cum_s[0, pl.ds(r0, q), :]
        cum_b = cum_s[1, pl.ds(r0, q), :]
        cum_ft, cum_bt, dt_ft, dt_bt = tr_s[c, 0], tr_s[c, 1], tr_s[c, 2], tr_s[c, 3]
        ex_f = per_head(jnp.exp(cum_f))
        ex_b = per_head(jnp.exp(cum_b))
        y_off = []
        cbs = []
        for g in range(SSD_GROUPS):
            gs = slice(g * gw, (g + 1) * gw)
            bg = xc[:, SSD_WIDTH + g * SSD_STATE:SSD_WIDTH + (g + 1) * SSD_STATE]
            cg = xc[:, SSD_WIDTH + SSD_GN + g * SSD_STATE:SSD_WIDTH + SSD_GN + (g + 1) * SSD_STATE]
            cbs.append(_bdot_nt(cg, bg))
            y_off.append(ex_f[:, gs] * _bdot(cg, st_s[0, c, g]) + ex_b[:, gs] * _bdot(cg, st_s[1, c, g]))
        for h in range(SSD_HEADS):
            xh = xc[:, h * SSD_HEAD_DIM:(h + 1) * SSD_HEAD_DIM]
            seg_f = cum_f[:, h:h + 1] - cum_ft[h:h + 1, :]
            seg_b = cum_b[:, h:h + 1] - cum_bt[h:h + 1, :]
            m_f = jnp.exp(jnp.where(lower, seg_f, NEG_BIG)) * dt_ft[h:h + 1, :]
            m_b = jnp.exp(jnp.where(upper, seg_b, NEG_BIG)) * dt_bt[h:h + 1, :]
            ych[:, h * SSD_HEAD_DIM:(h + 1) * SSD_HEAD_DIM] = _bdot(cbs[h // SSD_HPG] * (m_f + m_b), xh)
        yf = ych[...] + jnp.concatenate(y_off, axis=1) + dsk_ref[...] * xc[:, :SSD_WIDTH]
        yf = yf * _silu(z_ref[0, pl.ds(r0, q), :])
        y_ref[0, pl.ds(r0, q), :] = _rmsnorm(yf, ng_ref[...])
        return carry

    lax.fori_loop(0, nc, pass_c, 0)


def _ssd_state_to_kernel(s):
    b = s.shape[0]
    s = s.reshape(b, 2, SSD_GROUPS, SSD_HPG, SSD_HEAD_DIM, SSD_STATE)
    return jnp.transpose(s, (0, 1, 2, 5, 3, 4)).reshape(b, 2, SSD_GROUPS, SSD_STATE, SSD_HPG * SSD_HEAD_DIM)


def _ssd_state_from_kernel(s):
    b = s.shape[0]
    s = s.reshape(b, 2, SSD_GROUPS, SSD_STATE, SSD_HPG, SSD_HEAD_DIM)
    return jnp.transpose(s, (0, 1, 2, 4, 5, 3)).reshape(b, 2, SSD_HEADS, SSD_HEAD_DIM, SSD_STATE)


def _ssd(z, xbc, dtp, conv_w, conv_b, a_log, dt_bias, d_skip, norm_g, s0t, want_final):
    b, l, _ = z.shape
    nc = l // SSD_CHUNK
    gw = SSD_HPG * SSD_HEAD_DIM
    has_s0 = s0t is not None
    cw = jnp.zeros((8, SSD_XBC), F32).at[:CONV_W].set(conv_w)
    hp = jnp.zeros((8, LANES), F32)
    hp = hp.at[0:2, :SSD_HEADS].set(a_log).at[2:4, :SSD_HEADS].set(dt_bias)
    dsk = jnp.repeat(d_skip, SSD_HEAD_DIM).reshape(1, SSD_WIDTH)
    seq_spec = lambda n: pl.BlockSpec((1, l, n), lambda i: (i, 0, 0))
    st_spec = pl.BlockSpec((1, 2, SSD_GROUPS, SSD_STATE, gw), lambda i: (i, 0, 0, 0, 0))
    in_specs = [seq_spec(SSD_WIDTH), seq_spec(SSD_XBC), seq_spec(LANES),
                _const_spec((8, SSD_XBC)), _const_spec((1, SSD_XBC)), _const_spec((8, LANES)),
                _const_spec((1, SSD_WIDTH)), _const_spec((1, SSD_WIDTH))]
    args = [z, xbc, dtp, cw, conv_b.reshape(1, SSD_XBC), hp, dsk, norm_g.reshape(1, SSD_WIDTH)]
    if has_s0:
        in_specs.append(st_spec)
        args.append(s0t)
    out_specs = [seq_spec(SSD_WIDTH)]
    out_shape = [jax.ShapeDtypeStruct((b, l, SSD_WIDTH), F32)]
    if want_final:
        out_specs.append(st_spec)
        out_shape.append(jax.ShapeDtypeStruct((b, 2, SSD_GROUPS, SSD_STATE, gw), F32))
    return pl.pallas_call(
        functools.partial(_ssd_body, seq=l, has_s0=has_s0, want_final=want_final),
        grid=(b,),
        in_specs=in_specs,
        out_specs=out_specs,
        out_shape=out_shape,
        scratch_shapes=[pltpu.VMEM((2, nc, SSD_GROUPS, SSD_STATE, gw), F32),
                        pltpu.VMEM((2, nc, 8, SSD_WIDTH), F32),
                        pltpu.VMEM((2, SSD_GROUPS, SSD_STATE, gw), F32),
                        pltpu.VMEM((SSD_CHUNK, SSD_WIDTH), F32),
                        pltpu.VMEM((l, SSD_XBC), F32),
                        pltpu.VMEM((2, l, LANES), F32),
                        pltpu.VMEM((nc, 4, SSD_HEADS, LANES), F32)],
        compiler_params=_cparams("arbitrary"),
        name="ssd",
    )(*args)


def _softmax_pv(parts):
    m = parts[0][0].max(axis=-1, keepdims=True)
    for s, _ in parts[1:]:
        m = jnp.maximum(m, s.max(axis=-1, keepdims=True))
    den = 0.0
    out = 0.0
    for s, v in parts:
        p = jnp.exp(s - m)
        den = den + p.sum(axis=-1, keepdims=True)
        out = out + _bdot(p, v)
    return out / den


def _ctx_attn_body(q_ref, k_ref, v_ref, o_ref):
    for h in range(NA_HEADS):
        hs = slice(h * NA_HEAD_DIM, (h + 1) * NA_HEAD_DIM)
        s = _bdot_nt(q_ref[0, :, hs], k_ref[0, :, hs]) * NA_SCALE
        o_ref[0, :, hs] = _softmax_pv([(s, v_ref[0, :, hs])])


def _ctx_attn(q, k, v):
    b, l, _ = q.shape
    spec = pl.BlockSpec((1, l, NA_WIDTH), lambda i: (i, 0, 0))
    return pl.pallas_call(
        _ctx_attn_body,
        grid=(b,),
        in_specs=[spec, spec, spec],
        out_specs=spec,
        out_shape=jax.ShapeDtypeStruct((b, l, NA_WIDTH), F32),
        compiler_params=_cparams("arbitrary"),
        name="ctx_attn",
    )(q, k, v)


def _na_bias_tables(rpb, rows):
    kr = min(NA_ROWS, rows)
    qi = np.arange(NA_QROWS)[:, None]
    wi = np.arange(NA_WIN)[None, :]
    qc = np.arange(GRID_W)[:, None]
    kc = np.arange(GRID_W)[None, :]
    cs = np.clip(qc - NA_COLS // 2, 0, GRID_W - NA_COLS)
    col_ok = (kc >= cs) & (kc < cs + NA_COLS)
    dc = np.clip(kc - qc + NA_COLS - 1, 0, 2 * NA_COLS - 2)
    col_sel = (dc[None] == np.arange(2 * NA_COLS - 1)[:, None, None]).astype(np.float32)
    row_sel, row_ok = [], []
    for r0 in (0, NA_QROWS, rows - NA_QROWS):
        ws = min(max(r0 - kr // 2, 0), rows - NA_WIN)
        r = r0 + qi
        krow = ws + wi
        rs = np.clip(r - kr // 2, 0, rows - kr)
        row_ok.append((krow >= rs) & (krow < rs + kr))
        dr = np.clip(krow - r + NA_ROWS - 1, 0, 2 * NA_ROWS - 2)
        row_sel.append((dr[..., None] == np.arange(2 * NA_ROWS - 1)).astype(np.float32))
    ok = np.stack(row_ok)[:, None, :, None, :, None] & col_ok[None, None, None, :, None, :]
    by_col = jnp.einsum('hab,bqk->haqk', rpb.astype(F32), jnp.asarray(col_sel), precision=HIGHEST)
    bias = jnp.einsum('viwa,haqk->vhiqwk', jnp.asarray(np.stack(row_sel)), by_col, precision=HIGHEST)
    return jnp.where(jnp.asarray(ok), bias, NEG_BIG).reshape(
        3, NA_HEADS, NA_QROWS * GRID_W, NA_WIN * GRID_W)


def _na_body(q_ref, k_ref, v_ref, kc_ref, vc_ref, bias_ref, o_ref, *, rows):
    rb = pl.program_id(1)
    ws = jnp.clip(rb * NA_QROWS - NA_ROWS // 2, 0, rows - NA_WIN)
    k0 = pl.multiple_of(ws * GRID_W, NA_QROWS * GRID_W)
    nk = NA_WIN * GRID_W
    for h in range(NA_HEADS):
        hs = slice(h * NA_HEAD_DIM, (h + 1) * NA_HEAD_DIM)
        qh = q_ref[0, :, hs]
        s_loc = _bdot_nt(qh, k_ref[0, pl.ds(k0, nk), hs]) * NA_SCALE + bias_ref[0, h]
        s_ctx = _bdot_nt(qh, kc_ref[0, :, hs]) * NA_SCALE
        o_ref[0, :, hs] = _softmax_pv([(s_loc, v_ref[0, pl.ds(k0, nk), hs]), (s_ctx, vc_ref[0, :, hs])])


def _latent_na(q, k, v, k_ctx, v_ctx, rpb):
    b, l, _ = q.shape
    lc = k_ctx.shape[1]
    rows = l // GRID_W
    assert rows >= NA_WIN and rows % NA_QROWS == 0 and NA_QROWS * 2 <= NA_ROWS
    nrb = rows // NA_QROWS
    tq = NA_QROWS * GRID_W
    bias = _na_bias_tables(rpb, rows)
    variant = lambda i, r: (jnp.where(r == 0, 0, jnp.where(r == nrb - 1, 2, 1)), 0, 0, 0)
    return pl.pallas_call(
        functools.partial(_na_body, rows=rows),
        grid=(b, nrb),
        in_specs=[pl.BlockSpec((1, tq, NA_WIDTH), lambda i, r: (i, r, 0)),
                  pl.BlockSpec((1, l, NA_WIDTH), lambda i, r: (i, 0, 0)),
                  pl.BlockSpec((1, l, NA_WIDTH), lambda i, r: (i, 0, 0)),
                  pl.BlockSpec((1, lc, NA_WIDTH), lambda i, r: (i, 0, 0)),
                  pl.BlockSpec((1, lc, NA_WIDTH), lambda i, r: (i, 0, 0)),
                  pl.BlockSpec((1, NA_HEADS, tq, NA_WIN * GRID_W), variant)],
        out_specs=pl.BlockSpec((1, tq, NA_WIDTH), lambda i, r: (i, r, 0)),
        out_shape=jax.ShapeDtypeStruct((b, l, NA_WIDTH), F32),
        compiler_params=_cparams("arbitrary", "arbitrary"),
        name="latent_na",
    )(q, k, v, k_ctx, v_ctx, bias)


def _s5_prep_body(lre_ref, lim_ref, lst_ref, btre_ref, btim_ref, cre_ref, cim_ref,
                  t_ref, g_ref, e_ref, pw_ref, g_s, et_s, e0_s):
    q, ch, p = S5_Q, S5_GROUP_CH, S5_STATE
    lane = lax.broadcasted_iota(jnp.int32, (ch, S5_CW), 1)
    bt_re, bt_im = btre_ref[0], btim_ref[0]
    c_re, c_im = cre_ref[0], cim_ref[0]
    taps = []
    for d in range(2):
        lam_re, lam_im = lre_ref[d, 0], lim_ref[d, 0]
        step = jnp.exp(lst_ref[d, 0])
        mag = jnp.exp(lam_re * step)
        a_re = mag * jnp.cos(lam_im * step)
        a_im = mag * jnp.sin(lam_im * step)
        den = lam_re * lam_re + lam_im * lam_im
        k_re = ((a_re - 1.0) * lam_re + a_im * lam_im) / den
        k_im = (a_im * lam_re - (a_re - 1.0) * lam_im) / den
        bb_re = k_re * bt_re - k_im * bt_im
        bb_im = k_re * bt_im + k_im * bt_re
        p_re = jnp.ones_like(a_re)
        p_im = jnp.zeros_like(a_re)
        for t in range(q):
            tg = q - 1 - t if d == 0 else t
            te = t if d == 0 else q - 1 - t
            g_s[tg * ch:(tg + 1) * ch, 2 * p * d:2 * p * d + p] = p_re * bb_re - p_im * bb_im
            g_s[tg * ch:(tg + 1) * ch, 2 * p * d + p:2 * p * (d + 1)] = p_re * bb_im + p_im * bb_re
            e0_s[te * ch:(te + 1) * ch, 0:p] = c_re * p_re - c_im * p_im
            e0_s[te * ch:(te + 1) * ch, p:2 * p] = -(c_re * p_im + c_im * p_re)
            p_re, p_im = p_re * a_re - p_im * a_im, p_re * a_im + p_im * a_re
            et_s[d, te * ch:(te + 1) * ch, 0:p] = c_re * p_re - c_im * p_im
            et_s[d, te * ch:(te + 1) * ch, p:2 * p] = -(c_re * p_im + c_im * p_re)
        bb = jnp.concatenate([bb_re, bb_im], axis=1)
        taps.append(lax.dot_general(bb, e0_s[...], (((1,), (1,)), ((), ())), precision=HIGHEST,
                                    preferred_element_type=F32))
        pw_ref[d, 0] = jnp.zeros(pw_ref.shape[2:], F32)
        for k in range(S5_LOG_MAX):
            pw_ref[d, 0, 2 * k:2 * k + 1, :] = jnp.concatenate([p_re, p_re], axis=1)
            pw_ref[d, 0, 2 * k + 1:2 * k + 2, :] = jnp.concatenate([-p_im, p_im], axis=1)
            p_re, p_im = p_re * p_re - p_im * p_im, 2.0 * p_re * p_im
    for ti in range(q):
        fwd = taps[0] if ti == 0 else pltpu.roll(taps[0], ch * ti, axis=1)
        bwd = taps[1] if ti == q - 1 else pltpu.roll(taps[1], S5_CW - ch * (q - 1 - ti), axis=1)
        row = jnp.where(lane >= ch * ti, fwd, 0.0) + jnp.where(lane < ch * (ti + 1), bwd, 0.0)
        t_ref[0, ti * ch:(ti + 1) * ch, :] = row.astype(t_ref.dtype)
    g_ref[0] = g_s[...].astype(g_ref.dtype)
    e_ref[0, 0:2 * p, :] = et_s[0].T.astype(e_ref.dtype)
    e_ref[0, 2 * p:4 * p, :] = et_s[1].T.astype(e_ref.dtype)


def _s5_tables(lam_re, lam_im, log_step, b_re, b_im, c_re, c_im):
    g, p, ch = S5_GROUPS, S5_STATE, S5_GROUP_CH
    row = lambda a: a.astype(F32).reshape(2, g, 1, p)
    lst = jnp.broadcast_to(log_step.astype(F32)[:, :, None, None], (2, g, 1, p))
    bt_re = jnp.swapaxes(b_re.astype(F32), 1, 2)
    bt_im = jnp.swapaxes(b_im.astype(F32), 1, 2)
    dspec = pl.BlockSpec((2, 1, 1, p), lambda i: (0, i, 0, 0))
    gspec = pl.BlockSpec((1, ch, p), lambda i: (i, 0, 0))
    tspec = pl.BlockSpec((1, S5_CW, S5_CW), lambda i: (i, 0, 0))
    tshape = jax.ShapeDtypeStruct((g, S5_CW, S5_CW), BF16)
    return pl.pallas_call(
        _s5_prep_body,
        grid=(g,),
        in_specs=[dspec, dspec, dspec, gspec, gspec, gspec, gspec],
        out_specs=[tspec, tspec, tspec, pl.BlockSpec((2, 1, 16, 2 * p), lambda i: (0, i, 0, 0))],
        out_shape=[tshape, tshape, tshape, jax.ShapeDtypeStruct((2, g, 16, 2 * p), F32)],
        scratch_shapes=[pltpu.VMEM((S5_CW, 4 * p), F32), pltpu.VMEM((2, S5_CW, 2 * p), F32),
                        pltpu.VMEM((S5_CW, 2 * p), F32)],
        compiler_params=_cparams("arbitrary"),
        name="s5_prep",
    )(row(lam_re), row(lam_im), lst, bt_re, bt_im, c_re.astype(F32), c_im.astype(F32))


def _cmul(a_full, a_sgn, s):
    return a_full * s + a_sgn * pltpu.roll(s, S5_STATE, axis=1)


def _s5_body(u_ref, perm_ref, t_ref, g_ref, e_ref, pw_ref, dsk_ref, *rest, nb, nc, has_s0,
             want_final):
    rest = list(rest)
    s0_ref = rest.pop(0) if has_s0 else None
    o_ref = rest.pop(0)
    fin_ref = rest.pop(0) if want_final else None
    ys, xs_f, xs_b, tok_s = rest
    m = nb * nc
    w = 2 * S5_STATE
    lane_blk = lax.broadcasted_iota(jnp.int32, (m, LANES), 1) // S5_GROUP_CH
    row_blk = (lax.broadcasted_iota(jnp.int32, (2 * LANES, S5_CW), 0) % LANES) // S5_GROUP_CH
    cidx = lax.broadcasted_iota(jnp.int32, (m, w), 0) & (nc - 1)
    first = cidx == 0
    last = cidx == nc - 1

    def tokens(t):
        return u_ref[:, pl.ds(t, nc, stride=S5_Q), :].reshape(m, LANES)

    for t in range(S5_Q):
        tok_s[t // 2, :, (t % 2) * LANES:(t % 2 + 1) * LANES] = tokens(t).astype(BF16)

    for gg in range(S5_LG):
        own = row_blk == gg
        parts = [None, None]
        for t2 in range(S5_Q // 2):
            p = jnp.where(own, perm_ref[2 * t2 - gg + S5_LG - 1], jnp.zeros((), BF16))
            d = jnp.dot(tok_s[t2], p, preferred_element_type=F32)
            parts[t2 % 2] = d if parts[t2 % 2] is None else parts[t2 % 2] + d
        ub = (parts[0] + parts[1]).astype(BF16)
        z = jnp.dot(ub, g_ref[gg], preferred_element_type=F32)
        x_f, x_b = z[:, :w], z[:, w:]
        pw = lambda d, r: pw_ref[d, gg, r:r + 1, :]
        if has_s0:
            rep = lambda a: jnp.broadcast_to(a[:, None, :], (nb, nc, w)).reshape(m, w)
            s0_f, s0_b = rep(s0_ref[0, gg, :, :w]), rep(s0_ref[0, gg, :, w:])
            x_f = x_f + jnp.where(first, _cmul(pw(0, 0), pw(0, 1), s0_f), 0.0)
            x_b = x_b + jnp.where(last, _cmul(pw(1, 0), pw(1, 1), s0_b), 0.0)
        for k in range(nc.bit_length() - 1):
            sh = 1 << k
            prev = pltpu.roll(x_f, sh, axis=0)
            x_f = x_f + jnp.where(cidx >= sh, _cmul(pw(0, 2 * k), pw(0, 2 * k + 1), prev), 0.0)
            nxt = pltpu.roll(x_b, m - sh, axis=0)
            x_b = x_b + jnp.where(cidx < nc - sh, _cmul(pw(1, 2 * k), pw(1, 2 * k + 1), nxt), 0.0)
        in_f = pltpu.roll(x_f, 1, axis=0)
        in_b = pltpu.roll(x_b, m - 1, axis=0)
        if has_s0:
            in_f = jnp.where(first, s0_f, in_f)
            in_b = jnp.where(last, s0_b, in_b)
        else:
            in_f = jnp.where(first, 0.0, in_f)
            in_b = jnp.where(last, 0.0, in_b)
        s_in = jnp.concatenate([in_f, in_b], axis=1).astype(BF16)
        ys[gg] = (jnp.dot(ub, t_ref[gg], preferred_element_type=F32)
                  + jnp.dot(s_in, e_ref[gg], preferred_element_type=F32))
        if want_final:
            xs_f[...] = x_f
            xs_b[...] = x_b
            fin_ref[0, gg, :, :w] = xs_f[pl.ds(nc - 1, nb, stride=nc), :]
            fin_ref[0, gg, :, w:] = xs_b[pl.ds(0, nb, stride=nc), :]

    for t in range(S5_Q):
        tb, tt = divmod(t, S5_LG)
        y = None
        for k in range(S5_LG):
            sh = (S5_GROUP_CH * (k - tt)) % LANES
            r = ys[k, :, tb * LANES:(tb + 1) * LANES]
            if sh:
                r = pltpu.roll(r, sh, axis=1)
            y = r if y is None else jnp.where(lane_blk == k, r, y)
        y = y + tokens(t) * dsk_ref[0]
        o_ref[:, pl.ds(t, nc, stride=S5_Q), :] = _gelu_tanh(y).reshape(nb, nc, LANES)


def _s5(u, tables, d_skip, s0, want_final):
    t_all, g_fb, e_fb, pows = tables
    b, l, _ = u.shape
    nc = l // S5_Q
    bb = min(b, max(1, S5_ROWS // nc))
    nbb = b // bb
    m = bb * nc
    has_s0 = s0 is not None
    seq_spec = pl.BlockSpec((bb, l, LANES), lambda i, j: (j, 0, i))
    tab_spec = pl.BlockSpec((S5_LG, S5_CW, S5_CW), lambda i, j: (i, 0, 0),
                            pipeline_mode=pl.Buffered(1))
    st_spec = pl.BlockSpec((1, S5_LG, bb, S5_CW), lambda i, j: (j, i, 0, 0))
    st_blocks = lambda a: jnp.transpose(a.reshape(S5_GROUPS, nbb, bb, S5_CW), (1, 0, 2, 3))
    r = np.arange(LANES)[:, None]
    cc = np.arange(S5_CW)[None, :]
    shift = lambda d: (cc - r == S5_GROUP_CH * d).astype(np.float32)
    perm = np.stack([np.concatenate([shift(d), shift(d + 1)], axis=0)
                     for d in range(1 - S5_LG, S5_Q - 1)])
    in_specs = [seq_spec, _const_spec(perm.shape), tab_spec, tab_spec, tab_spec,
                pl.BlockSpec((2, S5_LG, 16, 2 * S5_STATE), lambda i, j: (0, i, 0, 0)),
                pl.BlockSpec((1, 1, LANES), lambda i, j: (i, 0, 0))]
    args = [u, jnp.asarray(perm, BF16), t_all, g_fb, e_fb, pows,
            d_skip.astype(F32).reshape(S5_WIDTH // LANES, 1, LANES)]
    if has_s0:
        in_specs.append(st_spec)
        args.append(st_blocks(s0))
    out_specs = [seq_spec]
    out_shape = [jax.ShapeDtypeStruct((b, l, S5_WIDTH), F32)]
    if want_final:
        out_specs.append(st_spec)
        out_shape.append(jax.ShapeDtypeStruct((nbb, S5_GROUPS, bb, S5_CW), F32))
    res = pl.pallas_call(
        functools.partial(_s5_body, nb=bb, nc=nc, has_s0=has_s0, want_final=want_final),
        grid=(S5_WIDTH // LANES, nbb),
        in_specs=in_specs,
        out_specs=out_specs,
        out_shape=out_shape,
        scratch_shapes=[pltpu.VMEM((S5_LG, m, S5_CW), F32),
                        pltpu.VMEM((m, 2 * S5_STATE), F32), pltpu.VMEM((m, 2 * S5_STATE), F32),
                        pltpu.VMEM((S5_Q // 2, m, 2 * LANES), BF16)],
        compiler_params=_cparams("arbitrary", "arbitrary"),
        name="s5",
    )(*args)
    fin = None
    if want_final:
        fin = jnp.transpose(res[1], (1, 0, 2, 3)).reshape(S5_GROUPS, b, S5_CW)
    return res[0], fin


def _dft_cos_sin(n, rows=None, row_step=1):
    rows = n if rows is None else rows
    j = jnp.arange(rows, dtype=jnp.int32)[:, None] * row_step
    jk = (j * jnp.arange(n, dtype=jnp.int32)[None, :]) % n
    ang = jk.astype(F32) * (2.0 * math.pi / n)
    return jnp.cos(ang), jnp.sin(ang)


def _dft_cos_sin_split(n, minor):
    c1, s1 = _dft_cos_sin(n, n // minor, minor)
    c2, s2 = _dft_cos_sin(n, minor)
    cos = c1[:, None, :] * c2[None, :, :] - s1[:, None, :] * s2[None, :, :]
    sin = s1[:, None, :] * c2[None, :, :] + c1[:, None, :] * s2[None, :, :]
    return cos.reshape(n, n), sin.reshape(n, n)


def _fnet_body(u_ref, cs_ref, dl_ref, o_ref, xs, *, seq):
    @pl.when(pl.program_id(1) == 0)
    def _():
        xcs = jnp.dot(u_ref[0].astype(BF16), cs_ref[...], preferred_element_type=F32)
        xs[0:seq, :] = xcs[:, :FNET_WIDTH].astype(BF16)
        xs[seq:2 * seq, :] = xcs[:, FNET_WIDTH:].astype(BF16)

    scale = 1.0 / math.sqrt(seq * FNET_GROUP_CH)
    o_ref[0] = jnp.dot(dl_ref[...], xs[...], preferred_element_type=F32) * scale


def _fnet(u):
    b, l, _ = u.shape
    tr = min(l, 512)
    cc, sc = _dft_cos_sin(FNET_GROUP_CH)
    eye = jnp.eye(FNET_GROUPS, dtype=F32)
    cs = jnp.concatenate([jnp.kron(eye, cc), jnp.kron(eye, sc)], axis=1).astype(BF16)
    cl, sl = _dft_cos_sin_split(l, DFT_MINOR) if l > 4 * DFT_MINOR else _dft_cos_sin(l)
    dl = jnp.concatenate([cl, -sl], axis=1).astype(BF16)
    return pl.pallas_call(
        functools.partial(_fnet_body, seq=l),
        grid=(b, l // tr),
        in_specs=[pl.BlockSpec((1, l, FNET_WIDTH), lambda i, t: (i, 0, 0)),
                  _const_spec((FNET_WIDTH, 2 * FNET_WIDTH)),
                  pl.BlockSpec((tr, 2 * l), lambda i, t: (t, 0))],
        out_specs=pl.BlockSpec((1, tr, FNET_WIDTH), lambda i, t: (i, t, 0)),
        out_shape=jax.ShapeDtypeStruct((b, l, FNET_WIDTH), F32),
        scratch_shapes=[pltpu.VMEM((2 * l, FNET_WIDTH), BF16)],
        compiler_params=_cparams("arbitrary", "arbitrary"),
        name="fnet",
    )(u, cs, dl)


def _even_w_in(w):
    pad = jnp.zeros((D_MODEL, LANES - SSD_HEADS), w.dtype)
    return jnp.concatenate([w[:, :O_DT], w[:, O_Q:], w[:, O_DT:O_Q], pad], axis=1).astype(BF16)


EVEN_SPLITS = (SSD_WIDTH, SSD_XBC, NA_WIDTH, NA_WIDTH, NA_WIDTH, LANES)


def kernel(x_prompt, x_sample, cache_na_k, cache_na_v, state_ssd, state_s5, c, c_ctx, mod_w, mod_b, norm1_g, norm2_g, ffn_w1, ffn_w3, ffn_w2, final_g, ev_w_in, ev_conv_w, ev_conv_b, ev_a_log, ev_dt_bias, ev_d_skip, ev_ssd_norm_g, ev_rpb, ev_w_out, od_w_in, od_lam_re, od_lam_im, od_log_step, od_b_re, od_b_im, od_c_re, od_c_im, od_d_skip, od_glu_w, od_glu_b, od_w_out):
    bc, lc, _ = x_prompt.shape
    bl, ll, _ = x_sample.shape
    cond = jnp.zeros((MOD_ROWS, D_MODEL), F32).at[0].set(c_ctx).at[1:1 + bl].set(c)
    mods = _adaln(cond, mod_w, mod_b)
    xc = x_prompt.reshape(1, bc * lc, D_MODEL)
    xl = x_sample
    tm = 512
    new_k, new_v, new_ssd, new_s5 = [], [], [], []
    for i in range(DEPTH):
        j = i // 2
        mod = mods[i]
        last = i == DEPTH - 1
        if i % 2 == 0:
            w_in = _even_w_in(ev_w_in[j])
            w_out = ev_w_out[j].astype(BF16)
            ssd_args = (ev_conv_w[j], ev_conv_b[j], ev_a_log[j], ev_dt_bias[j], ev_d_skip[j],
                        ev_ssd_norm_g[j])
            z, xbc, q, k, v, dtp = _inproj(xc, mod, 0, norm1_g[i], w_in, EVEN_SPLITS, tm)
            seqs = lambda a: a.reshape(bc, lc, a.shape[-1])
            y_ssd, fin = _ssd(seqs(z), seqs(xbc), seqs(dtp), *ssd_args, None, True)
            o_na = _ctx_attn(seqs(q), seqs(k), seqs(v))
            flat = lambda a: a.reshape(1, bc * lc, a.shape[-1])
            mix_c = (flat(y_ssd), flat(o_na))
            new_k.append(k.reshape(bc, lc, NA_HEADS, NA_HEAD_DIM))
            new_v.append(v.reshape(bc, lc, NA_HEADS, NA_HEAD_DIM))
            new_ssd.append(_ssd_state_from_kernel(fin))

            z, xbc, q, k, v, dtp = _inproj(xl, mod, 1, norm1_g[i], w_in, EVEN_SPLITS, tm)
            s0t = _ssd_state_to_kernel(state_ssd[:, j].astype(F32))
            (y_ssd,) = _ssd(z, xbc, dtp, *ssd_args, s0t, False)
            pl_ = cache_na_k.shape[2]
            o_na = _latent_na(q, k, v, cache_na_k[:, j].reshape(bl, pl_, NA_WIDTH),
                              cache_na_v[:, j].reshape(bl, pl_, NA_WIDTH), ev_rpb[j])
            mix_l = (y_ssd, o_na)
            glu_w = glu_b = None
        else:
            w_in = od_w_in[j].astype(BF16)
            w_out = od_w_out[j].astype(BF16)
            glu_w, glu_b = od_glu_w[j].astype(BF16), od_glu_b[j]
            tables = _s5_tables(od_lam_re[j], od_lam_im[j], od_log_step[j], od_b_re[j], od_b_im[j],
                                od_c_re[j], od_c_im[j])
            splits = (S5_WIDTH, FNET_WIDTH)
            u_s, u_f = _inproj(xc, mod, 0, norm1_g[i], w_in, splits, tm)
            seqs = lambda a: a.reshape(bc, lc, a.shape[-1])
            g_s, fin = _s5(seqs(u_s), tables, od_d_skip[j], None, True)
            y_f = _fnet(seqs(u_f))
            flat = lambda a: a.reshape(1, bc * lc, a.shape[-1])
            mix_c = (flat(g_s), flat(y_f))
            fin = fin.reshape(S5_GROUPS, bc, 2, 2, S5_STATE)
            new_s5.append(jnp.transpose(fin, (1, 2, 0, 4, 3)))

            u_s, u_f = _inproj(xl, mod, 1, norm1_g[i], w_in, splits, tm)
            s0 = jnp.transpose(state_s5[:, j].astype(F32), (2, 0, 1, 4, 3)).reshape(
                S5_GROUPS, bl, 4 * S5_STATE)
            g_s, _ = _s5(u_s, tables, od_d_skip[j], s0, False)
            y_f = _fnet(u_f)
            mix_l = (g_s, y_f)
        w_o = (w_out[:SSD_WIDTH], w_out[SSD_WIDTH:])
        w1, w3, w2 = ffn_w1[i].astype(BF16), ffn_w3[i].astype(BF16), ffn_w2[i].astype(BF16)
        fg = final_g if last else None
        xc = _ffn(xc, mod, 0, *mix_c, *w_o, glu_w, glu_b, norm2_g[i], w1, w3, w2, fg, tm)
        xl = _ffn(xl, mod, 1, *mix_l, *w_o, glu_w, glu_b, norm2_g[i], w1, w3, w2, fg, tm)
    return (xc.reshape(bc, lc, D_MODEL), xl,
            jnp.stack(new_k, axis=1), jnp.stack(new_v, axis=1),
            jnp.stack(new_ssd, axis=1), jnp.stack(new_s5, axis=1))
```

```python
import functools
import math

import numpy as np
import jax
import jax.numpy as jnp
from jax import lax
from jax.experimental import pallas as pl
from jax.experimental.pallas import tpu as pltpu

F32 = jnp.float32
BF16 = jnp.bfloat16
HIGHEST = lax.Precision.HIGHEST

D_MODEL = 1024
DEPTH = 2
EPS = 1e-6
GRID_W = 64
FFN_HIDDEN = 2816
FFN_CHUNK = 1408

SSD_HEADS = 8
SSD_HEAD_DIM = 64
SSD_WIDTH = 512
SSD_GROUPS = 2
SSD_HPG = SSD_HEADS // SSD_GROUPS
SSD_STATE = 64
SSD_GN = SSD_GROUPS * SSD_STATE
SSD_XBC = SSD_WIDTH + 2 * SSD_GN
SSD_CHUNK = 128
CONV_W = 5
CONV_HALO = 8

NA_HEADS = 8
NA_HEAD_DIM = 64
NA_WIDTH = 512
NA_ROWS = 8
NA_COLS = 16
NA_SCALE = NA_HEAD_DIM ** -0.5
NA_QROWS = 4
NA_WIN = NA_ROWS + NA_QROWS
NEG_BIG = -1e30

O_XBC = SSD_WIDTH
O_DT = O_XBC + SSD_XBC
O_Q = O_DT + SSD_HEADS
O_K = O_Q + NA_WIDTH
O_V = O_K + NA_WIDTH
LANES = 128

S5_GROUP_CH = 16
S5_GROUPS = 32
S5_WIDTH = 512
S5_STATE = 64
S5_Q = 16
S5_CW = S5_Q * S5_GROUP_CH
S5_LOG_MAX = 7
S5_LG = LANES // S5_GROUP_CH
S5_ROWS = 512

FNET_GROUP_CH = 64
FNET_GROUPS = 8
FNET_WIDTH = 512
DFT_MINOR = 64

MOD_ROWS = 16
VMEM_LIMIT = 56 * 1024 * 1024


def _cparams(*sem):
    return pltpu.CompilerParams(dimension_semantics=sem, vmem_limit_bytes=VMEM_LIMIT)


def _const_spec(shape):
    nd = len(shape)
    return pl.BlockSpec(shape, lambda *_: (0,) * nd, pipeline_mode=pl.Buffered(1))


def _sigmoid(x):
    return 1.0 / (1.0 + jnp.exp(-x))


def _silu(x):
    return x * _sigmoid(x)


def _softplus(x):
    return jnp.maximum(x, 0.0) + jnp.log(1.0 + jnp.exp(-jnp.abs(x)))


def _gelu_tanh(x):
    return 0.5 * x * (1.0 + jnp.tanh(math.sqrt(2.0 / math.pi) * (x + 0.044715 * (x * x * x))))


def _rmsnorm(x, g):
    return x * lax.rsqrt(jnp.mean(x * x, axis=-1, keepdims=True) + EPS) * g


def _bdot(a, b):
    return jnp.dot(a.astype(BF16), b.astype(BF16), preferred_element_type=F32)


def _bdot_nt(a, b):
    return lax.dot_general(a.astype(BF16), b.astype(BF16), (((1,), (1,)), ((), ())),
                           preferred_element_type=F32)


def _bdot_tn(a, b):
    return lax.dot_general(a.astype(BF16), b.astype(BF16), (((0,), (0,)), ((), ())),
                           preferred_element_type=F32)


def _adaln_body(c_ref, w_ref, b_ref, o_ref):
    s = _silu(c_ref[...])
    o_ref[0] = jnp.dot(s, w_ref[0], precision=HIGHEST, preferred_element_type=F32) + b_ref[0]


def _adaln(cond, mod_w, mod_b):
    n = 6 * D_MODEL
    tn = 1536
    out = pl.pallas_call(
        _adaln_body,
        grid=(DEPTH, n // tn),
        in_specs=[pl.BlockSpec((MOD_ROWS, D_MODEL), lambda i, j: (0, 0)),
                  pl.BlockSpec((1, D_MODEL, tn), lambda i, j: (i, 0, j)),
                  pl.BlockSpec((1, 1, tn), lambda i, j: (i, 0, j))],
        out_specs=pl.BlockSpec((1, MOD_ROWS, tn), lambda i, j: (i, 0, j)),
        out_shape=jax.ShapeDtypeStruct((DEPTH, MOD_ROWS, n), F32),
        compiler_params=_cparams("arbitrary", "arbitrary"),
        name="adaln",
    )(cond, mod_w, mod_b.reshape(DEPTH, 1, n))
    return out.reshape(DEPTH, MOD_ROWS, 6, D_MODEL)


def _mod_spec(mrow0):
    return pl.BlockSpec((1, 6, D_MODEL), lambda b, t: (b + mrow0, 0, 0))


def _inproj_body(x_ref, mod_ref, g_ref, w_ref, *o_refs, splits):
    h = _rmsnorm(x_ref[0], g_ref[...]) * (1.0 + mod_ref[0, 1:2, :]) + mod_ref[0, 0:1, :]
    acc = jnp.dot(h.astype(BF16), w_ref[...], preferred_element_type=F32)
    off = 0
    for o_ref, n in zip(o_refs, splits):
        o_ref[0] = acc[:, off:off + n].astype(o_ref.dtype)
        off += n


def _inproj(x, mod, mrow0, g, w, splits, tm, dtypes=None):
    b, l, _ = x.shape
    n = w.shape[1]
    dtypes = (F32,) * len(splits) if dtypes is None else dtypes
    return pl.pallas_call(
        functools.partial(_inproj_body, splits=splits),
        grid=(b, l // tm),
        in_specs=[pl.BlockSpec((1, tm, D_MODEL), lambda i, t: (i, t, 0)),
                  _mod_spec(mrow0),
                  _const_spec((1, D_MODEL)),
                  _const_spec((D_MODEL, n))],
        out_specs=[pl.BlockSpec((1, tm, s), lambda i, t: (i, t, 0)) for s in splits],
        out_shape=[jax.ShapeDtypeStruct((b, l, s), dt) for s, dt in zip(splits, dtypes)],
        compiler_params=_cparams("arbitrary", "arbitrary"),
        name="inproj",
    )(x, mod, g.reshape(1, D_MODEL), w)


def _ffn_body(x_ref, mod_ref, ya_ref, yb_ref, wa_ref, wb_ref, g_ref, w1_ref, w3_ref, w2_ref, *rest,
              glu, final):
    rest = list(rest)
    ya = ya_ref[0]
    if glu:
        gw_ref, gb_ref = rest.pop(0), rest.pop(0)
        ya = ya * _sigmoid(_bdot(ya, gw_ref[...]) + gb_ref[...])
    mix = _bdot(ya, wa_ref[...]) + _bdot(yb_ref[0], wb_ref[...])
    x = x_ref[0] + mod_ref[0, 2:3, :] * mix
    h = (_rmsnorm(x, g_ref[...]) * (1.0 + mod_ref[0, 4:5, :]) + mod_ref[0, 3:4, :]).astype(BF16)
    acc = jnp.zeros(x.shape, F32)
    for c in range(FFN_HIDDEN // FFN_CHUNK):
        cols = slice(c * FFN_CHUNK, (c + 1) * FFN_CHUNK)
        a = jnp.dot(h, w1_ref[:, cols], preferred_element_type=F32)
        u = _silu(a) * jnp.dot(h, w3_ref[:, cols], preferred_element_type=F32)
        acc = acc + jnp.dot(u.astype(BF16), w2_ref[cols, :], preferred_element_type=F32)
    y = x + mod_ref[0, 5:6, :] * acc
    if final:
        fg_ref, o_ref = rest
        y = _rmsnorm(y, fg_ref[...])
    else:
        (o_ref,) = rest
    o_ref[0] = y


def _ffn(x, mod, mrow0, ya, yb, wa, wb, glu_w, glu_b, g, w1, w3, w2, final_g, tm):
    b, l, _ = x.shape
    wa_n, wb_n = ya.shape[-1], yb.shape[-1]
    glu = glu_w is not None
    final = final_g is not None
    row_spec = lambda n: pl.BlockSpec((1, tm, n), lambda i, t: (i, t, 0))
    in_specs = [row_spec(D_MODEL), _mod_spec(mrow0), row_spec(wa_n), row_spec(wb_n),
                _const_spec((wa_n, D_MODEL)), _const_spec((wb_n, D_MODEL)),
                _const_spec((1, D_MODEL)),
                _const_spec((D_MODEL, FFN_HIDDEN)),
                _const_spec((D_MODEL, FFN_HIDDEN)),
                _const_spec((FFN_HIDDEN, D_MODEL))]
    args = [x, mod, ya, yb, wa, wb, g.reshape(1, D_MODEL), w1, w3, w2]
    if glu:
        in_specs += [_const_spec((wa_n, wa_n)), _const_spec((1, wa_n))]
        args += [glu_w, glu_b.reshape(1, wa_n)]
    if final:
        in_specs.append(_const_spec((1, D_MODEL)))
        args.append(final_g.reshape(1, D_MODEL))
    return pl.pallas_call(
        functools.partial(_ffn_body, glu=glu, final=final),
        grid=(b, l // tm),
        in_specs=in_specs,
        out_specs=pl.BlockSpec((1, tm, D_MODEL), lambda i, t: (i, t, 0)),
        out_shape=jax.ShapeDtypeStruct((b, l, D_MODEL), F32),
        compiler_params=_cparams("arbitrary", "arbitrary"),
        name="ffn",
    )(*args)


def _ssd_body(z_ref, xbc_ref, dt_ref, cw_ref, cb_ref, hp_ref, dsk_ref, ng_ref, *rest,
              seq, has_s0, want_final):
    rest = list(rest)
    s0_ref = rest.pop(0) if has_s0 else None
    y_ref = rest.pop(0)
    fin_ref = rest.pop(0) if want_final else None
    st_s, cd_s, scur, ych, xc_s, cum_s, tr_s = rest
    q = SSD_CHUNK
    nc = seq // q
    gw = SSD_HPG * SSD_HEAD_DIM
    expand = (lax.broadcasted_iota(jnp.int32, (LANES, SSD_WIDTH), 1) // SSD_HEAD_DIM
              == lax.broadcasted_iota(jnp.int32, (LANES, SSD_WIDTH), 0)).astype(BF16)

    def per_head(v):
        hi = v.astype(BF16)
        lo = (v - hi.astype(F32)).astype(BF16)
        return (jnp.dot(hi, expand, preferred_element_type=F32)
                + jnp.dot(lo, expand, preferred_element_type=F32))
    row = lax.broadcasted_iota(jnp.int32, (q, q), 0)
    col = lax.broadcasted_iota(jnp.int32, (q, q), 1)
    lower = row >= col
    upper = col >= row
    tri_l = lower.astype(F32)
    tri_u = upper.astype(F32)
    a_f = -jnp.exp(hp_ref[0:1, :])
    a_b = -jnp.exp(hp_ref[1:2, :])
    lane = lax.broadcasted_iota(jnp.int32, (1, LANES), 1)
    a_f = jnp.where(lane < SSD_HEADS, a_f, 0.0)
    a_b = jnp.where(lane < SSD_HEADS, a_b, 0.0)
    bias_f = hp_ref[2:3, :]
    bias_b = hp_ref[3:4, :]

    def chunk_pre(c):
        r0 = pl.multiple_of(c * q, q)
        lo = pl.multiple_of(jnp.maximum(r0 - CONV_HALO, 0), CONV_HALO)
        hi = pl.multiple_of(jnp.minimum(r0 + q, seq - CONV_HALO), CONV_HALO)
        prev = jnp.where(c > 0, xbc_ref[0, pl.ds(lo, CONV_HALO), :], 0.0)
        nxt = jnp.where(c < nc - 1, xbc_ref[0, pl.ds(hi, CONV_HALO), :], 0.0)
        win = jnp.concatenate([prev, xbc_ref[0, pl.ds(r0, q), :], nxt], axis=0)
        acc = cb_ref[...] + cw_ref[0:1, :] * win[CONV_HALO - 2:CONV_HALO - 2 + q]
        for k in range(1, CONV_W):
            o = CONV_HALO - CONV_W // 2 + k
            acc = acc + cw_ref[k:k + 1, :] * win[o:o + q]
        xc = _silu(acc)
        dtr = dt_ref[0, pl.ds(r0, q), :]
        dt_f = _softplus(dtr + bias_f)
        dt_b = _softplus(dtr + bias_b)
        cum_f = jnp.dot(tri_l, dt_f * a_f, precision=HIGHEST, preferred_element_type=F32)
        cum_b = jnp.dot(tri_u, dt_b * a_b, precision=HIGHEST, preferred_element_type=F32)
        return r0, xc, dt_f, dt_b, cum_f, cum_b

    def pass_a(c, carry):
        r0, xc, dt_f, dt_b, cum_f, cum_b = chunk_pre(c)
        xc_s[pl.ds(r0, q), :] = xc
        cum_s[0, pl.ds(r0, q), :] = cum_f
        cum_s[1, pl.ds(r0, q), :] = cum_b
        for i, v in enumerate((cum_f, cum_b, dt_f, dt_b)):
            tr_s[c, i] = v.T[0:SSD_HEADS, :]
        end_f = cum_f[q - 1:q, :]
        end_b = cum_b[0:1, :]
        xw_f = xc[:, :SSD_WIDTH] * per_head(jnp.exp(end_f - cum_f) * dt_f)
        xw_b = xc[:, :SSD_WIDTH] * per_head(jnp.exp(end_b - cum_b) * dt_b)
        cd_s[0, c] = per_head(jnp.broadcast_to(jnp.exp(end_f), (8, LANES)))
        cd_s[1, c] = per_head(jnp.broadcast_to(jnp.exp(end_b), (8, LANES)))
        for g in range(SSD_GROUPS):
            bg = xc[:, SSD_WIDTH + g * SSD_STATE:SSD_WIDTH + (g + 1) * SSD_STATE]
            st_s[0, c, g] = _bdot_tn(bg, xw_f[:, g * gw:(g + 1) * gw])
            st_s[1, c, g] = _bdot_tn(bg, xw_b[:, g * gw:(g + 1) * gw])
        return carry

    lax.fori_loop(0, nc, pass_a, 0)

    if has_s0:
        scur[...] = s0_ref[0]
    else:
        scur[...] = jnp.zeros(scur.shape, F32)

    def pass_b(c, carry):
        cr = nc - 1 - c
        for g in range(SSD_GROUPS):
            s_in = scur[0, g]
            scur[0, g] = s_in * cd_s[0, c, 0:1, g * gw:(g + 1) * gw] + st_s[0, c, g]
            st_s[0, c, g] = s_in
            s_in = scur[1, g]
            scur[1, g] = s_in * cd_s[1, cr, 0:1, g * gw:(g + 1) * gw] + st_s[1, cr, g]
            st_s[1, cr, g] = s_in
        return carry

    lax.fori_loop(0, nc, pass_b, 0)
    if want_final:
        fin_ref[0] = scur[...]

    def pass_c(c, carry):
        r0 = pl.multiple_of(c * q, q)
        xc = xc_s[pl.ds(r0, q), :]
        cum_f = cum_s[0, pl.ds(r0, q), :]
        cum_b = cum_s[1, pl.ds(r0, q), :]
        cum_ft, cum_bt, dt_ft, dt_bt = tr_s[c, 0], tr_s[c, 1], tr_s[c, 2], tr_s[c, 3]
        ex_f = per_head(jnp.exp(cum_f))
        ex_b = per_head(jnp.exp(cum_b))
        y_off = []
        cbs = []
        for g in range(SSD_GROUPS):
            gs = slice(g * gw, (g + 1) * gw)
            bg = xc[:, SSD_WIDTH + g * SSD_STATE:SSD_WIDTH + (g + 1) * SSD_STATE]
            cg = xc[:, SSD_WIDTH + SSD_GN + g * SSD_STATE:SSD_WIDTH + SSD_GN + (g + 1) * SSD_STATE]
            cbs.append(_bdot_nt(cg, bg))
            y_off.append(ex_f[:, gs] * _bdot(cg, st_s[0, c, g]) + ex_b[:, gs] * _bdot(cg, st_s[1, c, g]))
        for h in range(SSD_HEADS):
            xh = xc[:, h * SSD_HEAD_DIM:(h + 1) * SSD_HEAD_DIM]
            seg_f = cum_f[:, h:h + 1] - cum_ft[h:h + 1, :]
            seg_b = cum_b[:, h:h + 1] - cum_bt[h:h + 1, :]
            m_f = jnp.exp(jnp.where(lower, seg_f, NEG_BIG)) * dt_ft[h:h + 1, :]
            m_b = jnp.exp(jnp.where(upper, seg_b, NEG_BIG)) * dt_bt[h:h + 1, :]
            ych[:, h * SSD_HEAD_DIM:(h + 1) * SSD_HEAD_DIM] = _bdot(cbs[h // SSD_HPG] * (m_f + m_b), xh)
        yf = ych[...] + jnp.concatenate(y_off, axis=1) + dsk_ref[...] * xc[:, :SSD_WIDTH]
        yf = yf * _silu(z_ref[0, pl.ds(r0, q), :])
        y_ref[0, pl.ds(r0, q), :] = _rmsnorm(yf, ng_ref[...])
        return carry

    lax.fori_loop(0, nc, pass_c, 0)


def _ssd_state_to_kernel(s):
    b = s.shape[0]
    s = s.reshape(b, 2, SSD_GROUPS, SSD_HPG, SSD_HEAD_DIM, SSD_STATE)
    return jnp.transpose(s, (0, 1, 2, 5, 3, 4)).reshape(b, 2, SSD_GROUPS, SSD_STATE, SSD_HPG * SSD_HEAD_DIM)


def _ssd_state_from_kernel(s):
    b = s.shape[0]
    s = s.reshape(b, 2, SSD_GROUPS, SSD_STATE, SSD_HPG, SSD_HEAD_DIM)
    return jnp.transpose(s, (0, 1, 2, 4, 5, 3)).reshape(b, 2, SSD_HEADS, SSD_HEAD_DIM, SSD_STATE)


def _ssd(z, xbc, dtp, conv_w, conv_b, a_log, dt_bias, d_skip, norm_g, s0t, want_final):
    b, l, _ = z.shape
    nc = l // SSD_CHUNK
    gw = SSD_HPG * SSD_HEAD_DIM
    has_s0 = s0t is not None
    cw = jnp.zeros((8, SSD_XBC), F32).at[:CONV_W].set(conv_w)
    hp = jnp.zeros((8, LANES), F32)
    hp = hp.at[0:2, :SSD_HEADS].set(a_log).at[2:4, :SSD_HEADS].set(dt_bias)
    dsk = jnp.repeat(d_skip, SSD_HEAD_DIM).reshape(1, SSD_WIDTH)
    seq_spec = lambda n: pl.BlockSpec((1, l, n), lambda i: (i, 0, 0))
    st_spec = pl.BlockSpec((1, 2, SSD_GROUPS, SSD_STATE, gw), lambda i: (i, 0, 0, 0, 0))
    in_specs = [seq_spec(SSD_WIDTH), seq_spec(SSD_XBC), seq_spec(LANES),
                _const_spec((8, SSD_XBC)), _const_spec((1, SSD_XBC)), _const_spec((8, LANES)),
                _const_spec((1, SSD_WIDTH)), _const_spec((1, SSD_WIDTH))]
    args = [z, xbc, dtp, cw, conv_b.reshape(1, SSD_XBC), hp, dsk, norm_g.reshape(1, SSD_WIDTH)]
    if has_s0:
        in_specs.append(st_spec)
        args.append(s0t)
    out_specs = [seq_spec(SSD_WIDTH)]
    out_shape = [jax.ShapeDtypeStruct((b, l, SSD_WIDTH), F32)]
    if want_final:
        out_specs.append(st_spec)
        out_shape.append(jax.ShapeDtypeStruct((b, 2, SSD_GROUPS, SSD_STATE, gw), F32))
    return pl.pallas_call(
        functools.partial(_ssd_body, seq=l, has_s0=has_s0, want_final=want_final),
        grid=(b,),
        in_specs=in_specs,
        out_specs=out_specs,
        out_shape=out_shape,
        scratch_shapes=[pltpu.VMEM((2, nc, SSD_GROUPS, SSD_STATE, gw), F32),
                        pltpu.VMEM((2, nc, 8, SSD_WIDTH), F32),
                        pltpu.VMEM((2, SSD_GROUPS, SSD_STATE, gw), F32),
                        pltpu.VMEM((SSD_CHUNK, SSD_WIDTH), F32),
                        pltpu.VMEM((l, SSD_XBC), F32),
                        pltpu.VMEM((2, l, LANES), F32),
                        pltpu.VMEM((nc, 4, SSD_HEADS, LANES), F32)],
        compiler_params=_cparams("arbitrary"),
        name="ssd",
    )(*args)


def _softmax_pv(parts):
    m = parts[0][0].max(axis=-1, keepdims=True)
    for s, _ in parts[1:]:
        m = jnp.maximum(m, s.max(axis=-1, keepdims=True))
    den = 0.0
    out = 0.0
    for s, v in parts:
        p = jnp.exp(s - m)
        den = den + p.sum(axis=-1, keepdims=True)
        out = out + _bdot(p, v)
    return out / den


def _ctx_attn_body(q_ref, k_ref, v_ref, o_ref):
    for h in range(NA_HEADS):
        hs = slice(h * NA_HEAD_DIM, (h + 1) * NA_HEAD_DIM)
        s = _bdot_nt(q_ref[0, :, hs], k_ref[0, :, hs]) * NA_SCALE
        o_ref[0, :, hs] = _softmax_pv([(s, v_ref[0, :, hs])])


def _ctx_attn(q, k, v):
    b, l, _ = q.shape
    spec = pl.BlockSpec((1, l, NA_WIDTH), lambda i: (i, 0, 0))
    return pl.pallas_call(
        _ctx_attn_body,
        grid=(b,),
        in_specs=[spec, spec, spec],
        out_specs=spec,
        out_shape=jax.ShapeDtypeStruct((b, l, NA_WIDTH), F32),
        compiler_params=_cparams("arbitrary"),
        name="ctx_attn",
    )(q, k, v)


def _na_bias_tables(rpb, rows):
    qc = np.arange(GRID_W)[:, None]
    kc = np.arange(GRID_W)[None, :]
    cs = np.clip(qc - NA_COLS // 2, 0, GRID_W - NA_COLS)
    col_ok = (kc >= cs) & (kc < cs + NA_COLS)
    dc = np.clip(kc - qc + NA_COLS - 1, 0, 2 * NA_COLS - 2)
    col_sel = (dc[None] == np.arange(2 * NA_COLS - 1)[:, None, None]).astype(np.float32)
    by_col = jnp.einsum('hab,bqk->haqk', rpb.astype(F32), jnp.asarray(col_sel), precision=HIGHEST)
    by_col = jnp.where(jnp.asarray(col_ok), by_col, NEG_BIG)
    n_dr = 2 * NA_ROWS - 1
    return pl.pallas_call(
        functools.partial(_na_bias_body, rows=rows),
        grid=(3, NA_HEADS),
        in_specs=[pl.BlockSpec((1, n_dr, GRID_W, GRID_W), lambda v, h: (h, 0, 0, 0))],
        out_specs=pl.BlockSpec((1, 1, NA_QROWS * GRID_W, NA_WIN * GRID_W), lambda v, h: (v, h, 0, 0)),
        out_shape=jax.ShapeDtypeStruct((3, NA_HEADS, NA_QROWS * GRID_W, NA_WIN * GRID_W), F32),
        compiler_params=_cparams("arbitrary", "arbitrary"),
        name="na_bias",
    )(by_col)


def _na_bias_body(u_ref, o_ref, *, rows):
    v = pl.program_id(0)
    kr = min(NA_ROWS, rows)
    r0 = jnp.where(v == 0, 0, jnp.where(v == 1, NA_QROWS, rows - NA_QROWS))
    ws = jnp.clip(r0 - kr // 2, 0, rows - NA_WIN)
    for qi in range(NA_QROWS):
        r = r0 + qi
        rs = jnp.clip(r - kr // 2, 0, rows - kr)
        for w in range(NA_WIN):
            krow = ws + w
            ok = (krow >= rs) & (krow < rs + kr)
            dr = jnp.clip(krow - r + NA_ROWS - 1, 0, 2 * NA_ROWS - 2)
            o_ref[0, 0, qi * GRID_W:(qi + 1) * GRID_W, w * GRID_W:(w + 1) * GRID_W] = jnp.where(
                ok, u_ref[0, dr], NEG_BIG)


def _na_body(q_ref, k_ref, v_ref, kc_ref, vc_ref, bias_ref, o_ref, *, rows):
    rb = pl.program_id(1)
    ws = jnp.clip(rb * NA_QROWS - NA_ROWS // 2, 0, rows - NA_WIN)
    k0 = pl.multiple_of(ws * GRID_W, NA_QROWS * GRID_W)
    nk = NA_WIN * GRID_W
    for h in range(NA_HEADS):
        hs = slice(h * NA_HEAD_DIM, (h + 1) * NA_HEAD_DIM)
        qh = q_ref[0, :, hs]
        s_loc = _bdot_nt(qh, k_ref[0, pl.ds(k0, nk), hs]) * NA_SCALE + bias_ref[0, h]
        s_ctx = _bdot_nt(qh, kc_ref[0, :, hs]) * NA_SCALE
        o_ref[0, :, hs] = _softmax_pv([(s_loc, v_ref[0, pl.ds(k0, nk), hs]), (s_ctx, vc_ref[0, :, hs])])


def _latent_na(q, k, v, k_ctx, v_ctx, rpb):
    b, l, _ = q.shape
    lc = k_ctx.shape[1]
    rows = l // GRID_W
    assert rows >= NA_WIN and rows % NA_QROWS == 0 and NA_QROWS * 2 <= NA_ROWS
    nrb = rows // NA_QROWS
    tq = NA_QROWS * GRID_W
    bias = _na_bias_tables(rpb, rows)
    variant = lambda i, r: (jnp.where(r == 0, 0, jnp.where(r == nrb - 1, 2, 1)), 0, 0, 0)
    return pl.pallas_call(
        functools.partial(_na_body, rows=rows),
        grid=(b, nrb),
        in_specs=[pl.BlockSpec((1, tq, NA_WIDTH), lambda i, r: (i, r, 0)),
                  pl.BlockSpec((1, l, NA_WIDTH), lambda i, r: (i, 0, 0)),
                  pl.BlockSpec((1, l, NA_WIDTH), lambda i, r: (i, 0, 0)),
                  pl.BlockSpec((1, lc, NA_WIDTH), lambda i, r: (i, 0, 0)),
                  pl.BlockSpec((1, lc, NA_WIDTH), lambda i, r: (i, 0, 0)),
                  pl.BlockSpec((1, NA_HEADS, tq, NA_WIN * GRID_W), variant)],
        out_specs=pl.BlockSpec((1, tq, NA_WIDTH), lambda i, r: (i, r, 0)),
        out_shape=jax.ShapeDtypeStruct((b, l, NA_WIDTH), F32),
        compiler_params=_cparams("arbitrary", "arbitrary"),
        name="latent_na",
    )(q, k, v, k_ctx, v_ctx, bias)


def _s5_prep_body(lre_ref, lim_ref, lst_ref, btre_ref, btim_ref, cre_ref, cim_ref,
                  t_ref, g_ref, e_ref, pw_ref, g_s, et_s, e0_s):
    q, ch, p = S5_Q, S5_GROUP_CH, S5_STATE
    lane = lax.broadcasted_iota(jnp.int32, (ch, S5_CW), 1)
    bt_re, bt_im = btre_ref[0], btim_ref[0]
    c_re, c_im = cre_ref[0], cim_ref[0]
    taps = []
    for d in range(2):
        lam_re, lam_im = lre_ref[d, 0], lim_ref[d, 0]
        step = jnp.exp(lst_ref[d, 0])
        mag = jnp.exp(lam_re * step)
        a_re = mag * jnp.cos(lam_im * step)
        a_im = mag * jnp.sin(lam_im * step)
        den = lam_re * lam_re + lam_im * lam_im
        k_re = ((a_re - 1.0) * lam_re + a_im * lam_im) / den
        k_im = (a_im * lam_re - (a_re - 1.0) * lam_im) / den
        bb_re = k_re * bt_re - k_im * bt_im
        bb_im = k_re * bt_im + k_im * bt_re
        p_re = jnp.ones_like(a_re)
        p_im = jnp.zeros_like(a_re)
        for t in range(q):
            tg = q - 1 - t if d == 0 else t
            te = t if d == 0 else q - 1 - t
            g_s[tg * ch:(tg + 1) * ch, 2 * p * d:2 * p * d + p] = p_re * bb_re - p_im * bb_im
            g_s[tg * ch:(tg + 1) * ch, 2 * p * d + p:2 * p * (d + 1)] = p_re * bb_im + p_im * bb_re
            e0_s[te * ch:(te + 1) * ch, 0:p] = c_re * p_re - c_im * p_im
            e0_s[te * ch:(te + 1) * ch, p:2 * p] = -(c_re * p_im + c_im * p_re)
            p_re, p_im = p_re * a_re - p_im * a_im, p_re * a_im + p_im * a_re
            et_s[d, te * ch:(te + 1) * ch, 0:p] = c_re * p_re - c_im * p_im
            et_s[d, te * ch:(te + 1) * ch, p:2 * p] = -(c_re * p_im + c_im * p_re)
        bb = jnp.concatenate([bb_re, bb_im], axis=1)
        taps.append(lax.dot_general(bb, e0_s[...], (((1,), (1,)), ((), ())), precision=HIGHEST,
                                    preferred_element_type=F32))
        pw_ref[d, 0] = jnp.zeros(pw_ref.shape[2:], F32)
        for k in range(S5_LOG_MAX):
            pw_ref[d, 0, 2 * k:2 * k + 1, :] = jnp.concatenate([p_re, p_re], axis=1)
            pw_ref[d, 0, 2 * k + 1:2 * k + 2, :] = jnp.concatenate([-p_im, p_im], axis=1)
            p_re, p_im = p_re * p_re - p_im * p_im, 2.0 * p_re * p_im
    for ti in range(q):
        fwd = taps[0] if ti == 0 else pltpu.roll(taps[0], ch * ti, axis=1)
        bwd = taps[1] if ti == q - 1 else pltpu.roll(taps[1], S5_CW - ch * (q - 1 - ti), axis=1)
        row = jnp.where(lane >= ch * ti, fwd, 0.0) + jnp.where(lane < ch * (ti + 1), bwd, 0.0)
        t_ref[0, ti * ch:(ti + 1) * ch, :] = row.astype(t_ref.dtype)
    g_ref[0] = g_s[...].astype(g_ref.dtype)
    e_ref[0, 0:2 * p, :] = et_s[0].T.astype(e_ref.dtype)
    e_ref[0, 2 * p:4 * p, :] = et_s[1].T.astype(e_ref.dtype)


def _s5_tables(lam_re, lam_im, log_step, b_re, b_im, c_re, c_im):
    g, p, ch = S5_GROUPS, S5_STATE, S5_GROUP_CH
    row = lambda a: a.astype(F32).reshape(2, g, 1, p)
    lst = jnp.broadcast_to(log_step.astype(F32)[:, :, None, None], (2, g, 1, p))
    bt_re = jnp.swapaxes(b_re.astype(F32), 1, 2)
    bt_im = jnp.swapaxes(b_im.astype(F32), 1, 2)
    dspec = pl.BlockSpec((2, 1, 1, p), lambda i: (0, i, 0, 0))
    gspec = pl.BlockSpec((1, ch, p), lambda i: (i, 0, 0))
    tspec = pl.BlockSpec((1, S5_CW, S5_CW), lambda i: (i, 0, 0))
    tshape = jax.ShapeDtypeStruct((g, S5_CW, S5_CW), BF16)
    return pl.pallas_call(
        _s5_prep_body,
        grid=(g,),
        in_specs=[dspec, dspec, dspec, gspec, gspec, gspec, gspec],
        out_specs=[tspec, tspec, tspec, pl.BlockSpec((2, 1, 16, 2 * p), lambda i: (0, i, 0, 0))],
        out_shape=[tshape, tshape, tshape, jax.ShapeDtypeStruct((2, g, 16, 2 * p), F32)],
        scratch_shapes=[pltpu.VMEM((S5_CW, 4 * p), F32), pltpu.VMEM((2, S5_CW, 2 * p), F32),
                        pltpu.VMEM((S5_CW, 2 * p), F32)],
        compiler_params=_cparams("arbitrary"),
        name="s5_prep",
    )(row(lam_re), row(lam_im), lst, bt_re, bt_im, c_re.astype(F32), c_im.astype(F32))


def _cmul(a_full, a_sgn, s):
    return a_full * s + a_sgn * pltpu.roll(s, S5_STATE, axis=1)


def _s5_body(u_ref, perm_ref, t_ref, g_ref, e_ref, pw_ref, dsk_ref, *rest, nb, nc, has_s0,
             want_final):
    rest = list(rest)
    s0_ref = rest.pop(0) if has_s0 else None
    o_ref = rest.pop(0)
    fin_ref = rest.pop(0) if want_final else None
    ys, xs_f, xs_b, tok_s = rest
    m = nb * nc
    w = 2 * S5_STATE
    lane_blk = lax.broadcasted_iota(jnp.int32, (m, LANES), 1) // S5_GROUP_CH
    row_blk = (lax.broadcasted_iota(jnp.int32, (2 * LANES, S5_CW), 0) % LANES) // S5_GROUP_CH
    cidx = lax.broadcasted_iota(jnp.int32, (m, w), 0) & (nc - 1)
    first = cidx == 0
    last = cidx == nc - 1

    def tokens(t):
        return u_ref[:, pl.ds(t, nc, stride=S5_Q), :].reshape(m, LANES)

    for t in range(S5_Q):
        tok_s[t // 2, :, (t % 2) * LANES:(t % 2 + 1) * LANES] = tokens(t).astype(BF16)

    for gg in range(S5_LG):
        own = row_blk == gg
        parts = [None, None]
        for t2 in range(S5_Q // 2):
            p = jnp.where(own, perm_ref[2 * t2 - gg + S5_LG - 1], jnp.zeros((), BF16))
            d = jnp.dot(tok_s[t2], p, preferred_element_type=F32)
            parts[t2 % 2] = d if parts[t2 % 2] is None else parts[t2 % 2] + d
        ub = (parts[0] + parts[1]).astype(BF16)
        z = jnp.dot(ub, g_ref[gg], preferred_element_type=F32)
        x_f, x_b = z[:, :w], z[:, w:]
        pw = lambda d, r: pw_ref[d, gg, r:r + 1, :]
        if has_s0:
            rep = lambda a: jnp.broadcast_to(a[:, None, :], (nb, nc, w)).reshape(m, w)
            s0_f, s0_b = rep(s0_ref[0, gg, :, :w]), rep(s0_ref[0, gg, :, w:])
            x_f = x_f + jnp.where(first, _cmul(pw(0, 0), pw(0, 1), s0_f), 0.0)
            x_b = x_b + jnp.where(last, _cmul(pw(1, 0), pw(1, 1), s0_b), 0.0)
        for k in range(nc.bit_length() - 1):
            sh = 1 << k
            prev = pltpu.roll(x_f, sh, axis=0)
            x_f = x_f + jnp.where(cidx >= sh, _cmul(pw(0, 2 * k), pw(0, 2 * k + 1), prev), 0.0)
            nxt = pltpu.roll(x_b, m - sh, axis=0)
            x_b = x_b + jnp.where(cidx < nc - sh, _cmul(pw(1, 2 * k), pw(1, 2 * k + 1), nxt), 0.0)
        in_f = pltpu.roll(x_f, 1, axis=0)
        in_b = pltpu.roll(x_b, m - 1, axis=0)
        if has_s0:
            in_f = jnp.where(first, s0_f, in_f)
            in_b = jnp.where(last, s0_b, in_b)
        else:
            in_f = jnp.where(first, 0.0, in_f)
            in_b = jnp.where(last, 0.0, in_b)
        s_in = jnp.concatenate([in_f, in_b], axis=1).astype(BF16)
        ys[gg] = (jnp.dot(ub, t_ref[gg], preferred_element_type=F32)
                  + jnp.dot(s_in, e_ref[gg], preferred_element_type=F32))
        if want_final:
            xs_f[...] = x_f
            xs_b[...] = x_b
            fin_ref[0, gg, :, :w] = xs_f[pl.ds(nc - 1, nb, stride=nc), :]
            fin_ref[0, gg, :, w:] = xs_b[pl.ds(0, nb, stride=nc), :]

    for t in range(S5_Q):
        tb, tt = divmod(t, S5_LG)
        y = None
        for k in range(S5_LG):
            sh = (S5_GROUP_CH * (k - tt)) % LANES
            r = ys[k, :, tb * LANES:(tb + 1) * LANES]
            if sh:
                r = pltpu.roll(r, sh, axis=1)
            y = r if y is None else jnp.where(lane_blk == k, r, y)
        y = y + tokens(t) * dsk_ref[0]
        o_ref[:, pl.ds(t, nc, stride=S5_Q), :] = _gelu_tanh(y).reshape(nb, nc, LANES)


def _s5(u, tables, d_skip, s0, want_final):
    t_all, g_fb, e_fb, pows = tables
    b, l, _ = u.shape
    nc = l // S5_Q
    bb = min(b, max(1, S5_ROWS // nc))
    nbb = b // bb
    m = bb * nc
    has_s0 = s0 is not None
    seq_spec = pl.BlockSpec((bb, l, LANES), lambda i, j: (j, 0, i))
    tab_spec = pl.BlockSpec((S5_LG, S5_CW, S5_CW), lambda i, j: (i, 0, 0),
                            pipeline_mode=pl.Buffered(1))
    st_spec = pl.BlockSpec((1, S5_LG, bb, S5_CW), lambda i, j: (j, i, 0, 0))
    st_blocks = lambda a: jnp.transpose(a.reshape(S5_GROUPS, nbb, bb, S5_CW), (1, 0, 2, 3))
    r = np.arange(LANES)[:, None]
    cc = np.arange(S5_CW)[None, :]
    shift = lambda d: (cc - r == S5_GROUP_CH * d).astype(np.float32)
    perm = np.stack([np.concatenate([shift(d), shift(d + 1)], axis=0)
                     for d in range(1 - S5_LG, S5_Q - 1)])
    in_specs = [seq_spec, _const_spec(perm.shape), tab_spec, tab_spec, tab_spec,
                pl.BlockSpec((2, S5_LG, 16, 2 * S5_STATE), lambda i, j: (0, i, 0, 0)),
                pl.BlockSpec((1, 1, LANES), lambda i, j: (i, 0, 0))]
    args = [u, jnp.asarray(perm, BF16), t_all, g_fb, e_fb, pows,
            d_skip.astype(F32).reshape(S5_WIDTH // LANES, 1, LANES)]
    if has_s0:
        in_specs.append(st_spec)
        args.append(st_blocks(s0))
    out_specs = [seq_spec]
    out_shape = [jax.ShapeDtypeStruct((b, l, S5_WIDTH), F32)]
    if want_final:
        out_specs.append(st_spec)
        out_shape.append(jax.ShapeDtypeStruct((nbb, S5_GROUPS, bb, S5_CW), F32))
    res = pl.pallas_call(
        functools.partial(_s5_body, nb=bb, nc=nc, has_s0=has_s0, want_final=want_final),
        grid=(S5_WIDTH // LANES, nbb),
        in_specs=in_specs,
        out_specs=out_specs,
        out_shape=out_shape,
        scratch_shapes=[pltpu.VMEM((S5_LG, m, S5_CW), F32),
                        pltpu.VMEM((m, 2 * S5_STATE), F32), pltpu.VMEM((m, 2 * S5_STATE), F32),
                        pltpu.VMEM((S5_Q // 2, m, 2 * LANES), BF16)],
        compiler_params=_cparams("arbitrary", "arbitrary"),
        name="s5",
    )(*args)
    fin = None
    if want_final:
        fin = jnp.transpose(res[1], (1, 0, 2, 3)).reshape(S5_GROUPS, b, S5_CW)
    return res[0], fin


def _dft_cos_sin(n, rows=None, row_step=1):
    rows = n if rows is None else rows
    j = jnp.arange(rows, dtype=jnp.int32)[:, None] * row_step
    jk = (j * jnp.arange(n, dtype=jnp.int32)[None, :]) % n
    ang = jk.astype(F32) * (2.0 * math.pi / n)
    return jnp.cos(ang), jnp.sin(ang)


def _dft_cos_sin_split(n, minor):
    c1, s1 = _dft_cos_sin(n, n // minor, minor)
    c2, s2 = _dft_cos_sin(n, minor)
    cos = c1[:, None, :] * c2[None, :, :] - s1[:, None, :] * s2[None, :, :]
    sin = s1[:, None, :] * c2[None, :, :] + c1[:, None, :] * s2[None, :, :]
    return cos.reshape(n, n), sin.reshape(n, n)


def _fnet_body(u_ref, cs_ref, dl_ref, o_ref, xs, *, seq):
    @pl.when(pl.program_id(1) == 0)
    def _():
        xcs = jnp.dot(u_ref[0].astype(BF16), cs_ref[...], preferred_element_type=F32)
        xs[0:seq, :] = xcs[:, :FNET_WIDTH].astype(BF16)
        xs[seq:2 * seq, :] = xcs[:, FNET_WIDTH:].astype(BF16)

    scale = 1.0 / math.sqrt(seq * FNET_GROUP_CH)
    o_ref[0] = jnp.dot(dl_ref[...], xs[...], preferred_element_type=F32) * scale


def _fnet(u):
    b, l, _ = u.shape
    tr = min(l, 512)
    cc, sc = _dft_cos_sin(FNET_GROUP_CH)
    eye = jnp.eye(FNET_GROUPS, dtype=F32)
    cs = jnp.concatenate([jnp.kron(eye, cc), jnp.kron(eye, sc)], axis=1).astype(BF16)
    cl, sl = _dft_cos_sin_split(l, DFT_MINOR) if l > 4 * DFT_MINOR else _dft_cos_sin(l)
    dl = jnp.concatenate([cl, -sl], axis=1).astype(BF16)
    return pl.pallas_call(
        functools.partial(_fnet_body, seq=l),
        grid=(b, l // tr),
        in_specs=[pl.BlockSpec((1, l, FNET_WIDTH), lambda i, t: (i, 0, 0)),
                  _const_spec((FNET_WIDTH, 2 * FNET_WIDTH)),
                  pl.BlockSpec((tr, 2 * l), lambda i, t: (t, 0))],
        out_specs=pl.BlockSpec((1, tr, FNET_WIDTH), lambda i, t: (i, t, 0)),
        out_shape=jax.ShapeDtypeStruct((b, l, FNET_WIDTH), F32),
        scratch_shapes=[pltpu.VMEM((2 * l, FNET_WIDTH), BF16)],
        compiler_params=_cparams("arbitrary", "arbitrary"),
        name="fnet",
    )(u, cs, dl)


def _even_w_in(w):
    pad = jnp.zeros((D_MODEL, LANES - SSD_HEADS), w.dtype)
    return jnp.concatenate([w[:, :O_DT], w[:, O_Q:], w[:, O_DT:O_Q], pad], axis=1).astype(BF16)


EVEN_SPLITS = (SSD_WIDTH, SSD_XBC, NA_WIDTH, NA_WIDTH, NA_WIDTH, LANES)


def kernel(x_prompt, x_sample, cache_na_k, cache_na_v, state_ssd, state_s5, c, c_ctx, mod_w, mod_b, norm1_g, norm2_g, ffn_w1, ffn_w3, ffn_w2, final_g, ev_w_in, ev_conv_w, ev_conv_b, ev_a_log, ev_dt_bias, ev_d_skip, ev_ssd_norm_g, ev_rpb, ev_w_out, od_w_in, od_lam_re, od_lam_im, od_log_step, od_b_re, od_b_im, od_c_re, od_c_im, od_d_skip, od_glu_w, od_glu_b, od_w_out):
    bc, lc, _ = x_prompt.shape
    bl, ll, _ = x_sample.shape
    cond = jnp.zeros((MOD_ROWS, D_MODEL), F32).at[0].set(c_ctx).at[1:1 + bl].set(c)
    mods = _adaln(cond, mod_w, mod_b)
    xc = x_prompt.reshape(1, bc * lc, D_MODEL)
    xl = x_sample
    tm = 512
    new_k, new_v, new_ssd, new_s5 = [], [], [], []
    for i in range(DEPTH):
        j = i // 2
        mod = mods[i]
        last = i == DEPTH - 1
        if i % 2 == 0:
            w_in = _even_w_in(ev_w_in[j])
            w_out = ev_w_out[j].astype(BF16)
            ssd_args = (ev_conv_w[j], ev_conv_b[j], ev_a_log[j], ev_dt_bias[j], ev_d_skip[j],
                        ev_ssd_norm_g[j])
            z, xbc, q, k, v, dtp = _inproj(xc, mod, 0, norm1_g[i], w_in, EVEN_SPLITS, tm,
                                           (F32, F32, BF16, F32, F32, F32))
            seqs = lambda a: a.reshape(bc, lc, a.shape[-1])
            y_ssd, fin = _ssd(seqs(z), seqs(xbc), seqs(dtp), *ssd_args, None, True)
            o_na = _ctx_attn(seqs(q), seqs(k), seqs(v))
            flat = lambda a: a.reshape(1, bc * lc, a.shape[-1])
            mix_c = (flat(y_ssd), flat(o_na))
            new_k.append(k.reshape(bc, lc, NA_HEADS, NA_HEAD_DIM))
            new_v.append(v.reshape(bc, lc, NA_HEADS, NA_HEAD_DIM))
            new_ssd.append(_ssd_state_from_kernel(fin))

            z, xbc, q, k, v, dtp = _inproj(xl, mod, 1, norm1_g[i], w_in, EVEN_SPLITS, tm,
                                           (F32, F32, BF16, BF16, BF16, F32))
            s0t = _ssd_state_to_kernel(state_ssd[:, j].astype(F32))
            (y_ssd,) = _ssd(z, xbc, dtp, *ssd_args, s0t, False)
            pl_ = cache_na_k.shape[2]
            o_na = _latent_na(q, k, v, cache_na_k[:, j].reshape(bl, pl_, NA_WIDTH),
                              cache_na_v[:, j].reshape(bl, pl_, NA_WIDTH), ev_rpb[j])
            mix_l = (y_ssd, o_na)
            glu_w = glu_b = None
        else:
            w_in = od_w_in[j].astype(BF16)
            w_out = od_w_out[j].astype(BF16)
            glu_w, glu_b = od_glu_w[j].astype(BF16), od_glu_b[j]
            tables = _s5_tables(od_lam_re[j], od_lam_im[j], od_log_step[j], od_b_re[j], od_b_im[j],
                                od_c_re[j], od_c_im[j])
            splits = (S5_WIDTH, FNET_WIDTH)
            u_s, u_f = _inproj(xc, mod, 0, norm1_g[i], w_in, splits, tm)
            seqs = lambda a: a.reshape(bc, lc, a.shape[-1])
            g_s, fin = _s5(seqs(u_s), tables, od_d_skip[j], None, True)
            y_f = _fnet(seqs(u_f))
            flat = lambda a: a.reshape(1, bc * lc, a.shape[-1])
            mix_c = (flat(g_s), flat(y_f))
            fin = fin.reshape(S5_GROUPS, bc, 2, 2, S5_STATE)
            new_s5.append(jnp.transpose(fin, (1, 2, 0, 4, 3)))

            u_s, u_f = _inproj(xl, mod, 1, norm1_g[i], w_in, splits, tm)
            s0 = jnp.transpose(state_s5[:, j].astype(F32), (2, 0, 1, 4, 3)).reshape(
                S5_GROUPS, bl, 4 * S5_STATE)
            g_s, _ = _s5(u_s, tables, od_d_skip[j], s0, False)
            y_f = _fnet(u_f)
            mix_l = (g_s, y_f)
        w_o = (w_out[:SSD_WIDTH], w_out[SSD_WIDTH:])
        w1, w3, w2 = ffn_w1[i].astype(BF16), ffn_w3[i].astype(BF16), ffn_w2[i].astype(BF16)
        fg = final_g if last else None
        xc = _ffn(xc, mod, 0, *mix_c, *w_o, glu_w, glu_b, norm2_g[i], w1, w3, w2, fg, tm)
        xl = _ffn(xl, mod, 1, *mix_l, *w_o, glu_w, glu_b, norm2_g[i], w1, w3, w2, fg, tm)
    return (xc.reshape(bc, lc, D_MODEL), xl,
            jnp.stack(new_k, axis=1), jnp.stack(new_v, axis=1),
            jnp.stack(new_ssd, axis=1), jnp.stack(new_s5, axis=1))
```

```python
import functools
import math

import numpy as np
import jax
import jax.numpy as jnp
from jax import lax
from jax.experimental import pallas as pl
from jax.experimental.pallas import tpu as pltpu

F32 = jnp.float32
BF16 = jnp.bfloat16
HIGHEST = lax.Precision.HIGHEST

D_MODEL = 1024
DEPTH = 2
EPS = 1e-6
GRID_W = 64
FFN_HIDDEN = 2816
FFN_CHUNK = 1408

SSD_HEADS = 8
SSD_HEAD_DIM = 64
SSD_WIDTH = 512
SSD_GROUPS = 2
SSD_HPG = SSD_HEADS // SSD_GROUPS
SSD_STATE = 64
SSD_GN = SSD_GROUPS * SSD_STATE
SSD_XBC = SSD_WIDTH + 2 * SSD_GN
SSD_CHUNK = 128
CONV_W = 5
CONV_HALO = 8

NA_HEADS = 8
NA_HEAD_DIM = 64
NA_WIDTH = 512
NA_ROWS = 8
NA_COLS = 16
NA_SCALE = NA_HEAD_DIM ** -0.5
NA_QROWS = 4
NA_WIN = NA_ROWS + NA_QROWS
NEG_BIG = -1e30

O_XBC = SSD_WIDTH
O_DT = O_XBC + SSD_XBC
O_Q = O_DT + SSD_HEADS
O_K = O_Q + NA_WIDTH
O_V = O_K + NA_WIDTH
LANES = 128

S5_GROUP_CH = 16
S5_GROUPS = 32
S5_WIDTH = 512
S5_STATE = 64
S5_Q = 16
S5_CW = S5_Q * S5_GROUP_CH
S5_LOG_MAX = 7
S5_LG = LANES // S5_GROUP_CH
S5_ROWS = 512

FNET_GROUP_CH = 64
FNET_GROUPS = 8
FNET_WIDTH = 512
DFT_MINOR = 64

MOD_ROWS = 16
VMEM_LIMIT = 56 * 1024 * 1024


def _cparams(*sem):
    return pltpu.CompilerParams(dimension_semantics=sem, vmem_limit_bytes=VMEM_LIMIT)


def _const_spec(shape):
    nd = len(shape)
    return pl.BlockSpec(shape, lambda *_: (0,) * nd, pipeline_mode=pl.Buffered(1))


def _sigmoid(x):
    return 1.0 / (1.0 + jnp.exp(-x))


def _silu(x):
    return x * _sigmoid(x)


def _softplus(x):
    return jnp.maximum(x, 0.0) + jnp.log(1.0 + jnp.exp(-jnp.abs(x)))


def _gelu_tanh(x):
    return 0.5 * x * (1.0 + jnp.tanh(math.sqrt(2.0 / math.pi) * (x + 0.044715 * (x * x * x))))


def _rmsnorm(x, g):
    return x * lax.rsqrt(jnp.mean(x * x, axis=-1, keepdims=True) + EPS) * g


def _bdot(a, b):
    return jnp.dot(a.astype(BF16), b.astype(BF16), preferred_element_type=F32)


def _bdot_nt(a, b):
    return lax.dot_general(a.astype(BF16), b.astype(BF16), (((1,), (1,)), ((), ())),
                           preferred_element_type=F32)


def _bdot_tn(a, b):
    return lax.dot_general(a.astype(BF16), b.astype(BF16), (((0,), (0,)), ((), ())),
                           preferred_element_type=F32)


def _adaln_body(c_ref, w_ref, b_ref, o_ref):
    s = _silu(c_ref[...])
    o_ref[0] = jnp.dot(s, w_ref[0], precision=HIGHEST, preferred_element_type=F32) + b_ref[0]


def _adaln(cond, mod_w, mod_b):
    n = 6 * D_MODEL
    tn = 1536
    out = pl.pallas_call(
        _adaln_body,
        grid=(DEPTH, n // tn),
        in_specs=[pl.BlockSpec((MOD_ROWS, D_MODEL), lambda i, j: (0, 0)),
                  pl.BlockSpec((1, D_MODEL, tn), lambda i, j: (i, 0, j)),
                  pl.BlockSpec((1, 1, tn), lambda i, j: (i, 0, j))],
        out_specs=pl.BlockSpec((1, MOD_ROWS, tn), lambda i, j: (i, 0, j)),
        out_shape=jax.ShapeDtypeStruct((DEPTH, MOD_ROWS, n), F32),
        compiler_params=_cparams("arbitrary", "arbitrary"),
        name="adaln",
    )(cond, mod_w, mod_b.reshape(DEPTH, 1, n))
    return out.reshape(DEPTH, MOD_ROWS, 6, D_MODEL)


def _mod_spec(mrow0):
    return pl.BlockSpec((1, 6, D_MODEL), lambda b, t: (b + mrow0, 0, 0))


def _inproj_body(x_ref, mod_ref, g_ref, w_ref, *o_refs, splits, heads_of):
    h = _rmsnorm(x_ref[0], g_ref[...]) * (1.0 + mod_ref[0, 1:2, :]) + mod_ref[0, 0:1, :]
    acc = jnp.dot(h.astype(BF16), w_ref[...], preferred_element_type=F32)
    off = 0
    offs = []
    for o_ref, n in zip(o_refs, splits):
        o_ref[0] = acc[:, off:off + n].astype(o_ref.dtype)
        offs.append(off)
        off += n
    for o_ref, i in zip(o_refs[len(splits):], heads_of):
        nh, hd = o_ref.shape[2], o_ref.shape[3]
        for hh in range(nh):
            o_ref[0, :, hh, :] = acc[:, offs[i] + hh * hd:offs[i] + (hh + 1) * hd]


def _inproj(x, mod, mrow0, g, w, splits, tm, dtypes=None, heads_of=()):
    b, l, _ = x.shape
    n = w.shape[1]
    dtypes = (F32,) * len(splits) if dtypes is None else dtypes
    head_shape = lambda i: (splits[i] // NA_HEAD_DIM, NA_HEAD_DIM)
    return pl.pallas_call(
        functools.partial(_inproj_body, splits=splits, heads_of=heads_of),
        grid=(b, l // tm),
        in_specs=[pl.BlockSpec((1, tm, D_MODEL), lambda i, t: (i, t, 0)),
                  _mod_spec(mrow0),
                  _const_spec((1, D_MODEL)),
                  _const_spec((D_MODEL, n))],
        out_specs=([pl.BlockSpec((1, tm, s), lambda i, t: (i, t, 0)) for s in splits]
                   + [pl.BlockSpec((1, tm) + head_shape(i), lambda i_, t: (i_, t, 0, 0))
                      for i in heads_of]),
        out_shape=([jax.ShapeDtypeStruct((b, l, s), dt) for s, dt in zip(splits, dtypes)]
                   + [jax.ShapeDtypeStruct((b, l) + head_shape(i), F32) for i in heads_of]),
        compiler_params=_cparams("arbitrary", "arbitrary"),
        name="inproj",
    )(x, mod, g.reshape(1, D_MODEL), w)


def _ffn_body(x_ref, mod_ref, ya_ref, yb_ref, wa_ref, wb_ref, g_ref, w1_ref, w3_ref, w2_ref, *rest,
              glu, final):
    rest = list(rest)
    ya = ya_ref[0]
    if glu:
        gw_ref, gb_ref = rest.pop(0), rest.pop(0)
        ya = ya * _sigmoid(_bdot(ya, gw_ref[...]) + gb_ref[...])
    mix = _bdot(ya, wa_ref[...]) + _bdot(yb_ref[0], wb_ref[...])
    x = x_ref[0] + mod_ref[0, 2:3, :] * mix
    h = (_rmsnorm(x, g_ref[...]) * (1.0 + mod_ref[0, 4:5, :]) + mod_ref[0, 3:4, :]).astype(BF16)
    acc = jnp.zeros(x.shape, F32)
    for c in range(FFN_HIDDEN // FFN_CHUNK):
        cols = slice(c * FFN_CHUNK, (c + 1) * FFN_CHUNK)
        a = jnp.dot(h, w1_ref[:, cols], preferred_element_type=F32)
        u = _silu(a) * jnp.dot(h, w3_ref[:, cols], preferred_element_type=F32)
        acc = acc + jnp.dot(u.astype(BF16), w2_ref[cols, :], preferred_element_type=F32)
    y = x + mod_ref[0, 5:6, :] * acc
    if final:
        fg_ref, o_ref = rest
        y = _rmsnorm(y, fg_ref[...])
    else:
        (o_ref,) = rest
    o_ref[0] = y


def _ffn(x, mod, mrow0, ya, yb, wa, wb, glu_w, glu_b, g, w1, w3, w2, final_g, tm):
    b, l, _ = x.shape
    wa_n, wb_n = ya.shape[-1], yb.shape[-1]
    glu = glu_w is not None
    final = final_g is not None
    row_spec = lambda n: pl.BlockSpec((1, tm, n), lambda i, t: (i, t, 0))
    in_specs = [row_spec(D_MODEL), _mod_spec(mrow0), row_spec(wa_n), row_spec(wb_n),
                _const_spec((wa_n, D_MODEL)), _const_spec((wb_n, D_MODEL)),
                _const_spec((1, D_MODEL)),
                _const_spec((D_MODEL, FFN_HIDDEN)),
                _const_spec((D_MODEL, FFN_HIDDEN)),
                _const_spec((FFN_HIDDEN, D_MODEL))]
    args = [x, mod, ya, yb, wa, wb, g.reshape(1, D_MODEL), w1, w3, w2]
    if glu:
        in_specs += [_const_spec((wa_n, wa_n)), _const_spec((1, wa_n))]
        args += [glu_w, glu_b.reshape(1, wa_n)]
    if final:
        in_specs.append(_const_spec((1, D_MODEL)))
        args.append(final_g.reshape(1, D_MODEL))
    return pl.pallas_call(
        functools.partial(_ffn_body, glu=glu, final=final),
        grid=(b, l // tm),
        in_specs=in_specs,
        out_specs=pl.BlockSpec((1, tm, D_MODEL), lambda i, t: (i, t, 0)),
        out_shape=jax.ShapeDtypeStruct((b, l, D_MODEL), F32),
        compiler_params=_cparams("arbitrary", "arbitrary"),
        name="ffn",
    )(*args)


def _ssd_body(z_ref, xbc_ref, dt_ref, cw_ref, cb_ref, hp_ref, dsk_ref, ng_ref, *rest,
              seq, has_s0, want_final):
    rest = list(rest)
    s0_ref = rest.pop(0) if has_s0 else None
    y_ref = rest.pop(0)
    fin_ref = rest.pop(0) if want_final else None
    st_s, cd_s, scur, ych, xc_s, cum_s, tr_s = rest
    q = SSD_CHUNK
    nc = seq // q
    gw = SSD_HPG * SSD_HEAD_DIM
    expand = (lax.broadcasted_iota(jnp.int32, (LANES, SSD_WIDTH), 1) // SSD_HEAD_DIM
              == lax.broadcasted_iota(jnp.int32, (LANES, SSD_WIDTH), 0)).astype(BF16)

    def per_head(v):
        hi = v.astype(BF16)
        lo = (v - hi.astype(F32)).astype(BF16)
        return (jnp.dot(hi, expand, preferred_element_type=F32)
                + jnp.dot(lo, expand, preferred_element_type=F32))
    row = lax.broadcasted_iota(jnp.int32, (q, q), 0)
    col = lax.broadcasted_iota(jnp.int32, (q, q), 1)
    lower = row >= col
    upper = col >= row
    tri_l = lower.astype(F32)
    tri_u = upper.astype(F32)
    a_f = -jnp.exp(hp_ref[0:1, :])
    a_b = -jnp.exp(hp_ref[1:2, :])
    lane = lax.broadcasted_iota(jnp.int32, (1, LANES), 1)
    a_f = jnp.where(lane < SSD_HEADS, a_f, 0.0)
    a_b = jnp.where(lane < SSD_HEADS, a_b, 0.0)
    bias_f = hp_ref[2:3, :]
    bias_b = hp_ref[3:4, :]

    def chunk_pre(c):
        r0 = pl.multiple_of(c * q, q)
        lo = pl.multiple_of(jnp.maximum(r0 - CONV_HALO, 0), CONV_HALO)
        hi = pl.multiple_of(jnp.minimum(r0 + q, seq - CONV_HALO), CONV_HALO)
        prev = jnp.where(c > 0, xbc_ref[0, pl.ds(lo, CONV_HALO), :], 0.0)
        nxt = jnp.where(c < nc - 1, xbc_ref[0, pl.ds(hi, CONV_HALO), :], 0.0)
        win = jnp.concatenate([prev, xbc_ref[0, pl.ds(r0, q), :], nxt], axis=0)
        acc = cb_ref[...] + cw_ref[0:1, :] * win[CONV_HALO - 2:CONV_HALO - 2 + q]
        for k in range(1, CONV_W):
            o = CONV_HALO - CONV_W // 2 + k
            acc = acc + cw_ref[k:k + 1, :] * win[o:o + q]
        xc = _silu(acc)
        dtr = dt_ref[0, pl.ds(r0, q), :]
        dt_f = _softplus(dtr + bias_f)
        dt_b = _softplus(dtr + bias_b)
        cum_f = jnp.dot(tri_l, dt_f * a_f, precision=HIGHEST, preferred_element_type=F32)
        cum_b = jnp.dot(tri_u, dt_b * a_b, precision=HIGHEST, preferred_element_type=F32)
        return r0, xc, dt_f, dt_b, cum_f, cum_b

    def pass_a(c, carry):
        r0, xc, dt_f, dt_b, cum_f, cum_b = chunk_pre(c)
        xc_s[pl.ds(r0, q), :] = xc
        cum_s[0, pl.ds(r0, q), :] = cum_f
        cum_s[1, pl.ds(r0, q), :] = cum_b
        for i, v in enumerate((cum_f, cum_b, dt_f, dt_b)):
            tr_s[c, i] = v.T[0:SSD_HEADS, :]
        end_f = cum_f[q - 1:q, :]
        end_b = cum_b[0:1, :]
        xw_f = xc[:, :SSD_WIDTH] * per_head(jnp.exp(end_f - cum_f) * dt_f)
        xw_b = xc[:, :SSD_WIDTH] * per_head(jnp.exp(end_b - cum_b) * dt_b)
        cd_s[0, c] = per_head(jnp.broadcast_to(jnp.exp(end_f), (8, LANES)))
        cd_s[1, c] = per_head(jnp.broadcast_to(jnp.exp(end_b), (8, LANES)))
        for g in range(SSD_GROUPS):
            bg = xc[:, SSD_WIDTH + g * SSD_STATE:SSD_WIDTH + (g + 1) * SSD_STATE]
            st_s[0, c, g] = _bdot_tn(bg, xw_f[:, g * gw:(g + 1) * gw])
            st_s[1, c, g] = _bdot_tn(bg, xw_b[:, g * gw:(g + 1) * gw])
        return carry

    lax.fori_loop(0, nc, pass_a, 0)

    if has_s0:
        scur[...] = s0_ref[0]
    else:
        scur[...] = jnp.zeros(scur.shape, F32)

    def pass_b(c, carry):
        cr = nc - 1 - c
        for g in range(SSD_GROUPS):
            s_in = scur[0, g]
            scur[0, g] = s_in * cd_s[0, c, 0:1, g * gw:(g + 1) * gw] + st_s[0, c, g]
            st_s[0, c, g] = s_in
            s_in = scur[1, g]
            scur[1, g] = s_in * cd_s[1, cr, 0:1, g * gw:(g + 1) * gw] + st_s[1, cr, g]
            st_s[1, cr, g] = s_in
        return carry

    lax.fori_loop(0, nc, pass_b, 0)
    if want_final:
        fin_ref[0] = scur[...]

    def pass_c(c, carry):
        r0 = pl.multiple_of(c * q, q)
        xc = xc_s[pl.ds(r0, q), :]
        cum_f = cum_s[0, pl.ds(r0, q), :]
        cum_b = cum_s[1, pl.ds(r0, q), :]
        cum_ft, cum_bt, dt_ft, dt_bt = tr_s[c, 0], tr_s[c, 1], tr_s[c, 2], tr_s[c, 3]
        ex_f = per_head(jnp.exp(cum_f))
        ex_b = per_head(jnp.exp(cum_b))
        y_off = []
        cbs = []
        for g in range(SSD_GROUPS):
            gs = slice(g * gw, (g + 1) * gw)
            bg = xc[:, SSD_WIDTH + g * SSD_STATE:SSD_WIDTH + (g + 1) * SSD_STATE]
            cg = xc[:, SSD_WIDTH + SSD_GN + g * SSD_STATE:SSD_WIDTH + SSD_GN + (g + 1) * SSD_STATE]
            cbs.append(_bdot_nt(cg, bg))
            y_off.append(ex_f[:, gs] * _bdot(cg, st_s[0, c, g]) + ex_b[:, gs] * _bdot(cg, st_s[1, c, g]))
        for h in range(SSD_HEADS):
            xh = xc[:, h * SSD_HEAD_DIM:(h + 1) * SSD_HEAD_DIM]
            seg_f = cum_f[:, h:h + 1] - cum_ft[h:h + 1, :]
            seg_b = cum_b[:, h:h + 1] - cum_bt[h:h + 1, :]
            m_f = jnp.exp(jnp.where(lower, seg_f, NEG_BIG)) * dt_ft[h:h + 1, :]
            m_b = jnp.exp(jnp.where(upper, seg_b, NEG_BIG)) * dt_bt[h:h + 1, :]
            ych[:, h * SSD_HEAD_DIM:(h + 1) * SSD_HEAD_DIM] = _bdot(cbs[h // SSD_HPG] * (m_f + m_b), xh)
        yf = ych[...] + jnp.concatenate(y_off, axis=1) + dsk_ref[...] * xc[:, :SSD_WIDTH]
        yf = yf * _silu(z_ref[0, pl.ds(r0, q), :])
        y_ref[0, pl.ds(r0, q), :] = _rmsnorm(yf, ng_ref[...]).astype(y_ref.dtype)
        return carry

    lax.fori_loop(0, nc, pass_c, 0)


def _ssd_state_to_kernel(s):
    b = s.shape[0]
    s = s.reshape(b, 2, SSD_GROUPS, SSD_HPG, SSD_HEAD_DIM, SSD_STATE)
    return jnp.transpose(s, (0, 1, 2, 5, 3, 4)).reshape(b, 2, SSD_GROUPS, SSD_STATE, SSD_HPG * SSD_HEAD_DIM)


def _ssd_state_from_kernel(s):
    b = s.shape[0]
    s = s.reshape(b, 2, SSD_GROUPS, SSD_STATE, SSD_HPG, SSD_HEAD_DIM)
    return jnp.transpose(s, (0, 1, 2, 4, 5, 3)).reshape(b, 2, SSD_HEADS, SSD_HEAD_DIM, SSD_STATE)


def _ssd(z, xbc, dtp, conv_w, conv_b, a_log, dt_bias, d_skip, norm_g, s0t, want_final):
    b, l, _ = z.shape
    nc = l // SSD_CHUNK
    gw = SSD_HPG * SSD_HEAD_DIM
    has_s0 = s0t is not None
    cw = jnp.zeros((8, SSD_XBC), F32).at[:CONV_W].set(conv_w)
    hp = jnp.zeros((8, LANES), F32)
    hp = hp.at[0:2, :SSD_HEADS].set(a_log).at[2:4, :SSD_HEADS].set(dt_bias)
    dsk = jnp.repeat(d_skip, SSD_HEAD_DIM).reshape(1, SSD_WIDTH)
    seq_spec = lambda n: pl.BlockSpec((1, l, n), lambda i: (i, 0, 0))
    st_spec = pl.BlockSpec((1, 2, SSD_GROUPS, SSD_STATE, gw), lambda i: (i, 0, 0, 0, 0))
    in_specs = [seq_spec(SSD_WIDTH), seq_spec(SSD_XBC), seq_spec(LANES),
                _const_spec((8, SSD_XBC)), _const_spec((1, SSD_XBC)), _const_spec((8, LANES)),
                _const_spec((1, SSD_WIDTH)), _const_spec((1, SSD_WIDTH))]
    args = [z, xbc, dtp, cw, conv_b.reshape(1, SSD_XBC), hp, dsk, norm_g.reshape(1, SSD_WIDTH)]
    if has_s0:
        in_specs.append(st_spec)
        args.append(s0t)
    out_specs = [seq_spec(SSD_WIDTH)]
    out_shape = [jax.ShapeDtypeStruct((b, l, SSD_WIDTH), BF16)]
    if want_final:
        out_specs.append(st_spec)
        out_shape.append(jax.ShapeDtypeStruct((b, 2, SSD_GROUPS, SSD_STATE, gw), F32))
    return pl.pallas_call(
        functools.partial(_ssd_body, seq=l, has_s0=has_s0, want_final=want_final),
        grid=(b,),
        in_specs=in_specs,
        out_specs=out_specs,
        out_shape=out_shape,
        scratch_shapes=[pltpu.VMEM((2, nc, SSD_GROUPS, SSD_STATE, gw), F32),
                        pltpu.VMEM((2, nc, 8, SSD_WIDTH), F32),
                        pltpu.VMEM((2, SSD_GROUPS, SSD_STATE, gw), F32),
                        pltpu.VMEM((SSD_CHUNK, SSD_WIDTH), F32),
                        pltpu.VMEM((l, SSD_XBC), F32),
                        pltpu.VMEM((2, l, LANES), F32),
                        pltpu.VMEM((nc, 4, SSD_HEADS, LANES), F32)],
        compiler_params=_cparams("arbitrary"),
        name="ssd",
    )(*args)


def _softmax_pv(parts):
    m = parts[0][0].max(axis=-1, keepdims=True)
    for s, _ in parts[1:]:
        m = jnp.maximum(m, s.max(axis=-1, keepdims=True))
    den = 0.0
    out = 0.0
    for s, v in parts:
        p = jnp.exp(s - m)
        den = den + p.sum(axis=-1, keepdims=True)
        out = out + _bdot(p, v)
    return out / den


def _ctx_attn_body(q_ref, k_ref, v_ref, o_ref):
    for h in range(NA_HEADS):
        hs = slice(h * NA_HEAD_DIM, (h + 1) * NA_HEAD_DIM)
        s = _bdot_nt(q_ref[0, :, hs], k_ref[0, :, hs]) * NA_SCALE
        o_ref[0, :, hs] = _softmax_pv([(s, v_ref[0, :, hs])]).astype(o_ref.dtype)


def _ctx_attn(q, k, v):
    b, l, _ = q.shape
    spec = pl.BlockSpec((1, l, NA_WIDTH), lambda i: (i, 0, 0))
    return pl.pallas_call(
        _ctx_attn_body,
        grid=(b,),
        in_specs=[spec, spec, spec],
        out_specs=spec,
        out_shape=jax.ShapeDtypeStruct((b, l, NA_WIDTH), BF16),
        compiler_params=_cparams("arbitrary"),
        name="ctx_attn",
    )(q, k, v)


def _na_bias_tables(rpb, rows):
    qc = np.arange(GRID_W)[:, None]
    kc = np.arange(GRID_W)[None, :]
    cs = np.clip(qc - NA_COLS // 2, 0, GRID_W - NA_COLS)
    col_ok = (kc >= cs) & (kc < cs + NA_COLS)
    dc = np.clip(kc - qc + NA_COLS - 1, 0, 2 * NA_COLS - 2)
    col_sel = (dc[None] == np.arange(2 * NA_COLS - 1)[:, None, None]).astype(np.float32)
    by_col = jnp.einsum('hab,bqk->haqk', rpb.astype(F32), jnp.asarray(col_sel), precision=HIGHEST)
    by_col = jnp.where(jnp.asarray(col_ok), by_col, NEG_BIG)
    n_dr = 2 * NA_ROWS - 1
    return pl.pallas_call(
        functools.partial(_na_bias_body, rows=rows),
        grid=(3, NA_HEADS),
        in_specs=[pl.BlockSpec((1, n_dr, GRID_W, GRID_W), lambda v, h: (h, 0, 0, 0))],
        out_specs=pl.BlockSpec((1, 1, NA_QROWS * GRID_W, NA_WIN * GRID_W), lambda v, h: (v, h, 0, 0)),
        out_shape=jax.ShapeDtypeStruct((3, NA_HEADS, NA_QROWS * GRID_W, NA_WIN * GRID_W), F32),
        compiler_params=_cparams("arbitrary", "arbitrary"),
        name="na_bias",
    )(by_col)


def _na_bias_body(u_ref, o_ref, *, rows):
    v = pl.program_id(0)
    kr = min(NA_ROWS, rows)
    r0 = jnp.where(v == 0, 0, jnp.where(v == 1, NA_QROWS, rows - NA_QROWS))
    ws = jnp.clip(r0 - kr // 2, 0, rows - NA_WIN)
    for qi in range(NA_QROWS):
        r = r0 + qi
        rs = jnp.clip(r - kr // 2, 0, rows - kr)
        for w in range(NA_WIN):
            krow = ws + w
            ok = (krow >= rs) & (krow < rs + kr)
            dr = jnp.clip(krow - r + NA_ROWS - 1, 0, 2 * NA_ROWS - 2)
            o_ref[0, 0, qi * GRID_W:(qi + 1) * GRID_W, w * GRID_W:(w + 1) * GRID_W] = jnp.where(
                ok, u_ref[0, dr], NEG_BIG)


def _na_body(q_ref, k_ref, v_ref, kc_ref, vc_ref, bias_ref, o_ref, *, rows):
    rb = pl.program_id(1)
    ws = jnp.clip(rb * NA_QROWS - NA_ROWS // 2, 0, rows - NA_WIN)
    k0 = pl.multiple_of(ws * GRID_W, NA_QROWS * GRID_W)
    nk = NA_WIN * GRID_W
    for h in range(NA_HEADS):
        hs = slice(h * NA_HEAD_DIM, (h + 1) * NA_HEAD_DIM)
        qh = q_ref[0, :, hs]
        s_loc = _bdot_nt(qh, k_ref[0, pl.ds(k0, nk), hs]) * NA_SCALE + bias_ref[0, h]
        s_ctx = _bdot_nt(qh, kc_ref[0, :, hs]) * NA_SCALE
        o_ref[0, :, hs] = _softmax_pv([(s_loc, v_ref[0, pl.ds(k0, nk), hs]),
                                       (s_ctx, vc_ref[0, :, hs])]).astype(o_ref.dtype)


def _latent_na(q, k, v, k_ctx, v_ctx, rpb):
    b, l, _ = q.shape
    lc = k_ctx.shape[1]
    rows = l // GRID_W
    assert rows >= NA_WIN and rows % NA_QROWS == 0 and NA_QROWS * 2 <= NA_ROWS
    nrb = rows // NA_QROWS
    tq = NA_QROWS * GRID_W
    bias = _na_bias_tables(rpb, rows)
    variant = lambda i, r: (jnp.where(r == 0, 0, jnp.where(r == nrb - 1, 2, 1)), 0, 0, 0)
    return pl.pallas_call(
        functools.partial(_na_body, rows=rows),
        grid=(b, nrb),
        in_specs=[pl.BlockSpec((1, tq, NA_WIDTH), lambda i, r: (i, r, 0)),
                  pl.BlockSpec((1, l, NA_WIDTH), lambda i, r: (i, 0, 0)),
                  pl.BlockSpec((1, l, NA_WIDTH), lambda i, r: (i, 0, 0)),
                  pl.BlockSpec((1, lc, NA_WIDTH), lambda i, r: (i, 0, 0)),
                  pl.BlockSpec((1, lc, NA_WIDTH), lambda i, r: (i, 0, 0)),
                  pl.BlockSpec((1, NA_HEADS, tq, NA_WIN * GRID_W), variant)],
        out_specs=pl.BlockSpec((1, tq, NA_WIDTH), lambda i, r: (i, r, 0)),
        out_shape=jax.ShapeDtypeStruct((b, l, NA_WIDTH), BF16),
        compiler_params=_cparams("arbitrary", "arbitrary"),
        name="latent_na",
    )(q, k, v, k_ctx, v_ctx, bias)


def _s5_prep_body(lre_ref, lim_ref, lst_ref, btre_ref, btim_ref, cre_ref, cim_ref,
                  t_ref, g_ref, e_ref, pw_ref, g_s, et_s, e0_s):
    q, ch, p = S5_Q, S5_GROUP_CH, S5_STATE
    lane = lax.broadcasted_iota(jnp.int32, (ch, S5_CW), 1)
    bt_re, bt_im = btre_ref[0], btim_ref[0]
    c_re, c_im = cre_ref[0], cim_ref[0]
    taps = []
    for d in range(2):
        lam_re, lam_im = lre_ref[d, 0], lim_ref[d, 0]
        step = jnp.exp(lst_ref[d, 0])
        mag = jnp.exp(lam_re * step)
        a_re = mag * jnp.cos(lam_im * step)
        a_im = mag * jnp.sin(lam_im * step)
        den = lam_re * lam_re + lam_im * lam_im
        k_re = ((a_re - 1.0) * lam_re + a_im * lam_im) / den
        k_im = (a_im * lam_re - (a_re - 1.0) * lam_im) / den
        bb_re = k_re * bt_re - k_im * bt_im
        bb_im = k_re * bt_im + k_im * bt_re
        p_re = jnp.ones_like(a_re)
        p_im = jnp.zeros_like(a_re)
        for t in range(q):
            tg = q - 1 - t if d == 0 else t
            te = t if d == 0 else q - 1 - t
            g_s[tg * ch:(tg + 1) * ch, 2 * p * d:2 * p * d + p] = p_re * bb_re - p_im * bb_im
            g_s[tg * ch:(tg + 1) * ch, 2 * p * d + p:2 * p * (d + 1)] = p_re * bb_im + p_im * bb_re
            e0_s[te * ch:(te + 1) * ch, 0:p] = c_re * p_re - c_im * p_im
            e0_s[te * ch:(te + 1) * ch, p:2 * p] = -(c_re * p_im + c_im * p_re)
            p_re, p_im = p_re * a_re - p_im * a_im, p_re * a_im + p_im * a_re
            et_s[d, te * ch:(te + 1) * ch, 0:p] = c_re * p_re - c_im * p_im
            et_s[d, te * ch:(te + 1) * ch, p:2 * p] = -(c_re * p_im + c_im * p_re)
        bb = jnp.concatenate([bb_re, bb_im], axis=1)
        taps.append(lax.dot_general(bb, e0_s[...], (((1,), (1,)), ((), ())), precision=HIGHEST,
                                    preferred_element_type=F32))
        pw_ref[d, 0] = jnp.zeros(pw_ref.shape[2:], F32)
        for k in range(S5_LOG_MAX):
            pw_ref[d, 0, 2 * k:2 * k + 1, :] = jnp.concatenate([p_re, p_re], axis=1)
            pw_ref[d, 0, 2 * k + 1:2 * k + 2, :] = jnp.concatenate([-p_im, p_im], axis=1)
            p_re, p_im = p_re * p_re - p_im * p_im, 2.0 * p_re * p_im
    for ti in range(q):
        fwd = taps[0] if ti == 0 else pltpu.roll(taps[0], ch * ti, axis=1)
        bwd = taps[1] if ti == q - 1 else pltpu.roll(taps[1], S5_CW - ch * (q - 1 - ti), axis=1)
        row = jnp.where(lane >= ch * ti, fwd, 0.0) + jnp.where(lane < ch * (ti + 1), bwd, 0.0)
        t_ref[0, ti * ch:(ti + 1) * ch, :] = row.astype(t_ref.dtype)
    g_ref[0] = g_s[...].astype(g_ref.dtype)
    e_ref[0, 0:2 * p, :] = et_s[0].T.astype(e_ref.dtype)
    e_ref[0, 2 * p:4 * p, :] = et_s[1].T.astype(e_ref.dtype)


def _s5_tables(lam_re, lam_im, log_step, b_re, b_im, c_re, c_im):
    g, p, ch = S5_GROUPS, S5_STATE, S5_GROUP_CH
    row = lambda a: a.astype(F32).reshape(2, g, 1, p)
    lst = jnp.broadcast_to(log_step.astype(F32)[:, :, None, None], (2, g, 1, p))
    bt_re = jnp.swapaxes(b_re.astype(F32), 1, 2)
    bt_im = jnp.swapaxes(b_im.astype(F32), 1, 2)
    dspec = pl.BlockSpec((2, 1, 1, p), lambda i: (0, i, 0, 0))
    gspec = pl.BlockSpec((1, ch, p), lambda i: (i, 0, 0))
    tspec = pl.BlockSpec((1, S5_CW, S5_CW), lambda i: (i, 0, 0))
    tshape = jax.ShapeDtypeStruct((g, S5_CW, S5_CW), BF16)
    return pl.pallas_call(
        _s5_prep_body,
        grid=(g,),
        in_specs=[dspec, dspec, dspec, gspec, gspec, gspec, gspec],
        out_specs=[tspec, tspec, tspec, pl.BlockSpec((2, 1, 16, 2 * p), lambda i: (0, i, 0, 0))],
        out_shape=[tshape, tshape, tshape, jax.ShapeDtypeStruct((2, g, 16, 2 * p), F32)],
        scratch_shapes=[pltpu.VMEM((S5_CW, 4 * p), F32), pltpu.VMEM((2, S5_CW, 2 * p), F32),
                        pltpu.VMEM((S5_CW, 2 * p), F32)],
        compiler_params=_cparams("arbitrary"),
        name="s5_prep",
    )(row(lam_re), row(lam_im), lst, bt_re, bt_im, c_re.astype(F32), c_im.astype(F32))


def _cmul(a_full, a_sgn, s):
    return a_full * s + a_sgn * pltpu.roll(s, S5_STATE, axis=1)


def _s5_body(u_ref, perm_ref, t_ref, g_ref, e_ref, pw_ref, dsk_ref, *rest, nb, nc, has_s0,
             want_final):
    rest = list(rest)
    s0_ref = rest.pop(0) if has_s0 else None
    o_ref = rest.pop(0)
    fin_ref = rest.pop(0) if want_final else None
    ys, xs_f, xs_b, tok_s = rest
    m = nb * nc
    w = 2 * S5_STATE
    lane_blk = lax.broadcasted_iota(jnp.int32, (m, LANES), 1) // S5_GROUP_CH
    row_blk = (lax.broadcasted_iota(jnp.int32, (2 * LANES, S5_CW), 0) % LANES) // S5_GROUP_CH
    cidx = lax.broadcasted_iota(jnp.int32, (m, w), 0) & (nc - 1)
    first = cidx == 0
    last = cidx == nc - 1

    def tokens(t):
        return u_ref[:, pl.ds(t, nc, stride=S5_Q), :].reshape(m, LANES)

    for t in range(S5_Q):
        tok_s[t // 2, :, (t % 2) * LANES:(t % 2 + 1) * LANES] = tokens(t).astype(BF16)

    for gg in range(S5_LG):
        own = row_blk == gg
        parts = [None, None]
        for t2 in range(S5_Q // 2):
            p = jnp.where(own, perm_ref[2 * t2 - gg + S5_LG - 1], jnp.zeros((), BF16))
            d = jnp.dot(tok_s[t2], p, preferred_element_type=F32)
            parts[t2 % 2] = d if parts[t2 % 2] is None else parts[t2 % 2] + d
        ub = (parts[0] + parts[1]).astype(BF16)
        z = jnp.dot(ub, g_ref[gg], preferred_element_type=F32)
        x_f, x_b = z[:, :w], z[:, w:]
        pw = lambda d, r: pw_ref[d, gg, r:r + 1, :]
        if has_s0:
            rep = lambda a: jnp.broadcast_to(a[:, None, :], (nb, nc, w)).reshape(m, w)
            s0_f, s0_b = rep(s0_ref[0, gg, :, :w]), rep(s0_ref[0, gg, :, w:])
            x_f = x_f + jnp.where(first, _cmul(pw(0, 0), pw(0, 1), s0_f), 0.0)
            x_b = x_b + jnp.where(last, _cmul(pw(1, 0), pw(1, 1), s0_b), 0.0)
        for k in range(nc.bit_length() - 1):
            sh = 1 << k
            prev = pltpu.roll(x_f, sh, axis=0)
            x_f = x_f + jnp.where(cidx >= sh, _cmul(pw(0, 2 * k), pw(0, 2 * k + 1), prev), 0.0)
            nxt = pltpu.roll(x_b, m - sh, axis=0)
            x_b = x_b + jnp.where(cidx < nc - sh, _cmul(pw(1, 2 * k), pw(1, 2 * k + 1), nxt), 0.0)
        in_f = pltpu.roll(x_f, 1, axis=0)
        in_b = pltpu.roll(x_b, m - 1, axis=0)
        if has_s0:
            in_f = jnp.where(first, s0_f, in_f)
            in_b = jnp.where(last, s0_b, in_b)
        else:
            in_f = jnp.where(first, 0.0, in_f)
            in_b = jnp.where(last, 0.0, in_b)
        s_in = jnp.concatenate([in_f, in_b], axis=1).astype(BF16)
        ys[gg] = (jnp.dot(ub, t_ref[gg], preferred_element_type=F32)
                  + jnp.dot(s_in, e_ref[gg], preferred_element_type=F32))
        if want_final:
            xs_f[...] = x_f
            xs_b[...] = x_b
            fin_ref[0, gg, :, :w] = xs_f[pl.ds(nc - 1, nb, stride=nc), :]
            fin_ref[0, gg, :, w:] = xs_b[pl.ds(0, nb, stride=nc), :]

    for t in range(S5_Q):
        tb, tt = divmod(t, S5_LG)
        y = None
        for k in range(S5_LG):
            sh = (S5_GROUP_CH * (k - tt)) % LANES
            r = ys[k, :, tb * LANES:(tb + 1) * LANES]
            if sh:
                r = pltpu.roll(r, sh, axis=1)
            y = r if y is None else jnp.where(lane_blk == k, r, y)
        y = y + tokens(t) * dsk_ref[0]
        o_ref[:, pl.ds(t, nc, stride=S5_Q), :] = _gelu_tanh(y).reshape(nb, nc, LANES)


def _s5(u, tables, d_skip, s0, want_final):
    t_all, g_fb, e_fb, pows = tables
    b, l, _ = u.shape
    nc = l // S5_Q
    bb = min(b, max(1, S5_ROWS // nc))
    nbb = b // bb
    m = bb * nc
    has_s0 = s0 is not None
    seq_spec = pl.BlockSpec((bb, l, LANES), lambda i, j: (j, 0, i))
    tab_spec = pl.BlockSpec((S5_LG, S5_CW, S5_CW), lambda i, j: (i, 0, 0),
                            pipeline_mode=pl.Buffered(1))
    st_spec = pl.BlockSpec((1, S5_LG, bb, S5_CW), lambda i, j: (j, i, 0, 0))
    st_blocks = lambda a: jnp.transpose(a.reshape(S5_GROUPS, nbb, bb, S5_CW), (1, 0, 2, 3))
    r = np.arange(LANES)[:, None]
    cc = np.arange(S5_CW)[None, :]
    shift = lambda d: (cc - r == S5_GROUP_CH * d).astype(np.float32)
    perm = np.stack([np.concatenate([shift(d), shift(d + 1)], axis=0)
                     for d in range(1 - S5_LG, S5_Q - 1)])
    in_specs = [seq_spec, _const_spec(perm.shape), tab_spec, tab_spec, tab_spec,
                pl.BlockSpec((2, S5_LG, 16, 2 * S5_STATE), lambda i, j: (0, i, 0, 0)),
                pl.BlockSpec((1, 1, LANES), lambda i, j: (i, 0, 0))]
    args = [u, jnp.asarray(perm, BF16), t_all, g_fb, e_fb, pows,
            d_skip.astype(F32).reshape(S5_WIDTH // LANES, 1, LANES)]
    if has_s0:
        in_specs.append(st_spec)
        args.append(st_blocks(s0))
    out_specs = [seq_spec]
    out_shape = [jax.ShapeDtypeStruct((b, l, S5_WIDTH), F32)]
    if want_final:
        out_specs.append(st_spec)
        out_shape.append(jax.ShapeDtypeStruct((nbb, S5_GROUPS, bb, S5_CW), F32))
    res = pl.pallas_call(
        functools.partial(_s5_body, nb=bb, nc=nc, has_s0=has_s0, want_final=want_final),
        grid=(S5_WIDTH // LANES, nbb),
        in_specs=in_specs,
        out_specs=out_specs,
        out_shape=out_shape,
        scratch_shapes=[pltpu.VMEM((S5_LG, m, S5_CW), F32),
                        pltpu.VMEM((m, 2 * S5_STATE), F32), pltpu.VMEM((m, 2 * S5_STATE), F32),
                        pltpu.VMEM((S5_Q // 2, m, 2 * LANES), BF16)],
        compiler_params=_cparams("arbitrary", "arbitrary"),
        name="s5",
    )(*args)
    fin = None
    if want_final:
        fin = jnp.transpose(res[1], (1, 0, 2, 3)).reshape(S5_GROUPS, b, S5_CW)
    return res[0], fin


def _dft_cos_sin(n, rows=None, row_step=1):
    rows = n if rows is None else rows
    j = jnp.arange(rows, dtype=jnp.int32)[:, None] * row_step
    jk = (j * jnp.arange(n, dtype=jnp.int32)[None, :]) % n
    ang = jk.astype(F32) * (2.0 * math.pi / n)
    return jnp.cos(ang), jnp.sin(ang)


def _dft_cos_sin_split(n, minor):
    c1, s1 = _dft_cos_sin(n, n // minor, minor)
    c2, s2 = _dft_cos_sin(n, minor)
    cos = c1[:, None, :] * c2[None, :, :] - s1[:, None, :] * s2[None, :, :]
    sin = s1[:, None, :] * c2[None, :, :] + c1[:, None, :] * s2[None, :, :]
    return cos.reshape(n, n), sin.reshape(n, n)


def _fnet_body(u_ref, cs_ref, dl_ref, o_ref, xs, *, seq):
    @pl.when(pl.program_id(1) == 0)
    def _():
        xcs = jnp.dot(u_ref[0].astype(BF16), cs_ref[...], preferred_element_type=F32)
        xs[0:seq, :] = xcs[:, :FNET_WIDTH].astype(BF16)
        xs[seq:2 * seq, :] = xcs[:, FNET_WIDTH:].astype(BF16)

    scale = 1.0 / math.sqrt(seq * FNET_GROUP_CH)
    o_ref[0] = (jnp.dot(dl_ref[...], xs[...], preferred_element_type=F32) * scale).astype(o_ref.dtype)


def _fnet(u):
    b, l, _ = u.shape
    tr = min(l, 512)
    cc, sc = _dft_cos_sin(FNET_GROUP_CH)
    eye = jnp.eye(FNET_GROUPS, dtype=F32)
    cs = jnp.concatenate([jnp.kron(eye, cc), jnp.kron(eye, sc)], axis=1).astype(BF16)
    cl, sl = _dft_cos_sin_split(l, DFT_MINOR) if l > 4 * DFT_MINOR else _dft_cos_sin(l)
    dl = jnp.concatenate([cl, -sl], axis=1).astype(BF16)
    return pl.pallas_call(
        functools.partial(_fnet_body, seq=l),
        grid=(b, l // tr),
        in_specs=[pl.BlockSpec((1, l, FNET_WIDTH), lambda i, t: (i, 0, 0)),
                  _const_spec((FNET_WIDTH, 2 * FNET_WIDTH)),
                  pl.BlockSpec((tr, 2 * l), lambda i, t: (t, 0))],
        out_specs=pl.BlockSpec((1, tr, FNET_WIDTH), lambda i, t: (i, t, 0)),
        out_shape=jax.ShapeDtypeStruct((b, l, FNET_WIDTH), BF16),
        scratch_shapes=[pltpu.VMEM((2 * l, FNET_WIDTH), BF16)],
        compiler_params=_cparams("arbitrary", "arbitrary"),
        name="fnet",
    )(u, cs, dl)


def _even_w_in(w):
    pad = jnp.zeros((D_MODEL, LANES - SSD_HEADS), w.dtype)
    return jnp.concatenate([w[:, :O_DT], w[:, O_Q:], w[:, O_DT:O_Q], pad], axis=1).astype(BF16)


EVEN_SPLITS = (SSD_WIDTH, SSD_XBC, NA_WIDTH, NA_WIDTH, NA_WIDTH, LANES)


def kernel(x_prompt, x_sample, cache_na_k, cache_na_v, state_ssd, state_s5, c, c_ctx, mod_w, mod_b, norm1_g, norm2_g, ffn_w1, ffn_w3, ffn_w2, final_g, ev_w_in, ev_conv_w, ev_conv_b, ev_a_log, ev_dt_bias, ev_d_skip, ev_ssd_norm_g, ev_rpb, ev_w_out, od_w_in, od_lam_re, od_lam_im, od_log_step, od_b_re, od_b_im, od_c_re, od_c_im, od_d_skip, od_glu_w, od_glu_b, od_w_out):
    bc, lc, _ = x_prompt.shape
    bl, ll, _ = x_sample.shape
    cond = jnp.zeros((MOD_ROWS, D_MODEL), F32).at[0].set(c_ctx).at[1:1 + bl].set(c)
    mods = _adaln(cond, mod_w, mod_b)
    xc = x_prompt.reshape(1, bc * lc, D_MODEL)
    xl = x_sample
    tm = 512
    new_k, new_v, new_ssd, new_s5 = [], [], [], []
    for i in range(DEPTH):
        j = i // 2
        mod = mods[i]
        last = i == DEPTH - 1
        if i % 2 == 0:
            w_in = _even_w_in(ev_w_in[j])
            w_out = ev_w_out[j].astype(BF16)
            ssd_args = (ev_conv_w[j], ev_conv_b[j], ev_a_log[j], ev_dt_bias[j], ev_d_skip[j],
                        ev_ssd_norm_g[j])
            z, xbc, q, k, v, dtp, k_hd, v_hd = _inproj(
                xc, mod, 0, norm1_g[i], w_in, EVEN_SPLITS, tm, (F32, F32, BF16, BF16, BF16, F32), (3, 4))
            seqs = lambda a: a.reshape(bc, lc, a.shape[-1])
            y_ssd, fin = _ssd(seqs(z), seqs(xbc), seqs(dtp), *ssd_args, None, True)
            o_na = _ctx_attn(seqs(q), seqs(k), seqs(v))
            flat = lambda a: a.reshape(1, bc * lc, a.shape[-1])
            mix_c = (flat(y_ssd), flat(o_na))
            new_k.append(k_hd.reshape(bc, lc, NA_HEADS, NA_HEAD_DIM))
            new_v.append(v_hd.reshape(bc, lc, NA_HEADS, NA_HEAD_DIM))
            new_ssd.append(_ssd_state_from_kernel(fin))

            z, xbc, q, k, v, dtp = _inproj(xl, mod, 1, norm1_g[i], w_in, EVEN_SPLITS, tm,
                                           (F32, F32, BF16, BF16, BF16, F32))
            s0t = _ssd_state_to_kernel(state_ssd[:, j].astype(F32))
            (y_ssd,) = _ssd(z, xbc, dtp, *ssd_args, s0t, False)
            pl_ = cache_na_k.shape[2]
            o_na = _latent_na(q, k, v, cache_na_k[:, j].reshape(bl, pl_, NA_WIDTH),
                              cache_na_v[:, j].reshape(bl, pl_, NA_WIDTH), ev_rpb[j])
            mix_l = (y_ssd, o_na)
            glu_w = glu_b = None
        else:
            w_in = od_w_in[j].astype(BF16)
            w_out = od_w_out[j].astype(BF16)
            glu_w, glu_b = od_glu_w[j].astype(BF16), od_glu_b[j]
            tables = _s5_tables(od_lam_re[j], od_lam_im[j], od_log_step[j], od_b_re[j], od_b_im[j],
                                od_c_re[j], od_c_im[j])
            splits = (S5_WIDTH, FNET_WIDTH)
            u_s, u_f = _inproj(xc, mod, 0, norm1_g[i], w_in, splits, tm, (F32, BF16))
            seqs = lambda a: a.reshape(bc, lc, a.shape[-1])
            g_s, fin = _s5(seqs(u_s), tables, od_d_skip[j], None, True)
            y_f = _fnet(seqs(u_f))
            flat = lambda a: a.reshape(1, bc * lc, a.shape[-1])
            mix_c = (flat(g_s), flat(y_f))
            fin = fin.reshape(S5_GROUPS, bc, 2, 2, S5_STATE)
            new_s5.append(jnp.transpose(fin, (1, 2, 0, 4, 3)))

            u_s, u_f = _inproj(xl, mod, 1, norm1_g[i], w_in, splits, tm, (F32, BF16))
            s0 = jnp.transpose(state_s5[:, j].astype(F32), (2, 0, 1, 4, 3)).reshape(
                S5_GROUPS, bl, 4 * S5_STATE)
            g_s, _ = _s5(u_s, tables, od_d_skip[j], s0, False)
            y_f = _fnet(u_f)
            mix_l = (g_s, y_f)
        w_o = (w_out[:SSD_WIDTH], w_out[SSD_WIDTH:])
        w1, w3, w2 = ffn_w1[i].astype(BF16), ffn_w3[i].astype(BF16), ffn_w2[i].astype(BF16)
        fg = final_g if last else None
        xc = _ffn(xc, mod, 0, *mix_c, *w_o, glu_w, glu_b, norm2_g[i], w1, w3, w2, fg, tm)
        xl = _ffn(xl, mod, 1, *mix_l, *w_o, glu_w, glu_b, norm2_g[i], w1, w3, w2, fg, tm)
    return (xc.reshape(bc, lc, D_MODEL), xl,
            jnp.stack(new_k, axis=1), jnp.stack(new_v, axis=1),
            jnp.stack(new_ssd, axis=1), jnp.stack(new_s5, axis=1))
```

```python
import functools
import math

import numpy as np
import jax
import jax.numpy as jnp
from jax import lax
from jax.experimental import pallas as pl
from jax.experimental.pallas import tpu as pltpu

F32 = jnp.float32
BF16 = jnp.bfloat16
HIGHEST = lax.Precision.HIGHEST

D_MODEL = 1024
DEPTH = 2
EPS = 1e-6
GRID_W = 64
FFN_HIDDEN = 2816
FFN_CHUNK = 1408

SSD_HEADS = 8
SSD_HEAD_DIM = 64
SSD_WIDTH = 512
SSD_GROUPS = 2
SSD_HPG = SSD_HEADS // SSD_GROUPS
SSD_STATE = 64
SSD_GN = SSD_GROUPS * SSD_STATE
SSD_XBC = SSD_WIDTH + 2 * SSD_GN
SSD_CHUNK = 128
CONV_W = 5
CONV_HALO = 8

NA_HEADS = 8
NA_HEAD_DIM = 64
NA_WIDTH = 512
NA_ROWS = 8
NA_COLS = 16
NA_SCALE = NA_HEAD_DIM ** -0.5
NA_QROWS = 4
NA_WIN = NA_ROWS + NA_QROWS
NEG_BIG = -1e30

O_XBC = SSD_WIDTH
O_DT = O_XBC + SSD_XBC
O_Q = O_DT + SSD_HEADS
O_K = O_Q + NA_WIDTH
O_V = O_K + NA_WIDTH
LANES = 128

S5_GROUP_CH = 16
S5_GROUPS = 32
S5_WIDTH = 512
S5_STATE = 64
S5_Q = 16
S5_CW = S5_Q * S5_GROUP_CH
S5_LOG_MAX = 7
S5_LG = LANES // S5_GROUP_CH
S5_ROWS = 512

FNET_GROUP_CH = 64
FNET_GROUPS = 8
FNET_WIDTH = 512
DFT_MINOR = 64

MOD_ROWS = 16
VMEM_LIMIT = 56 * 1024 * 1024


def _cparams(*sem):
    return pltpu.CompilerParams(dimension_semantics=sem, vmem_limit_bytes=VMEM_LIMIT)


def _const_spec(shape):
    nd = len(shape)
    return pl.BlockSpec(shape, lambda *_: (0,) * nd, pipeline_mode=pl.Buffered(1))


def _sigmoid(x):
    return 1.0 / (1.0 + jnp.exp(-x))


def _silu(x):
    return x * _sigmoid(x)


def _softplus(x):
    return jnp.maximum(x, 0.0) + jnp.log(1.0 + jnp.exp(-jnp.abs(x)))


def _gelu_tanh(x):
    return 0.5 * x * (1.0 + jnp.tanh(math.sqrt(2.0 / math.pi) * (x + 0.044715 * (x * x * x))))


def _rmsnorm(x, g):
    return x * lax.rsqrt(jnp.mean(x * x, axis=-1, keepdims=True) + EPS) * g


def _bdot(a, b):
    return jnp.dot(a.astype(BF16), b.astype(BF16), preferred_element_type=F32)


def _bdot_nt(a, b):
    return lax.dot_general(a.astype(BF16), b.astype(BF16), (((1,), (1,)), ((), ())),
                           preferred_element_type=F32)


def _bdot_tn(a, b):
    return lax.dot_general(a.astype(BF16), b.astype(BF16), (((0,), (0,)), ((), ())),
                           preferred_element_type=F32)


def _adaln_body(c_ref, w_ref, b_ref, o_ref):
    s = _silu(c_ref[...])
    o_ref[0] = jnp.dot(s, w_ref[0], precision=HIGHEST, preferred_element_type=F32) + b_ref[0]


def _adaln(cond, mod_w, mod_b):
    n = 6 * D_MODEL
    tn = 1536
    out = pl.pallas_call(
        _adaln_body,
        grid=(DEPTH, n // tn),
        in_specs=[pl.BlockSpec((MOD_ROWS, D_MODEL), lambda i, j: (0, 0)),
                  pl.BlockSpec((1, D_MODEL, tn), lambda i, j: (i, 0, j)),
                  pl.BlockSpec((1, 1, tn), lambda i, j: (i, 0, j))],
        out_specs=pl.BlockSpec((1, MOD_ROWS, tn), lambda i, j: (i, 0, j)),
        out_shape=jax.ShapeDtypeStruct((DEPTH, MOD_ROWS, n), F32),
        compiler_params=_cparams("arbitrary", "arbitrary"),
        name="adaln",
    )(cond, mod_w, mod_b.reshape(DEPTH, 1, n))
    return out.reshape(DEPTH, MOD_ROWS, 6, D_MODEL)


def _mod_spec(mrow0):
    return pl.BlockSpec((1, 6, D_MODEL), lambda b, t: (b + mrow0, 0, 0))


def _inproj_body(x_ref, mod_ref, g_ref, w_ref, *o_refs, splits, heads_of):
    h = _rmsnorm(x_ref[0], g_ref[...]) * (1.0 + mod_ref[0, 1:2, :]) + mod_ref[0, 0:1, :]
    acc = jnp.dot(h.astype(BF16), w_ref[...], preferred_element_type=F32)
    off = 0
    offs = []
    for o_ref, n in zip(o_refs, splits):
        o_ref[0] = acc[:, off:off + n].astype(o_ref.dtype)
        offs.append(off)
        off += n
    for o_ref, i in zip(o_refs[len(splits):], heads_of):
        nh, hd = o_ref.shape[2], o_ref.shape[3]
        for hh in range(nh):
            o_ref[0, :, hh, :] = acc[:, offs[i] + hh * hd:offs[i] + (hh + 1) * hd]


def _inproj(x, mod, mrow0, g, w, splits, tm, dtypes=None, heads_of=()):
    b, l, _ = x.shape
    n = w.shape[1]
    dtypes = (F32,) * len(splits) if dtypes is None else dtypes
    head_shape = lambda i: (splits[i] // NA_HEAD_DIM, NA_HEAD_DIM)
    return pl.pallas_call(
        functools.partial(_inproj_body, splits=splits, heads_of=heads_of),
        grid=(b, l // tm),
        in_specs=[pl.BlockSpec((1, tm, D_MODEL), lambda i, t: (i, t, 0)),
                  _mod_spec(mrow0),
                  _const_spec((1, D_MODEL)),
                  _const_spec((D_MODEL, n))],
        out_specs=([pl.BlockSpec((1, tm, s), lambda i, t: (i, t, 0)) for s in splits]
                   + [pl.BlockSpec((1, tm) + head_shape(i), lambda i_, t: (i_, t, 0, 0))
                      for i in heads_of]),
        out_shape=([jax.ShapeDtypeStruct((b, l, s), dt) for s, dt in zip(splits, dtypes)]
                   + [jax.ShapeDtypeStruct((b, l) + head_shape(i), F32) for i in heads_of]),
        compiler_params=_cparams("arbitrary", "arbitrary"),
        name="inproj",
    )(x, mod, g.reshape(1, D_MODEL), w)


def _ffn_body(x_ref, mod_ref, ya_ref, yb_ref, wa_ref, wb_ref, g_ref, w1_ref, w3_ref, w2_ref, *rest,
              glu, final):
    rest = list(rest)
    ya = ya_ref[0]
    if glu:
        gw_ref, gb_ref = rest.pop(0), rest.pop(0)
        ya = ya * _sigmoid(_bdot(ya, gw_ref[...]) + gb_ref[...])
    mix = _bdot(ya, wa_ref[...]) + _bdot(yb_ref[0], wb_ref[...])
    x = x_ref[0] + mod_ref[0, 2:3, :] * mix
    h = (_rmsnorm(x, g_ref[...]) * (1.0 + mod_ref[0, 4:5, :]) + mod_ref[0, 3:4, :]).astype(BF16)
    acc = jnp.zeros(x.shape, F32)
    for c in range(FFN_HIDDEN // FFN_CHUNK):
        cols = slice(c * FFN_CHUNK, (c + 1) * FFN_CHUNK)
        a = jnp.dot(h, w1_ref[:, cols], preferred_element_type=F32)
        u = _silu(a) * jnp.dot(h, w3_ref[:, cols], preferred_element_type=F32)
        acc = acc + jnp.dot(u.astype(BF16), w2_ref[cols, :], preferred_element_type=F32)
    y = x + mod_ref[0, 5:6, :] * acc
    if final:
        fg_ref, o_ref = rest
        y = _rmsnorm(y, fg_ref[...])
    else:
        (o_ref,) = rest
    o_ref[0] = y


def _ffn(x, mod, mrow0, ya, yb, wa, wb, glu_w, glu_b, g, w1, w3, w2, final_g, tm):
    b, l, _ = x.shape
    wa_n, wb_n = ya.shape[-1], yb.shape[-1]
    glu = glu_w is not None
    final = final_g is not None
    row_spec = lambda n: pl.BlockSpec((1, tm, n), lambda i, t: (i, t, 0))
    in_specs = [row_spec(D_MODEL), _mod_spec(mrow0), row_spec(wa_n), row_spec(wb_n),
                _const_spec((wa_n, D_MODEL)), _const_spec((wb_n, D_MODEL)),
                _const_spec((1, D_MODEL)),
                _const_spec((D_MODEL, FFN_HIDDEN)),
                _const_spec((D_MODEL, FFN_HIDDEN)),
                _const_spec((FFN_HIDDEN, D_MODEL))]
    args = [x, mod, ya, yb, wa, wb, g.reshape(1, D_MODEL), w1, w3, w2]
    if glu:
        in_specs += [_const_spec((wa_n, wa_n)), _const_spec((1, wa_n))]
        args += [glu_w, glu_b.reshape(1, wa_n)]
    if final:
        in_specs.append(_const_spec((1, D_MODEL)))
        args.append(final_g.reshape(1, D_MODEL))
    return pl.pallas_call(
        functools.partial(_ffn_body, glu=glu, final=final),
        grid=(b, l // tm),
        in_specs=in_specs,
        out_specs=pl.BlockSpec((1, tm, D_MODEL), lambda i, t: (i, t, 0)),
        out_shape=jax.ShapeDtypeStruct((b, l, D_MODEL), F32),
        compiler_params=_cparams("arbitrary", "arbitrary"),
        name="ffn",
    )(*args)


def _ssd_body(z_ref, xbc_ref, dt_ref, cw_ref, cb_ref, hp_ref, dsk_ref, ng_ref, *rest,
              seq, has_s0, want_final):
    rest = list(rest)
    s0_ref = rest.pop(0) if has_s0 else None
    y_ref = rest.pop(0)
    fin_ref = rest.pop(0) if want_final else None
    st_s, cd_s, scur, ych, xc_s, cum_s, tr_s = rest
    q = SSD_CHUNK
    nc = seq // q
    gw = SSD_HPG * SSD_HEAD_DIM
    expand = (lax.broadcasted_iota(jnp.int32, (LANES, SSD_WIDTH), 1) // SSD_HEAD_DIM
              == lax.broadcasted_iota(jnp.int32, (LANES, SSD_WIDTH), 0)).astype(BF16)

    def per_head(v):
        hi = v.astype(BF16)
        lo = (v - hi.astype(F32)).astype(BF16)
        return (jnp.dot(hi, expand, preferred_element_type=F32)
                + jnp.dot(lo, expand, preferred_element_type=F32))
    row = lax.broadcasted_iota(jnp.int32, (q, q), 0)
    col = lax.broadcasted_iota(jnp.int32, (q, q), 1)
    lower = row >= col
    upper = col >= row
    tri_l = lower.astype(F32)
    tri_u = upper.astype(F32)
    a_f = -jnp.exp(hp_ref[0:1, :])
    a_b = -jnp.exp(hp_ref[1:2, :])
    lane = lax.broadcasted_iota(jnp.int32, (1, LANES), 1)
    a_f = jnp.where(lane < SSD_HEADS, a_f, 0.0)
    a_b = jnp.where(lane < SSD_HEADS, a_b, 0.0)
    bias_f = hp_ref[2:3, :]
    bias_b = hp_ref[3:4, :]

    def chunk_pre(c):
        r0 = pl.multiple_of(c * q, q)
        lo = pl.multiple_of(jnp.maximum(r0 - CONV_HALO, 0), CONV_HALO)
        hi = pl.multiple_of(jnp.minimum(r0 + q, seq - CONV_HALO), CONV_HALO)
        prev = jnp.where(c > 0, xbc_ref[0, pl.ds(lo, CONV_HALO), :], 0.0)
        nxt = jnp.where(c < nc - 1, xbc_ref[0, pl.ds(hi, CONV_HALO), :], 0.0)
        win = jnp.concatenate([prev, xbc_ref[0, pl.ds(r0, q), :], nxt], axis=0)
        acc = cb_ref[...] + cw_ref[0:1, :] * win[CONV_HALO - 2:CONV_HALO - 2 + q]
        for k in range(1, CONV_W):
            o = CONV_HALO - CONV_W // 2 + k
            acc = acc + cw_ref[k:k + 1, :] * win[o:o + q]
        xc = _silu(acc)
        dtr = dt_ref[0, pl.ds(r0, q), :]
        dt_f = _softplus(dtr + bias_f)
        dt_b = _softplus(dtr + bias_b)
        cum_f = jnp.dot(tri_l, dt_f * a_f, precision=HIGHEST, preferred_element_type=F32)
        cum_b = jnp.dot(tri_u, dt_b * a_b, precision=HIGHEST, preferred_element_type=F32)
        return r0, xc, dt_f, dt_b, cum_f, cum_b

    def pass_a(c, carry):
        r0, xc, dt_f, dt_b, cum_f, cum_b = chunk_pre(c)
        xc_s[pl.ds(r0, q), :] = xc
        cum_s[0, pl.ds(r0, q), :] = cum_f
        cum_s[1, pl.ds(r0, q), :] = cum_b
        for i, v in enumerate((cum_f, cum_b, dt_f, dt_b)):
            tr_s[c, i] = v.T[0:SSD_HEADS, :]
        end_f = cum_f[q - 1:q, :]
        end_b = cum_b[0:1, :]
        xw_f = xc[:, :SSD_WIDTH] * per_head(jnp.exp(end_f - cum_f) * dt_f)
        xw_b = xc[:, :SSD_WIDTH] * per_head(jnp.exp(end_b - cum_b) * dt_b)
        cd_s[0, c] = per_head(jnp.broadcast_to(jnp.exp(end_f), (8, LANES)))
        cd_s[1, c] = per_head(jnp.broadcast_to(jnp.exp(end_b), (8, LANES)))
        for g in range(SSD_GROUPS):
            bg = xc[:, SSD_WIDTH + g * SSD_STATE:SSD_WIDTH + (g + 1) * SSD_STATE]
            st_s[0, c, g] = _bdot_tn(bg, xw_f[:, g * gw:(g + 1) * gw])
            st_s[1, c, g] = _bdot_tn(bg, xw_b[:, g * gw:(g + 1) * gw])
        return carry

    lax.fori_loop(0, nc, pass_a, 0)

    if has_s0:
        scur[...] = s0_ref[0]
    else:
        scur[...] = jnp.zeros(scur.shape, F32)

    def pass_b(c, carry):
        cr = nc - 1 - c
        for g in range(SSD_GROUPS):
            s_in = scur[0, g]
            scur[0, g] = s_in * cd_s[0, c, 0:1, g * gw:(g + 1) * gw] + st_s[0, c, g]
            st_s[0, c, g] = s_in
            s_in = scur[1, g]
            scur[1, g] = s_in * cd_s[1, cr, 0:1, g * gw:(g + 1) * gw] + st_s[1, cr, g]
            st_s[1, cr, g] = s_in
        return carry

    lax.fori_loop(0, nc, pass_b, 0)
    if want_final:
        fin_ref[0] = scur[...]

    def pass_c(c, carry):
        r0 = pl.multiple_of(c * q, q)
        xc = xc_s[pl.ds(r0, q), :]
        cum_f = cum_s[0, pl.ds(r0, q), :]
        cum_b = cum_s[1, pl.ds(r0, q), :]
        cum_ft, cum_bt, dt_ft, dt_bt = tr_s[c, 0], tr_s[c, 1], tr_s[c, 2], tr_s[c, 3]
        ex_f = per_head(jnp.exp(cum_f))
        ex_b = per_head(jnp.exp(cum_b))
        y_off = []
        cbs = []
        for g in range(SSD_GROUPS):
            gs = slice(g * gw, (g + 1) * gw)
            bg = xc[:, SSD_WIDTH + g * SSD_STATE:SSD_WIDTH + (g + 1) * SSD_STATE]
            cg = xc[:, SSD_WIDTH + SSD_GN + g * SSD_STATE:SSD_WIDTH + SSD_GN + (g + 1) * SSD_STATE]
            cbs.append(_bdot_nt(cg, bg))
            y_off.append(ex_f[:, gs] * _bdot(cg, st_s[0, c, g]) + ex_b[:, gs] * _bdot(cg, st_s[1, c, g]))
        for h in range(SSD_HEADS):
            xh = xc[:, h * SSD_HEAD_DIM:(h + 1) * SSD_HEAD_DIM]
            seg_f = cum_f[:, h:h + 1] - cum_ft[h:h + 1, :]
            seg_b = cum_b[:, h:h + 1] - cum_bt[h:h + 1, :]
            m_f = jnp.exp(jnp.where(lower, seg_f, NEG_BIG)) * dt_ft[h:h + 1, :]
            m_b = jnp.exp(jnp.where(upper, seg_b, NEG_BIG)) * dt_bt[h:h + 1, :]
            ych[:, h * SSD_HEAD_DIM:(h + 1) * SSD_HEAD_DIM] = _bdot(cbs[h // SSD_HPG] * (m_f + m_b), xh)
        yf = ych[...] + jnp.concatenate(y_off, axis=1) + dsk_ref[...] * xc[:, :SSD_WIDTH]
        yf = yf * _silu(z_ref[0, pl.ds(r0, q), :])
        y_ref[0, pl.ds(r0, q), :] = _rmsnorm(yf, ng_ref[...]).astype(y_ref.dtype)
        return carry

    lax.fori_loop(0, nc, pass_c, 0)


def _ssd_state_to_kernel(s):
    b = s.shape[0]
    s = s.reshape(b, 2, SSD_GROUPS, SSD_HPG, SSD_HEAD_DIM, SSD_STATE)
    return jnp.transpose(s, (0, 1, 2, 5, 3, 4)).reshape(b, 2, SSD_GROUPS, SSD_STATE, SSD_HPG * SSD_HEAD_DIM)


def _ssd_state_from_kernel(s):
    b = s.shape[0]
    s = s.reshape(b, 2, SSD_GROUPS, SSD_STATE, SSD_HPG, SSD_HEAD_DIM)
    return jnp.transpose(s, (0, 1, 2, 4, 5, 3)).reshape(b, 2, SSD_HEADS, SSD_HEAD_DIM, SSD_STATE)


def _ssd(z, xbc, dtp, conv_w, conv_b, a_log, dt_bias, d_skip, norm_g, s0t, want_final):
    b, l, _ = z.shape
    nc = l // SSD_CHUNK
    gw = SSD_HPG * SSD_HEAD_DIM
    has_s0 = s0t is not None
    cw = jnp.zeros((8, SSD_XBC), F32).at[:CONV_W].set(conv_w)
    hp = jnp.zeros((8, LANES), F32)
    hp = hp.at[0:2, :SSD_HEADS].set(a_log).at[2:4, :SSD_HEADS].set(dt_bias)
    dsk = jnp.repeat(d_skip, SSD_HEAD_DIM).reshape(1, SSD_WIDTH)
    seq_spec = lambda n: pl.BlockSpec((1, l, n), lambda i: (i, 0, 0))
    st_spec = pl.BlockSpec((1, 2, SSD_GROUPS, SSD_STATE, gw), lambda i: (i, 0, 0, 0, 0))
    in_specs = [seq_spec(SSD_WIDTH), seq_spec(SSD_XBC), seq_spec(LANES),
                _const_spec((8, SSD_XBC)), _const_spec((1, SSD_XBC)), _const_spec((8, LANES)),
                _const_spec((1, SSD_WIDTH)), _const_spec((1, SSD_WIDTH))]
    args = [z, xbc, dtp, cw, conv_b.reshape(1, SSD_XBC), hp, dsk, norm_g.reshape(1, SSD_WIDTH)]
    if has_s0:
        in_specs.append(st_spec)
        args.append(s0t)
    out_specs = [seq_spec(SSD_WIDTH)]
    out_shape = [jax.ShapeDtypeStruct((b, l, SSD_WIDTH), BF16)]
    if want_final:
        out_specs.append(st_spec)
        out_shape.append(jax.ShapeDtypeStruct((b, 2, SSD_GROUPS, SSD_STATE, gw), F32))
    return pl.pallas_call(
        functools.partial(_ssd_body, seq=l, has_s0=has_s0, want_final=want_final),
        grid=(b,),
        in_specs=in_specs,
        out_specs=out_specs,
        out_shape=out_shape,
        scratch_shapes=[pltpu.VMEM((2, nc, SSD_GROUPS, SSD_STATE, gw), F32),
                        pltpu.VMEM((2, nc, 8, SSD_WIDTH), F32),
                        pltpu.VMEM((2, SSD_GROUPS, SSD_STATE, gw), F32),
                        pltpu.VMEM((SSD_CHUNK, SSD_WIDTH), F32),
                        pltpu.VMEM((l, SSD_XBC), F32),
                        pltpu.VMEM((2, l, LANES), F32),
                        pltpu.VMEM((nc, 4, SSD_HEADS, LANES), F32)],
        compiler_params=_cparams("arbitrary"),
        name="ssd",
    )(*args)


def _softmax_pv(parts):
    m = parts[0][0].max(axis=-1, keepdims=True)
    for s, _ in parts[1:]:
        m = jnp.maximum(m, s.max(axis=-1, keepdims=True))
    den = 0.0
    out = 0.0
    for s, v in parts:
        p = jnp.exp(s - m)
        den = den + p.sum(axis=-1, keepdims=True)
        out = out + _bdot(p, v)
    return out / den


def _ctx_attn_body(q_ref, k_ref, v_ref, o_ref):
    for h in range(NA_HEADS):
        hs = slice(h * NA_HEAD_DIM, (h + 1) * NA_HEAD_DIM)
        s = _bdot_nt(q_ref[0, :, hs], k_ref[0, :, hs]) * NA_SCALE
        o_ref[0, :, hs] = _softmax_pv([(s, v_ref[0, :, hs])]).astype(o_ref.dtype)


def _ctx_attn(q, k, v):
    b, l, _ = q.shape
    spec = pl.BlockSpec((1, l, NA_WIDTH), lambda i: (i, 0, 0))
    return pl.pallas_call(
        _ctx_attn_body,
        grid=(b,),
        in_specs=[spec, spec, spec],
        out_specs=spec,
        out_shape=jax.ShapeDtypeStruct((b, l, NA_WIDTH), BF16),
        compiler_params=_cparams("arbitrary"),
        name="ctx_attn",
    )(q, k, v)


def _na_bias_tables(rpb, rows):
    qc = np.arange(GRID_W)[:, None]
    kc = np.arange(GRID_W)[None, :]
    cs = np.clip(qc - NA_COLS // 2, 0, GRID_W - NA_COLS)
    col_ok = (kc >= cs) & (kc < cs + NA_COLS)
    dc = np.clip(kc - qc + NA_COLS - 1, 0, 2 * NA_COLS - 2)
    col_sel = (dc[None] == np.arange(2 * NA_COLS - 1)[:, None, None]).astype(np.float32)
    by_col = jnp.einsum('hab,bqk->haqk', rpb.astype(F32), jnp.asarray(col_sel), precision=HIGHEST)
    by_col = jnp.where(jnp.asarray(col_ok), by_col, NEG_BIG)
    n_dr = 2 * NA_ROWS - 1
    return pl.pallas_call(
        functools.partial(_na_bias_body, rows=rows),
        grid=(3, NA_HEADS),
        in_specs=[pl.BlockSpec((1, n_dr, GRID_W, GRID_W), lambda v, h: (h, 0, 0, 0))],
        out_specs=pl.BlockSpec((1, 1, NA_QROWS * GRID_W, NA_WIN * GRID_W), lambda v, h: (v, h, 0, 0)),
        out_shape=jax.ShapeDtypeStruct((3, NA_HEADS, NA_QROWS * GRID_W, NA_WIN * GRID_W), F32),
        compiler_params=_cparams("arbitrary", "arbitrary"),
        name="na_bias",
    )(by_col)


def _na_bias_body(u_ref, o_ref, *, rows):
    v = pl.program_id(0)
    kr = min(NA_ROWS, rows)
    r0 = jnp.where(v == 0, 0, jnp.where(v == 1, NA_QROWS, rows - NA_QROWS))
    ws = jnp.clip(r0 - kr // 2, 0, rows - NA_WIN)
    for qi in range(NA_QROWS):
        r = r0 + qi
        rs = jnp.clip(r - kr // 2, 0, rows - kr)
        for w in range(NA_WIN):
            krow = ws + w
            ok = (krow >= rs) & (krow < rs + kr)
            dr = jnp.clip(krow - r + NA_ROWS - 1, 0, 2 * NA_ROWS - 2)
            o_ref[0, 0, qi * GRID_W:(qi + 1) * GRID_W, w * GRID_W:(w + 1) * GRID_W] = jnp.where(
                ok, u_ref[0, dr], NEG_BIG)


def _na_body(q_ref, k_ref, v_ref, kc_ref, vc_ref, bias_ref, o_ref, *, rows):
    rb = pl.program_id(1)
    ws = jnp.clip(rb * NA_QROWS - NA_ROWS // 2, 0, rows - NA_WIN)
    k0 = pl.multiple_of(ws * GRID_W, NA_QROWS * GRID_W)
    nk = NA_WIN * GRID_W
    for h in range(NA_HEADS):
        hs = slice(h * NA_HEAD_DIM, (h + 1) * NA_HEAD_DIM)
        qh = q_ref[0, :, hs]
        s_loc = _bdot_nt(qh, k_ref[0, pl.ds(k0, nk), hs]) * NA_SCALE + bias_ref[0, h]
        s_ctx = _bdot_nt(qh, kc_ref[0, :, hs]) * NA_SCALE
        o_ref[0, :, hs] = _softmax_pv([(s_loc, v_ref[0, pl.ds(k0, nk), hs]),
                                       (s_ctx, vc_ref[0, :, hs])]).astype(o_ref.dtype)


def _latent_na(q, k, v, k_ctx, v_ctx, rpb):
    b, l, _ = q.shape
    lc = k_ctx.shape[1]
    rows = l // GRID_W
    assert rows >= NA_WIN and rows % NA_QROWS == 0 and NA_QROWS * 2 <= NA_ROWS
    nrb = rows // NA_QROWS
    tq = NA_QROWS * GRID_W
    bias = _na_bias_tables(rpb, rows)
    variant = lambda i, r: (jnp.where(r == 0, 0, jnp.where(r == nrb - 1, 2, 1)), 0, 0, 0)
    return pl.pallas_call(
        functools.partial(_na_body, rows=rows),
        grid=(b, nrb),
        in_specs=[pl.BlockSpec((1, tq, NA_WIDTH), lambda i, r: (i, r, 0)),
                  pl.BlockSpec((1, l, NA_WIDTH), lambda i, r: (i, 0, 0)),
                  pl.BlockSpec((1, l, NA_WIDTH), lambda i, r: (i, 0, 0)),
                  pl.BlockSpec((1, lc, NA_WIDTH), lambda i, r: (i, 0, 0)),
                  pl.BlockSpec((1, lc, NA_WIDTH), lambda i, r: (i, 0, 0)),
                  pl.BlockSpec((1, NA_HEADS, tq, NA_WIN * GRID_W), variant)],
        out_specs=pl.BlockSpec((1, tq, NA_WIDTH), lambda i, r: (i, r, 0)),
        out_shape=jax.ShapeDtypeStruct((b, l, NA_WIDTH), BF16),
        compiler_params=_cparams("arbitrary", "arbitrary"),
        name="latent_na",
    )(q, k, v, k_ctx, v_ctx, bias)


def _s5_prep_body(lre_ref, lim_ref, lst_ref, btre_ref, btim_ref, cre_ref, cim_ref,
                  t_ref, g_ref, e_ref, pw_ref, g_s, et_s, e0_s):
    q, ch, p = S5_Q, S5_GROUP_CH, S5_STATE
    lane = lax.broadcasted_iota(jnp.int32, (ch, S5_CW), 1)
    bt_re, bt_im = btre_ref[0], btim_ref[0]
    c_re, c_im = cre_ref[0], cim_ref[0]
    taps = []
    for d in range(2):
        lam_re, lam_im = lre_ref[d, 0], lim_ref[d, 0]
        step = jnp.exp(lst_ref[d, 0])
        mag = jnp.exp(lam_re * step)
        a_re = mag * jnp.cos(lam_im * step)
        a_im = mag * jnp.sin(lam_im * step)
        den = lam_re * lam_re + lam_im * lam_im
        k_re = ((a_re - 1.0) * lam_re + a_im * lam_im) / den
        k_im = (a_im * lam_re - (a_re - 1.0) * lam_im) / den
        bb_re = k_re * bt_re - k_im * bt_im
        bb_im = k_re * bt_im + k_im * bt_re
        p_re = jnp.ones_like(a_re)
        p_im = jnp.zeros_like(a_re)
        for t in range(q):
            tg = q - 1 - t if d == 0 else t
            te = t if d == 0 else q - 1 - t
            g_s[tg * ch:(tg + 1) * ch, 2 * p * d:2 * p * d + p] = p_re * bb_re - p_im * bb_im
            g_s[tg * ch:(tg + 1) * ch, 2 * p * d + p:2 * p * (d + 1)] = p_re * bb_im + p_im * bb_re
            e0_s[te * ch:(te + 1) * ch, 0:p] = c_re * p_re - c_im * p_im
            e0_s[te * ch:(te + 1) * ch, p:2 * p] = -(c_re * p_im + c_im * p_re)
            p_re, p_im = p_re * a_re - p_im * a_im, p_re * a_im + p_im * a_re
            et_s[d, te * ch:(te + 1) * ch, 0:p] = c_re * p_re - c_im * p_im
            et_s[d, te * ch:(te + 1) * ch, p:2 * p] = -(c_re * p_im + c_im * p_re)
        bb = jnp.concatenate([bb_re, bb_im], axis=1)
        taps.append(lax.dot_general(bb, e0_s[...], (((1,), (1,)), ((), ())), precision=HIGHEST,
                                    preferred_element_type=F32))
        pw_ref[d, 0] = jnp.zeros(pw_ref.shape[2:], F32)
        for k in range(S5_LOG_MAX):
            pw_ref[d, 0, 2 * k:2 * k + 1, :] = jnp.concatenate([p_re, p_re], axis=1)
            pw_ref[d, 0, 2 * k + 1:2 * k + 2, :] = jnp.concatenate([-p_im, p_im], axis=1)
            p_re, p_im = p_re * p_re - p_im * p_im, 2.0 * p_re * p_im
    for ti in range(q):
        fwd = taps[0] if ti == 0 else pltpu.roll(taps[0], ch * ti, axis=1)
        bwd = taps[1] if ti == q - 1 else pltpu.roll(taps[1], S5_CW - ch * (q - 1 - ti), axis=1)
        row = jnp.where(lane >= ch * ti, fwd, 0.0) + jnp.where(lane < ch * (ti + 1), bwd, 0.0)
        t_ref[0, ti * ch:(ti + 1) * ch, :] = row.astype(t_ref.dtype)
    g_ref[0] = g_s[...].astype(g_ref.dtype)
    e_ref[0, 0:2 * p, :] = et_s[0].T.astype(e_ref.dtype)
    e_ref[0, 2 * p:4 * p, :] = et_s[1].T.astype(e_ref.dtype)


def _s5_tables(lam_re, lam_im, log_step, b_re, b_im, c_re, c_im):
    g, p, ch = S5_GROUPS, S5_STATE, S5_GROUP_CH
    row = lambda a: a.astype(F32).reshape(2, g, 1, p)
    lst = jnp.broadcast_to(log_step.astype(F32)[:, :, None, None], (2, g, 1, p))
    bt_re = jnp.swapaxes(b_re.astype(F32), 1, 2)
    bt_im = jnp.swapaxes(b_im.astype(F32), 1, 2)
    dspec = pl.BlockSpec((2, 1, 1, p), lambda i: (0, i, 0, 0))
    gspec = pl.BlockSpec((1, ch, p), lambda i: (i, 0, 0))
    tspec = pl.BlockSpec((1, S5_CW, S5_CW), lambda i: (i, 0, 0))
    tshape = jax.ShapeDtypeStruct((g, S5_CW, S5_CW), BF16)
    return pl.pallas_call(
        _s5_prep_body,
        grid=(g,),
        in_specs=[dspec, dspec, dspec, gspec, gspec, gspec, gspec],
        out_specs=[tspec, tspec, tspec, pl.BlockSpec((2, 1, 16, 2 * p), lambda i: (0, i, 0, 0))],
        out_shape=[tshape, tshape, tshape, jax.ShapeDtypeStruct((2, g, 16, 2 * p), F32)],
        scratch_shapes=[pltpu.VMEM((S5_CW, 4 * p), F32), pltpu.VMEM((2, S5_CW, 2 * p), F32),
                        pltpu.VMEM((S5_CW, 2 * p), F32)],
        compiler_params=_cparams("arbitrary"),
        name="s5_prep",
    )(row(lam_re), row(lam_im), lst, bt_re, bt_im, c_re.astype(F32), c_im.astype(F32))


def _cmul(a_full, a_sgn, s):
    return a_full * s + a_sgn * pltpu.roll(s, S5_STATE, axis=1)


def _s5_body(u_ref, t_ref, g_ref, e_ref, pw_ref, dsk_ref, *rest, nb, nc, has_s0, want_final):
    rest = list(rest)
    s0_ref = rest.pop(0) if has_s0 else None
    o_ref = rest.pop(0)
    fin_ref = rest.pop(0) if want_final else None
    yt_s, xs_f, xs_b, tok_s = rest
    m = nb * nc
    w = 2 * S5_STATE
    ch = S5_GROUP_CH
    cidx = lax.broadcasted_iota(jnp.int32, (m, w), 0) & (nc - 1)
    first = cidx == 0
    last = cidx == nc - 1

    def tokens(t):
        return u_ref[:, pl.ds(t, nc, stride=S5_Q), :].reshape(m, LANES)

    for t in range(S5_Q):
        tok_s[t] = tokens(t).T

    for gg in range(S5_LG):
        ug_t = jnp.concatenate([tok_s[t, gg * ch:(gg + 1) * ch, :] for t in range(S5_Q)], axis=0)
        ub = ug_t.T.astype(BF16)
        z = jnp.dot(ub, g_ref[gg], preferred_element_type=F32)
        x_f, x_b = z[:, :w], z[:, w:]
        pw = lambda d, r: pw_ref[d, gg, r:r + 1, :]
        if has_s0:
            rep = lambda a: jnp.broadcast_to(a[:, None, :], (nb, nc, w)).reshape(m, w)
            s0_f, s0_b = rep(s0_ref[0, gg, :, :w]), rep(s0_ref[0, gg, :, w:])
            x_f = x_f + jnp.where(first, _cmul(pw(0, 0), pw(0, 1), s0_f), 0.0)
            x_b = x_b + jnp.where(last, _cmul(pw(1, 0), pw(1, 1), s0_b), 0.0)
        for k in range(nc.bit_length() - 1):
            sh = 1 << k
            prev = pltpu.roll(x_f, sh, axis=0)
            x_f = x_f + jnp.where(cidx >= sh, _cmul(pw(0, 2 * k), pw(0, 2 * k + 1), prev), 0.0)
            nxt = pltpu.roll(x_b, m - sh, axis=0)
            x_b = x_b + jnp.where(cidx < nc - sh, _cmul(pw(1, 2 * k), pw(1, 2 * k + 1), nxt), 0.0)
        in_f = pltpu.roll(x_f, 1, axis=0)
        in_b = pltpu.roll(x_b, m - 1, axis=0)
        if has_s0:
            in_f = jnp.where(first, s0_f, in_f)
            in_b = jnp.where(last, s0_b, in_b)
        else:
            in_f = jnp.where(first, 0.0, in_f)
            in_b = jnp.where(last, 0.0, in_b)
        s_in = jnp.concatenate([in_f, in_b], axis=1).astype(BF16)
        y = (jnp.dot(ub, t_ref[gg], preferred_element_type=F32)
             + jnp.dot(s_in, e_ref[gg], preferred_element_type=F32))
        yt_s[gg] = y.T
        if want_final:
            xs_f[...] = x_f
            xs_b[...] = x_b
            fin_ref[0, gg, :, :w] = xs_f[pl.ds(nc - 1, nb, stride=nc), :]
            fin_ref[0, gg, :, w:] = xs_b[pl.ds(0, nb, stride=nc), :]

    for t in range(S5_Q):
        y_t = jnp.concatenate([yt_s[k, t * ch:(t + 1) * ch, :] for k in range(S5_LG)], axis=0)
        y = y_t.T + tokens(t) * dsk_ref[0]
        o_ref[:, pl.ds(t, nc, stride=S5_Q), :] = _gelu_tanh(y).reshape(nb, nc, LANES)


def _s5(u, tables, d_skip, s0, want_final):
    t_all, g_fb, e_fb, pows = tables
    b, l, _ = u.shape
    nc = l // S5_Q
    bb = min(b, max(1, S5_ROWS // nc))
    nbb = b // bb
    m = bb * nc
    has_s0 = s0 is not None
    seq_spec = pl.BlockSpec((bb, l, LANES), lambda i, j: (j, 0, i))
    tab_spec = pl.BlockSpec((S5_LG, S5_CW, S5_CW), lambda i, j: (i, 0, 0),
                            pipeline_mode=pl.Buffered(1))
    st_spec = pl.BlockSpec((1, S5_LG, bb, S5_CW), lambda i, j: (j, i, 0, 0))
    st_blocks = lambda a: jnp.transpose(a.reshape(S5_GROUPS, nbb, bb, S5_CW), (1, 0, 2, 3))
    in_specs = [seq_spec, tab_spec, tab_spec, tab_spec,
                pl.BlockSpec((2, S5_LG, 16, 2 * S5_STATE), lambda i, j: (0, i, 0, 0)),
                pl.BlockSpec((1, 1, LANES), lambda i, j: (i, 0, 0))]
    args = [u, t_all, g_fb, e_fb, pows, d_skip.astype(F32).reshape(S5_WIDTH // LANES, 1, LANES)]
    if has_s0:
        in_specs.append(st_spec)
        args.append(st_blocks(s0))
    out_specs = [seq_spec]
    out_shape = [jax.ShapeDtypeStruct((b, l, S5_WIDTH), F32)]
    if want_final:
        out_specs.append(st_spec)
        out_shape.append(jax.ShapeDtypeStruct((nbb, S5_GROUPS, bb, S5_CW), F32))
    res = pl.pallas_call(
        functools.partial(_s5_body, nb=bb, nc=nc, has_s0=has_s0, want_final=want_final),
        grid=(S5_WIDTH // LANES, nbb),
        in_specs=in_specs,
        out_specs=out_specs,
        out_shape=out_shape,
        scratch_shapes=[pltpu.VMEM((S5_LG, S5_CW, m), F32),
                        pltpu.VMEM((m, 2 * S5_STATE), F32), pltpu.VMEM((m, 2 * S5_STATE), F32),
                        pltpu.VMEM((S5_Q, LANES, m), F32)],
        compiler_params=_cparams("arbitrary", "arbitrary"),
        name="s5",
    )(*args)
    fin = None
    if want_final:
        fin = jnp.transpose(res[1], (1, 0, 2, 3)).reshape(S5_GROUPS, b, S5_CW)
    return res[0], fin


def _dft_cos_sin(n, rows=None, row_step=1):
    rows = n if rows is None else rows
    j = jnp.arange(rows, dtype=jnp.int32)[:, None] * row_step
    jk = (j * jnp.arange(n, dtype=jnp.int32)[None, :]) % n
    ang = jk.astype(F32) * (2.0 * math.pi / n)
    return jnp.cos(ang), jnp.sin(ang)


def _dft_cos_sin_split(n, minor):
    c1, s1 = _dft_cos_sin(n, n // minor, minor)
    c2, s2 = _dft_cos_sin(n, minor)
    cos = c1[:, None, :] * c2[None, :, :] - s1[:, None, :] * s2[None, :, :]
    sin = s1[:, None, :] * c2[None, :, :] + c1[:, None, :] * s2[None, :, :]
    return cos.reshape(n, n), sin.reshape(n, n)


def _fnet_body(u_ref, cs_ref, dl_ref, o_ref, xs, *, seq):
    @pl.when(pl.program_id(1) == 0)
    def _():
        xcs = jnp.dot(u_ref[0].astype(BF16), cs_ref[...], preferred_element_type=F32)
        xs[0:seq, :] = xcs[:, :FNET_WIDTH].astype(BF16)
        xs[seq:2 * seq, :] = xcs[:, FNET_WIDTH:].astype(BF16)

    scale = 1.0 / math.sqrt(seq * FNET_GROUP_CH)
    o_ref[0] = (jnp.dot(dl_ref[...], xs[...], preferred_element_type=F32) * scale).astype(o_ref.dtype)


def _fnet(u):
    b, l, _ = u.shape
    tr = min(l, 512)
    cc, sc = _dft_cos_sin(FNET_GROUP_CH)
    eye = jnp.eye(FNET_GROUPS, dtype=F32)
    cs = jnp.concatenate([jnp.kron(eye, cc), jnp.kron(eye, sc)], axis=1).astype(BF16)
    cl, sl = _dft_cos_sin_split(l, DFT_MINOR) if l > 4 * DFT_MINOR else _dft_cos_sin(l)
    dl = jnp.concatenate([cl, -sl], axis=1).astype(BF16)
    return pl.pallas_call(
        functools.partial(_fnet_body, seq=l),
        grid=(b, l // tr),
        in_specs=[pl.BlockSpec((1, l, FNET_WIDTH), lambda i, t: (i, 0, 0)),
                  _const_spec((FNET_WIDTH, 2 * FNET_WIDTH)),
                  pl.BlockSpec((tr, 2 * l), lambda i, t: (t, 0))],
        out_specs=pl.BlockSpec((1, tr, FNET_WIDTH), lambda i, t: (i, t, 0)),
        out_shape=jax.ShapeDtypeStruct((b, l, FNET_WIDTH), BF16),
        scratch_shapes=[pltpu.VMEM((2 * l, FNET_WIDTH), BF16)],
        compiler_params=_cparams("arbitrary", "arbitrary"),
        name="fnet",
    )(u, cs, dl)


def _even_w_in(w):
    pad = jnp.zeros((D_MODEL, LANES - SSD_HEADS), w.dtype)
    return jnp.concatenate([w[:, :O_DT], w[:, O_Q:], w[:, O_DT:O_Q], pad], axis=1).astype(BF16)


EVEN_SPLITS = (SSD_WIDTH, SSD_XBC, NA_WIDTH, NA_WIDTH, NA_WIDTH, LANES)


def kernel(x_prompt, x_sample, cache_na_k, cache_na_v, state_ssd, state_s5, c, c_ctx, mod_w, mod_b, norm1_g, norm2_g, ffn_w1, ffn_w3, ffn_w2, final_g, ev_w_in, ev_conv_w, ev_conv_b, ev_a_log, ev_dt_bias, ev_d_skip, ev_ssd_norm_g, ev_rpb, ev_w_out, od_w_in, od_lam_re, od_lam_im, od_log_step, od_b_re, od_b_im, od_c_re, od_c_im, od_d_skip, od_glu_w, od_glu_b, od_w_out):
    bc, lc, _ = x_prompt.shape
    bl, ll, _ = x_sample.shape
    cond = jnp.zeros((MOD_ROWS, D_MODEL), F32).at[0].set(c_ctx).at[1:1 + bl].set(c)
    mods = _adaln(cond, mod_w, mod_b)
    xc = x_prompt.reshape(1, bc * lc, D_MODEL)
    xl = x_sample
    tm = 512
    new_k, new_v, new_ssd, new_s5 = [], [], [], []
    for i in range(DEPTH):
        j = i // 2
        mod = mods[i]
        last = i == DEPTH - 1
        if i % 2 == 0:
            w_in = _even_w_in(ev_w_in[j])
            w_out = ev_w_out[j].astype(BF16)
            ssd_args = (ev_conv_w[j], ev_conv_b[j], ev_a_log[j], ev_dt_bias[j], ev_d_skip[j],
                        ev_ssd_norm_g[j])
            z, xbc, q, k, v, dtp, k_hd, v_hd = _inproj(
                xc, mod, 0, norm1_g[i], w_in, EVEN_SPLITS, tm, (F32, F32, BF16, BF16, BF16, F32), (3, 4))
            seqs = lambda a: a.reshape(bc, lc, a.shape[-1])
            y_ssd, fin = _ssd(seqs(z), seqs(xbc), seqs(dtp), *ssd_args, None, True)
            o_na = _ctx_attn(seqs(q), seqs(k), seqs(v))
            flat = lambda a: a.reshape(1, bc * lc, a.shape[-1])
            mix_c = (flat(y_ssd), flat(o_na))
            new_k.append(k_hd.reshape(bc, lc, NA_HEADS, NA_HEAD_DIM))
            new_v.append(v_hd.reshape(bc, lc, NA_HEADS, NA_HEAD_DIM))
            new_ssd.append(_ssd_state_from_kernel(fin))

            z, xbc, q, k, v, dtp = _inproj(xl, mod, 1, norm1_g[i], w_in, EVEN_SPLITS, tm,
                                           (F32, F32, BF16, BF16, BF16, F32))
            s0t = _ssd_state_to_kernel(state_ssd[:, j].astype(F32))
            (y_ssd,) = _ssd(z, xbc, dtp, *ssd_args, s0t, False)
            pl_ = cache_na_k.shape[2]
            o_na = _latent_na(q, k, v, cache_na_k[:, j].reshape(bl, pl_, NA_WIDTH),
                              cache_na_v[:, j].reshape(bl, pl_, NA_WIDTH), ev_rpb[j])
            mix_l = (y_ssd, o_na)
            glu_w = glu_b = None
        else:
            w_in = od_w_in[j].astype(BF16)
            w_out = od_w_out[j].astype(BF16)
            glu_w, glu_b = od_glu_w[j].astype(BF16), od_glu_b[j]
            tables = _s5_tables(od_lam_re[j], od_lam_im[j], od_log_step[j], od_b_re[j], od_b_im[j],
                                od_c_re[j], od_c_im[j])
            splits = (S5_WIDTH, FNET_WIDTH)
            u_s, u_f = _inproj(xc, mod, 0, norm1_g[i], w_in, splits, tm, (F32, BF16))
            seqs = lambda a: a.reshape(bc, lc, a.shape[-1])
            g_s, fin = _s5(seqs(u_s), tables, od_d_skip[j], None, True)
            y_f = _fnet(seqs(u_f))
            flat = lambda a: a.reshape(1, bc * lc, a.shape[-1])
            mix_c = (flat(g_s), flat(y_f))
            fin = fin.reshape(S5_GROUPS, bc, 2, 2, S5_STATE)
            new_s5.append(jnp.transpose(fin, (1, 2, 0, 4, 3)))

            u_s, u_f = _inproj(xl, mod, 1, norm1_g[i], w_in, splits, tm, (F32, BF16))
            s0 = jnp.transpose(state_s5[:, j].astype(F32), (2, 0, 1, 4, 3)).reshape(
                S5_GROUPS, bl, 4 * S5_STATE)
            g_s, _ = _s5(u_s, tables, od_d_skip[j], s0, False)
            y_f = _fnet(u_f)
            mix_l = (g_s, y_f)
        w_o = (w_out[:SSD_WIDTH], w_out[SSD_WIDTH:])
        w1, w3, w2 = ffn_w1[i].astype(BF16), ffn_w3[i].astype(BF16), ffn_w2[i].astype(BF16)
        fg = final_g if last else None
        xc = _ffn(xc, mod, 0, *mix_c, *w_o, glu_w, glu_b, norm2_g[i], w1, w3, w2, fg, tm)
        xl = _ffn(xl, mod, 1, *mix_l, *w_o, glu_w, glu_b, norm2_g[i], w1, w3, w2, fg, tm)
    return (xc.reshape(bc, lc, D_MODEL), xl,
            jnp.stack(new_k, axis=1), jnp.stack(new_v, axis=1),
            jnp.stack(new_ssd, axis=1), jnp.stack(new_s5, axis=1))
```

```python
import functools
import math

import numpy as np
import jax
import jax.numpy as jnp
from jax import lax
from jax.experimental import pallas as pl
from jax.experimental.pallas import tpu as pltpu

F32 = jnp.float32
BF16 = jnp.bfloat16
HIGHEST = lax.Precision.HIGHEST

D_MODEL = 1024
DEPTH = 2
EPS = 1e-6
GRID_W = 64
FFN_HIDDEN = 2816
MXU_COLS = 256
FFN_CHUNKS = (6 * MXU_COLS, 5 * MXU_COLS)
assert sum(FFN_CHUNKS) == FFN_HIDDEN

SSD_HEADS = 8
SSD_HEAD_DIM = 64
SSD_WIDTH = 512
SSD_GROUPS = 2
SSD_HPG = SSD_HEADS // SSD_GROUPS
SSD_STATE = 64
SSD_GN = SSD_GROUPS * SSD_STATE
SSD_XBC = SSD_WIDTH + 2 * SSD_GN
SSD_CHUNK = 128
CONV_W = 5
CONV_HALO = 8

NA_HEADS = 8
NA_HEAD_DIM = 64
NA_WIDTH = 512
NA_ROWS = 8
NA_COLS = 16
NA_SCALE = NA_HEAD_DIM ** -0.5
NA_QROWS = 4
NA_WIN = NA_ROWS + NA_QROWS
NEG_BIG = -1e30

O_XBC = SSD_WIDTH
O_DT = O_XBC + SSD_XBC
O_Q = O_DT + SSD_HEADS
O_K = O_Q + NA_WIDTH
O_V = O_K + NA_WIDTH
LANES = 128

S5_GROUP_CH = 16
S5_GROUPS = 32
S5_WIDTH = 512
S5_STATE = 64
S5_Q = 16
S5_CW = S5_Q * S5_GROUP_CH
S5_LOG_MAX = 7
S5_LG = LANES // S5_GROUP_CH
S5_ROWS = 512

FNET_GROUP_CH = 64
FNET_GROUPS = 8
FNET_WIDTH = 512
DFT_MINOR = 64

MOD_ROWS = 16
VMEM_LIMIT = 56 * 1024 * 1024


def _cparams(*sem):
    return pltpu.CompilerParams(dimension_semantics=sem, vmem_limit_bytes=VMEM_LIMIT)


def _const_spec(shape):
    nd = len(shape)
    return pl.BlockSpec(shape, lambda *_: (0,) * nd, pipeline_mode=pl.Buffered(1))


def _sigmoid(x):
    return 1.0 / (1.0 + jnp.exp(-x))


def _silu(x):
    return x * _sigmoid(x)


def _softplus(x):
    return jnp.maximum(x, 0.0) + jnp.log(1.0 + jnp.exp(-jnp.abs(x)))


def _gelu_tanh(x):
    return 0.5 * x * (1.0 + jnp.tanh(math.sqrt(2.0 / math.pi) * (x + 0.044715 * (x * x * x))))


def _rmsnorm(x, g):
    return x * lax.rsqrt(jnp.mean(x * x, axis=-1, keepdims=True) + EPS) * g


def _bdot(a, b):
    return jnp.dot(a.astype(BF16), b.astype(BF16), preferred_element_type=F32)


def _bdot_nt(a, b):
    return lax.dot_general(a.astype(BF16), b.astype(BF16), (((1,), (1,)), ((), ())),
                           preferred_element_type=F32)


def _bdot_tn(a, b):
    return lax.dot_general(a.astype(BF16), b.astype(BF16), (((0,), (0,)), ((), ())),
                           preferred_element_type=F32)


def _adaln_body(c_ref, w_ref, b_ref, o_ref):
    s = _silu(c_ref[...])
    o_ref[0] = jnp.dot(s, w_ref[0], precision=HIGHEST, preferred_element_type=F32) + b_ref[0]


def _adaln(cond, mod_w, mod_b):
    n = 6 * D_MODEL
    tn = 1536
    out = pl.pallas_call(
        _adaln_body,
        grid=(DEPTH, n // tn),
        in_specs=[pl.BlockSpec((MOD_ROWS, D_MODEL), lambda i, j: (0, 0)),
                  pl.BlockSpec((1, D_MODEL, tn), lambda i, j: (i, 0, j)),
                  pl.BlockSpec((1, 1, tn), lambda i, j: (i, 0, j))],
        out_specs=pl.BlockSpec((1, MOD_ROWS, tn), lambda i, j: (i, 0, j)),
        out_shape=jax.ShapeDtypeStruct((DEPTH, MOD_ROWS, n), F32),
        compiler_params=_cparams("arbitrary", "arbitrary"),
        name="adaln",
    )(cond, mod_w, mod_b.reshape(DEPTH, 1, n))
    return out.reshape(DEPTH, MOD_ROWS, 6, D_MODEL)


def _mod_spec(mrow0):
    return pl.BlockSpec((1, 6, D_MODEL), lambda b, t: (b + mrow0, 0, 0))


def _inproj_body(x_ref, mod_ref, g_ref, w_ref, *o_refs, splits, heads_of):
    h = _rmsnorm(x_ref[0], g_ref[...]) * (1.0 + mod_ref[0, 1:2, :]) + mod_ref[0, 0:1, :]
    acc = jnp.dot(h.astype(BF16), w_ref[...], preferred_element_type=F32)
    off = 0
    offs = []
    for o_ref, n in zip(o_refs, splits):
        o_ref[0] = acc[:, off:off + n].astype(o_ref.dtype)
        offs.append(off)
        off += n
    for o_ref, i in zip(o_refs[len(splits):], heads_of):
        nh, hd = o_ref.shape[2], o_ref.shape[3]
        for hh in range(nh):
            o_ref[0, :, hh, :] = acc[:, offs[i] + hh * hd:offs[i] + (hh + 1) * hd]


def _inproj(x, mod, mrow0, g, w, splits, tm, dtypes=None, heads_of=()):
    b, l, _ = x.shape
    n = w.shape[1]
    dtypes = (F32,) * len(splits) if dtypes is None else dtypes
    head_shape = lambda i: (splits[i] // NA_HEAD_DIM, NA_HEAD_DIM)
    return pl.pallas_call(
        functools.partial(_inproj_body, splits=splits, heads_of=heads_of),
        grid=(b, l // tm),
        in_specs=[pl.BlockSpec((1, tm, D_MODEL), lambda i, t: (i, t, 0)),
                  _mod_spec(mrow0),
                  _const_spec((1, D_MODEL)),
                  _const_spec((D_MODEL, n))],
        out_specs=([pl.BlockSpec((1, tm, s), lambda i, t: (i, t, 0)) for s in splits]
                   + [pl.BlockSpec((1, tm) + head_shape(i), lambda i_, t: (i_, t, 0, 0))
                      for i in heads_of]),
        out_shape=([jax.ShapeDtypeStruct((b, l, s), dt) for s, dt in zip(splits, dtypes)]
                   + [jax.ShapeDtypeStruct((b, l) + head_shape(i), F32) for i in heads_of]),
        compiler_params=_cparams("arbitrary", "arbitrary"),
        name="inproj",
    )(x, mod, g.reshape(1, D_MODEL), w)


def _ffn_body(x_ref, mod_ref, ya_ref, yb_ref, wa_ref, wb_ref, g_ref, w1_ref, w3_ref, w2_ref, *rest,
              glu, final):
    rest = list(rest)
    ya = ya_ref[0]
    if glu:
        gw_ref, gb_ref = rest.pop(0), rest.pop(0)
        ya = ya * _sigmoid(_bdot(ya, gw_ref[...]) + gb_ref[...])
    mix = _bdot(ya, wa_ref[...]) + _bdot(yb_ref[0], wb_ref[...])
    x = x_ref[0] + mod_ref[0, 2:3, :] * mix
    h = (_rmsnorm(x, g_ref[...]) * (1.0 + mod_ref[0, 4:5, :]) + mod_ref[0, 3:4, :]).astype(BF16)
    acc = jnp.zeros(x.shape, F32)
    for c in range(len(FFN_CHUNKS)):
        cols = slice(sum(FFN_CHUNKS[:c]), sum(FFN_CHUNKS[:c + 1]))
        a = jnp.dot(h, w1_ref[:, cols], preferred_element_type=F32)
        u = _silu(a) * jnp.dot(h, w3_ref[:, cols], preferred_element_type=F32)
        acc = acc + jnp.dot(u.astype(BF16), w2_ref[cols, :], preferred_element_type=F32)
    y = x + mod_ref[0, 5:6, :] * acc
    if final:
        fg_ref, o_ref = rest
        y = _rmsnorm(y, fg_ref[...])
    else:
        (o_ref,) = rest
    o_ref[0] = y


def _ffn(x, mod, mrow0, ya, yb, wa, wb, glu_w, glu_b, g, w1, w3, w2, final_g, tm):
    b, l, _ = x.shape
    wa_n, wb_n = ya.shape[-1], yb.shape[-1]
    glu = glu_w is not None
    final = final_g is not None
    row_spec = lambda n: pl.BlockSpec((1, tm, n), lambda i, t: (i, t, 0))
    in_specs = [row_spec(D_MODEL), _mod_spec(mrow0), row_spec(wa_n), row_spec(wb_n),
                _const_spec((wa_n, D_MODEL)), _const_spec((wb_n, D_MODEL)),
                _const_spec((1, D_MODEL)),
                _const_spec((D_MODEL, FFN_HIDDEN)),
                _const_spec((D_MODEL, FFN_HIDDEN)),
                _const_spec((FFN_HIDDEN, D_MODEL))]
    args = [x, mod, ya, yb, wa, wb, g.reshape(1, D_MODEL), w1, w3, w2]
    if glu:
        in_specs += [_const_spec((wa_n, wa_n)), _const_spec((1, wa_n))]
        args += [glu_w, glu_b.reshape(1, wa_n)]
    if final:
        in_specs.append(_const_spec((1, D_MODEL)))
        args.append(final_g.reshape(1, D_MODEL))
    return pl.pallas_call(
        functools.partial(_ffn_body, glu=glu, final=final),
        grid=(b, l // tm),
        in_specs=in_specs,
        out_specs=pl.BlockSpec((1, tm, D_MODEL), lambda i, t: (i, t, 0)),
        out_shape=jax.ShapeDtypeStruct((b, l, D_MODEL), F32),
        compiler_params=_cparams("arbitrary", "arbitrary"),
        name="ffn",
    )(*args)


def _ssd_body(z_ref, xbc_ref, dt_ref, cw_ref, cb_ref, hp_ref, dsk_ref, ng_ref, *rest,
              seq, has_s0, want_final):
    rest = list(rest)
    s0_ref = rest.pop(0) if has_s0 else None
    y_ref = rest.pop(0)
    fin_ref = rest.pop(0) if want_final else None
    st_s, cd_s, scur, ych, xc_s, cum_s, tr_s = rest
    q = SSD_CHUNK
    nc = seq // q
    gw = SSD_HPG * SSD_HEAD_DIM
    expand = (lax.broadcasted_iota(jnp.int32, (LANES, SSD_WIDTH), 1) // SSD_HEAD_DIM
              == lax.broadcasted_iota(jnp.int32, (LANES, SSD_WIDTH), 0)).astype(BF16)

    def per_head(v):
        hi = v.astype(BF16)
        lo = (v - hi.astype(F32)).astype(BF16)
        return (jnp.dot(hi, expand, preferred_element_type=F32)
                + jnp.dot(lo, expand, preferred_element_type=F32))
    row = lax.broadcasted_iota(jnp.int32, (q, q), 0)
    col = lax.broadcasted_iota(jnp.int32, (q, q), 1)
    lower = row >= col
    upper = col >= row
    tri_l = lower.astype(F32)
    tri_u = upper.astype(F32)
    a_f = -jnp.exp(hp_ref[0:1, :])
    a_b = -jnp.exp(hp_ref[1:2, :])
    lane = lax.broadcasted_iota(jnp.int32, (1, LANES), 1)
    a_f = jnp.where(lane < SSD_HEADS, a_f, 0.0)
    a_b = jnp.where(lane < SSD_HEADS, a_b, 0.0)
    bias_f = hp_ref[2:3, :]
    bias_b = hp_ref[3:4, :]

    def chunk_pre(c):
        r0 = pl.multiple_of(c * q, q)
        lo = pl.multiple_of(jnp.maximum(r0 - CONV_HALO, 0), CONV_HALO)
        hi = pl.multiple_of(jnp.minimum(r0 + q, seq - CONV_HALO), CONV_HALO)
        prev = jnp.where(c > 0, xbc_ref[0, pl.ds(lo, CONV_HALO), :], 0.0)
        nxt = jnp.where(c < nc - 1, xbc_ref[0, pl.ds(hi, CONV_HALO), :], 0.0)
        win = jnp.concatenate([prev, xbc_ref[0, pl.ds(r0, q), :], nxt], axis=0)
        acc = cb_ref[...] + cw_ref[0:1, :] * win[CONV_HALO - 2:CONV_HALO - 2 + q]
        for k in range(1, CONV_W):
            o = CONV_HALO - CONV_W // 2 + k
            acc = acc + cw_ref[k:k + 1, :] * win[o:o + q]
        xc = _silu(acc)
        dtr = dt_ref[0, pl.ds(r0, q), :]
        dt_f = _softplus(dtr + bias_f)
        dt_b = _softplus(dtr + bias_b)
        cum_f = jnp.dot(tri_l, dt_f * a_f, precision=HIGHEST, preferred_element_type=F32)
        cum_b = jnp.dot(tri_u, dt_b * a_b, precision=HIGHEST, preferred_element_type=F32)
        return r0, xc, dt_f, dt_b, cum_f, cum_b

    def pass_a(c, carry):
        r0, xc, dt_f, dt_b, cum_f, cum_b = chunk_pre(c)
        xc_s[pl.ds(r0, q), :] = xc
        cum_s[0, pl.ds(r0, q), :] = cum_f
        cum_s[1, pl.ds(r0, q), :] = cum_b
        for i, v in enumerate((cum_f, cum_b, dt_f, dt_b)):
            tr_s[c, i] = v.T[0:SSD_HEADS, :]
        end_f = cum_f[q - 1:q, :]
        end_b = cum_b[0:1, :]
        xw_f = xc[:, :SSD_WIDTH] * per_head(jnp.exp(end_f - cum_f) * dt_f)
        xw_b = xc[:, :SSD_WIDTH] * per_head(jnp.exp(end_b - cum_b) * dt_b)
        cd_s[0, c] = per_head(jnp.broadcast_to(jnp.exp(end_f), (8, LANES)))
        cd_s[1, c] = per_head(jnp.broadcast_to(jnp.exp(end_b), (8, LANES)))
        for g in range(SSD_GROUPS):
            bg = xc[:, SSD_WIDTH + g * SSD_STATE:SSD_WIDTH + (g + 1) * SSD_STATE]
            st_s[0, c, g] = _bdot_tn(bg, xw_f[:, g * gw:(g + 1) * gw])
            st_s[1, c, g] = _bdot_tn(bg, xw_b[:, g * gw:(g + 1) * gw])
        return carry

    lax.fori_loop(0, nc, pass_a, 0)

    if has_s0:
        scur[...] = s0_ref[0]
    else:
        scur[...] = jnp.zeros(scur.shape, F32)

    def pass_b(c, carry):
        cr = nc - 1 - c
        for g in range(SSD_GROUPS):
            s_in = scur[0, g]
            scur[0, g] = s_in * cd_s[0, c, 0:1, g * gw:(g + 1) * gw] + st_s[0, c, g]
            st_s[0, c, g] = s_in
            s_in = scur[1, g]
            scur[1, g] = s_in * cd_s[1, cr, 0:1, g * gw:(g + 1) * gw] + st_s[1, cr, g]
            st_s[1, cr, g] = s_in
        return carry

    lax.fori_loop(0, nc, pass_b, 0)
    if want_final:
        fin_ref[0] = scur[...]

    def pass_c(c, carry):
        r0 = pl.multiple_of(c * q, q)
        xc = xc_s[pl.ds(r0, q), :]
        cum_f = cum_s[0, pl.ds(r0, q), :]
        cum_b = cum_s[1, pl.ds(r0, q), :]
        cum_ft, cum_bt, dt_ft, dt_bt = tr_s[c, 0], tr_s[c, 1], tr_s[c, 2], tr_s[c, 3]
        ex_f = per_head(jnp.exp(cum_f))
        ex_b = per_head(jnp.exp(cum_b))
        y_off = []
        cbs = []
        for g in range(SSD_GROUPS):
            gs = slice(g * gw, (g + 1) * gw)
            bg = xc[:, SSD_WIDTH + g * SSD_STATE:SSD_WIDTH + (g + 1) * SSD_STATE]
            cg = xc[:, SSD_WIDTH + SSD_GN + g * SSD_STATE:SSD_WIDTH + SSD_GN + (g + 1) * SSD_STATE]
            cbs.append(_bdot_nt(cg, bg))
            y_off.append(ex_f[:, gs] * _bdot(cg, st_s[0, c, g]) + ex_b[:, gs] * _bdot(cg, st_s[1, c, g]))
        for h in range(SSD_HEADS):
            xh = xc[:, h * SSD_HEAD_DIM:(h + 1) * SSD_HEAD_DIM]
            seg_f = cum_f[:, h:h + 1] - cum_ft[h:h + 1, :]
            seg_b = cum_b[:, h:h + 1] - cum_bt[h:h + 1, :]
            m_f = jnp.exp(jnp.where(lower, seg_f, NEG_BIG)) * dt_ft[h:h + 1, :]
            m_b = jnp.exp(jnp.where(upper, seg_b, NEG_BIG)) * dt_bt[h:h + 1, :]
            ych[:, h * SSD_HEAD_DIM:(h + 1) * SSD_HEAD_DIM] = _bdot(cbs[h // SSD_HPG] * (m_f + m_b), xh)
        yf = ych[...] + jnp.concatenate(y_off, axis=1) + dsk_ref[...] * xc[:, :SSD_WIDTH]
        yf = yf * _silu(z_ref[0, pl.ds(r0, q), :])
        y_ref[0, pl.ds(r0, q), :] = _rmsnorm(yf, ng_ref[...]).astype(y_ref.dtype)
        return carry

    lax.fori_loop(0, nc, pass_c, 0)


def _ssd_state_to_kernel(s):
    b = s.shape[0]
    s = s.reshape(b, 2, SSD_GROUPS, SSD_HPG, SSD_HEAD_DIM, SSD_STATE)
    return jnp.transpose(s, (0, 1, 2, 5, 3, 4)).reshape(b, 2, SSD_GROUPS, SSD_STATE, SSD_HPG * SSD_HEAD_DIM)


def _ssd_state_from_kernel(s):
    b = s.shape[0]
    s = s.reshape(b, 2, SSD_GROUPS, SSD_STATE, SSD_HPG, SSD_HEAD_DIM)
    return jnp.transpose(s, (0, 1, 2, 4, 5, 3)).reshape(b, 2, SSD_HEADS, SSD_HEAD_DIM, SSD_STATE)


def _ssd(z, xbc, dtp, conv_w, conv_b, a_log, dt_bias, d_skip, norm_g, s0t, want_final):
    b, l, _ = z.shape
    nc = l // SSD_CHUNK
    gw = SSD_HPG * SSD_HEAD_DIM
    has_s0 = s0t is not None
    cw = jnp.zeros((8, SSD_XBC), F32).at[:CONV_W].set(conv_w)
    hp = jnp.zeros((8, LANES), F32)
    hp = hp.at[0:2, :SSD_HEADS].set(a_log).at[2:4, :SSD_HEADS].set(dt_bias)
    dsk = jnp.repeat(d_skip, SSD_HEAD_DIM).reshape(1, SSD_WIDTH)
    seq_spec = lambda n: pl.BlockSpec((1, l, n), lambda i: (i, 0, 0))
    st_spec = pl.BlockSpec((1, 2, SSD_GROUPS, SSD_STATE, gw), lambda i: (i, 0, 0, 0, 0))
    in_specs = [seq_spec(SSD_WIDTH), seq_spec(SSD_XBC), seq_spec(LANES),
                _const_spec((8, SSD_XBC)), _const_spec((1, SSD_XBC)), _const_spec((8, LANES)),
                _const_spec((1, SSD_WIDTH)), _const_spec((1, SSD_WIDTH))]
    args = [z, xbc, dtp, cw, conv_b.reshape(1, SSD_XBC), hp, dsk, norm_g.reshape(1, SSD_WIDTH)]
    if has_s0:
        in_specs.append(st_spec)
        args.append(s0t)
    out_specs = [seq_spec(SSD_WIDTH)]
    out_shape = [jax.ShapeDtypeStruct((b, l, SSD_WIDTH), BF16)]
    if want_final:
        out_specs.append(st_spec)
        out_shape.append(jax.ShapeDtypeStruct((b, 2, SSD_GROUPS, SSD_STATE, gw), F32))
    return pl.pallas_call(
        functools.partial(_ssd_body, seq=l, has_s0=has_s0, want_final=want_final),
        grid=(b,),
        in_specs=in_specs,
        out_specs=out_specs,
        out_shape=out_shape,
        scratch_shapes=[pltpu.VMEM((2, nc, SSD_GROUPS, SSD_STATE, gw), F32),
                        pltpu.VMEM((2, nc, 8, SSD_WIDTH), F32),
                        pltpu.VMEM((2, SSD_GROUPS, SSD_STATE, gw), F32),
                        pltpu.VMEM((SSD_CHUNK, SSD_WIDTH), F32),
                        pltpu.VMEM((l, SSD_XBC), F32),
                        pltpu.VMEM((2, l, LANES), F32),
                        pltpu.VMEM((nc, 4, SSD_HEADS, LANES), F32)],
        compiler_params=_cparams("arbitrary"),
        name="ssd",
    )(*args)


def _softmax_pv(parts):
    m = parts[0][0].max(axis=-1, keepdims=True)
    for s, _ in parts[1:]:
        m = jnp.maximum(m, s.max(axis=-1, keepdims=True))
    den = 0.0
    out = 0.0
    for s, v in parts:
        p = jnp.exp(s - m)
        den = den + p.sum(axis=-1, keepdims=True)
        out = out + _bdot(p, v)
    return out / den


def _ctx_attn_body(q_ref, k_ref, v_ref, o_ref):
    for h in range(NA_HEADS):
        hs = slice(h * NA_HEAD_DIM, (h + 1) * NA_HEAD_DIM)
        s = _bdot_nt(q_ref[0, :, hs], k_ref[0, :, hs]) * NA_SCALE
        o_ref[0, :, hs] = _softmax_pv([(s, v_ref[0, :, hs])]).astype(o_ref.dtype)


def _ctx_attn(q, k, v):
    b, l, _ = q.shape
    spec = pl.BlockSpec((1, l, NA_WIDTH), lambda i: (i, 0, 0))
    return pl.pallas_call(
        _ctx_attn_body,
        grid=(b,),
        in_specs=[spec, spec, spec],
        out_specs=spec,
        out_shape=jax.ShapeDtypeStruct((b, l, NA_WIDTH), BF16),
        compiler_params=_cparams("arbitrary"),
        name="ctx_attn",
    )(q, k, v)


def _na_bias_tables(rpb, rows):
    qc = np.arange(GRID_W)[:, None]
    kc = np.arange(GRID_W)[None, :]
    cs = np.clip(qc - NA_COLS // 2, 0, GRID_W - NA_COLS)
    col_ok = (kc >= cs) & (kc < cs + NA_COLS)
    dc = np.clip(kc - qc + NA_COLS - 1, 0, 2 * NA_COLS - 2)
    col_sel = (dc[None] == np.arange(2 * NA_COLS - 1)[:, None, None]).astype(np.float32)
    by_col = jnp.einsum('hab,bqk->haqk', rpb.astype(F32), jnp.asarray(col_sel), precision=HIGHEST)
    by_col = jnp.where(jnp.asarray(col_ok), by_col, NEG_BIG)
    n_dr = 2 * NA_ROWS - 1
    return pl.pallas_call(
        functools.partial(_na_bias_body, rows=rows),
        grid=(3, NA_HEADS),
        in_specs=[pl.BlockSpec((1, n_dr, GRID_W, GRID_W), lambda v, h: (h, 0, 0, 0))],
        out_specs=pl.BlockSpec((1, 1, NA_QROWS * GRID_W, NA_WIN * GRID_W), lambda v, h: (v, h, 0, 0)),
        out_shape=jax.ShapeDtypeStruct((3, NA_HEADS, NA_QROWS * GRID_W, NA_WIN * GRID_W), F32),
        compiler_params=_cparams("arbitrary", "arbitrary"),
        name="na_bias",
    )(by_col)


def _na_bias_body(u_ref, o_ref, *, rows):
    v = pl.program_id(0)
    kr = min(NA_ROWS, rows)
    r0 = jnp.where(v == 0, 0, jnp.where(v == 1, NA_QROWS, rows - NA_QROWS))
    ws = jnp.clip(r0 - kr // 2, 0, rows - NA_WIN)
    for qi in range(NA_QROWS):
        r = r0 + qi
        rs = jnp.clip(r - kr // 2, 0, rows - kr)
        for w in range(NA_WIN):
            krow = ws + w
            ok = (krow >= rs) & (krow < rs + kr)
            dr = jnp.clip(krow - r + NA_ROWS - 1, 0, 2 * NA_ROWS - 2)
            o_ref[0, 0, qi * GRID_W:(qi + 1) * GRID_W, w * GRID_W:(w + 1) * GRID_W] = jnp.where(
                ok, u_ref[0, dr], NEG_BIG)


def _na_body(q_ref, k_ref, v_ref, kc_ref, vc_ref, bias_ref, o_ref, *, rows):
    rb = pl.program_id(1)
    ws = jnp.clip(rb * NA_QROWS - NA_ROWS // 2, 0, rows - NA_WIN)
    k0 = pl.multiple_of(ws * GRID_W, NA_QROWS * GRID_W)
    nk = NA_WIN * GRID_W
    for h in range(NA_HEADS):
        hs = slice(h * NA_HEAD_DIM, (h + 1) * NA_HEAD_DIM)
        qh = q_ref[0, :, hs]
        s_loc = _bdot_nt(qh, k_ref[0, pl.ds(k0, nk), hs]) * NA_SCALE + bias_ref[0, h]
        s_ctx = _bdot_nt(qh, kc_ref[0, :, hs]) * NA_SCALE
        o_ref[0, :, hs] = _softmax_pv([(s_loc, v_ref[0, pl.ds(k0, nk), hs]),
                                       (s_ctx, vc_ref[0, :, hs])]).astype(o_ref.dtype)


def _latent_na(q, k, v, k_ctx, v_ctx, rpb):
    b, l, _ = q.shape
    lc = k_ctx.shape[1]
    rows = l // GRID_W
    assert rows >= NA_WIN and rows % NA_QROWS == 0 and NA_QROWS * 2 <= NA_ROWS
    nrb = rows // NA_QROWS
    tq = NA_QROWS * GRID_W
    bias = _na_bias_tables(rpb, rows)
    variant = lambda i, r: (jnp.where(r == 0, 0, jnp.where(r == nrb - 1, 2, 1)), 0, 0, 0)
    return pl.pallas_call(
        functools.partial(_na_body, rows=rows),
        grid=(b, nrb),
        in_specs=[pl.BlockSpec((1, tq, NA_WIDTH), lambda i, r: (i, r, 0)),
                  pl.BlockSpec((1, l, NA_WIDTH), lambda i, r: (i, 0, 0)),
                  pl.BlockSpec((1, l, NA_WIDTH), lambda i, r: (i, 0, 0)),
                  pl.BlockSpec((1, lc, NA_WIDTH), lambda i, r: (i, 0, 0)),
                  pl.BlockSpec((1, lc, NA_WIDTH), lambda i, r: (i, 0, 0)),
                  pl.BlockSpec((1, NA_HEADS, tq, NA_WIN * GRID_W), variant)],
        out_specs=pl.BlockSpec((1, tq, NA_WIDTH), lambda i, r: (i, r, 0)),
        out_shape=jax.ShapeDtypeStruct((b, l, NA_WIDTH), BF16),
        compiler_params=_cparams("arbitrary", "arbitrary"),
        name="latent_na",
    )(q, k, v, k_ctx, v_ctx, bias)


def _s5_prep_body(lre_ref, lim_ref, lst_ref, btre_ref, btim_ref, cre_ref, cim_ref,
                  t_ref, g_ref, e_ref, pw_ref, g_s, et_s, e0_s):
    q, ch, p = S5_Q, S5_GROUP_CH, S5_STATE
    lane = lax.broadcasted_iota(jnp.int32, (ch, S5_CW), 1)
    bt_re, bt_im = btre_ref[0], btim_ref[0]
    c_re, c_im = cre_ref[0], cim_ref[0]
    taps = []
    for d in range(2):
        lam_re, lam_im = lre_ref[d, 0], lim_ref[d, 0]
        step = jnp.exp(lst_ref[d, 0])
        mag = jnp.exp(lam_re * step)
        a_re = mag * jnp.cos(lam_im * step)
        a_im = mag * jnp.sin(lam_im * step)
        den = lam_re * lam_re + lam_im * lam_im
        k_re = ((a_re - 1.0) * lam_re + a_im * lam_im) / den
        k_im = (a_im * lam_re - (a_re - 1.0) * lam_im) / den
        bb_re = k_re * bt_re - k_im * bt_im
        bb_im = k_re * bt_im + k_im * bt_re
        p_re = jnp.ones_like(a_re)
        p_im = jnp.zeros_like(a_re)
        for t in range(q):
            tg = q - 1 - t if d == 0 else t
            te = t if d == 0 else q - 1 - t
            g_s[tg * ch:(tg + 1) * ch, 2 * p * d:2 * p * d + p] = p_re * bb_re - p_im * bb_im
            g_s[tg * ch:(tg + 1) * ch, 2 * p * d + p:2 * p * (d + 1)] = p_re * bb_im + p_im * bb_re
            e0_s[te * ch:(te + 1) * ch, 0:p] = c_re * p_re - c_im * p_im
            e0_s[te * ch:(te + 1) * ch, p:2 * p] = -(c_re * p_im + c_im * p_re)
            p_re, p_im = p_re * a_re - p_im * a_im, p_re * a_im + p_im * a_re
            et_s[d, te * ch:(te + 1) * ch, 0:p] = c_re * p_re - c_im * p_im
            et_s[d, te * ch:(te + 1) * ch, p:2 * p] = -(c_re * p_im + c_im * p_re)
        bb = jnp.concatenate([bb_re, bb_im], axis=1)
        taps.append(lax.dot_general(bb, e0_s[...], (((1,), (1,)), ((), ())), precision=HIGHEST,
                                    preferred_element_type=F32))
        pw_ref[d, 0] = jnp.zeros(pw_ref.shape[2:], F32)
        for k in range(S5_LOG_MAX):
            pw_ref[d, 0, 2 * k:2 * k + 1, :] = jnp.concatenate([p_re, p_re], axis=1)
            pw_ref[d, 0, 2 * k + 1:2 * k + 2, :] = jnp.concatenate([-p_im, p_im], axis=1)
            p_re, p_im = p_re * p_re - p_im * p_im, 2.0 * p_re * p_im
    for ti in range(q):
        fwd = taps[0] if ti == 0 else pltpu.roll(taps[0], ch * ti, axis=1)
        bwd = taps[1] if ti == q - 1 else pltpu.roll(taps[1], S5_CW - ch * (q - 1 - ti), axis=1)
        row = jnp.where(lane >= ch * ti, fwd, 0.0) + jnp.where(lane < ch * (ti + 1), bwd, 0.0)
        t_ref[0, ti * ch:(ti + 1) * ch, :] = row.astype(t_ref.dtype)
    g_ref[0] = g_s[...].astype(g_ref.dtype)
    e_ref[0, 0:2 * p, :] = et_s[0].T.astype(e_ref.dtype)
    e_ref[0, 2 * p:4 * p, :] = et_s[1].T.astype(e_ref.dtype)


def _s5_tables(lam_re, lam_im, log_step, b_re, b_im, c_re, c_im):
    g, p, ch = S5_GROUPS, S5_STATE, S5_GROUP_CH
    row = lambda a: a.astype(F32).reshape(2, g, 1, p)
    lst = jnp.broadcast_to(log_step.astype(F32)[:, :, None, None], (2, g, 1, p))
    bt_re = jnp.swapaxes(b_re.astype(F32), 1, 2)
    bt_im = jnp.swapaxes(b_im.astype(F32), 1, 2)
    dspec = pl.BlockSpec((2, 1, 1, p), lambda i: (0, i, 0, 0))
    gspec = pl.BlockSpec((1, ch, p), lambda i: (i, 0, 0))
    tspec = pl.BlockSpec((1, S5_CW, S5_CW), lambda i: (i, 0, 0))
    tshape = jax.ShapeDtypeStruct((g, S5_CW, S5_CW), BF16)
    return pl.pallas_call(
        _s5_prep_body,
        grid=(g,),
        in_specs=[dspec, dspec, dspec, gspec, gspec, gspec, gspec],
        out_specs=[tspec, tspec, tspec, pl.BlockSpec((2, 1, 16, 2 * p), lambda i: (0, i, 0, 0))],
        out_shape=[tshape, tshape, tshape, jax.ShapeDtypeStruct((2, g, 16, 2 * p), F32)],
        scratch_shapes=[pltpu.VMEM((S5_CW, 4 * p), F32), pltpu.VMEM((2, S5_CW, 2 * p), F32),
                        pltpu.VMEM((S5_CW, 2 * p), F32)],
        compiler_params=_cparams("arbitrary"),
        name="s5_prep",
    )(row(lam_re), row(lam_im), lst, bt_re, bt_im, c_re.astype(F32), c_im.astype(F32))


def _cmul(a_full, a_sgn, s):
    return a_full * s + a_sgn * pltpu.roll(s, S5_STATE, axis=1)


def _s5_body(u_ref, t_ref, g_ref, e_ref, pw_ref, dsk_ref, *rest, nb, nc, has_s0, want_final):
    rest = list(rest)
    s0_ref = rest.pop(0) if has_s0 else None
    o_ref = rest.pop(0)
    fin_ref = rest.pop(0) if want_final else None
    yt_s, xs_f, xs_b, tok_s = rest
    m = nb * nc
    w = 2 * S5_STATE
    ch = S5_GROUP_CH
    cidx = lax.broadcasted_iota(jnp.int32, (m, w), 0) & (nc - 1)
    first = cidx == 0
    last = cidx == nc - 1

    def tokens(t):
        return u_ref[:, pl.ds(t, nc, stride=S5_Q), :].reshape(m, LANES)

    for t in range(S5_Q):
        tok_s[t] = tokens(t).T

    for gg in range(S5_LG):
        ug_t = jnp.concatenate([tok_s[t, gg * ch:(gg + 1) * ch, :] for t in range(S5_Q)], axis=0)
        ub = ug_t.T.astype(BF16)
        z = jnp.dot(ub, g_ref[gg], preferred_element_type=F32)
        x_f, x_b = z[:, :w], z[:, w:]
        pw = lambda d, r: pw_ref[d, gg, r:r + 1, :]
        if has_s0:
            rep = lambda a: jnp.broadcast_to(a[:, None, :], (nb, nc, w)).reshape(m, w)
            s0_f, s0_b = rep(s0_ref[0, gg, :, :w]), rep(s0_ref[0, gg, :, w:])
            x_f = x_f + jnp.where(first, _cmul(pw(0, 0), pw(0, 1), s0_f), 0.0)
            x_b = x_b + jnp.where(last, _cmul(pw(1, 0), pw(1, 1), s0_b), 0.0)
        for k in range(nc.bit_length() - 1):
            sh = 1 << k
            prev = pltpu.roll(x_f, sh, axis=0)
            x_f = x_f + jnp.where(cidx >= sh, _cmul(pw(0, 2 * k), pw(0, 2 * k + 1), prev), 0.0)
            nxt = pltpu.roll(x_b, m - sh, axis=0)
            x_b = x_b + jnp.where(cidx < nc - sh, _cmul(pw(1, 2 * k), pw(1, 2 * k + 1), nxt), 0.0)
        in_f = pltpu.roll(x_f, 1, axis=0)
        in_b = pltpu.roll(x_b, m - 1, axis=0)
        if has_s0:
            in_f = jnp.where(first, s0_f, in_f)
            in_b = jnp.where(last, s0_b, in_b)
        else:
            in_f = jnp.where(first, 0.0, in_f)
            in_b = jnp.where(last, 0.0, in_b)
        s_in = jnp.concatenate([in_f, in_b], axis=1).astype(BF16)
        y = (jnp.dot(ub, t_ref[gg], preferred_element_type=F32)
             + jnp.dot(s_in, e_ref[gg], preferred_element_type=F32))
        yt_s[gg] = y.T
        if want_final:
            xs_f[...] = x_f
            xs_b[...] = x_b
            fin_ref[0, gg, :, :w] = xs_f[pl.ds(nc - 1, nb, stride=nc), :]
            fin_ref[0, gg, :, w:] = xs_b[pl.ds(0, nb, stride=nc), :]

    for t in range(S5_Q):
        y_t = jnp.concatenate([yt_s[k, t * ch:(t + 1) * ch, :] for k in range(S5_LG)], axis=0)
        y = y_t.T + tokens(t) * dsk_ref[0]
        o_ref[:, pl.ds(t, nc, stride=S5_Q), :] = _gelu_tanh(y).reshape(nb, nc, LANES)


def _s5(u, tables, d_skip, s0, want_final):
    t_all, g_fb, e_fb, pows = tables
    b, l, _ = u.shape
    nc = l // S5_Q
    bb = min(b, max(1, S5_ROWS // nc))
    nbb = b // bb
    m = bb * nc
    has_s0 = s0 is not None
    seq_spec = pl.BlockSpec((bb, l, LANES), lambda i, j: (j, 0, i))
    tab_spec = pl.BlockSpec((S5_LG, S5_CW, S5_CW), lambda i, j: (i, 0, 0),
                            pipeline_mode=pl.Buffered(1))
    st_spec = pl.BlockSpec((1, S5_LG, bb, S5_CW), lambda i, j: (j, i, 0, 0))
    st_blocks = lambda a: jnp.transpose(a.reshape(S5_GROUPS, nbb, bb, S5_CW), (1, 0, 2, 3))
    in_specs = [seq_spec, tab_spec, tab_spec, tab_spec,
                pl.BlockSpec((2, S5_LG, 16, 2 * S5_STATE), lambda i, j: (0, i, 0, 0)),
                pl.BlockSpec((1, 1, LANES), lambda i, j: (i, 0, 0))]
    args = [u, t_all, g_fb, e_fb, pows, d_skip.astype(F32).reshape(S5_WIDTH // LANES, 1, LANES)]
    if has_s0:
        in_specs.append(st_spec)
        args.append(st_blocks(s0))
    out_specs = [seq_spec]
    out_shape = [jax.ShapeDtypeStruct((b, l, S5_WIDTH), F32)]
    if want_final:
        out_specs.append(st_spec)
        out_shape.append(jax.ShapeDtypeStruct((nbb, S5_GROUPS, bb, S5_CW), F32))
    res = pl.pallas_call(
        functools.partial(_s5_body, nb=bb, nc=nc, has_s0=has_s0, want_final=want_final),
        grid=(S5_WIDTH // LANES, nbb),
        in_specs=in_specs,
        out_specs=out_specs,
        out_shape=out_shape,
        scratch_shapes=[pltpu.VMEM((S5_LG, S5_CW, m), F32),
                        pltpu.VMEM((m, 2 * S5_STATE), F32), pltpu.VMEM((m, 2 * S5_STATE), F32),
                        pltpu.VMEM((S5_Q, LANES, m), F32)],
        compiler_params=_cparams("arbitrary", "arbitrary"),
        name="s5",
    )(*args)
    fin = None
    if want_final:
        fin = jnp.transpose(res[1], (1, 0, 2, 3)).reshape(S5_GROUPS, b, S5_CW)
    return res[0], fin


def _dft_cos_sin(n, rows=None, row_step=1):
    rows = n if rows is None else rows
    j = jnp.arange(rows, dtype=jnp.int32)[:, None] * row_step
    jk = (j * jnp.arange(n, dtype=jnp.int32)[None, :]) % n
    ang = jk.astype(F32) * (2.0 * math.pi / n)
    return jnp.cos(ang), jnp.sin(ang)


def _dft_cos_sin_split(n, minor):
    c1, s1 = _dft_cos_sin(n, n // minor, minor)
    c2, s2 = _dft_cos_sin(n, minor)
    cos = c1[:, None, :] * c2[None, :, :] - s1[:, None, :] * s2[None, :, :]
    sin = s1[:, None, :] * c2[None, :, :] + c1[:, None, :] * s2[None, :, :]
    return cos.reshape(n, n), sin.reshape(n, n)


def _fnet_body(u_ref, cs_ref, dl_ref, o_ref, xs, *, seq):
    @pl.when(pl.program_id(1) == 0)
    def _():
        xcs = jnp.dot(u_ref[0].astype(BF16), cs_ref[...], preferred_element_type=F32)
        xs[0:seq, :] = xcs[:, :FNET_WIDTH].astype(BF16)
        xs[seq:2 * seq, :] = xcs[:, FNET_WIDTH:].astype(BF16)

    scale = 1.0 / math.sqrt(seq * FNET_GROUP_CH)
    o_ref[0] = (jnp.dot(dl_ref[...], xs[...], preferred_element_type=F32) * scale).astype(o_ref.dtype)


def _fnet(u):
    b, l, _ = u.shape
    tr = min(l, 512)
    cc, sc = _dft_cos_sin(FNET_GROUP_CH)
    eye = jnp.eye(FNET_GROUPS, dtype=F32)
    cs = jnp.concatenate([jnp.kron(eye, cc), jnp.kron(eye, sc)], axis=1).astype(BF16)
    cl, sl = _dft_cos_sin_split(l, DFT_MINOR) if l > 4 * DFT_MINOR else _dft_cos_sin(l)
    dl = jnp.concatenate([cl, -sl], axis=1).astype(BF16)
    return pl.pallas_call(
        functools.partial(_fnet_body, seq=l),
        grid=(b, l // tr),
        in_specs=[pl.BlockSpec((1, l, FNET_WIDTH), lambda i, t: (i, 0, 0)),
                  _const_spec((FNET_WIDTH, 2 * FNET_WIDTH)),
                  pl.BlockSpec((tr, 2 * l), lambda i, t: (t, 0))],
        out_specs=pl.BlockSpec((1, tr, FNET_WIDTH), lambda i, t: (i, t, 0)),
        out_shape=jax.ShapeDtypeStruct((b, l, FNET_WIDTH), BF16),
        scratch_shapes=[pltpu.VMEM((2 * l, FNET_WIDTH), BF16)],
        compiler_params=_cparams("arbitrary", "arbitrary"),
        name="fnet",
    )(u, cs, dl)


def _even_w_in(w):
    pad = jnp.zeros((D_MODEL, LANES - SSD_HEADS), w.dtype)
    return jnp.concatenate([w[:, :O_DT], w[:, O_Q:], w[:, O_DT:O_Q], pad], axis=1).astype(BF16)


EVEN_SPLITS = (SSD_WIDTH, SSD_XBC, NA_WIDTH, NA_WIDTH, NA_WIDTH, LANES)


def kernel(x_prompt, x_sample, cache_na_k, cache_na_v, state_ssd, state_s5, c, c_ctx, mod_w, mod_b, norm1_g, norm2_g, ffn_w1, ffn_w3, ffn_w2, final_g, ev_w_in, ev_conv_w, ev_conv_b, ev_a_log, ev_dt_bias, ev_d_skip, ev_ssd_norm_g, ev_rpb, ev_w_out, od_w_in, od_lam_re, od_lam_im, od_log_step, od_b_re, od_b_im, od_c_re, od_c_im, od_d_skip, od_glu_w, od_glu_b, od_w_out):
    bc, lc, _ = x_prompt.shape
    bl, ll, _ = x_sample.shape
    cond = jnp.zeros((MOD_ROWS, D_MODEL), F32).at[0].set(c_ctx).at[1:1 + bl].set(c)
    mods = _adaln(cond, mod_w, mod_b)
    xc = x_prompt.reshape(1, bc * lc, D_MODEL)
    xl = x_sample
    tm = 512
    new_k, new_v, new_ssd, new_s5 = [], [], [], []
    for i in range(DEPTH):
        j = i // 2
        mod = mods[i]
        last = i == DEPTH - 1
        if i % 2 == 0:
            w_in = _even_w_in(ev_w_in[j])
            w_out = ev_w_out[j].astype(BF16)
            ssd_args = (ev_conv_w[j], ev_conv_b[j], ev_a_log[j], ev_dt_bias[j], ev_d_skip[j],
                        ev_ssd_norm_g[j])
            z, xbc, q, k, v, dtp, k_hd, v_hd = _inproj(
                xc, mod, 0, norm1_g[i], w_in, EVEN_SPLITS, tm, (F32, F32, BF16, BF16, BF16, F32), (3, 4))
            seqs = lambda a: a.reshape(bc, lc, a.shape[-1])
            y_ssd, fin = _ssd(seqs(z), seqs(xbc), seqs(dtp), *ssd_args, None, True)
            o_na = _ctx_attn(seqs(q), seqs(k), seqs(v))
            flat = lambda a: a.reshape(1, bc * lc, a.shape[-1])
            mix_c = (flat(y_ssd), flat(o_na))
            new_k.append(k_hd.reshape(bc, lc, NA_HEADS, NA_HEAD_DIM))
            new_v.append(v_hd.reshape(bc, lc, NA_HEADS, NA_HEAD_DIM))
            new_ssd.append(_ssd_state_from_kernel(fin))

            z, xbc, q, k, v, dtp = _inproj(xl, mod, 1, norm1_g[i], w_in, EVEN_SPLITS, tm,
                                           (F32, F32, BF16, BF16, BF16, F32))
            s0t = _ssd_state_to_kernel(state_ssd[:, j].astype(F32))
            (y_ssd,) = _ssd(z, xbc, dtp, *ssd_args, s0t, False)
            pl_ = cache_na_k.shape[2]
            o_na = _latent_na(q, k, v, cache_na_k[:, j].reshape(bl, pl_, NA_WIDTH),
                              cache_na_v[:, j].reshape(bl, pl_, NA_WIDTH), ev_rpb[j])
            mix_l = (y_ssd, o_na)
            glu_w = glu_b = None
        else:
            w_in = od_w_in[j].astype(BF16)
            w_out = od_w_out[j].astype(BF16)
            glu_w, glu_b = od_glu_w[j].astype(BF16), od_glu_b[j]
            tables = _s5_tables(od_lam_re[j], od_lam_im[j], od_log_step[j], od_b_re[j], od_b_im[j],
                                od_c_re[j], od_c_im[j])
            splits = (S5_WIDTH, FNET_WIDTH)
            u_s, u_f = _inproj(xc, mod, 0, norm1_g[i], w_in, splits, tm, (F32, BF16))
            seqs = lambda a: a.reshape(bc, lc, a.shape[-1])
            g_s, fin = _s5(seqs(u_s), tables, od_d_skip[j], None, True)
            y_f = _fnet(seqs(u_f))
            flat = lambda a: a.reshape(1, bc * lc, a.shape[-1])
            mix_c = (flat(g_s), flat(y_f))
            fin = fin.reshape(S5_GROUPS, bc, 2, 2, S5_STATE)
            new_s5.append(jnp.transpose(fin, (1, 2, 0, 4, 3)))

            u_s, u_f = _inproj(xl, mod, 1, norm1_g[i], w_in, splits, tm, (F32, BF16))
            s0 = jnp.transpose(state_s5[:, j].astype(F32), (2, 0, 1, 4, 3)).reshape(
                S5_GROUPS, bl, 4 * S5_STATE)
            g_s, _ = _s5(u_s, tables, od_d_skip[j], s0, False)
            y_f = _fnet(u_f)
            mix_l = (g_s, y_f)
        w_o = (w_out[:SSD_WIDTH], w_out[SSD_WIDTH:])
        w1, w3, w2 = ffn_w1[i].astype(BF16), ffn_w3[i].astype(BF16), ffn_w2[i].astype(BF16)
        fg = final_g if last else None
        xc = _ffn(xc, mod, 0, *mix_c, *w_o, glu_w, glu_b, norm2_g[i], w1, w3, w2, fg, tm)
        xl = _ffn(xl, mod, 1, *mix_l, *w_o, glu_w, glu_b, norm2_g[i], w1, w3, w2, fg, tm)
    return (xc.reshape(bc, lc, D_MODEL), xl,
            jnp.stack(new_k, axis=1), jnp.stack(new_v, axis=1),
            jnp.stack(new_ssd, axis=1), jnp.stack(new_s5, axis=1))
```

```python
import functools
import math

import numpy as np
import jax
import jax.numpy as jnp
from jax import lax
from jax.experimental import pallas as pl
from jax.experimental.pallas import tpu as pltpu

F32 = jnp.float32
BF16 = jnp.bfloat16
HIGHEST = lax.Precision.HIGHEST

D_MODEL = 1024
DEPTH = 2
EPS = 1e-6
GRID_W = 64
FFN_HIDDEN = 2816
MXU_COLS = 256
FFN_CHUNKS = (6 * MXU_COLS, 5 * MXU_COLS)
assert sum(FFN_CHUNKS) == FFN_HIDDEN

SSD_HEADS = 8
SSD_HEAD_DIM = 64
SSD_WIDTH = 512
SSD_GROUPS = 2
SSD_HPG = SSD_HEADS // SSD_GROUPS
SSD_STATE = 64
SSD_GN = SSD_GROUPS * SSD_STATE
SSD_XBC = SSD_WIDTH + 2 * SSD_GN
SSD_CHUNK = 128
CONV_W = 5
CONV_HALO = 8

NA_HEADS = 8
NA_HEAD_DIM = 64
NA_WIDTH = 512
NA_ROWS = 8
NA_COLS = 16
NA_SCALE = NA_HEAD_DIM ** -0.5
NA_QROWS = 4
NA_WIN = NA_ROWS + NA_QROWS
NEG_BIG = -1e30

O_XBC = SSD_WIDTH
O_DT = O_XBC + SSD_XBC
O_Q = O_DT + SSD_HEADS
O_K = O_Q + NA_WIDTH
O_V = O_K + NA_WIDTH
LANES = 128

S5_GROUP_CH = 16
S5_GROUPS = 32
S5_WIDTH = 512
S5_STATE = 64
S5_Q = 16
S5_CW = S5_Q * S5_GROUP_CH
S5_LOG_MAX = 7
S5_LG = LANES // S5_GROUP_CH
S5_ROWS = 512

FNET_GROUP_CH = 64
FNET_GROUPS = 8
FNET_WIDTH = 512
DFT_MINOR = 64

MOD_ROWS = 16
VMEM_LIMIT = 56 * 1024 * 1024


def _cparams(*sem):
    return pltpu.CompilerParams(dimension_semantics=sem, vmem_limit_bytes=VMEM_LIMIT)


def _const_spec(shape):
    nd = len(shape)
    return pl.BlockSpec(shape, lambda *_: (0,) * nd, pipeline_mode=pl.Buffered(1))


def _sigmoid(x):
    return 1.0 / (1.0 + jnp.exp(-x))


def _silu(x):
    return x * _sigmoid(x)


def _softplus(x):
    return jnp.maximum(x, 0.0) + jnp.log(1.0 + jnp.exp(-jnp.abs(x)))


def _gelu_tanh(x):
    return 0.5 * x * (1.0 + jnp.tanh(math.sqrt(2.0 / math.pi) * (x + 0.044715 * (x * x * x))))


def _rmsnorm(x, g):
    return x * lax.rsqrt(jnp.mean(x * x, axis=-1, keepdims=True) + EPS) * g


def _bdot(a, b):
    return jnp.dot(a.astype(BF16), b.astype(BF16), preferred_element_type=F32)


def _bdot_nt(a, b):
    return lax.dot_general(a.astype(BF16), b.astype(BF16), (((1,), (1,)), ((), ())),
                           preferred_element_type=F32)


def _bdot_tn(a, b):
    return lax.dot_general(a.astype(BF16), b.astype(BF16), (((0,), (0,)), ((), ())),
                           preferred_element_type=F32)


def _adaln_body(c_ref, w_ref, b_ref, o_ref):
    s = _silu(c_ref[...])
    o_ref[0] = jnp.dot(s, w_ref[0], precision=HIGHEST, preferred_element_type=F32) + b_ref[0]


def _adaln(cond, mod_w, mod_b):
    n = 6 * D_MODEL
    tn = 1536
    out = pl.pallas_call(
        _adaln_body,
        grid=(DEPTH, n // tn),
        in_specs=[pl.BlockSpec((MOD_ROWS, D_MODEL), lambda i, j: (0, 0)),
                  pl.BlockSpec((1, D_MODEL, tn), lambda i, j: (i, 0, j)),
                  pl.BlockSpec((1, 1, tn), lambda i, j: (i, 0, j))],
        out_specs=pl.BlockSpec((1, MOD_ROWS, tn), lambda i, j: (i, 0, j)),
        out_shape=jax.ShapeDtypeStruct((DEPTH, MOD_ROWS, n), F32),
        compiler_params=_cparams("arbitrary", "arbitrary"),
        name="adaln",
    )(cond, mod_w, mod_b.reshape(DEPTH, 1, n))
    return out.reshape(DEPTH, MOD_ROWS, 6, D_MODEL)


def _mod_spec(mrow0):
    return pl.BlockSpec((1, 6, D_MODEL), lambda b, t: (b + mrow0, 0, 0))


def _inproj_body(x_ref, mod_ref, g_ref, w_ref, *o_refs, splits, heads_of):
    h = _rmsnorm(x_ref[0], g_ref[...]) * (1.0 + mod_ref[0, 1:2, :]) + mod_ref[0, 0:1, :]
    acc = jnp.dot(h.astype(BF16), w_ref[...], preferred_element_type=F32)
    off = 0
    offs = []
    for o_ref, n in zip(o_refs, splits):
        o_ref[0] = acc[:, off:off + n].astype(o_ref.dtype)
        offs.append(off)
        off += n
    for o_ref, i in zip(o_refs[len(splits):], heads_of):
        nh, hd = o_ref.shape[2], o_ref.shape[3]
        for hh in range(nh):
            o_ref[0, :, hh, :] = acc[:, offs[i] + hh * hd:offs[i] + (hh + 1) * hd]


def _inproj(x, mod, mrow0, g, w, splits, tm, dtypes=None, heads_of=()):
    b, l, _ = x.shape
    n = w.shape[1]
    dtypes = (F32,) * len(splits) if dtypes is None else dtypes
    head_shape = lambda i: (splits[i] // NA_HEAD_DIM, NA_HEAD_DIM)
    return pl.pallas_call(
        functools.partial(_inproj_body, splits=splits, heads_of=heads_of),
        grid=(b, l // tm),
        in_specs=[pl.BlockSpec((1, tm, D_MODEL), lambda i, t: (i, t, 0)),
                  _mod_spec(mrow0),
                  _const_spec((1, D_MODEL)),
                  _const_spec((D_MODEL, n))],
        out_specs=([pl.BlockSpec((1, tm, s), lambda i, t: (i, t, 0)) for s in splits]
                   + [pl.BlockSpec((1, tm) + head_shape(i), lambda i_, t: (i_, t, 0, 0))
                      for i in heads_of]),
        out_shape=([jax.ShapeDtypeStruct((b, l, s), dt) for s, dt in zip(splits, dtypes)]
                   + [jax.ShapeDtypeStruct((b, l) + head_shape(i), F32) for i in heads_of]),
        compiler_params=_cparams("arbitrary", "arbitrary"),
        name="inproj",
    )(x, mod, g.reshape(1, D_MODEL), w)


def _ffn_body(x_ref, mod_ref, ya_ref, yb_ref, wa_ref, wb_ref, g_ref, w1_ref, w3_ref, w2_ref, *rest,
              glu, final):
    rest = list(rest)
    ya = ya_ref[0]
    if glu:
        gw_ref, gb_ref = rest.pop(0), rest.pop(0)
        ya = ya * _sigmoid(_bdot(ya, gw_ref[...]) + gb_ref[...])
    mix = _bdot(ya, wa_ref[...]) + _bdot(yb_ref[0], wb_ref[...])
    x = x_ref[0] + mod_ref[0, 2:3, :] * mix
    h = (_rmsnorm(x, g_ref[...]) * (1.0 + mod_ref[0, 4:5, :]) + mod_ref[0, 3:4, :]).astype(BF16)
    acc = jnp.zeros(x.shape, F32)
    for c in range(len(FFN_CHUNKS)):
        cols = slice(sum(FFN_CHUNKS[:c]), sum(FFN_CHUNKS[:c + 1]))
        a = jnp.dot(h, w1_ref[:, cols], preferred_element_type=F32)
        u = _silu(a) * jnp.dot(h, w3_ref[:, cols], preferred_element_type=F32)
        acc = acc + jnp.dot(u.astype(BF16), w2_ref[cols, :], preferred_element_type=F32)
    y = x + mod_ref[0, 5:6, :] * acc
    if final:
        fg_ref, o_ref = rest
        y = _rmsnorm(y, fg_ref[...])
    else:
        (o_ref,) = rest
    o_ref[0] = y


def _ffn(x, mod, mrow0, ya, yb, wa, wb, glu_w, glu_b, g, w1, w3, w2, final_g, tm):
    b, l, _ = x.shape
    wa_n, wb_n = ya.shape[-1], yb.shape[-1]
    glu = glu_w is not None
    final = final_g is not None
    row_spec = lambda n: pl.BlockSpec((1, tm, n), lambda i, t: (i, t, 0))
    in_specs = [row_spec(D_MODEL), _mod_spec(mrow0), row_spec(wa_n), row_spec(wb_n),
                _const_spec((wa_n, D_MODEL)), _const_spec((wb_n, D_MODEL)),
                _const_spec((1, D_MODEL)),
                _const_spec((D_MODEL, FFN_HIDDEN)),
                _const_spec((D_MODEL, FFN_HIDDEN)),
                _const_spec((FFN_HIDDEN, D_MODEL))]
    args = [x, mod, ya, yb, wa, wb, g.reshape(1, D_MODEL), w1, w3, w2]
    if glu:
        in_specs += [_const_spec((wa_n, wa_n)), _const_spec((1, wa_n))]
        args += [glu_w, glu_b.reshape(1, wa_n)]
    if final:
        in_specs.append(_const_spec((1, D_MODEL)))
        args.append(final_g.reshape(1, D_MODEL))
    return pl.pallas_call(
        functools.partial(_ffn_body, glu=glu, final=final),
        grid=(b, l // tm),
        in_specs=in_specs,
        out_specs=pl.BlockSpec((1, tm, D_MODEL), lambda i, t: (i, t, 0)),
        out_shape=jax.ShapeDtypeStruct((b, l, D_MODEL), F32),
        compiler_params=_cparams("arbitrary", "arbitrary"),
        name="ffn",
    )(*args)


def _ssd_body(z_ref, xbc_ref, dt_ref, cw_ref, cb_ref, hp_ref, dsk_ref, ng_ref, *rest,
              seq, has_s0, want_final):
    rest = list(rest)
    s0_ref = rest.pop(0) if has_s0 else None
    y_ref = rest.pop(0)
    fin_ref = rest.pop(0) if want_final else None
    st_s, cd_s, scur, ych, xc_s, cum_s, tr_s = rest
    q = SSD_CHUNK
    nc = seq // q
    gw = SSD_HPG * SSD_HEAD_DIM
    expand = (lax.broadcasted_iota(jnp.int32, (LANES, SSD_WIDTH), 1) // SSD_HEAD_DIM
              == lax.broadcasted_iota(jnp.int32, (LANES, SSD_WIDTH), 0)).astype(BF16)

    def per_head(v):
        hi = v.astype(BF16)
        lo = (v - hi.astype(F32)).astype(BF16)
        return (jnp.dot(hi, expand, preferred_element_type=F32)
                + jnp.dot(lo, expand, preferred_element_type=F32))
    row = lax.broadcasted_iota(jnp.int32, (q, q), 0)
    col = lax.broadcasted_iota(jnp.int32, (q, q), 1)
    lower = row >= col
    upper = col >= row
    tri_l = lower.astype(F32)
    tri_u = upper.astype(F32)
    a_f = -jnp.exp(hp_ref[0:1, :])
    a_b = -jnp.exp(hp_ref[1:2, :])
    lane = lax.broadcasted_iota(jnp.int32, (1, LANES), 1)
    a_f = jnp.where(lane < SSD_HEADS, a_f, 0.0)
    a_b = jnp.where(lane < SSD_HEADS, a_b, 0.0)
    bias_f = hp_ref[2:3, :]
    bias_b = hp_ref[3:4, :]

    def chunk_pre(c):
        r0 = pl.multiple_of(c * q, q)
        lo = pl.multiple_of(jnp.maximum(r0 - CONV_HALO, 0), CONV_HALO)
        hi = pl.multiple_of(jnp.minimum(r0 + q, seq - CONV_HALO), CONV_HALO)
        prev = jnp.where(c > 0, xbc_ref[0, pl.ds(lo, CONV_HALO), :], 0.0)
        nxt = jnp.where(c < nc - 1, xbc_ref[0, pl.ds(hi, CONV_HALO), :], 0.0)
        win = jnp.concatenate([prev, xbc_ref[0, pl.ds(r0, q), :], nxt], axis=0)
        acc = cb_ref[...] + cw_ref[0:1, :] * win[CONV_HALO - 2:CONV_HALO - 2 + q]
        for k in range(1, CONV_W):
            o = CONV_HALO - CONV_W // 2 + k
            acc = acc + cw_ref[k:k + 1, :] * win[o:o + q]
        xc = _silu(acc)
        dtr = dt_ref[0, pl.ds(r0, q), :]
        dt_f = _softplus(dtr + bias_f)
        dt_b = _softplus(dtr + bias_b)
        cum_f = jnp.dot(tri_l, dt_f * a_f, precision=HIGHEST, preferred_element_type=F32)
        cum_b = jnp.dot(tri_u, dt_b * a_b, precision=HIGHEST, preferred_element_type=F32)
        return r0, xc, dt_f, dt_b, cum_f, cum_b

    def pass_a(c, carry):
        r0, xc, dt_f, dt_b, cum_f, cum_b = chunk_pre(c)
        xc_s[pl.ds(r0, q), :] = xc
        cum_s[0, pl.ds(r0, q), :] = cum_f
        cum_s[1, pl.ds(r0, q), :] = cum_b
        for i, v in enumerate((cum_f, cum_b, dt_f, dt_b)):
            tr_s[c, i] = v.T[0:SSD_HEADS, :]
        end_f = cum_f[q - 1:q, :]
        end_b = cum_b[0:1, :]
        xw_f = xc[:, :SSD_WIDTH] * per_head(jnp.exp(end_f - cum_f) * dt_f)
        xw_b = xc[:, :SSD_WIDTH] * per_head(jnp.exp(end_b - cum_b) * dt_b)
        cd_s[0, c] = per_head(jnp.broadcast_to(jnp.exp(end_f), (8, LANES)))
        cd_s[1, c] = per_head(jnp.broadcast_to(jnp.exp(end_b), (8, LANES)))
        for g in range(SSD_GROUPS):
            bg = xc[:, SSD_WIDTH + g * SSD_STATE:SSD_WIDTH + (g + 1) * SSD_STATE]
            st_s[0, c, g] = _bdot_tn(bg, xw_f[:, g * gw:(g + 1) * gw])
            st_s[1, c, g] = _bdot_tn(bg, xw_b[:, g * gw:(g + 1) * gw])
        return carry

    lax.fori_loop(0, nc, pass_a, 0)

    if has_s0:
        scur[...] = s0_ref[0]
    else:
        scur[...] = jnp.zeros(scur.shape, F32)

    def pass_b(c, carry):
        cr = nc - 1 - c
        for g in range(SSD_GROUPS):
            s_in = scur[0, g]
            scur[0, g] = s_in * cd_s[0, c, 0:1, g * gw:(g + 1) * gw] + st_s[0, c, g]
            st_s[0, c, g] = s_in
            s_in = scur[1, g]
            scur[1, g] = s_in * cd_s[1, cr, 0:1, g * gw:(g + 1) * gw] + st_s[1, cr, g]
            st_s[1, cr, g] = s_in
        return carry

    lax.fori_loop(0, nc, pass_b, 0)
    if want_final:
        fin_ref[0] = scur[...]

    def pass_c(c, carry):
        r0 = pl.multiple_of(c * q, q)
        xc = xc_s[pl.ds(r0, q), :]
        cum_f = cum_s[0, pl.ds(r0, q), :]
        cum_b = cum_s[1, pl.ds(r0, q), :]
        cum_ft, cum_bt, dt_ft, dt_bt = tr_s[c, 0], tr_s[c, 1], tr_s[c, 2], tr_s[c, 3]
        ex_f = per_head(jnp.exp(cum_f))
        ex_b = per_head(jnp.exp(cum_b))
        y_off = []
        cbs = []
        for g in range(SSD_GROUPS):
            gs = slice(g * gw, (g + 1) * gw)
            bg = xc[:, SSD_WIDTH + g * SSD_STATE:SSD_WIDTH + (g + 1) * SSD_STATE]
            cg = xc[:, SSD_WIDTH + SSD_GN + g * SSD_STATE:SSD_WIDTH + SSD_GN + (g + 1) * SSD_STATE]
            cbs.append(_bdot_nt(cg, bg))
            y_off.append(ex_f[:, gs] * _bdot(cg, st_s[0, c, g]) + ex_b[:, gs] * _bdot(cg, st_s[1, c, g]))
        for h in range(SSD_HEADS):
            xh = xc[:, h * SSD_HEAD_DIM:(h + 1) * SSD_HEAD_DIM]
            seg_f = cum_f[:, h:h + 1] - cum_ft[h:h + 1, :]
            seg_b = cum_b[:, h:h + 1] - cum_bt[h:h + 1, :]
            m_f = jnp.exp(jnp.where(lower, seg_f, NEG_BIG)) * dt_ft[h:h + 1, :]
            m_b = jnp.exp(jnp.where(upper, seg_b, NEG_BIG)) * dt_bt[h:h + 1, :]
            ych[:, h * SSD_HEAD_DIM:(h + 1) * SSD_HEAD_DIM] = _bdot(cbs[h // SSD_HPG] * (m_f + m_b), xh)
        yf = ych[...] + jnp.concatenate(y_off, axis=1) + dsk_ref[...] * xc[:, :SSD_WIDTH]
        yf = yf * _silu(z_ref[0, pl.ds(r0, q), :])
        y_ref[0, pl.ds(r0, q), :] = _rmsnorm(yf, ng_ref[...]).astype(y_ref.dtype)
        return carry

    lax.fori_loop(0, nc, pass_c, 0)


def _ssd_state_to_kernel(s):
    b = s.shape[0]
    s = s.reshape(b, 2, SSD_GROUPS, SSD_HPG, SSD_HEAD_DIM, SSD_STATE)
    return jnp.transpose(s, (0, 1, 2, 5, 3, 4)).reshape(b, 2, SSD_GROUPS, SSD_STATE, SSD_HPG * SSD_HEAD_DIM)


def _ssd_state_from_kernel(s):
    b = s.shape[0]
    s = s.reshape(b, 2, SSD_GROUPS, SSD_STATE, SSD_HPG, SSD_HEAD_DIM)
    return jnp.transpose(s, (0, 1, 2, 4, 5, 3)).reshape(b, 2, SSD_HEADS, SSD_HEAD_DIM, SSD_STATE)


def _ssd(z, xbc, dtp, conv_w, conv_b, a_log, dt_bias, d_skip, norm_g, s0t, want_final):
    b, l, _ = z.shape
    nc = l // SSD_CHUNK
    gw = SSD_HPG * SSD_HEAD_DIM
    has_s0 = s0t is not None
    cw = jnp.zeros((8, SSD_XBC), F32).at[:CONV_W].set(conv_w)
    hp = jnp.zeros((8, LANES), F32)
    hp = hp.at[0:2, :SSD_HEADS].set(a_log).at[2:4, :SSD_HEADS].set(dt_bias)
    dsk = jnp.repeat(d_skip, SSD_HEAD_DIM).reshape(1, SSD_WIDTH)
    seq_spec = lambda n: pl.BlockSpec((1, l, n), lambda i: (i, 0, 0))
    st_spec = pl.BlockSpec((1, 2, SSD_GROUPS, SSD_STATE, gw), lambda i: (i, 0, 0, 0, 0))
    in_specs = [seq_spec(SSD_WIDTH), seq_spec(SSD_XBC), seq_spec(LANES),
                _const_spec((8, SSD_XBC)), _const_spec((1, SSD_XBC)), _const_spec((8, LANES)),
                _const_spec((1, SSD_WIDTH)), _const_spec((1, SSD_WIDTH))]
    args = [z, xbc, dtp, cw, conv_b.reshape(1, SSD_XBC), hp, dsk, norm_g.reshape(1, SSD_WIDTH)]
    if has_s0:
        in_specs.append(st_spec)
        args.append(s0t)
    out_specs = [seq_spec(SSD_WIDTH)]
    out_shape = [jax.ShapeDtypeStruct((b, l, SSD_WIDTH), BF16)]
    if want_final:
        out_specs.append(st_spec)
        out_shape.append(jax.ShapeDtypeStruct((b, 2, SSD_GROUPS, SSD_STATE, gw), F32))
    return pl.pallas_call(
        functools.partial(_ssd_body, seq=l, has_s0=has_s0, want_final=want_final),
        grid=(b,),
        in_specs=in_specs,
        out_specs=out_specs,
        out_shape=out_shape,
        scratch_shapes=[pltpu.VMEM((2, nc, SSD_GROUPS, SSD_STATE, gw), F32),
                        pltpu.VMEM((2, nc, 8, SSD_WIDTH), F32),
                        pltpu.VMEM((2, SSD_GROUPS, SSD_STATE, gw), F32),
                        pltpu.VMEM((SSD_CHUNK, SSD_WIDTH), F32),
                        pltpu.VMEM((l, SSD_XBC), F32),
                        pltpu.VMEM((2, l, LANES), F32),
                        pltpu.VMEM((nc, 4, SSD_HEADS, LANES), F32)],
        compiler_params=_cparams("arbitrary"),
        name="ssd",
    )(*args)


def _softmax_pv(parts):
    m = parts[0][0].max(axis=-1, keepdims=True)
    for s, _ in parts[1:]:
        m = jnp.maximum(m, s.max(axis=-1, keepdims=True))
    den = 0.0
    out = 0.0
    for s, v in parts:
        p = jnp.exp(s - m)
        den = den + p.sum(axis=-1, keepdims=True)
        out = out + _bdot(p, v)
    return out / den


def _ctx_attn_body(q_ref, k_ref, v_ref, o_ref):
    for h in range(NA_HEADS):
        hs = slice(h * NA_HEAD_DIM, (h + 1) * NA_HEAD_DIM)
        s = _bdot_nt(q_ref[0, :, hs], k_ref[0, :, hs]) * NA_SCALE
        o_ref[0, :, hs] = _softmax_pv([(s, v_ref[0, :, hs])]).astype(o_ref.dtype)


def _ctx_attn(q, k, v):
    b, l, _ = q.shape
    spec = pl.BlockSpec((1, l, NA_WIDTH), lambda i: (i, 0, 0))
    return pl.pallas_call(
        _ctx_attn_body,
        grid=(b,),
        in_specs=[spec, spec, spec],
        out_specs=spec,
        out_shape=jax.ShapeDtypeStruct((b, l, NA_WIDTH), BF16),
        compiler_params=_cparams("arbitrary"),
        name="ctx_attn",
    )(q, k, v)


def _na_bias_tables(rpb, rows):
    qc = np.arange(GRID_W)[:, None]
    kc = np.arange(GRID_W)[None, :]
    cs = np.clip(qc - NA_COLS // 2, 0, GRID_W - NA_COLS)
    col_ok = (kc >= cs) & (kc < cs + NA_COLS)
    dc = np.clip(kc - qc + NA_COLS - 1, 0, 2 * NA_COLS - 2)
    col_sel = (dc[None] == np.arange(2 * NA_COLS - 1)[:, None, None]).astype(np.float32)
    by_col = jnp.einsum('hab,bqk->haqk', rpb.astype(F32), jnp.asarray(col_sel), precision=HIGHEST)
    by_col = jnp.where(jnp.asarray(col_ok), by_col, NEG_BIG)
    n_dr = 2 * NA_ROWS - 1
    return pl.pallas_call(
        functools.partial(_na_bias_body, rows=rows),
        grid=(3, NA_HEADS),
        in_specs=[pl.BlockSpec((1, n_dr, GRID_W, GRID_W), lambda v, h: (h, 0, 0, 0))],
        out_specs=pl.BlockSpec((1, 1, NA_QROWS * GRID_W, NA_WIN * GRID_W), lambda v, h: (v, h, 0, 0)),
        out_shape=jax.ShapeDtypeStruct((3, NA_HEADS, NA_QROWS * GRID_W, NA_WIN * GRID_W), F32),
        compiler_params=_cparams("arbitrary", "arbitrary"),
        name="na_bias",
    )(by_col)


def _na_bias_body(u_ref, o_ref, *, rows):
    v = pl.program_id(0)
    kr = min(NA_ROWS, rows)
    r0 = jnp.where(v == 0, 0, jnp.where(v == 1, NA_QROWS, rows - NA_QROWS))
    ws = jnp.clip(r0 - kr // 2, 0, rows - NA_WIN)
    for qi in range(NA_QROWS):
        r = r0 + qi
        rs = jnp.clip(r - kr // 2, 0, rows - kr)
        for w in range(NA_WIN):
            krow = ws + w
            ok = (krow >= rs) & (krow < rs + kr)
            dr = jnp.clip(krow - r + NA_ROWS - 1, 0, 2 * NA_ROWS - 2)
            o_ref[0, 0, qi * GRID_W:(qi + 1) * GRID_W, w * GRID_W:(w + 1) * GRID_W] = jnp.where(
                ok, u_ref[0, dr], NEG_BIG)


def _na_body(q_ref, k_ref, v_ref, kc_ref, vc_ref, bias_ref, o_ref, *, rows):
    rb = pl.program_id(1)
    ws = jnp.clip(rb * NA_QROWS - NA_ROWS // 2, 0, rows - NA_WIN)
    k0 = pl.multiple_of(ws * GRID_W, NA_QROWS * GRID_W)
    nk = NA_WIN * GRID_W
    for h in range(NA_HEADS):
        hs = slice(h * NA_HEAD_DIM, (h + 1) * NA_HEAD_DIM)
        qh = q_ref[0, :, hs]
        s_loc = _bdot_nt(qh, k_ref[0, pl.ds(k0, nk), hs]) * NA_SCALE + bias_ref[0, h]
        s_ctx = _bdot_nt(qh, kc_ref[0, :, hs]) * NA_SCALE
        o_ref[0, :, hs] = _softmax_pv([(s_loc, v_ref[0, pl.ds(k0, nk), hs]),
                                       (s_ctx, vc_ref[0, :, hs])]).astype(o_ref.dtype)


def _latent_na(q, k, v, k_ctx, v_ctx, rpb):
    b, l, _ = q.shape
    lc = k_ctx.shape[1]
    rows = l // GRID_W
    assert rows >= NA_WIN and rows % NA_QROWS == 0 and NA_QROWS * 2 <= NA_ROWS
    nrb = rows // NA_QROWS
    tq = NA_QROWS * GRID_W
    bias = _na_bias_tables(rpb, rows)
    variant = lambda i, r: (jnp.where(r == 0, 0, jnp.where(r == nrb - 1, 2, 1)), 0, 0, 0)
    return pl.pallas_call(
        functools.partial(_na_body, rows=rows),
        grid=(b, nrb),
        in_specs=[pl.BlockSpec((1, tq, NA_WIDTH), lambda i, r: (i, r, 0)),
                  pl.BlockSpec((1, l, NA_WIDTH), lambda i, r: (i, 0, 0)),
                  pl.BlockSpec((1, l, NA_WIDTH), lambda i, r: (i, 0, 0)),
                  pl.BlockSpec((1, lc, NA_WIDTH), lambda i, r: (i, 0, 0)),
                  pl.BlockSpec((1, lc, NA_WIDTH), lambda i, r: (i, 0, 0)),
                  pl.BlockSpec((1, NA_HEADS, tq, NA_WIN * GRID_W), variant)],
        out_specs=pl.BlockSpec((1, tq, NA_WIDTH), lambda i, r: (i, r, 0)),
        out_shape=jax.ShapeDtypeStruct((b, l, NA_WIDTH), BF16),
        compiler_params=_cparams("arbitrary", "arbitrary"),
        name="latent_na",
    )(q, k, v, k_ctx, v_ctx, bias)


def _s5_prep_body(lre_ref, lim_ref, lst_ref, btre_ref, btim_ref, cre_ref, cim_ref,
                  t_ref, g_ref, e_ref, pw_ref, g_s, et_s, e0_s):
    q, ch, p = S5_Q, S5_GROUP_CH, S5_STATE
    lane = lax.broadcasted_iota(jnp.int32, (ch, S5_CW), 1)
    bt_re, bt_im = btre_ref[0], btim_ref[0]
    c_re, c_im = cre_ref[0], cim_ref[0]
    taps = []
    for d in range(2):
        lam_re, lam_im = lre_ref[d, 0], lim_ref[d, 0]
        step = jnp.exp(lst_ref[d, 0])
        mag = jnp.exp(lam_re * step)
        a_re = mag * jnp.cos(lam_im * step)
        a_im = mag * jnp.sin(lam_im * step)
        den = lam_re * lam_re + lam_im * lam_im
        k_re = ((a_re - 1.0) * lam_re + a_im * lam_im) / den
        k_im = (a_im * lam_re - (a_re - 1.0) * lam_im) / den
        bb_re = k_re * bt_re - k_im * bt_im
        bb_im = k_re * bt_im + k_im * bt_re
        p_re = jnp.ones_like(a_re)
        p_im = jnp.zeros_like(a_re)
        for t in range(q):
            tg = q - 1 - t if d == 0 else t
            te = t if d == 0 else q - 1 - t
            g_s[tg * ch:(tg + 1) * ch, 2 * p * d:2 * p * d + p] = p_re * bb_re - p_im * bb_im
            g_s[tg * ch:(tg + 1) * ch, 2 * p * d + p:2 * p * (d + 1)] = p_re * bb_im + p_im * bb_re
            e0_s[te * ch:(te + 1) * ch, 0:p] = c_re * p_re - c_im * p_im
            e0_s[te * ch:(te + 1) * ch, p:2 * p] = -(c_re * p_im + c_im * p_re)
            p_re, p_im = p_re * a_re - p_im * a_im, p_re * a_im + p_im * a_re
            et_s[d, te * ch:(te + 1) * ch, 0:p] = c_re * p_re - c_im * p_im
            et_s[d, te * ch:(te + 1) * ch, p:2 * p] = -(c_re * p_im + c_im * p_re)
        bb = jnp.concatenate([bb_re, bb_im], axis=1)
        taps.append(lax.dot_general(bb, e0_s[...], (((1,), (1,)), ((), ())), precision=HIGHEST,
                                    preferred_element_type=F32))
        pw_ref[d, 0] = jnp.zeros(pw_ref.shape[2:], F32)
        for k in range(S5_LOG_MAX):
            pw_ref[d, 0, 2 * k:2 * k + 1, :] = jnp.concatenate([p_re, p_re], axis=1)
            pw_ref[d, 0, 2 * k + 1:2 * k + 2, :] = jnp.concatenate([-p_im, p_im], axis=1)
            p_re, p_im = p_re * p_re - p_im * p_im, 2.0 * p_re * p_im
    for ti in range(q):
        fwd = taps[0] if ti == 0 else pltpu.roll(taps[0], ch * ti, axis=1)
        bwd = taps[1] if ti == q - 1 else pltpu.roll(taps[1], S5_CW - ch * (q - 1 - ti), axis=1)
        row = jnp.where(lane >= ch * ti, fwd, 0.0) + jnp.where(lane < ch * (ti + 1), bwd, 0.0)
        t_ref[0, ti * ch:(ti + 1) * ch, :] = row.astype(t_ref.dtype)
    g_ref[0] = g_s[...].astype(g_ref.dtype)
    e_ref[0, 0:2 * p, :] = et_s[0].T.astype(e_ref.dtype)
    e_ref[0, 2 * p:4 * p, :] = et_s[1].T.astype(e_ref.dtype)


def _s5_tables(lam_re, lam_im, log_step, b_re, b_im, c_re, c_im):
    g, p, ch = S5_GROUPS, S5_STATE, S5_GROUP_CH
    row = lambda a: a.astype(F32).reshape(2, g, 1, p)
    lst = jnp.broadcast_to(log_step.astype(F32)[:, :, None, None], (2, g, 1, p))
    bt_re = jnp.swapaxes(b_re.astype(F32), 1, 2)
    bt_im = jnp.swapaxes(b_im.astype(F32), 1, 2)
    dspec = pl.BlockSpec((2, 1, 1, p), lambda i: (0, i, 0, 0))
    gspec = pl.BlockSpec((1, ch, p), lambda i: (i, 0, 0))
    tspec = pl.BlockSpec((1, S5_CW, S5_CW), lambda i: (i, 0, 0))
    tshape = jax.ShapeDtypeStruct((g, S5_CW, S5_CW), BF16)
    return pl.pallas_call(
        _s5_prep_body,
        grid=(g,),
        in_specs=[dspec, dspec, dspec, gspec, gspec, gspec, gspec],
        out_specs=[tspec, tspec, tspec, pl.BlockSpec((2, 1, 16, 2 * p), lambda i: (0, i, 0, 0))],
        out_shape=[tshape, tshape, tshape, jax.ShapeDtypeStruct((2, g, 16, 2 * p), F32)],
        scratch_shapes=[pltpu.VMEM((S5_CW, 4 * p), F32), pltpu.VMEM((2, S5_CW, 2 * p), F32),
                        pltpu.VMEM((S5_CW, 2 * p), F32)],
        compiler_params=_cparams("arbitrary"),
        name="s5_prep",
    )(row(lam_re), row(lam_im), lst, bt_re, bt_im, c_re.astype(F32), c_im.astype(F32))


def _cmul(a_full, a_sgn, s):
    return a_full * s + a_sgn * pltpu.roll(s, S5_STATE, axis=1)


def _s5_body(u_ref, t_ref, g_ref, e_ref, pw_ref, dsk_ref, *rest, nb, nc, has_s0, want_final):
    rest = list(rest)
    s0_ref = rest.pop(0) if has_s0 else None
    o_ref = rest.pop(0)
    fin_ref = rest.pop(0) if want_final else None
    yt_s, xs_f, xs_b, tok_s = rest
    m = nb * nc
    w = 2 * S5_STATE
    ch = S5_GROUP_CH
    cidx = lax.broadcasted_iota(jnp.int32, (m, w), 0) & (nc - 1)
    first = cidx == 0
    last = cidx == nc - 1

    def tokens(t):
        return u_ref[:, pl.ds(t, nc, stride=S5_Q), :].reshape(m, LANES)

    for t in range(S5_Q):
        tok_s[t] = tokens(t).T

    def group_body(gg, carry):
        rows = pl.ds(pl.multiple_of(gg * ch, ch), ch)
        ug_t = jnp.concatenate([tok_s[t, rows, :] for t in range(S5_Q)], axis=0)
        ub = ug_t.T.astype(BF16)
        z = jnp.dot(ub, g_ref[gg], preferred_element_type=F32)
        x_f, x_b = z[:, :w], z[:, w:]
        pw = lambda d, r: pw_ref[d, gg, r:r + 1, :]
        if has_s0:
            rep = lambda a: jnp.broadcast_to(a[:, None, :], (nb, nc, w)).reshape(m, w)
            s0_f, s0_b = rep(s0_ref[0, gg, :, :w]), rep(s0_ref[0, gg, :, w:])
            x_f = x_f + jnp.where(first, _cmul(pw(0, 0), pw(0, 1), s0_f), 0.0)
            x_b = x_b + jnp.where(last, _cmul(pw(1, 0), pw(1, 1), s0_b), 0.0)
        for k in range(nc.bit_length() - 1):
            sh = 1 << k
            prev = pltpu.roll(x_f, sh, axis=0)
            x_f = x_f + jnp.where(cidx >= sh, _cmul(pw(0, 2 * k), pw(0, 2 * k + 1), prev), 0.0)
            nxt = pltpu.roll(x_b, m - sh, axis=0)
            x_b = x_b + jnp.where(cidx < nc - sh, _cmul(pw(1, 2 * k), pw(1, 2 * k + 1), nxt), 0.0)
        in_f = pltpu.roll(x_f, 1, axis=0)
        in_b = pltpu.roll(x_b, m - 1, axis=0)
        if has_s0:
            in_f = jnp.where(first, s0_f, in_f)
            in_b = jnp.where(last, s0_b, in_b)
        else:
            in_f = jnp.where(first, 0.0, in_f)
            in_b = jnp.where(last, 0.0, in_b)
        s_in = jnp.concatenate([in_f, in_b], axis=1).astype(BF16)
        y = (jnp.dot(ub, t_ref[gg], preferred_element_type=F32)
             + jnp.dot(s_in, e_ref[gg], preferred_element_type=F32))
        yt_s[gg] = y.T
        if want_final:
            xs_f[...] = x_f
            xs_b[...] = x_b
            fin_ref[0, gg, :, :w] = xs_f[pl.ds(nc - 1, nb, stride=nc), :]
            fin_ref[0, gg, :, w:] = xs_b[pl.ds(0, nb, stride=nc), :]
        return carry

    lax.fori_loop(0, S5_LG, group_body, 0)

    for t in range(S5_Q):
        y_t = jnp.concatenate([yt_s[k, t * ch:(t + 1) * ch, :] for k in range(S5_LG)], axis=0)
        y = y_t.T + tokens(t) * dsk_ref[0]
        o_ref[:, pl.ds(t, nc, stride=S5_Q), :] = _gelu_tanh(y).reshape(nb, nc, LANES)


def _s5(u, tables, d_skip, s0, want_final):
    t_all, g_fb, e_fb, pows = tables
    b, l, _ = u.shape
    nc = l // S5_Q
    bb = min(b, max(1, S5_ROWS // nc))
    nbb = b // bb
    m = bb * nc
    has_s0 = s0 is not None
    seq_spec = pl.BlockSpec((bb, l, LANES), lambda i, j: (j, 0, i))
    tab_spec = pl.BlockSpec((S5_LG, S5_CW, S5_CW), lambda i, j: (i, 0, 0),
                            pipeline_mode=pl.Buffered(1))
    st_spec = pl.BlockSpec((1, S5_LG, bb, S5_CW), lambda i, j: (j, i, 0, 0))
    st_blocks = lambda a: jnp.transpose(a.reshape(S5_GROUPS, nbb, bb, S5_CW), (1, 0, 2, 3))
    in_specs = [seq_spec, tab_spec, tab_spec, tab_spec,
                pl.BlockSpec((2, S5_LG, 16, 2 * S5_STATE), lambda i, j: (0, i, 0, 0)),
                pl.BlockSpec((1, 1, LANES), lambda i, j: (i, 0, 0))]
    args = [u, t_all, g_fb, e_fb, pows, d_skip.astype(F32).reshape(S5_WIDTH // LANES, 1, LANES)]
    if has_s0:
        in_specs.append(st_spec)
        args.append(st_blocks(s0))
    out_specs = [seq_spec]
    out_shape = [jax.ShapeDtypeStruct((b, l, S5_WIDTH), F32)]
    if want_final:
        out_specs.append(st_spec)
        out_shape.append(jax.ShapeDtypeStruct((nbb, S5_GROUPS, bb, S5_CW), F32))
    res = pl.pallas_call(
        functools.partial(_s5_body, nb=bb, nc=nc, has_s0=has_s0, want_final=want_final),
        grid=(S5_WIDTH // LANES, nbb),
        in_specs=in_specs,
        out_specs=out_specs,
        out_shape=out_shape,
        scratch_shapes=[pltpu.VMEM((S5_LG, S5_CW, m), F32),
                        pltpu.VMEM((m, 2 * S5_STATE), F32), pltpu.VMEM((m, 2 * S5_STATE), F32),
                        pltpu.VMEM((S5_Q, LANES, m), F32)],
        compiler_params=_cparams("arbitrary", "arbitrary"),
        name="s5",
    )(*args)
    fin = None
    if want_final:
        fin = jnp.transpose(res[1], (1, 0, 2, 3)).reshape(S5_GROUPS, b, S5_CW)
    return res[0], fin


def _dft_cos_sin(n, rows=None, row_step=1):
    rows = n if rows is None else rows
    j = jnp.arange(rows, dtype=jnp.int32)[:, None] * row_step
    jk = (j * jnp.arange(n, dtype=jnp.int32)[None, :]) % n
    ang = jk.astype(F32) * (2.0 * math.pi / n)
    return jnp.cos(ang), jnp.sin(ang)


def _dft_cos_sin_split(n, minor):
    c1, s1 = _dft_cos_sin(n, n // minor, minor)
    c2, s2 = _dft_cos_sin(n, minor)
    cos = c1[:, None, :] * c2[None, :, :] - s1[:, None, :] * s2[None, :, :]
    sin = s1[:, None, :] * c2[None, :, :] + c1[:, None, :] * s2[None, :, :]
    return cos.reshape(n, n), sin.reshape(n, n)


def _fnet_body(u_ref, cs_ref, dl_ref, o_ref, xs, *, seq):
    @pl.when(pl.program_id(1) == 0)
    def _():
        xcs = jnp.dot(u_ref[0].astype(BF16), cs_ref[...], preferred_element_type=F32)
        xs[0:seq, :] = xcs[:, :FNET_WIDTH].astype(BF16)
        xs[seq:2 * seq, :] = xcs[:, FNET_WIDTH:].astype(BF16)

    scale = 1.0 / math.sqrt(seq * FNET_GROUP_CH)
    o_ref[0] = (jnp.dot(dl_ref[...], xs[...], preferred_element_type=F32) * scale).astype(o_ref.dtype)


def _fnet(u):
    b, l, _ = u.shape
    tr = min(l, 512)
    cc, sc = _dft_cos_sin(FNET_GROUP_CH)
    eye = jnp.eye(FNET_GROUPS, dtype=F32)
    cs = jnp.concatenate([jnp.kron(eye, cc), jnp.kron(eye, sc)], axis=1).astype(BF16)
    cl, sl = _dft_cos_sin_split(l, DFT_MINOR) if l > 4 * DFT_MINOR else _dft_cos_sin(l)
    dl = jnp.concatenate([cl, -sl], axis=1).astype(BF16)
    return pl.pallas_call(
        functools.partial(_fnet_body, seq=l),
        grid=(b, l // tr),
        in_specs=[pl.BlockSpec((1, l, FNET_WIDTH), lambda i, t: (i, 0, 0)),
                  _const_spec((FNET_WIDTH, 2 * FNET_WIDTH)),
                  pl.BlockSpec((tr, 2 * l), lambda i, t: (t, 0))],
        out_specs=pl.BlockSpec((1, tr, FNET_WIDTH), lambda i, t: (i, t, 0)),
        out_shape=jax.ShapeDtypeStruct((b, l, FNET_WIDTH), BF16),
        scratch_shapes=[pltpu.VMEM((2 * l, FNET_WIDTH), BF16)],
        compiler_params=_cparams("arbitrary", "arbitrary"),
        name="fnet",
    )(u, cs, dl)


def _even_w_in(w):
    pad = jnp.zeros((D_MODEL, LANES - SSD_HEADS), w.dtype)
    return jnp.concatenate([w[:, :O_DT], w[:, O_Q:], w[:, O_DT:O_Q], pad], axis=1).astype(BF16)


EVEN_SPLITS = (SSD_WIDTH, SSD_XBC, NA_WIDTH, NA_WIDTH, NA_WIDTH, LANES)


def kernel(x_prompt, x_sample, cache_na_k, cache_na_v, state_ssd, state_s5, c, c_ctx, mod_w, mod_b, norm1_g, norm2_g, ffn_w1, ffn_w3, ffn_w2, final_g, ev_w_in, ev_conv_w, ev_conv_b, ev_a_log, ev_dt_bias, ev_d_skip, ev_ssd_norm_g, ev_rpb, ev_w_out, od_w_in, od_lam_re, od_lam_im, od_log_step, od_b_re, od_b_im, od_c_re, od_c_im, od_d_skip, od_glu_w, od_glu_b, od_w_out):
    bc, lc, _ = x_prompt.shape
    bl, ll, _ = x_sample.shape
    cond = jnp.zeros((MOD_ROWS, D_MODEL), F32).at[0].set(c_ctx).at[1:1 + bl].set(c)
    mods = _adaln(cond, mod_w, mod_b)
    xc = x_prompt.reshape(1, bc * lc, D_MODEL)
    xl = x_sample
    tm = 512
    new_k, new_v, new_ssd, new_s5 = [], [], [], []
    for i in range(DEPTH):
        j = i // 2
        mod = mods[i]
        last = i == DEPTH - 1
        if i % 2 == 0:
            w_in = _even_w_in(ev_w_in[j])
            w_out = ev_w_out[j].astype(BF16)
            ssd_args = (ev_conv_w[j], ev_conv_b[j], ev_a_log[j], ev_dt_bias[j], ev_d_skip[j],
                        ev_ssd_norm_g[j])
            z, xbc, q, k, v, dtp, k_hd, v_hd = _inproj(
                xc, mod, 0, norm1_g[i], w_in, EVEN_SPLITS, tm, (F32, F32, BF16, BF16, BF16, F32), (3, 4))
            seqs = lambda a: a.reshape(bc, lc, a.shape[-1])
            y_ssd, fin = _ssd(seqs(z), seqs(xbc), seqs(dtp), *ssd_args, None, True)
            o_na = _ctx_attn(seqs(q), seqs(k), seqs(v))
            flat = lambda a: a.reshape(1, bc * lc, a.shape[-1])
            mix_c = (flat(y_ssd), flat(o_na))
            new_k.append(k_hd.reshape(bc, lc, NA_HEADS, NA_HEAD_DIM))
            new_v.append(v_hd.reshape(bc, lc, NA_HEADS, NA_HEAD_DIM))
            new_ssd.append(_ssd_state_from_kernel(fin))

            z, xbc, q, k, v, dtp = _inproj(xl, mod, 1, norm1_g[i], w_in, EVEN_SPLITS, tm,
                                           (F32, F32, BF16, BF16, BF16, F32))
            s0t = _ssd_state_to_kernel(state_ssd[:, j].astype(F32))
            (y_ssd,) = _ssd(z, xbc, dtp, *ssd_args, s0t, False)
            pl_ = cache_na_k.shape[2]
            o_na = _latent_na(q, k, v, cache_na_k[:, j].reshape(bl, pl_, NA_WIDTH),
                              cache_na_v[:, j].reshape(bl, pl_, NA_WIDTH), ev_rpb[j])
            mix_l = (y_ssd, o_na)
            glu_w = glu_b = None
        else:
            w_in = od_w_in[j].astype(BF16)
            w_out = od_w_out[j].astype(BF16)
            glu_w, glu_b = od_glu_w[j].astype(BF16), od_glu_b[j]
            tables = _s5_tables(od_lam_re[j], od_lam_im[j], od_log_step[j], od_b_re[j], od_b_im[j],
                                od_c_re[j], od_c_im[j])
            splits = (S5_WIDTH, FNET_WIDTH)
            u_s, u_f = _inproj(xc, mod, 0, norm1_g[i], w_in, splits, tm, (F32, BF16))
            seqs = lambda a: a.reshape(bc, lc, a.shape[-1])
            g_s, fin = _s5(seqs(u_s), tables, od_d_skip[j], None, True)
            y_f = _fnet(seqs(u_f))
            flat = lambda a: a.reshape(1, bc * lc, a.shape[-1])
            mix_c = (flat(g_s), flat(y_f))
            fin = fin.reshape(S5_GROUPS, bc, 2, 2, S5_STATE)
            new_s5.append(jnp.transpose(fin, (1, 2, 0, 4, 3)))

            u_s, u_f = _inproj(xl, mod, 1, norm1_g[i], w_in, splits, tm, (F32, BF16))
            s0 = jnp.transpose(state_s5[:, j].astype(F32), (2, 0, 1, 4, 3)).reshape(
                S5_GROUPS, bl, 4 * S5_STATE)
            g_s, _ = _s5(u_s, tables, od_d_skip[j], s0, False)
            y_f = _fnet(u_f)
            mix_l = (g_s, y_f)
        w_o = (w_out[:SSD_WIDTH], w_out[SSD_WIDTH:])
        w1, w3, w2 = ffn_w1[i].astype(BF16), ffn_w3[i].astype(BF16), ffn_w2[i].astype(BF16)
        fg = final_g if last else None
        xc = _ffn(xc, mod, 0, *mix_c, *w_o, glu_w, glu_b, norm2_g[i], w1, w3, w2, fg, tm)
        xl = _ffn(xl, mod, 1, *mix_l, *w_o, glu_w, glu_b, norm2_g[i], w1, w3, w2, fg, tm)
    return (xc.reshape(bc, lc, D_MODEL), xl,
            jnp.stack(new_k, axis=1), jnp.stack(new_v, axis=1),
            jnp.stack(new_ssd, axis=1), jnp.stack(new_s5, axis=1))
```

```python
import functools
import math

import numpy as np
import jax
import jax.numpy as jnp
from jax import lax
from jax.experimental import pallas as pl
from jax.experimental.pallas import tpu as pltpu

F32 = jnp.float32
BF16 = jnp.bfloat16
HIGHEST = lax.Precision.HIGHEST

D_MODEL = 1024
DEPTH = 2
EPS = 1e-6
GRID_W = 64
FFN_HIDDEN = 2816
MXU_COLS = 256
FFN_CHUNKS = (6 * MXU_COLS, 5 * MXU_COLS)
assert sum(FFN_CHUNKS) == FFN_HIDDEN

SSD_HEADS = 8
SSD_HEAD_DIM = 64
SSD_WIDTH = 512
SSD_GROUPS = 2
SSD_HPG = SSD_HEADS // SSD_GROUPS
SSD_STATE = 64
SSD_GN = SSD_GROUPS * SSD_STATE
SSD_XBC = SSD_WIDTH + 2 * SSD_GN
SSD_CHUNK = 128
CONV_W = 5
CONV_HALO = 8

NA_HEADS = 8
NA_HEAD_DIM = 64
NA_WIDTH = 512
NA_ROWS = 8
NA_COLS = 16
NA_SCALE = NA_HEAD_DIM ** -0.5
NA_QROWS = 4
NA_WIN = NA_ROWS + NA_QROWS
NEG_BIG = -1e30

O_XBC = SSD_WIDTH
O_DT = O_XBC + SSD_XBC
O_Q = O_DT + SSD_HEADS
O_K = O_Q + NA_WIDTH
O_V = O_K + NA_WIDTH
LANES = 128

S5_GROUP_CH = 16
S5_GROUPS = 32
S5_WIDTH = 512
S5_STATE = 64
S5_Q = 16
S5_CW = S5_Q * S5_GROUP_CH
S5_LOG_MAX = 7
S5_LG = LANES // S5_GROUP_CH
S5_ROWS = 512

FNET_GROUP_CH = 64
FNET_GROUPS = 8
FNET_WIDTH = 512
DFT_MINOR = 64

MOD_ROWS = 16
VMEM_LIMIT = 56 * 1024 * 1024


def _cparams(*sem):
    return pltpu.CompilerParams(dimension_semantics=sem, vmem_limit_bytes=VMEM_LIMIT)


def _const_spec(shape):
    nd = len(shape)
    return pl.BlockSpec(shape, lambda *_: (0,) * nd, pipeline_mode=pl.Buffered(1))


def _sigmoid(x):
    return 1.0 / (1.0 + jnp.exp(-x))


def _silu(x):
    return x * _sigmoid(x)


def _softplus(x):
    return jnp.maximum(x, 0.0) + jnp.log(1.0 + jnp.exp(-jnp.abs(x)))


def _gelu_tanh(x):
    return 0.5 * x * (1.0 + jnp.tanh(math.sqrt(2.0 / math.pi) * (x + 0.044715 * (x * x * x))))


def _rmsnorm(x, g):
    return x * lax.rsqrt(jnp.mean(x * x, axis=-1, keepdims=True) + EPS) * g


def _bdot(a, b):
    return jnp.dot(a.astype(BF16), b.astype(BF16), preferred_element_type=F32)


def _bdot_nt(a, b):
    return lax.dot_general(a.astype(BF16), b.astype(BF16), (((1,), (1,)), ((), ())),
                           preferred_element_type=F32)


def _bdot_tn(a, b):
    return lax.dot_general(a.astype(BF16), b.astype(BF16), (((0,), (0,)), ((), ())),
                           preferred_element_type=F32)


def _adaln_body(c_ref, w_ref, b_ref, o_ref):
    s = _silu(c_ref[...])
    o_ref[0] = jnp.dot(s, w_ref[0], precision=HIGHEST, preferred_element_type=F32) + b_ref[0]


def _adaln(cond, mod_w, mod_b):
    n = 6 * D_MODEL
    tn = 1536
    out = pl.pallas_call(
        _adaln_body,
        grid=(DEPTH, n // tn),
        in_specs=[pl.BlockSpec((MOD_ROWS, D_MODEL), lambda i, j: (0, 0)),
                  pl.BlockSpec((1, D_MODEL, tn), lambda i, j: (i, 0, j)),
                  pl.BlockSpec((1, 1, tn), lambda i, j: (i, 0, j))],
        out_specs=pl.BlockSpec((1, MOD_ROWS, tn), lambda i, j: (i, 0, j)),
        out_shape=jax.ShapeDtypeStruct((DEPTH, MOD_ROWS, n), F32),
        compiler_params=_cparams("arbitrary", "arbitrary"),
        name="adaln",
    )(cond, mod_w, mod_b.reshape(DEPTH, 1, n))
    return out.reshape(DEPTH, MOD_ROWS, 6, D_MODEL)


def _mod_spec(mrow0):
    return pl.BlockSpec((1, 6, D_MODEL), lambda b, t: (b + mrow0, 0, 0))


def _inproj_body(x_ref, mod_ref, g_ref, w_ref, *o_refs, splits, heads_of):
    half = x_ref.shape[1] // 2
    for r in range(2):
        rows = slice(r * half, (r + 1) * half)
        h = _rmsnorm(x_ref[0, rows, :], g_ref[...]) * (1.0 + mod_ref[0, 1:2, :]) + mod_ref[0, 0:1, :]
        acc = jnp.dot(h.astype(BF16), w_ref[...], preferred_element_type=F32)
        off = 0
        offs = []
        for o_ref, n in zip(o_refs, splits):
            o_ref[0, rows, :] = acc[:, off:off + n].astype(o_ref.dtype)
            offs.append(off)
            off += n
        for o_ref, i in zip(o_refs[len(splits):], heads_of):
            nh, hd = o_ref.shape[2], o_ref.shape[3]
            for hh in range(nh):
                o_ref[0, rows, hh, :] = acc[:, offs[i] + hh * hd:offs[i] + (hh + 1) * hd]


def _inproj(x, mod, mrow0, g, w, splits, tm, dtypes=None, heads_of=()):
    b, l, _ = x.shape
    n = w.shape[1]
    dtypes = (F32,) * len(splits) if dtypes is None else dtypes
    head_shape = lambda i: (splits[i] // NA_HEAD_DIM, NA_HEAD_DIM)
    return pl.pallas_call(
        functools.partial(_inproj_body, splits=splits, heads_of=heads_of),
        grid=(b, l // tm),
        in_specs=[pl.BlockSpec((1, tm, D_MODEL), lambda i, t: (i, t, 0)),
                  _mod_spec(mrow0),
                  _const_spec((1, D_MODEL)),
                  _const_spec((D_MODEL, n))],
        out_specs=([pl.BlockSpec((1, tm, s), lambda i, t: (i, t, 0)) for s in splits]
                   + [pl.BlockSpec((1, tm) + head_shape(i), lambda i_, t: (i_, t, 0, 0))
                      for i in heads_of]),
        out_shape=([jax.ShapeDtypeStruct((b, l, s), dt) for s, dt in zip(splits, dtypes)]
                   + [jax.ShapeDtypeStruct((b, l) + head_shape(i), F32) for i in heads_of]),
        compiler_params=_cparams("arbitrary", "arbitrary"),
        name="inproj",
    )(x, mod, g.reshape(1, D_MODEL), w)


def _ffn_body(x_ref, mod_ref, ya_ref, yb_ref, wa_ref, wb_ref, g_ref, w1_ref, w3_ref, w2_ref, *rest,
              glu, final):
    rest = list(rest)
    ya = ya_ref[0]
    if glu:
        gw_ref, gb_ref = rest.pop(0), rest.pop(0)
        ya = ya * _sigmoid(_bdot(ya, gw_ref[...]) + gb_ref[...])
    mix = _bdot(ya, wa_ref[...]) + _bdot(yb_ref[0], wb_ref[...])
    x = x_ref[0] + mod_ref[0, 2:3, :] * mix
    h = (_rmsnorm(x, g_ref[...]) * (1.0 + mod_ref[0, 4:5, :]) + mod_ref[0, 3:4, :]).astype(BF16)
    acc = jnp.zeros(x.shape, F32)
    for c in range(len(FFN_CHUNKS)):
        cols = slice(sum(FFN_CHUNKS[:c]), sum(FFN_CHUNKS[:c + 1]))
        a = jnp.dot(h, w1_ref[:, cols], preferred_element_type=F32)
        u = _silu(a) * jnp.dot(h, w3_ref[:, cols], preferred_element_type=F32)
        acc = acc + jnp.dot(u.astype(BF16), w2_ref[cols, :], preferred_element_type=F32)
    y = x + mod_ref[0, 5:6, :] * acc
    if final:
        fg_ref, o_ref = rest
        y = _rmsnorm(y, fg_ref[...])
    else:
        (o_ref,) = rest
    o_ref[0] = y


def _ffn(x, mod, mrow0, ya, yb, wa, wb, glu_w, glu_b, g, w1, w3, w2, final_g, tm):
    b, l, _ = x.shape
    wa_n, wb_n = ya.shape[-1], yb.shape[-1]
    glu = glu_w is not None
    final = final_g is not None
    row_spec = lambda n: pl.BlockSpec((1, tm, n), lambda i, t: (i, t, 0))
    in_specs = [row_spec(D_MODEL), _mod_spec(mrow0), row_spec(wa_n), row_spec(wb_n),
                _const_spec((wa_n, D_MODEL)), _const_spec((wb_n, D_MODEL)),
                _const_spec((1, D_MODEL)),
                _const_spec((D_MODEL, FFN_HIDDEN)),
                _const_spec((D_MODEL, FFN_HIDDEN)),
                _const_spec((FFN_HIDDEN, D_MODEL))]
    args = [x, mod, ya, yb, wa, wb, g.reshape(1, D_MODEL), w1, w3, w2]
    if glu:
        in_specs += [_const_spec((wa_n, wa_n)), _const_spec((1, wa_n))]
        args += [glu_w, glu_b.reshape(1, wa_n)]
    if final:
        in_specs.append(_const_spec((1, D_MODEL)))
        args.append(final_g.reshape(1, D_MODEL))
    return pl.pallas_call(
        functools.partial(_ffn_body, glu=glu, final=final),
        grid=(b, l // tm),
        in_specs=in_specs,
        out_specs=pl.BlockSpec((1, tm, D_MODEL), lambda i, t: (i, t, 0)),
        out_shape=jax.ShapeDtypeStruct((b, l, D_MODEL), F32),
        compiler_params=_cparams("arbitrary", "arbitrary"),
        name="ffn",
    )(*args)


def _ssd_body(z_ref, xbc_ref, dt_ref, cw_ref, cb_ref, hp_ref, dsk_ref, ng_ref, *rest,
              seq, has_s0, want_final):
    rest = list(rest)
    s0_ref = rest.pop(0) if has_s0 else None
    y_ref = rest.pop(0)
    fin_ref = rest.pop(0) if want_final else None
    st_s, cd_s, scur, ych, xc_s, cum_s, tr_s = rest
    q = SSD_CHUNK
    nc = seq // q
    gw = SSD_HPG * SSD_HEAD_DIM
    expand = (lax.broadcasted_iota(jnp.int32, (LANES, SSD_WIDTH), 1) // SSD_HEAD_DIM
              == lax.broadcasted_iota(jnp.int32, (LANES, SSD_WIDTH), 0)).astype(BF16)

    def per_head(v):
        hi = v.astype(BF16)
        lo = (v - hi.astype(F32)).astype(BF16)
        return (jnp.dot(hi, expand, preferred_element_type=F32)
                + jnp.dot(lo, expand, preferred_element_type=F32))
    row = lax.broadcasted_iota(jnp.int32, (q, q), 0)
    col = lax.broadcasted_iota(jnp.int32, (q, q), 1)
    lower = row >= col
    upper = col >= row
    tri_l = lower.astype(F32)
    tri_u = upper.astype(F32)
    a_f = -jnp.exp(hp_ref[0:1, :])
    a_b = -jnp.exp(hp_ref[1:2, :])
    lane = lax.broadcasted_iota(jnp.int32, (1, LANES), 1)
    a_f = jnp.where(lane < SSD_HEADS, a_f, 0.0)
    a_b = jnp.where(lane < SSD_HEADS, a_b, 0.0)
    bias_f = hp_ref[2:3, :]
    bias_b = hp_ref[3:4, :]

    def chunk_pre(c):
        r0 = pl.multiple_of(c * q, q)
        lo = pl.multiple_of(jnp.maximum(r0 - CONV_HALO, 0), CONV_HALO)
        hi = pl.multiple_of(jnp.minimum(r0 + q, seq - CONV_HALO), CONV_HALO)
        prev = jnp.where(c > 0, xbc_ref[0, pl.ds(lo, CONV_HALO), :], 0.0)
        nxt = jnp.where(c < nc - 1, xbc_ref[0, pl.ds(hi, CONV_HALO), :], 0.0)
        win = jnp.concatenate([prev, xbc_ref[0, pl.ds(r0, q), :], nxt], axis=0)
        acc = cb_ref[...] + cw_ref[0:1, :] * win[CONV_HALO - 2:CONV_HALO - 2 + q]
        for k in range(1, CONV_W):
            o = CONV_HALO - CONV_W // 2 + k
            acc = acc + cw_ref[k:k + 1, :] * win[o:o + q]
        xc = _silu(acc)
        dtr = dt_ref[0, pl.ds(r0, q), :]
        dt_f = _softplus(dtr + bias_f)
        dt_b = _softplus(dtr + bias_b)
        cum_f = jnp.dot(tri_l, dt_f * a_f, precision=HIGHEST, preferred_element_type=F32)
        cum_b = jnp.dot(tri_u, dt_b * a_b, precision=HIGHEST, preferred_element_type=F32)
        return r0, xc, dt_f, dt_b, cum_f, cum_b

    def pass_a(c, carry):
        r0, xc, dt_f, dt_b, cum_f, cum_b = chunk_pre(c)
        xc_s[pl.ds(r0, q), :] = xc
        cum_s[0, pl.ds(r0, q), :] = cum_f
        cum_s[1, pl.ds(r0, q), :] = cum_b
        for i, v in enumerate((cum_f, cum_b, dt_f, dt_b)):
            tr_s[c, i] = v.T[0:SSD_HEADS, :]
        end_f = cum_f[q - 1:q, :]
        end_b = cum_b[0:1, :]
        xw_f = xc[:, :SSD_WIDTH] * per_head(jnp.exp(end_f - cum_f) * dt_f)
        xw_b = xc[:, :SSD_WIDTH] * per_head(jnp.exp(end_b - cum_b) * dt_b)
        cd_s[0, c] = per_head(jnp.broadcast_to(jnp.exp(end_f), (8, LANES)))
        cd_s[1, c] = per_head(jnp.broadcast_to(jnp.exp(end_b), (8, LANES)))
        for g in range(SSD_GROUPS):
            bg = xc[:, SSD_WIDTH + g * SSD_STATE:SSD_WIDTH + (g + 1) * SSD_STATE]
            st_s[0, c, g] = _bdot_tn(bg, xw_f[:, g * gw:(g + 1) * gw])
            st_s[1, c, g] = _bdot_tn(bg, xw_b[:, g * gw:(g + 1) * gw])
        return carry

    lax.fori_loop(0, nc, pass_a, 0)

    if has_s0:
        scur[...] = s0_ref[0]
    else:
        scur[...] = jnp.zeros(scur.shape, F32)

    def pass_b(c, carry):
        cr = nc - 1 - c
        for g in range(SSD_GROUPS):
            s_in = scur[0, g]
            scur[0, g] = s_in * cd_s[0, c, 0:1, g * gw:(g + 1) * gw] + st_s[0, c, g]
            st_s[0, c, g] = s_in
            s_in = scur[1, g]
            scur[1, g] = s_in * cd_s[1, cr, 0:1, g * gw:(g + 1) * gw] + st_s[1, cr, g]
            st_s[1, cr, g] = s_in
        return carry

    lax.fori_loop(0, nc, pass_b, 0)
    if want_final:
        fin_ref[0] = scur[...]

    def pass_c(c, carry):
        r0 = pl.multiple_of(c * q, q)
        xc = xc_s[pl.ds(r0, q), :]
        cum_f = cum_s[0, pl.ds(r0, q), :]
        cum_b = cum_s[1, pl.ds(r0, q), :]
        cum_ft, cum_bt, dt_ft, dt_bt = tr_s[c, 0], tr_s[c, 1], tr_s[c, 2], tr_s[c, 3]
        ex_f = per_head(jnp.exp(cum_f))
        ex_b = per_head(jnp.exp(cum_b))
        y_off = []
        cbs = []
        for g in range(SSD_GROUPS):
            gs = slice(g * gw, (g + 1) * gw)
            bg = xc[:, SSD_WIDTH + g * SSD_STATE:SSD_WIDTH + (g + 1) * SSD_STATE]
            cg = xc[:, SSD_WIDTH + SSD_GN + g * SSD_STATE:SSD_WIDTH + SSD_GN + (g + 1) * SSD_STATE]
            cbs.append(_bdot_nt(cg, bg))
            y_off.append(ex_f[:, gs] * _bdot(cg, st_s[0, c, g]) + ex_b[:, gs] * _bdot(cg, st_s[1, c, g]))
        for h in range(SSD_HEADS):
            xh = xc[:, h * SSD_HEAD_DIM:(h + 1) * SSD_HEAD_DIM]
            seg_f = cum_f[:, h:h + 1] - cum_ft[h:h + 1, :]
            seg_b = cum_b[:, h:h + 1] - cum_bt[h:h + 1, :]
            m_f = jnp.exp(jnp.where(lower, seg_f, NEG_BIG)) * dt_ft[h:h + 1, :]
            m_b = jnp.exp(jnp.where(upper, seg_b, NEG_BIG)) * dt_bt[h:h + 1, :]
            ych[:, h * SSD_HEAD_DIM:(h + 1) * SSD_HEAD_DIM] = _bdot(cbs[h // SSD_HPG] * (m_f + m_b), xh)
        yf = ych[...] + jnp.concatenate(y_off, axis=1) + dsk_ref[...] * xc[:, :SSD_WIDTH]
        yf = yf * _silu(z_ref[0, pl.ds(r0, q), :])
        y_ref[0, pl.ds(r0, q), :] = _rmsnorm(yf, ng_ref[...]).astype(y_ref.dtype)
        return carry

    lax.fori_loop(0, nc, pass_c, 0)


def _ssd_state_to_kernel(s):
    b = s.shape[0]
    s = s.reshape(b, 2, SSD_GROUPS, SSD_HPG, SSD_HEAD_DIM, SSD_STATE)
    return jnp.transpose(s, (0, 1, 2, 5, 3, 4)).reshape(b, 2, SSD_GROUPS, SSD_STATE, SSD_HPG * SSD_HEAD_DIM)


def _ssd_state_from_kernel(s):
    b = s.shape[0]
    s = s.reshape(b, 2, SSD_GROUPS, SSD_STATE, SSD_HPG, SSD_HEAD_DIM)
    return jnp.transpose(s, (0, 1, 2, 4, 5, 3)).reshape(b, 2, SSD_HEADS, SSD_HEAD_DIM, SSD_STATE)


def _ssd(z, xbc, dtp, conv_w, conv_b, a_log, dt_bias, d_skip, norm_g, s0t, want_final):
    b, l, _ = z.shape
    nc = l // SSD_CHUNK
    gw = SSD_HPG * SSD_HEAD_DIM
    has_s0 = s0t is not None
    cw = jnp.zeros((8, SSD_XBC), F32).at[:CONV_W].set(conv_w)
    hp = jnp.zeros((8, LANES), F32)
    hp = hp.at[0:2, :SSD_HEADS].set(a_log).at[2:4, :SSD_HEADS].set(dt_bias)
    dsk = jnp.repeat(d_skip, SSD_HEAD_DIM).reshape(1, SSD_WIDTH)
    seq_spec = lambda n: pl.BlockSpec((1, l, n), lambda i: (i, 0, 0))
    st_spec = pl.BlockSpec((1, 2, SSD_GROUPS, SSD_STATE, gw), lambda i: (i, 0, 0, 0, 0))
    in_specs = [seq_spec(SSD_WIDTH), seq_spec(SSD_XBC), seq_spec(LANES),
                _const_spec((8, SSD_XBC)), _const_spec((1, SSD_XBC)), _const_spec((8, LANES)),
                _const_spec((1, SSD_WIDTH)), _const_spec((1, SSD_WIDTH))]
    args = [z, xbc, dtp, cw, conv_b.reshape(1, SSD_XBC), hp, dsk, norm_g.reshape(1, SSD_WIDTH)]
    if has_s0:
        in_specs.append(st_spec)
        args.append(s0t)
    out_specs = [seq_spec(SSD_WIDTH)]
    out_shape = [jax.ShapeDtypeStruct((b, l, SSD_WIDTH), BF16)]
    if want_final:
        out_specs.append(st_spec)
        out_shape.append(jax.ShapeDtypeStruct((b, 2, SSD_GROUPS, SSD_STATE, gw), F32))
    return pl.pallas_call(
        functools.partial(_ssd_body, seq=l, has_s0=has_s0, want_final=want_final),
        grid=(b,),
        in_specs=in_specs,
        out_specs=out_specs,
        out_shape=out_shape,
        scratch_shapes=[pltpu.VMEM((2, nc, SSD_GROUPS, SSD_STATE, gw), F32),
                        pltpu.VMEM((2, nc, 8, SSD_WIDTH), F32),
                        pltpu.VMEM((2, SSD_GROUPS, SSD_STATE, gw), F32),
                        pltpu.VMEM((SSD_CHUNK, SSD_WIDTH), F32),
                        pltpu.VMEM((l, SSD_XBC), F32),
                        pltpu.VMEM((2, l, LANES), F32),
                        pltpu.VMEM((nc, 4, SSD_HEADS, LANES), F32)],
        compiler_params=_cparams("arbitrary"),
        name="ssd",
    )(*args)


def _softmax_pv(parts):
    m = parts[0][0].max(axis=-1, keepdims=True)
    for s, _ in parts[1:]:
        m = jnp.maximum(m, s.max(axis=-1, keepdims=True))
    den = 0.0
    out = 0.0
    for s, v in parts:
        p = jnp.exp(s - m)
        den = den + p.sum(axis=-1, keepdims=True)
        out = out + _bdot(p, v)
    return out / den


def _ctx_attn_body(q_ref, k_ref, v_ref, o_ref):
    for h in range(NA_HEADS):
        hs = slice(h * NA_HEAD_DIM, (h + 1) * NA_HEAD_DIM)
        s = _bdot_nt(q_ref[0, :, hs], k_ref[0, :, hs]) * NA_SCALE
        o_ref[0, :, hs] = _softmax_pv([(s, v_ref[0, :, hs])]).astype(o_ref.dtype)


def _ctx_attn(q, k, v):
    b, l, _ = q.shape
    spec = pl.BlockSpec((1, l, NA_WIDTH), lambda i: (i, 0, 0))
    return pl.pallas_call(
        _ctx_attn_body,
        grid=(b,),
        in_specs=[spec, spec, spec],
        out_specs=spec,
        out_shape=jax.ShapeDtypeStruct((b, l, NA_WIDTH), BF16),
        compiler_params=_cparams("arbitrary"),
        name="ctx_attn",
    )(q, k, v)


def _na_bias_tables(rpb, rows):
    qc = np.arange(GRID_W)[:, None]
    kc = np.arange(GRID_W)[None, :]
    cs = np.clip(qc - NA_COLS // 2, 0, GRID_W - NA_COLS)
    col_ok = (kc >= cs) & (kc < cs + NA_COLS)
    dc = np.clip(kc - qc + NA_COLS - 1, 0, 2 * NA_COLS - 2)
    col_sel = (dc[None] == np.arange(2 * NA_COLS - 1)[:, None, None]).astype(np.float32)
    by_col = jnp.einsum('hab,bqk->haqk', rpb.astype(F32), jnp.asarray(col_sel), precision=HIGHEST)
    by_col = jnp.where(jnp.asarray(col_ok), by_col, NEG_BIG)
    n_dr = 2 * NA_ROWS - 1
    return pl.pallas_call(
        functools.partial(_na_bias_body, rows=rows),
        grid=(3, NA_HEADS),
        in_specs=[pl.BlockSpec((1, n_dr, GRID_W, GRID_W), lambda v, h: (h, 0, 0, 0))],
        out_specs=pl.BlockSpec((1, 1, NA_QROWS * GRID_W, NA_WIN * GRID_W), lambda v, h: (v, h, 0, 0)),
        out_shape=jax.ShapeDtypeStruct((3, NA_HEADS, NA_QROWS * GRID_W, NA_WIN * GRID_W), F32),
        compiler_params=_cparams("arbitrary", "arbitrary"),
        name="na_bias",
    )(by_col)


def _na_bias_body(u_ref, o_ref, *, rows):
    v = pl.program_id(0)
    kr = min(NA_ROWS, rows)
    r0 = jnp.where(v == 0, 0, jnp.where(v == 1, NA_QROWS, rows - NA_QROWS))
    ws = jnp.clip(r0 - kr // 2, 0, rows - NA_WIN)
    for qi in range(NA_QROWS):
        r = r0 + qi
        rs = jnp.clip(r - kr // 2, 0, rows - kr)
        for w in range(NA_WIN):
            krow = ws + w
            ok = (krow >= rs) & (krow < rs + kr)
            dr = jnp.clip(krow - r + NA_ROWS - 1, 0, 2 * NA_ROWS - 2)
            o_ref[0, 0, qi * GRID_W:(qi + 1) * GRID_W, w * GRID_W:(w + 1) * GRID_W] = jnp.where(
                ok, u_ref[0, dr], NEG_BIG)


def _na_body(q_ref, k_ref, v_ref, kc_ref, vc_ref, bias_ref, o_ref, *, rows):
    rb = pl.program_id(1)
    ws = jnp.clip(rb * NA_QROWS - NA_ROWS // 2, 0, rows - NA_WIN)
    k0 = pl.multiple_of(ws * GRID_W, NA_QROWS * GRID_W)
    nk = NA_WIN * GRID_W
    for h in range(NA_HEADS):
        hs = slice(h * NA_HEAD_DIM, (h + 1) * NA_HEAD_DIM)
        qh = q_ref[0, :, hs]
        s_loc = _bdot_nt(qh, k_ref[0, pl.ds(k0, nk), hs]) * NA_SCALE + bias_ref[0, h]
        s_ctx = _bdot_nt(qh, kc_ref[0, :, hs]) * NA_SCALE
        o_ref[0, :, hs] = _softmax_pv([(s_loc, v_ref[0, pl.ds(k0, nk), hs]),
                                       (s_ctx, vc_ref[0, :, hs])]).astype(o_ref.dtype)


def _latent_na(q, k, v, k_ctx, v_ctx, rpb):
    b, l, _ = q.shape
    lc = k_ctx.shape[1]
    rows = l // GRID_W
    assert rows >= NA_WIN and rows % NA_QROWS == 0 and NA_QROWS * 2 <= NA_ROWS
    nrb = rows // NA_QROWS
    tq = NA_QROWS * GRID_W
    bias = _na_bias_tables(rpb, rows)
    variant = lambda i, r: (jnp.where(r == 0, 0, jnp.where(r == nrb - 1, 2, 1)), 0, 0, 0)
    return pl.pallas_call(
        functools.partial(_na_body, rows=rows),
        grid=(b, nrb),
        in_specs=[pl.BlockSpec((1, tq, NA_WIDTH), lambda i, r: (i, r, 0)),
                  pl.BlockSpec((1, l, NA_WIDTH), lambda i, r: (i, 0, 0)),
                  pl.BlockSpec((1, l, NA_WIDTH), lambda i, r: (i, 0, 0)),
                  pl.BlockSpec((1, lc, NA_WIDTH), lambda i, r: (i, 0, 0)),
                  pl.BlockSpec((1, lc, NA_WIDTH), lambda i, r: (i, 0, 0)),
                  pl.BlockSpec((1, NA_HEADS, tq, NA_WIN * GRID_W), variant)],
        out_specs=pl.BlockSpec((1, tq, NA_WIDTH), lambda i, r: (i, r, 0)),
        out_shape=jax.ShapeDtypeStruct((b, l, NA_WIDTH), BF16),
        compiler_params=_cparams("arbitrary", "arbitrary"),
        name="latent_na",
    )(q, k, v, k_ctx, v_ctx, bias)


def _s5_prep_body(lre_ref, lim_ref, lst_ref, btre_ref, btim_ref, cre_ref, cim_ref,
                  t_ref, g_ref, e_ref, pw_ref, g_s, et_s, e0_s):
    q, ch, p = S5_Q, S5_GROUP_CH, S5_STATE
    lane = lax.broadcasted_iota(jnp.int32, (ch, S5_CW), 1)
    bt_re, bt_im = btre_ref[0], btim_ref[0]
    c_re, c_im = cre_ref[0], cim_ref[0]
    taps = []
    for d in range(2):
        lam_re, lam_im = lre_ref[d, 0], lim_ref[d, 0]
        step = jnp.exp(lst_ref[d, 0])
        mag = jnp.exp(lam_re * step)
        a_re = mag * jnp.cos(lam_im * step)
        a_im = mag * jnp.sin(lam_im * step)
        den = lam_re * lam_re + lam_im * lam_im
        k_re = ((a_re - 1.0) * lam_re + a_im * lam_im) / den
        k_im = (a_im * lam_re - (a_re - 1.0) * lam_im) / den
        bb_re = k_re * bt_re - k_im * bt_im
        bb_im = k_re * bt_im + k_im * bt_re
        p_re = jnp.ones_like(a_re)
        p_im = jnp.zeros_like(a_re)
        for t in range(q):
            tg = q - 1 - t if d == 0 else t
            te = t if d == 0 else q - 1 - t
            g_s[tg * ch:(tg + 1) * ch, 2 * p * d:2 * p * d + p] = p_re * bb_re - p_im * bb_im
            g_s[tg * ch:(tg + 1) * ch, 2 * p * d + p:2 * p * (d + 1)] = p_re * bb_im + p_im * bb_re
            e0_s[te * ch:(te + 1) * ch, 0:p] = c_re * p_re - c_im * p_im
            e0_s[te * ch:(te + 1) * ch, p:2 * p] = -(c_re * p_im + c_im * p_re)
            p_re, p_im = p_re * a_re - p_im * a_im, p_re * a_im + p_im * a_re
            et_s[d, te * ch:(te + 1) * ch, 0:p] = c_re * p_re - c_im * p_im
            et_s[d, te * ch:(te + 1) * ch, p:2 * p] = -(c_re * p_im + c_im * p_re)
        bb = jnp.concatenate([bb_re, bb_im], axis=1)
        taps.append(lax.dot_general(bb, e0_s[...], (((1,), (1,)), ((), ())), precision=HIGHEST,
                                    preferred_element_type=F32))
        pw_ref[d, 0] = jnp.zeros(pw_ref.shape[2:], F32)
        for k in range(S5_LOG_MAX):
            pw_ref[d, 0, 2 * k:2 * k + 1, :] = jnp.concatenate([p_re, p_re], axis=1)
            pw_ref[d, 0, 2 * k + 1:2 * k + 2, :] = jnp.concatenate([-p_im, p_im], axis=1)
            p_re, p_im = p_re * p_re - p_im * p_im, 2.0 * p_re * p_im
    for ti in range(q):
        fwd = taps[0] if ti == 0 else pltpu.roll(taps[0], ch * ti, axis=1)
        bwd = taps[1] if ti == q - 1 else pltpu.roll(taps[1], S5_CW - ch * (q - 1 - ti), axis=1)
        row = jnp.where(lane >= ch * ti, fwd, 0.0) + jnp.where(lane < ch * (ti + 1), bwd, 0.0)
        t_ref[0, ti * ch:(ti + 1) * ch, :] = row.astype(t_ref.dtype)
    g_ref[0] = g_s[...].astype(g_ref.dtype)
    e_ref[0, 0:2 * p, :] = et_s[0].T.astype(e_ref.dtype)
    e_ref[0, 2 * p:4 * p, :] = et_s[1].T.astype(e_ref.dtype)


def _s5_tables(lam_re, lam_im, log_step, b_re, b_im, c_re, c_im):
    g, p, ch = S5_GROUPS, S5_STATE, S5_GROUP_CH
    row = lambda a: a.astype(F32).reshape(2, g, 1, p)
    lst = jnp.broadcast_to(log_step.astype(F32)[:, :, None, None], (2, g, 1, p))
    bt_re = jnp.swapaxes(b_re.astype(F32), 1, 2)
    bt_im = jnp.swapaxes(b_im.astype(F32), 1, 2)
    dspec = pl.BlockSpec((2, 1, 1, p), lambda i: (0, i, 0, 0))
    gspec = pl.BlockSpec((1, ch, p), lambda i: (i, 0, 0))
    tspec = pl.BlockSpec((1, S5_CW, S5_CW), lambda i: (i, 0, 0))
    tshape = jax.ShapeDtypeStruct((g, S5_CW, S5_CW), BF16)
    return pl.pallas_call(
        _s5_prep_body,
        grid=(g,),
        in_specs=[dspec, dspec, dspec, gspec, gspec, gspec, gspec],
        out_specs=[tspec, tspec, tspec, pl.BlockSpec((2, 1, 16, 2 * p), lambda i: (0, i, 0, 0))],
        out_shape=[tshape, tshape, tshape, jax.ShapeDtypeStruct((2, g, 16, 2 * p), F32)],
        scratch_shapes=[pltpu.VMEM((S5_CW, 4 * p), F32), pltpu.VMEM((2, S5_CW, 2 * p), F32),
                        pltpu.VMEM((S5_CW, 2 * p), F32)],
        compiler_params=_cparams("arbitrary"),
        name="s5_prep",
    )(row(lam_re), row(lam_im), lst, bt_re, bt_im, c_re.astype(F32), c_im.astype(F32))


def _cmul(a_full, a_sgn, s):
    return a_full * s + a_sgn * pltpu.roll(s, S5_STATE, axis=1)


def _s5_body(u_ref, t_ref, g_ref, e_ref, pw_ref, dsk_ref, *rest, nb, nc, has_s0, want_final):
    rest = list(rest)
    s0_ref = rest.pop(0) if has_s0 else None
    o_ref = rest.pop(0)
    fin_ref = rest.pop(0) if want_final else None
    yt_s, xs_f, xs_b, tok_s = rest
    m = nb * nc
    w = 2 * S5_STATE
    ch = S5_GROUP_CH
    cidx = lax.broadcasted_iota(jnp.int32, (m, w), 0) & (nc - 1)
    first = cidx == 0
    last = cidx == nc - 1

    def tokens(t):
        return u_ref[:, pl.ds(t, nc, stride=S5_Q), :].reshape(m, LANES)

    for t in range(S5_Q):
        tok_s[t] = tokens(t).T

    def group_body(gg, carry):
        rows = pl.ds(pl.multiple_of(gg * ch, ch), ch)
        ug_t = jnp.concatenate([tok_s[t, rows, :] for t in range(S5_Q)], axis=0)
        ub = ug_t.T.astype(BF16)
        z = jnp.dot(ub, g_ref[gg], preferred_element_type=F32)
        x_f, x_b = z[:, :w], z[:, w:]
        pw = lambda d, r: pw_ref[d, gg, r:r + 1, :]
        if has_s0:
            rep = lambda a: jnp.broadcast_to(a[:, None, :], (nb, nc, w)).reshape(m, w)
            s0_f, s0_b = rep(s0_ref[0, gg, :, :w]), rep(s0_ref[0, gg, :, w:])
            x_f = x_f + jnp.where(first, _cmul(pw(0, 0), pw(0, 1), s0_f), 0.0)
            x_b = x_b + jnp.where(last, _cmul(pw(1, 0), pw(1, 1), s0_b), 0.0)
        for k in range(nc.bit_length() - 1):
            sh = 1 << k
            prev = pltpu.roll(x_f, sh, axis=0)
            x_f = x_f + jnp.where(cidx >= sh, _cmul(pw(0, 2 * k), pw(0, 2 * k + 1), prev), 0.0)
            nxt = pltpu.roll(x_b, m - sh, axis=0)
            x_b = x_b + jnp.where(cidx < nc - sh, _cmul(pw(1, 2 * k), pw(1, 2 * k + 1), nxt), 0.0)
        in_f = pltpu.roll(x_f, 1, axis=0)
        in_b = pltpu.roll(x_b, m - 1, axis=0)
        if has_s0:
            in_f = jnp.where(first, s0_f, in_f)
            in_b = jnp.where(last, s0_b, in_b)
        else:
            in_f = jnp.where(first, 0.0, in_f)
            in_b = jnp.where(last, 0.0, in_b)
        s_in = jnp.concatenate([in_f, in_b], axis=1).astype(BF16)
        y = (jnp.dot(ub, t_ref[gg], preferred_element_type=F32)
             + jnp.dot(s_in, e_ref[gg], preferred_element_type=F32))
        yt_s[gg] = y.T
        if want_final:
            xs_f[...] = x_f
            xs_b[...] = x_b
            fin_ref[0, gg, :, :w] = xs_f[pl.ds(nc - 1, nb, stride=nc), :]
            fin_ref[0, gg, :, w:] = xs_b[pl.ds(0, nb, stride=nc), :]
        return carry

    lax.fori_loop(0, S5_LG, group_body, 0)

    for t in range(S5_Q):
        y_t = jnp.concatenate([yt_s[k, t * ch:(t + 1) * ch, :] for k in range(S5_LG)], axis=0)
        y = y_t.T + tokens(t) * dsk_ref[0]
        o_ref[:, pl.ds(t, nc, stride=S5_Q), :] = _gelu_tanh(y).reshape(nb, nc, LANES)


def _s5(u, tables, d_skip, s0, want_final):
    t_all, g_fb, e_fb, pows = tables
    b, l, _ = u.shape
    nc = l // S5_Q
    bb = min(b, max(1, S5_ROWS // nc))
    nbb = b // bb
    m = bb * nc
    has_s0 = s0 is not None
    seq_spec = pl.BlockSpec((bb, l, LANES), lambda i, j: (j, 0, i))
    tab_spec = pl.BlockSpec((S5_LG, S5_CW, S5_CW), lambda i, j: (i, 0, 0),
                            pipeline_mode=pl.Buffered(1))
    st_spec = pl.BlockSpec((1, S5_LG, bb, S5_CW), lambda i, j: (j, i, 0, 0))
    st_blocks = lambda a: jnp.transpose(a.reshape(S5_GROUPS, nbb, bb, S5_CW), (1, 0, 2, 3))
    in_specs = [seq_spec, tab_spec, tab_spec, tab_spec,
                pl.BlockSpec((2, S5_LG, 16, 2 * S5_STATE), lambda i, j: (0, i, 0, 0)),
                pl.BlockSpec((1, 1, LANES), lambda i, j: (i, 0, 0))]
    args = [u, t_all, g_fb, e_fb, pows, d_skip.astype(F32).reshape(S5_WIDTH // LANES, 1, LANES)]
    if has_s0:
        in_specs.append(st_spec)
        args.append(st_blocks(s0))
    out_specs = [seq_spec]
    out_shape = [jax.ShapeDtypeStruct((b, l, S5_WIDTH), F32)]
    if want_final:
        out_specs.append(st_spec)
        out_shape.append(jax.ShapeDtypeStruct((nbb, S5_GROUPS, bb, S5_CW), F32))
    res = pl.pallas_call(
        functools.partial(_s5_body, nb=bb, nc=nc, has_s0=has_s0, want_final=want_final),
        grid=(S5_WIDTH // LANES, nbb),
        in_specs=in_specs,
        out_specs=out_specs,
        out_shape=out_shape,
        scratch_shapes=[pltpu.VMEM((S5_LG, S5_CW, m), F32),
                        pltpu.VMEM((m, 2 * S5_STATE), F32), pltpu.VMEM((m, 2 * S5_STATE), F32),
                        pltpu.VMEM((S5_Q, LANES, m), F32)],
        compiler_params=_cparams("arbitrary", "arbitrary"),
        name="s5",
    )(*args)
    fin = None
    if want_final:
        fin = jnp.transpose(res[1], (1, 0, 2, 3)).reshape(S5_GROUPS, b, S5_CW)
    return res[0], fin


def _dft_cos_sin(n, rows=None, row_step=1):
    rows = n if rows is None else rows
    j = jnp.arange(rows, dtype=jnp.int32)[:, None] * row_step
    jk = (j * jnp.arange(n, dtype=jnp.int32)[None, :]) % n
    ang = jk.astype(F32) * (2.0 * math.pi / n)
    return jnp.cos(ang), jnp.sin(ang)


def _dft_cos_sin_split(n, minor):
    c1, s1 = _dft_cos_sin(n, n // minor, minor)
    c2, s2 = _dft_cos_sin(n, minor)
    cos = c1[:, None, :] * c2[None, :, :] - s1[:, None, :] * s2[None, :, :]
    sin = s1[:, None, :] * c2[None, :, :] + c1[:, None, :] * s2[None, :, :]
    return cos.reshape(n, n), sin.reshape(n, n)


def _fnet_body(u_ref, cs_ref, dl_ref, o_ref, xs, *, seq):
    @pl.when(pl.program_id(1) == 0)
    def _():
        xcs = jnp.dot(u_ref[0].astype(BF16), cs_ref[...], preferred_element_type=F32)
        xs[0:seq, :] = xcs[:, :FNET_WIDTH].astype(BF16)
        xs[seq:2 * seq, :] = xcs[:, FNET_WIDTH:].astype(BF16)

    scale = 1.0 / math.sqrt(seq * FNET_GROUP_CH)
    o_ref[0] = (jnp.dot(dl_ref[...], xs[...], preferred_element_type=F32) * scale).astype(o_ref.dtype)


def _fnet(u):
    b, l, _ = u.shape
    tr = min(l, 512)
    cc, sc = _dft_cos_sin(FNET_GROUP_CH)
    eye = jnp.eye(FNET_GROUPS, dtype=F32)
    cs = jnp.concatenate([jnp.kron(eye, cc), jnp.kron(eye, sc)], axis=1).astype(BF16)
    cl, sl = _dft_cos_sin_split(l, DFT_MINOR) if l > 4 * DFT_MINOR else _dft_cos_sin(l)
    dl = jnp.concatenate([cl, -sl], axis=1).astype(BF16)
    return pl.pallas_call(
        functools.partial(_fnet_body, seq=l),
        grid=(b, l // tr),
        in_specs=[pl.BlockSpec((1, l, FNET_WIDTH), lambda i, t: (i, 0, 0)),
                  _const_spec((FNET_WIDTH, 2 * FNET_WIDTH)),
                  pl.BlockSpec((tr, 2 * l), lambda i, t: (t, 0))],
        out_specs=pl.BlockSpec((1, tr, FNET_WIDTH), lambda i, t: (i, t, 0)),
        out_shape=jax.ShapeDtypeStruct((b, l, FNET_WIDTH), BF16),
        scratch_shapes=[pltpu.VMEM((2 * l, FNET_WIDTH), BF16)],
        compiler_params=_cparams("arbitrary", "arbitrary"),
        name="fnet",
    )(u, cs, dl)


def _even_w_in(w):
    pad = jnp.zeros((D_MODEL, LANES - SSD_HEADS), w.dtype)
    return jnp.concatenate([w[:, :O_DT], w[:, O_Q:], w[:, O_DT:O_Q], pad], axis=1).astype(BF16)


EVEN_SPLITS = (SSD_WIDTH, SSD_XBC, NA_WIDTH, NA_WIDTH, NA_WIDTH, LANES)


def kernel(x_prompt, x_sample, cache_na_k, cache_na_v, state_ssd, state_s5, c, c_ctx, mod_w, mod_b, norm1_g, norm2_g, ffn_w1, ffn_w3, ffn_w2, final_g, ev_w_in, ev_conv_w, ev_conv_b, ev_a_log, ev_dt_bias, ev_d_skip, ev_ssd_norm_g, ev_rpb, ev_w_out, od_w_in, od_lam_re, od_lam_im, od_log_step, od_b_re, od_b_im, od_c_re, od_c_im, od_d_skip, od_glu_w, od_glu_b, od_w_out):
    bc, lc, _ = x_prompt.shape
    bl, ll, _ = x_sample.shape
    cond = jnp.zeros((MOD_ROWS, D_MODEL), F32).at[0].set(c_ctx).at[1:1 + bl].set(c)
    mods = _adaln(cond, mod_w, mod_b)
    xc = x_prompt.reshape(1, bc * lc, D_MODEL)
    xl = x_sample
    tm = 512
    new_k, new_v, new_ssd, new_s5 = [], [], [], []
    for i in range(DEPTH):
        j = i // 2
        mod = mods[i]
        last = i == DEPTH - 1
        if i % 2 == 0:
            w_in = _even_w_in(ev_w_in[j])
            w_out = ev_w_out[j].astype(BF16)
            ssd_args = (ev_conv_w[j], ev_conv_b[j], ev_a_log[j], ev_dt_bias[j], ev_d_skip[j],
                        ev_ssd_norm_g[j])
            z, xbc, q, k, v, dtp, k_hd, v_hd = _inproj(
                xc, mod, 0, norm1_g[i], w_in, EVEN_SPLITS, tm, (F32, F32, BF16, BF16, BF16, F32), (3, 4))
            seqs = lambda a: a.reshape(bc, lc, a.shape[-1])
            y_ssd, fin = _ssd(seqs(z), seqs(xbc), seqs(dtp), *ssd_args, None, True)
            o_na = _ctx_attn(seqs(q), seqs(k), seqs(v))
            flat = lambda a: a.reshape(1, bc * lc, a.shape[-1])
            mix_c = (flat(y_ssd), flat(o_na))
            new_k.append(k_hd.reshape(bc, lc, NA_HEADS, NA_HEAD_DIM))
            new_v.append(v_hd.reshape(bc, lc, NA_HEADS, NA_HEAD_DIM))
            new_ssd.append(_ssd_state_from_kernel(fin))

            z, xbc, q, k, v, dtp = _inproj(xl, mod, 1, norm1_g[i], w_in, EVEN_SPLITS, tm,
                                           (F32, F32, BF16, BF16, BF16, F32))
            s0t = _ssd_state_to_kernel(state_ssd[:, j].astype(F32))
            (y_ssd,) = _ssd(z, xbc, dtp, *ssd_args, s0t, False)
            pl_ = cache_na_k.shape[2]
            o_na = _latent_na(q, k, v, cache_na_k[:, j].reshape(bl, pl_, NA_WIDTH),
                              cache_na_v[:, j].reshape(bl, pl_, NA_WIDTH), ev_rpb[j])
            mix_l = (y_ssd, o_na)
            glu_w = glu_b = None
        else:
            w_in = od_w_in[j].astype(BF16)
            w_out = od_w_out[j].astype(BF16)
            glu_w, glu_b = od_glu_w[j].astype(BF16), od_glu_b[j]
            tables = _s5_tables(od_lam_re[j], od_lam_im[j], od_log_step[j], od_b_re[j], od_b_im[j],
                                od_c_re[j], od_c_im[j])
            splits = (S5_WIDTH, FNET_WIDTH)
            u_s, u_f = _inproj(xc, mod, 0, norm1_g[i], w_in, splits, tm, (F32, BF16))
            seqs = lambda a: a.reshape(bc, lc, a.shape[-1])
            g_s, fin = _s5(seqs(u_s), tables, od_d_skip[j], None, True)
            y_f = _fnet(seqs(u_f))
            flat = lambda a: a.reshape(1, bc * lc, a.shape[-1])
            mix_c = (flat(g_s), flat(y_f))
            fin = fin.reshape(S5_GROUPS, bc, 2, 2, S5_STATE)
            new_s5.append(jnp.transpose(fin, (1, 2, 0, 4, 3)))

            u_s, u_f = _inproj(xl, mod, 1, norm1_g[i], w_in, splits, tm, (F32, BF16))
            s0 = jnp.transpose(state_s5[:, j].astype(F32), (2, 0, 1, 4, 3)).reshape(
                S5_GROUPS, bl, 4 * S5_STATE)
            g_s, _ = _s5(u_s, tables, od_d_skip[j], s0, False)
            y_f = _fnet(u_f)
            mix_l = (g_s, y_f)
        w_o = (w_out[:SSD_WIDTH], w_out[SSD_WIDTH:])
        w1, w3, w2 = ffn_w1[i].astype(BF16), ffn_w3[i].astype(BF16), ffn_w2[i].astype(BF16)
        fg = final_g if last else None
        xc = _ffn(xc, mod, 0, *mix_c, *w_o, glu_w, glu_b, norm2_g[i], w1, w3, w2, fg, tm)
        xl = _ffn(xl, mod, 1, *mix_l, *w_o, glu_w, glu_b, norm2_g[i], w1, w3, w2, fg, tm)
    return (xc.reshape(bc, lc, D_MODEL), xl,
            jnp.stack(new_k, axis=1), jnp.stack(new_v, axis=1),
            jnp.stack(new_ssd, axis=1), jnp.stack(new_s5, axis=1))
```

```python
import functools
import math

import numpy as np
import jax
import jax.numpy as jnp
from jax import lax
from jax.experimental import pallas as pl
from jax.experimental.pallas import tpu as pltpu

F32 = jnp.float32
BF16 = jnp.bfloat16
HIGHEST = lax.Precision.HIGHEST

D_MODEL = 1024
DEPTH = 2
EPS = 1e-6
GRID_W = 64
FFN_HIDDEN = 2816
MXU_COLS = 256
FFN_CHUNKS = (6 * MXU_COLS, 5 * MXU_COLS)
assert sum(FFN_CHUNKS) == FFN_HIDDEN

SSD_HEADS = 8
SSD_HEAD_DIM = 64
SSD_WIDTH = 512
SSD_GROUPS = 2
SSD_HPG = SSD_HEADS // SSD_GROUPS
SSD_STATE = 64
SSD_GN = SSD_GROUPS * SSD_STATE
SSD_XBC = SSD_WIDTH + 2 * SSD_GN
SSD_CHUNK = 128
CONV_W = 5
CONV_HALO = 8

NA_HEADS = 8
NA_HEAD_DIM = 64
NA_WIDTH = 512
NA_ROWS = 8
NA_COLS = 16
NA_SCALE = NA_HEAD_DIM ** -0.5
NA_QROWS = 4
NA_WIN = NA_ROWS + NA_QROWS
NEG_BIG = -1e30

O_XBC = SSD_WIDTH
O_DT = O_XBC + SSD_XBC
O_Q = O_DT + SSD_HEADS
O_K = O_Q + NA_WIDTH
O_V = O_K + NA_WIDTH
LANES = 128

S5_GROUP_CH = 16
S5_GROUPS = 32
S5_WIDTH = 512
S5_STATE = 64
S5_Q = 16
S5_CW = S5_Q * S5_GROUP_CH
S5_LOG_MAX = 7
S5_LG = LANES // S5_GROUP_CH
S5_ROWS = 512

FNET_GROUP_CH = 64
FNET_GROUPS = 8
FNET_WIDTH = 512
DFT_MINOR = 64

MOD_ROWS = 16
VMEM_LIMIT = 56 * 1024 * 1024


def _cparams(*sem):
    return pltpu.CompilerParams(dimension_semantics=sem, vmem_limit_bytes=VMEM_LIMIT)


def _const_spec(shape):
    nd = len(shape)
    return pl.BlockSpec(shape, lambda *_: (0,) * nd, pipeline_mode=pl.Buffered(1))


def _sigmoid(x):
    return 1.0 / (1.0 + jnp.exp(-x))


def _silu(x):
    return x * _sigmoid(x)


def _softplus(x):
    return jnp.maximum(x, 0.0) + jnp.log(1.0 + jnp.exp(-jnp.abs(x)))


def _gelu_tanh(x):
    return 0.5 * x * (1.0 + jnp.tanh(math.sqrt(2.0 / math.pi) * (x + 0.044715 * (x * x * x))))


def _rmsnorm(x, g):
    return x * lax.rsqrt(jnp.mean(x * x, axis=-1, keepdims=True) + EPS) * g


def _bdot(a, b):
    return jnp.dot(a.astype(BF16), b.astype(BF16), preferred_element_type=F32)


def _bdot_nt(a, b):
    return lax.dot_general(a.astype(BF16), b.astype(BF16), (((1,), (1,)), ((), ())),
                           preferred_element_type=F32)


def _bdot_tn(a, b):
    return lax.dot_general(a.astype(BF16), b.astype(BF16), (((0,), (0,)), ((), ())),
                           preferred_element_type=F32)


def _adaln_body(c_ref, w_ref, b_ref, o_ref):
    s = _silu(c_ref[...])
    o_ref[0] = jnp.dot(s, w_ref[0], precision=HIGHEST, preferred_element_type=F32) + b_ref[0]


def _adaln(cond, mod_w, mod_b):
    n = 6 * D_MODEL
    tn = 3072
    out = pl.pallas_call(
        _adaln_body,
        grid=(DEPTH, n // tn),
        in_specs=[pl.BlockSpec((MOD_ROWS, D_MODEL), lambda i, j: (0, 0)),
                  pl.BlockSpec((1, D_MODEL, tn), lambda i, j: (i, 0, j)),
                  pl.BlockSpec((1, 1, tn), lambda i, j: (i, 0, j))],
        out_specs=pl.BlockSpec((1, MOD_ROWS, tn), lambda i, j: (i, 0, j)),
        out_shape=jax.ShapeDtypeStruct((DEPTH, MOD_ROWS, n), F32),
        compiler_params=_cparams("arbitrary", "arbitrary"),
        name="adaln",
    )(cond, mod_w, mod_b.reshape(DEPTH, 1, n))
    return out.reshape(DEPTH, MOD_ROWS, 6, D_MODEL)


def _mod_spec(mrow0):
    return pl.BlockSpec((1, 6, D_MODEL), lambda b, t: (b + mrow0, 0, 0))


def _inproj_body(x_ref, mod_ref, g_ref, w_ref, *o_refs, splits, heads_of):
    h = _rmsnorm(x_ref[0], g_ref[...]) * (1.0 + mod_ref[0, 1:2, :]) + mod_ref[0, 0:1, :]
    acc = jnp.dot(h.astype(BF16), w_ref[...], preferred_element_type=F32)
    off = 0
    offs = []
    for o_ref, n in zip(o_refs, splits):
        o_ref[0] = acc[:, off:off + n].astype(o_ref.dtype)
        offs.append(off)
        off += n
    for o_ref, i in zip(o_refs[len(splits):], heads_of):
        nh, hd = o_ref.shape[2], o_ref.shape[3]
        for hh in range(nh):
            o_ref[0, :, hh, :] = acc[:, offs[i] + hh * hd:offs[i] + (hh + 1) * hd]


def _inproj(x, mod, mrow0, g, w, splits, tm, dtypes=None, heads_of=()):
    b, l, _ = x.shape
    n = w.shape[1]
    dtypes = (F32,) * len(splits) if dtypes is None else dtypes
    head_shape = lambda i: (splits[i] // NA_HEAD_DIM, NA_HEAD_DIM)
    return pl.pallas_call(
        functools.partial(_inproj_body, splits=splits, heads_of=heads_of),
        grid=(b, l // tm),
        in_specs=[pl.BlockSpec((1, tm, D_MODEL), lambda i, t: (i, t, 0)),
                  _mod_spec(mrow0),
                  _const_spec((1, D_MODEL)),
                  _const_spec((D_MODEL, n))],
        out_specs=([pl.BlockSpec((1, tm, s), lambda i, t: (i, t, 0)) for s in splits]
                   + [pl.BlockSpec((1, tm) + head_shape(i), lambda i_, t: (i_, t, 0, 0))
                      for i in heads_of]),
        out_shape=([jax.ShapeDtypeStruct((b, l, s), dt) for s, dt in zip(splits, dtypes)]
                   + [jax.ShapeDtypeStruct((b, l) + head_shape(i), F32) for i in heads_of]),
        compiler_params=_cparams("arbitrary", "arbitrary"),
        name="inproj",
    )(x, mod, g.reshape(1, D_MODEL), w)


def _ffn_body(x_ref, mod_ref, ya_ref, yb_ref, wa_ref, wb_ref, g_ref, w1_ref, w3_ref, w2_ref, *rest,
              glu, final):
    rest = list(rest)
    ya = ya_ref[0]
    if glu:
        gw_ref, gb_ref = rest.pop(0), rest.pop(0)
        ya = ya * _sigmoid(_bdot(ya, gw_ref[...]) + gb_ref[...])
    mix = _bdot(ya, wa_ref[...]) + _bdot(yb_ref[0], wb_ref[...])
    x = x_ref[0] + mod_ref[0, 2:3, :] * mix
    h = (_rmsnorm(x, g_ref[...]) * (1.0 + mod_ref[0, 4:5, :]) + mod_ref[0, 3:4, :]).astype(BF16)
    acc = jnp.zeros(x.shape, F32)
    for c in range(len(FFN_CHUNKS)):
        cols = slice(sum(FFN_CHUNKS[:c]), sum(FFN_CHUNKS[:c + 1]))
        a = jnp.dot(h, w1_ref[:, cols], preferred_element_type=F32)
        u = _silu(a) * jnp.dot(h, w3_ref[:, cols], preferred_element_type=F32)
        acc = acc + jnp.dot(u.astype(BF16), w2_ref[cols, :], preferred_element_type=F32)
    y = x + mod_ref[0, 5:6, :] * acc
    if final:
        fg_ref, o_ref = rest
        y = _rmsnorm(y, fg_ref[...])
    else:
        (o_ref,) = rest
    o_ref[0] = y


def _ffn(x, mod, mrow0, ya, yb, wa, wb, glu_w, glu_b, g, w1, w3, w2, final_g, tm):
    b, l, _ = x.shape
    wa_n, wb_n = ya.shape[-1], yb.shape[-1]
    glu = glu_w is not None
    final = final_g is not None
    row_spec = lambda n: pl.BlockSpec((1, tm, n), lambda i, t: (i, t, 0))
    in_specs = [row_spec(D_MODEL), _mod_spec(mrow0), row_spec(wa_n), row_spec(wb_n),
                _const_spec((wa_n, D_MODEL)), _const_spec((wb_n, D_MODEL)),
                _const_spec((1, D_MODEL)),
                _const_spec((D_MODEL, FFN_HIDDEN)),
                _const_spec((D_MODEL, FFN_HIDDEN)),
                _const_spec((FFN_HIDDEN, D_MODEL))]
    args = [x, mod, ya, yb, wa, wb, g.reshape(1, D_MODEL), w1, w3, w2]
    if glu:
        in_specs += [_const_spec((wa_n, wa_n)), _const_spec((1, wa_n))]
        args += [glu_w, glu_b.reshape(1, wa_n)]
    if final:
        in_specs.append(_const_spec((1, D_MODEL)))
        args.append(final_g.reshape(1, D_MODEL))
    return pl.pallas_call(
        functools.partial(_ffn_body, glu=glu, final=final),
        grid=(b, l // tm),
        in_specs=in_specs,
        out_specs=pl.BlockSpec((1, tm, D_MODEL), lambda i, t: (i, t, 0)),
        out_shape=jax.ShapeDtypeStruct((b, l, D_MODEL), F32),
        compiler_params=_cparams("arbitrary", "arbitrary"),
        name="ffn",
    )(*args)


def _ssd_body(z_ref, xbc_ref, dt_ref, cw_ref, cb_ref, hp_ref, dsk_ref, ng_ref, *rest,
              seq, has_s0, want_final):
    rest = list(rest)
    s0_ref = rest.pop(0) if has_s0 else None
    y_ref = rest.pop(0)
    fin_ref = rest.pop(0) if want_final else None
    st_s, cd_s, scur, ych, xc_s, cum_s, tr_s = rest
    q = SSD_CHUNK
    nc = seq // q
    gw = SSD_HPG * SSD_HEAD_DIM
    expand = (lax.broadcasted_iota(jnp.int32, (LANES, SSD_WIDTH), 1) // SSD_HEAD_DIM
              == lax.broadcasted_iota(jnp.int32, (LANES, SSD_WIDTH), 0)).astype(BF16)

    def per_head(v):
        hi = v.astype(BF16)
        lo = (v - hi.astype(F32)).astype(BF16)
        return (jnp.dot(hi, expand, preferred_element_type=F32)
                + jnp.dot(lo, expand, preferred_element_type=F32))
    row = lax.broadcasted_iota(jnp.int32, (q, q), 0)
    col = lax.broadcasted_iota(jnp.int32, (q, q), 1)
    lower = row >= col
    upper = col >= row
    tri_l = lower.astype(F32)
    tri_u = upper.astype(F32)
    a_f = -jnp.exp(hp_ref[0:1, :])
    a_b = -jnp.exp(hp_ref[1:2, :])
    lane = lax.broadcasted_iota(jnp.int32, (1, LANES), 1)
    a_f = jnp.where(lane < SSD_HEADS, a_f, 0.0)
    a_b = jnp.where(lane < SSD_HEADS, a_b, 0.0)
    bias_f = hp_ref[2:3, :]
    bias_b = hp_ref[3:4, :]

    def chunk_pre(c):
        r0 = pl.multiple_of(c * q, q)
        lo = pl.multiple_of(jnp.maximum(r0 - CONV_HALO, 0), CONV_HALO)
        hi = pl.multiple_of(jnp.minimum(r0 + q, seq - CONV_HALO), CONV_HALO)
        prev = jnp.where(c > 0, xbc_ref[0, pl.ds(lo, CONV_HALO), :], 0.0)
        nxt = jnp.where(c < nc - 1, xbc_ref[0, pl.ds(hi, CONV_HALO), :], 0.0)
        win = jnp.concatenate([prev, xbc_ref[0, pl.ds(r0, q), :], nxt], axis=0)
        acc = cb_ref[...] + cw_ref[0:1, :] * win[CONV_HALO - 2:CONV_HALO - 2 + q]
        for k in range(1, CONV_W):
            o = CONV_HALO - CONV_W // 2 + k
            acc = acc + cw_ref[k:k + 1, :] * win[o:o + q]
        xc = _silu(acc)
        dtr = dt_ref[0, pl.ds(r0, q), :]
        dt_f = _softplus(dtr + bias_f)
        dt_b = _softplus(dtr + bias_b)
        cum_f = jnp.dot(tri_l, dt_f * a_f, precision=HIGHEST, preferred_element_type=F32)
        cum_b = jnp.dot(tri_u, dt_b * a_b, precision=HIGHEST, preferred_element_type=F32)
        return r0, xc, dt_f, dt_b, cum_f, cum_b

    def pass_a(c, carry):
        r0, xc, dt_f, dt_b, cum_f, cum_b = chunk_pre(c)
        xc_s[pl.ds(r0, q), :] = xc
        cum_s[0, pl.ds(r0, q), :] = cum_f
        cum_s[1, pl.ds(r0, q), :] = cum_b
        for i, v in enumerate((cum_f, cum_b, dt_f, dt_b)):
            tr_s[c, i] = v.T[0:SSD_HEADS, :]
        end_f = cum_f[q - 1:q, :]
        end_b = cum_b[0:1, :]
        xw_f = xc[:, :SSD_WIDTH] * per_head(jnp.exp(end_f - cum_f) * dt_f)
        xw_b = xc[:, :SSD_WIDTH] * per_head(jnp.exp(end_b - cum_b) * dt_b)
        cd_s[0, c] = per_head(jnp.broadcast_to(jnp.exp(end_f), (8, LANES)))
        cd_s[1, c] = per_head(jnp.broadcast_to(jnp.exp(end_b), (8, LANES)))
        for g in range(SSD_GROUPS):
            bg = xc[:, SSD_WIDTH + g * SSD_STATE:SSD_WIDTH + (g + 1) * SSD_STATE]
            st_s[0, c, g] = _bdot_tn(bg, xw_f[:, g * gw:(g + 1) * gw])
            st_s[1, c, g] = _bdot_tn(bg, xw_b[:, g * gw:(g + 1) * gw])
        return carry

    lax.fori_loop(0, nc, pass_a, 0)

    if has_s0:
        scur[...] = s0_ref[0]
    else:
        scur[...] = jnp.zeros(scur.shape, F32)

    def pass_b(c, carry):
        cr = nc - 1 - c
        for g in range(SSD_GROUPS):
            s_in = scur[0, g]
            scur[0, g] = s_in * cd_s[0, c, 0:1, g * gw:(g + 1) * gw] + st_s[0, c, g]
            st_s[0, c, g] = s_in
            s_in = scur[1, g]
            scur[1, g] = s_in * cd_s[1, cr, 0:1, g * gw:(g + 1) * gw] + st_s[1, cr, g]
            st_s[1, cr, g] = s_in
        return carry

    lax.fori_loop(0, nc, pass_b, 0)
    if want_final:
        fin_ref[0] = scur[...]

    def pass_c(c, carry):
        r0 = pl.multiple_of(c * q, q)
        xc = xc_s[pl.ds(r0, q), :]
        cum_f = cum_s[0, pl.ds(r0, q), :]
        cum_b = cum_s[1, pl.ds(r0, q), :]
        cum_ft, cum_bt, dt_ft, dt_bt = tr_s[c, 0], tr_s[c, 1], tr_s[c, 2], tr_s[c, 3]
        ex_f = per_head(jnp.exp(cum_f))
        ex_b = per_head(jnp.exp(cum_b))
        y_off = []
        cbs = []
        for g in range(SSD_GROUPS):
            gs = slice(g * gw, (g + 1) * gw)
            bg = xc[:, SSD_WIDTH + g * SSD_STATE:SSD_WIDTH + (g + 1) * SSD_STATE]
            cg = xc[:, SSD_WIDTH + SSD_GN + g * SSD_STATE:SSD_WIDTH + SSD_GN + (g + 1) * SSD_STATE]
            cbs.append(_bdot_nt(cg, bg))
            y_off.append(ex_f[:, gs] * _bdot(cg, st_s[0, c, g]) + ex_b[:, gs] * _bdot(cg, st_s[1, c, g]))
        for h in range(SSD_HEADS):
            xh = xc[:, h * SSD_HEAD_DIM:(h + 1) * SSD_HEAD_DIM]
            seg_f = cum_f[:, h:h + 1] - cum_ft[h:h + 1, :]
            seg_b = cum_b[:, h:h + 1] - cum_bt[h:h + 1, :]
            m_f = jnp.exp(jnp.where(lower, seg_f, NEG_BIG)) * dt_ft[h:h + 1, :]
            m_b = jnp.exp(jnp.where(upper, seg_b, NEG_BIG)) * dt_bt[h:h + 1, :]
            ych[:, h * SSD_HEAD_DIM:(h + 1) * SSD_HEAD_DIM] = _bdot(cbs[h // SSD_HPG] * (m_f + m_b), xh)
        yf = ych[...] + jnp.concatenate(y_off, axis=1) + dsk_ref[...] * xc[:, :SSD_WIDTH]
        yf = yf * _silu(z_ref[0, pl.ds(r0, q), :])
        y_ref[0, pl.ds(r0, q), :] = _rmsnorm(yf, ng_ref[...]).astype(y_ref.dtype)
        return carry

    lax.fori_loop(0, nc, pass_c, 0)


def _ssd_state_to_kernel(s):
    b = s.shape[0]
    s = s.reshape(b, 2, SSD_GROUPS, SSD_HPG, SSD_HEAD_DIM, SSD_STATE)
    return jnp.transpose(s, (0, 1, 2, 5, 3, 4)).reshape(b, 2, SSD_GROUPS, SSD_STATE, SSD_HPG * SSD_HEAD_DIM)


def _ssd_state_from_kernel(s):
    b = s.shape[0]
    s = s.reshape(b, 2, SSD_GROUPS, SSD_STATE, SSD_HPG, SSD_HEAD_DIM)
    return jnp.transpose(s, (0, 1, 2, 4, 5, 3)).reshape(b, 2, SSD_HEADS, SSD_HEAD_DIM, SSD_STATE)


def _ssd(z, xbc, dtp, conv_w, conv_b, a_log, dt_bias, d_skip, norm_g, s0t, want_final):
    b, l, _ = z.shape
    nc = l // SSD_CHUNK
    gw = SSD_HPG * SSD_HEAD_DIM
    has_s0 = s0t is not None
    cw = jnp.zeros((8, SSD_XBC), F32).at[:CONV_W].set(conv_w)
    hp = jnp.zeros((8, LANES), F32)
    hp = hp.at[0:2, :SSD_HEADS].set(a_log).at[2:4, :SSD_HEADS].set(dt_bias)
    dsk = jnp.repeat(d_skip, SSD_HEAD_DIM).reshape(1, SSD_WIDTH)
    seq_spec = lambda n: pl.BlockSpec((1, l, n), lambda i: (i, 0, 0))
    st_spec = pl.BlockSpec((1, 2, SSD_GROUPS, SSD_STATE, gw), lambda i: (i, 0, 0, 0, 0))
    in_specs = [seq_spec(SSD_WIDTH), seq_spec(SSD_XBC), seq_spec(LANES),
                _const_spec((8, SSD_XBC)), _const_spec((1, SSD_XBC)), _const_spec((8, LANES)),
                _const_spec((1, SSD_WIDTH)), _const_spec((1, SSD_WIDTH))]
    args = [z, xbc, dtp, cw, conv_b.reshape(1, SSD_XBC), hp, dsk, norm_g.reshape(1, SSD_WIDTH)]
    if has_s0:
        in_specs.append(st_spec)
        args.append(s0t)
    out_specs = [seq_spec(SSD_WIDTH)]
    out_shape = [jax.ShapeDtypeStruct((b, l, SSD_WIDTH), BF16)]
    if want_final:
        out_specs.append(st_spec)
        out_shape.append(jax.ShapeDtypeStruct((b, 2, SSD_GROUPS, SSD_STATE, gw), F32))
    return pl.pallas_call(
        functools.partial(_ssd_body, seq=l, has_s0=has_s0, want_final=want_final),
        grid=(b,),
        in_specs=in_specs,
        out_specs=out_specs,
        out_shape=out_shape,
        scratch_shapes=[pltpu.VMEM((2, nc, SSD_GROUPS, SSD_STATE, gw), F32),
                        pltpu.VMEM((2, nc, 8, SSD_WIDTH), F32),
                        pltpu.VMEM((2, SSD_GROUPS, SSD_STATE, gw), F32),
                        pltpu.VMEM((SSD_CHUNK, SSD_WIDTH), F32),
                        pltpu.VMEM((l, SSD_XBC), F32),
                        pltpu.VMEM((2, l, LANES), F32),
                        pltpu.VMEM((nc, 4, SSD_HEADS, LANES), F32)],
        compiler_params=_cparams("arbitrary"),
        name="ssd",
    )(*args)


def _softmax_pv(parts):
    m = parts[0][0].max(axis=-1, keepdims=True)
    for s, _ in parts[1:]:
        m = jnp.maximum(m, s.max(axis=-1, keepdims=True))
    den = 0.0
    out = 0.0
    for s, v in parts:
        p = jnp.exp(s - m)
        den = den + p.sum(axis=-1, keepdims=True)
        out = out + _bdot(p, v)
    return out / den


def _ctx_attn_body(q_ref, k_ref, v_ref, o_ref):
    for h in range(NA_HEADS):
        hs = slice(h * NA_HEAD_DIM, (h + 1) * NA_HEAD_DIM)
        s = _bdot_nt(q_ref[0, :, hs], k_ref[0, :, hs]) * NA_SCALE
        o_ref[0, :, hs] = _softmax_pv([(s, v_ref[0, :, hs])]).astype(o_ref.dtype)


def _ctx_attn(q, k, v):
    b, l, _ = q.shape
    spec = pl.BlockSpec((1, l, NA_WIDTH), lambda i: (i, 0, 0))
    return pl.pallas_call(
        _ctx_attn_body,
        grid=(b,),
        in_specs=[spec, spec, spec],
        out_specs=spec,
        out_shape=jax.ShapeDtypeStruct((b, l, NA_WIDTH), BF16),
        compiler_params=_cparams("arbitrary"),
        name="ctx_attn",
    )(q, k, v)


def _na_bias_tables(rpb, rows):
    qc = np.arange(GRID_W)[:, None]
    kc = np.arange(GRID_W)[None, :]
    cs = np.clip(qc - NA_COLS // 2, 0, GRID_W - NA_COLS)
    col_ok = (kc >= cs) & (kc < cs + NA_COLS)
    dc = np.clip(kc - qc + NA_COLS - 1, 0, 2 * NA_COLS - 2)
    col_sel = (dc[None] == np.arange(2 * NA_COLS - 1)[:, None, None]).astype(np.float32)
    by_col = jnp.einsum('hab,bqk->haqk', rpb.astype(F32), jnp.asarray(col_sel), precision=HIGHEST)
    by_col = jnp.where(jnp.asarray(col_ok), by_col, NEG_BIG)
    n_dr = 2 * NA_ROWS - 1
    return pl.pallas_call(
        functools.partial(_na_bias_body, rows=rows),
        grid=(3, NA_HEADS),
        in_specs=[pl.BlockSpec((1, n_dr, GRID_W, GRID_W), lambda v, h: (h, 0, 0, 0))],
        out_specs=pl.BlockSpec((1, 1, NA_QROWS * GRID_W, NA_WIN * GRID_W), lambda v, h: (v, h, 0, 0)),
        out_shape=jax.ShapeDtypeStruct((3, NA_HEADS, NA_QROWS * GRID_W, NA_WIN * GRID_W), F32),
        compiler_params=_cparams("arbitrary", "arbitrary"),
        name="na_bias",
    )(by_col)


def _na_bias_body(u_ref, o_ref, *, rows):
    v = pl.program_id(0)
    kr = min(NA_ROWS, rows)
    r0 = jnp.where(v == 0, 0, jnp.where(v == 1, NA_QROWS, rows - NA_QROWS))
    ws = jnp.clip(r0 - kr // 2, 0, rows - NA_WIN)
    for qi in range(NA_QROWS):
        r = r0 + qi
        rs = jnp.clip(r - kr // 2, 0, rows - kr)
        for w in range(NA_WIN):
            krow = ws + w
            ok = (krow >= rs) & (krow < rs + kr)
            dr = jnp.clip(krow - r + NA_ROWS - 1, 0, 2 * NA_ROWS - 2)
            o_ref[0, 0, qi * GRID_W:(qi + 1) * GRID_W, w * GRID_W:(w + 1) * GRID_W] = jnp.where(
                ok, u_ref[0, dr], NEG_BIG)


def _na_body(q_ref, k_ref, v_ref, kc_ref, vc_ref, bias_ref, o_ref, *, rows):
    rb = pl.program_id(1)
    ws = jnp.clip(rb * NA_QROWS - NA_ROWS // 2, 0, rows - NA_WIN)
    k0 = pl.multiple_of(ws * GRID_W, NA_QROWS * GRID_W)
    nk = NA_WIN * GRID_W
    for h in range(NA_HEADS):
        hs = slice(h * NA_HEAD_DIM, (h + 1) * NA_HEAD_DIM)
        qh = q_ref[0, :, hs]
        s_loc = _bdot_nt(qh, k_ref[0, pl.ds(k0, nk), hs]) * NA_SCALE + bias_ref[0, h]
        s_ctx = _bdot_nt(qh, kc_ref[0, :, hs]) * NA_SCALE
        o_ref[0, :, hs] = _softmax_pv([(s_loc, v_ref[0, pl.ds(k0, nk), hs]),
                                       (s_ctx, vc_ref[0, :, hs])]).astype(o_ref.dtype)


def _latent_na(q, k, v, k_ctx, v_ctx, rpb):
    b, l, _ = q.shape
    lc = k_ctx.shape[1]
    rows = l // GRID_W
    assert rows >= NA_WIN and rows % NA_QROWS == 0 and NA_QROWS * 2 <= NA_ROWS
    nrb = rows // NA_QROWS
    tq = NA_QROWS * GRID_W
    bias = _na_bias_tables(rpb, rows)
    variant = lambda i, r: (jnp.where(r == 0, 0, jnp.where(r == nrb - 1, 2, 1)), 0, 0, 0)
    return pl.pallas_call(
        functools.partial(_na_body, rows=rows),
        grid=(b, nrb),
        in_specs=[pl.BlockSpec((1, tq, NA_WIDTH), lambda i, r: (i, r, 0)),
                  pl.BlockSpec((1, l, NA_WIDTH), lambda i, r: (i, 0, 0)),
                  pl.BlockSpec((1, l, NA_WIDTH), lambda i, r: (i, 0, 0)),
                  pl.BlockSpec((1, lc, NA_WIDTH), lambda i, r: (i, 0, 0)),
                  pl.BlockSpec((1, lc, NA_WIDTH), lambda i, r: (i, 0, 0)),
                  pl.BlockSpec((1, NA_HEADS, tq, NA_WIN * GRID_W), variant)],
        out_specs=pl.BlockSpec((1, tq, NA_WIDTH), lambda i, r: (i, r, 0)),
        out_shape=jax.ShapeDtypeStruct((b, l, NA_WIDTH), BF16),
        compiler_params=_cparams("arbitrary", "arbitrary"),
        name="latent_na",
    )(q, k, v, k_ctx, v_ctx, bias)


def _s5_prep_body(lre_ref, lim_ref, lst_ref, btre_ref, btim_ref, cre_ref, cim_ref,
                  t_ref, g_ref, e_ref, pw_ref, g_s, et_s, e0_s):
    q, ch, p = S5_Q, S5_GROUP_CH, S5_STATE
    lane = lax.broadcasted_iota(jnp.int32, (ch, S5_CW), 1)
    bt_re, bt_im = btre_ref[0], btim_ref[0]
    c_re, c_im = cre_ref[0], cim_ref[0]
    taps = []
    for d in range(2):
        lam_re, lam_im = lre_ref[d, 0], lim_ref[d, 0]
        step = jnp.exp(lst_ref[d, 0])
        mag = jnp.exp(lam_re * step)
        a_re = mag * jnp.cos(lam_im * step)
        a_im = mag * jnp.sin(lam_im * step)
        den = lam_re * lam_re + lam_im * lam_im
        k_re = ((a_re - 1.0) * lam_re + a_im * lam_im) / den
        k_im = (a_im * lam_re - (a_re - 1.0) * lam_im) / den
        bb_re = k_re * bt_re - k_im * bt_im
        bb_im = k_re * bt_im + k_im * bt_re
        p_re = jnp.ones_like(a_re)
        p_im = jnp.zeros_like(a_re)
        for t in range(q):
            tg = q - 1 - t if d == 0 else t
            te = t if d == 0 else q - 1 - t
            g_s[tg * ch:(tg + 1) * ch, 2 * p * d:2 * p * d + p] = p_re * bb_re - p_im * bb_im
            g_s[tg * ch:(tg + 1) * ch, 2 * p * d + p:2 * p * (d + 1)] = p_re * bb_im + p_im * bb_re
            e0_s[te * ch:(te + 1) * ch, 0:p] = c_re * p_re - c_im * p_im
            e0_s[te * ch:(te + 1) * ch, p:2 * p] = -(c_re * p_im + c_im * p_re)
            p_re, p_im = p_re * a_re - p_im * a_im, p_re * a_im + p_im * a_re
            et_s[d, te * ch:(te + 1) * ch, 0:p] = c_re * p_re - c_im * p_im
            et_s[d, te * ch:(te + 1) * ch, p:2 * p] = -(c_re * p_im + c_im * p_re)
        bb = jnp.concatenate([bb_re, bb_im], axis=1)
        taps.append(lax.dot_general(bb, e0_s[...], (((1,), (1,)), ((), ())), precision=HIGHEST,
                                    preferred_element_type=F32))
        pw_ref[d, 0] = jnp.zeros(pw_ref.shape[2:], F32)
        for k in range(S5_LOG_MAX):
            pw_ref[d, 0, 2 * k:2 * k + 1, :] = jnp.concatenate([p_re, p_re], axis=1)
            pw_ref[d, 0, 2 * k + 1:2 * k + 2, :] = jnp.concatenate([-p_im, p_im], axis=1)
            p_re, p_im = p_re * p_re - p_im * p_im, 2.0 * p_re * p_im
    for ti in range(q):
        fwd = taps[0] if ti == 0 else pltpu.roll(taps[0], ch * ti, axis=1)
        bwd = taps[1] if ti == q - 1 else pltpu.roll(taps[1], S5_CW - ch * (q - 1 - ti), axis=1)
        row = jnp.where(lane >= ch * ti, fwd, 0.0) + jnp.where(lane < ch * (ti + 1), bwd, 0.0)
        t_ref[0, ti * ch:(ti + 1) * ch, :] = row.astype(t_ref.dtype)
    g_ref[0] = g_s[...].astype(g_ref.dtype)
    e_ref[0, 0:2 * p, :] = et_s[0].T.astype(e_ref.dtype)
    e_ref[0, 2 * p:4 * p, :] = et_s[1].T.astype(e_ref.dtype)


def _s5_tables(lam_re, lam_im, log_step, b_re, b_im, c_re, c_im):
    g, p, ch = S5_GROUPS, S5_STATE, S5_GROUP_CH
    row = lambda a: a.astype(F32).reshape(2, g, 1, p)
    lst = jnp.broadcast_to(log_step.astype(F32)[:, :, None, None], (2, g, 1, p))
    bt_re = jnp.swapaxes(b_re.astype(F32), 1, 2)
    bt_im = jnp.swapaxes(b_im.astype(F32), 1, 2)
    dspec = pl.BlockSpec((2, 1, 1, p), lambda i: (0, i, 0, 0))
    gspec = pl.BlockSpec((1, ch, p), lambda i: (i, 0, 0))
    tspec = pl.BlockSpec((1, S5_CW, S5_CW), lambda i: (i, 0, 0))
    tshape = jax.ShapeDtypeStruct((g, S5_CW, S5_CW), BF16)
    return pl.pallas_call(
        _s5_prep_body,
        grid=(g,),
        in_specs=[dspec, dspec, dspec, gspec, gspec, gspec, gspec],
        out_specs=[tspec, tspec, tspec, pl.BlockSpec((2, 1, 16, 2 * p), lambda i: (0, i, 0, 0))],
        out_shape=[tshape, tshape, tshape, jax.ShapeDtypeStruct((2, g, 16, 2 * p), F32)],
        scratch_shapes=[pltpu.VMEM((S5_CW, 4 * p), F32), pltpu.VMEM((2, S5_CW, 2 * p), F32),
                        pltpu.VMEM((S5_CW, 2 * p), F32)],
        compiler_params=_cparams("arbitrary"),
        name="s5_prep",
    )(row(lam_re), row(lam_im), lst, bt_re, bt_im, c_re.astype(F32), c_im.astype(F32))


def _cmul(a_full, a_sgn, s):
    return a_full * s + a_sgn * pltpu.roll(s, S5_STATE, axis=1)


def _s5_body(u_ref, t_ref, g_ref, e_ref, pw_ref, dsk_ref, *rest, nb, nc, has_s0, want_final):
    rest = list(rest)
    s0_ref = rest.pop(0) if has_s0 else None
    o_ref = rest.pop(0)
    fin_ref = rest.pop(0) if want_final else None
    yt_s, xs_f, xs_b, tok_s = rest
    m = nb * nc
    w = 2 * S5_STATE
    ch = S5_GROUP_CH
    cidx = lax.broadcasted_iota(jnp.int32, (m, w), 0) & (nc - 1)
    first = cidx == 0
    last = cidx == nc - 1

    def tokens(t):
        return u_ref[:, pl.ds(t, nc, stride=S5_Q), :].reshape(m, LANES)

    for t in range(S5_Q):
        tok_s[t] = tokens(t).T

    def group_body(gg, carry):
        rows = pl.ds(pl.multiple_of(gg * ch, ch), ch)
        ug_t = jnp.concatenate([tok_s[t, rows, :] for t in range(S5_Q)], axis=0)
        ub = ug_t.T.astype(BF16)
        z = jnp.dot(ub, g_ref[gg], preferred_element_type=F32)
        x_f, x_b = z[:, :w], z[:, w:]
        pw = lambda d, r: pw_ref[d, gg, r:r + 1, :]
        if has_s0:
            rep = lambda a: jnp.broadcast_to(a[:, None, :], (nb, nc, w)).reshape(m, w)
            s0_f, s0_b = rep(s0_ref[0, gg, :, :w]), rep(s0_ref[0, gg, :, w:])
            x_f = x_f + jnp.where(first, _cmul(pw(0, 0), pw(0, 1), s0_f), 0.0)
            x_b = x_b + jnp.where(last, _cmul(pw(1, 0), pw(1, 1), s0_b), 0.0)
        for k in range(nc.bit_length() - 1):
            sh = 1 << k
            prev = pltpu.roll(x_f, sh, axis=0)
            x_f = x_f + jnp.where(cidx >= sh, _cmul(pw(0, 2 * k), pw(0, 2 * k + 1), prev), 0.0)
            nxt = pltpu.roll(x_b, m - sh, axis=0)
            x_b = x_b + jnp.where(cidx < nc - sh, _cmul(pw(1, 2 * k), pw(1, 2 * k + 1), nxt), 0.0)
        in_f = pltpu.roll(x_f, 1, axis=0)
        in_b = pltpu.roll(x_b, m - 1, axis=0)
        if has_s0:
            in_f = jnp.where(first, s0_f, in_f)
            in_b = jnp.where(last, s0_b, in_b)
        else:
            in_f = jnp.where(first, 0.0, in_f)
            in_b = jnp.where(last, 0.0, in_b)
        s_in = jnp.concatenate([in_f, in_b], axis=1).astype(BF16)
        y = (jnp.dot(ub, t_ref[gg], preferred_element_type=F32)
             + jnp.dot(s_in, e_ref[gg], preferred_element_type=F32))
        yt_s[gg] = y.T
        if want_final:
            xs_f[...] = x_f
            xs_b[...] = x_b
            fin_ref[0, gg, :, :w] = xs_f[pl.ds(nc - 1, nb, stride=nc), :]
            fin_ref[0, gg, :, w:] = xs_b[pl.ds(0, nb, stride=nc), :]
        return carry

    lax.fori_loop(0, S5_LG, group_body, 0)

    for t in range(S5_Q):
        y_t = jnp.concatenate([yt_s[k, t * ch:(t + 1) * ch, :] for k in range(S5_LG)], axis=0)
        y = y_t.T + tokens(t) * dsk_ref[0]
        o_ref[:, pl.ds(t, nc, stride=S5_Q), :] = _gelu_tanh(y).reshape(nb, nc, LANES)


def _s5(u, tables, d_skip, s0, want_final):
    t_all, g_fb, e_fb, pows = tables
    b, l, _ = u.shape
    nc = l // S5_Q
    bb = min(b, max(1, S5_ROWS // nc))
    nbb = b // bb
    m = bb * nc
    has_s0 = s0 is not None
    seq_spec = pl.BlockSpec((bb, l, LANES), lambda i, j: (j, 0, i))
    tab_spec = pl.BlockSpec((S5_LG, S5_CW, S5_CW), lambda i, j: (i, 0, 0),
                            pipeline_mode=pl.Buffered(1))
    st_spec = pl.BlockSpec((1, S5_LG, bb, S5_CW), lambda i, j: (j, i, 0, 0))
    st_blocks = lambda a: jnp.transpose(a.reshape(S5_GROUPS, nbb, bb, S5_CW), (1, 0, 2, 3))
    in_specs = [seq_spec, tab_spec, tab_spec, tab_spec,
                pl.BlockSpec((2, S5_LG, 16, 2 * S5_STATE), lambda i, j: (0, i, 0, 0)),
                pl.BlockSpec((1, 1, LANES), lambda i, j: (i, 0, 0))]
    args = [u, t_all, g_fb, e_fb, pows, d_skip.astype(F32).reshape(S5_WIDTH // LANES, 1, LANES)]
    if has_s0:
        in_specs.append(st_spec)
        args.append(st_blocks(s0))
    out_specs = [seq_spec]
    out_shape = [jax.ShapeDtypeStruct((b, l, S5_WIDTH), F32)]
    if want_final:
        out_specs.append(st_spec)
        out_shape.append(jax.ShapeDtypeStruct((nbb, S5_GROUPS, bb, S5_CW), F32))
    res = pl.pallas_call(
        functools.partial(_s5_body, nb=bb, nc=nc, has_s0=has_s0, want_final=want_final),
        grid=(S5_WIDTH // LANES, nbb),
        in_specs=in_specs,
        out_specs=out_specs,
        out_shape=out_shape,
        scratch_shapes=[pltpu.VMEM((S5_LG, S5_CW, m), F32),
                        pltpu.VMEM((m, 2 * S5_STATE), F32), pltpu.VMEM((m, 2 * S5_STATE), F32),
                        pltpu.VMEM((S5_Q, LANES, m), F32)],
        compiler_params=_cparams("arbitrary", "arbitrary"),
        name="s5",
    )(*args)
    fin = None
    if want_final:
        fin = jnp.transpose(res[1], (1, 0, 2, 3)).reshape(S5_GROUPS, b, S5_CW)
    return res[0], fin


def _dft_cos_sin(n, rows=None, row_step=1):
    rows = n if rows is None else rows
    j = jnp.arange(rows, dtype=jnp.int32)[:, None] * row_step
    jk = (j * jnp.arange(n, dtype=jnp.int32)[None, :]) % n
    ang = jk.astype(F32) * (2.0 * math.pi / n)
    return jnp.cos(ang), jnp.sin(ang)


def _dft_cos_sin_split(n, minor):
    c1, s1 = _dft_cos_sin(n, n // minor, minor)
    c2, s2 = _dft_cos_sin(n, minor)
    cos = c1[:, None, :] * c2[None, :, :] - s1[:, None, :] * s2[None, :, :]
    sin = s1[:, None, :] * c2[None, :, :] + c1[:, None, :] * s2[None, :, :]
    return cos.reshape(n, n), sin.reshape(n, n)


def _fnet_body(u_ref, cs_ref, dl_ref, o_ref, xs, *, seq):
    @pl.when(pl.program_id(1) == 0)
    def _():
        xcs = jnp.dot(u_ref[0].astype(BF16), cs_ref[...], preferred_element_type=F32)
        xs[0:seq, :] = xcs[:, :FNET_WIDTH].astype(BF16)
        xs[seq:2 * seq, :] = xcs[:, FNET_WIDTH:].astype(BF16)

    scale = 1.0 / math.sqrt(seq * FNET_GROUP_CH)
    o_ref[0] = (jnp.dot(dl_ref[...], xs[...], preferred_element_type=F32) * scale).astype(o_ref.dtype)


def _fnet(u):
    b, l, _ = u.shape
    tr = min(l, 512)
    cc, sc = _dft_cos_sin(FNET_GROUP_CH)
    eye = jnp.eye(FNET_GROUPS, dtype=F32)
    cs = jnp.concatenate([jnp.kron(eye, cc), jnp.kron(eye, sc)], axis=1).astype(BF16)
    cl, sl = _dft_cos_sin_split(l, DFT_MINOR) if l > 4 * DFT_MINOR else _dft_cos_sin(l)
    dl = jnp.concatenate([cl, -sl], axis=1).astype(BF16)
    return pl.pallas_call(
        functools.partial(_fnet_body, seq=l),
        grid=(b, l // tr),
        in_specs=[pl.BlockSpec((1, l, FNET_WIDTH), lambda i, t: (i, 0, 0)),
                  _const_spec((FNET_WIDTH, 2 * FNET_WIDTH)),
                  pl.BlockSpec((tr, 2 * l), lambda i, t: (t, 0))],
        out_specs=pl.BlockSpec((1, tr, FNET_WIDTH), lambda i, t: (i, t, 0)),
        out_shape=jax.ShapeDtypeStruct((b, l, FNET_WIDTH), BF16),
        scratch_shapes=[pltpu.VMEM((2 * l, FNET_WIDTH), BF16)],
        compiler_params=_cparams("arbitrary", "arbitrary"),
        name="fnet",
    )(u, cs, dl)


def _even_w_in(w):
    pad = jnp.zeros((D_MODEL, LANES - SSD_HEADS), w.dtype)
    return jnp.concatenate([w[:, :O_DT], w[:, O_Q:], w[:, O_DT:O_Q], pad], axis=1).astype(BF16)


EVEN_SPLITS = (SSD_WIDTH, SSD_XBC, NA_WIDTH, NA_WIDTH, NA_WIDTH, LANES)


def kernel(x_prompt, x_sample, cache_na_k, cache_na_v, state_ssd, state_s5, c, c_ctx, mod_w, mod_b, norm1_g, norm2_g, ffn_w1, ffn_w3, ffn_w2, final_g, ev_w_in, ev_conv_w, ev_conv_b, ev_a_log, ev_dt_bias, ev_d_skip, ev_ssd_norm_g, ev_rpb, ev_w_out, od_w_in, od_lam_re, od_lam_im, od_log_step, od_b_re, od_b_im, od_c_re, od_c_im, od_d_skip, od_glu_w, od_glu_b, od_w_out):
    bc, lc, _ = x_prompt.shape
    bl, ll, _ = x_sample.shape
    cond = jnp.zeros((MOD_ROWS, D_MODEL), F32).at[0].set(c_ctx).at[1:1 + bl].set(c)
    mods = _adaln(cond, mod_w, mod_b)
    xc = x_prompt.reshape(1, bc * lc, D_MODEL)
    xl = x_sample
    tm = 512
    new_k, new_v, new_ssd, new_s5 = [], [], [], []
    for i in range(DEPTH):
        j = i // 2
        mod = mods[i]
        last = i == DEPTH - 1
        if i % 2 == 0:
            w_in = _even_w_in(ev_w_in[j])
            w_out = ev_w_out[j].astype(BF16)
            ssd_args = (ev_conv_w[j], ev_conv_b[j], ev_a_log[j], ev_dt_bias[j], ev_d_skip[j],
                        ev_ssd_norm_g[j])
            z, xbc, q, k, v, dtp, k_hd, v_hd = _inproj(
                xc, mod, 0, norm1_g[i], w_in, EVEN_SPLITS, tm, (F32, F32, BF16, BF16, BF16, F32), (3, 4))
            seqs = lambda a: a.reshape(bc, lc, a.shape[-1])
            y_ssd, fin = _ssd(seqs(z), seqs(xbc), seqs(dtp), *ssd_args, None, True)
            o_na = _ctx_attn(seqs(q), seqs(k), seqs(v))
            flat = lambda a: a.reshape(1, bc * lc, a.shape[-1])
            mix_c = (flat(y_ssd), flat(o_na))
            new_k.append(k_hd.reshape(bc, lc, NA_HEADS, NA_HEAD_DIM))
            new_v.append(v_hd.reshape(bc, lc, NA_HEADS, NA_HEAD_DIM))
            new_ssd.append(_ssd_state_from_kernel(fin))

            z, xbc, q, k, v, dtp = _inproj(xl, mod, 1, norm1_g[i], w_in, EVEN_SPLITS, tm,
                                           (F32, F32, BF16, BF16, BF16, F32))
            s0t = _ssd_state_to_kernel(state_ssd[:, j].astype(F32))
            (y_ssd,) = _ssd(z, xbc, dtp, *ssd_args, s0t, False)
            pl_ = cache_na_k.shape[2]
            o_na = _latent_na(q, k, v, cache_na_k[:, j].reshape(bl, pl_, NA_WIDTH),
                              cache_na_v[:, j].reshape(bl, pl_, NA_WIDTH), ev_rpb[j])
            mix_l = (y_ssd, o_na)
            glu_w = glu_b = None
        else:
            w_in = od_w_in[j].astype(BF16)
            w_out = od_w_out[j].astype(BF16)
            glu_w, glu_b = od_glu_w[j].astype(BF16), od_glu_b[j]
            tables = _s5_tables(od_lam_re[j], od_lam_im[j], od_log_step[j], od_b_re[j], od_b_im[j],
                                od_c_re[j], od_c_im[j])
            splits = (S5_WIDTH, FNET_WIDTH)
            u_s, u_f = _inproj(xc, mod, 0, norm1_g[i], w_in, splits, 2 * tm, (F32, BF16))
            seqs = lambda a: a.reshape(bc, lc, a.shape[-1])
            g_s, fin = _s5(seqs(u_s), tables, od_d_skip[j], None, True)
            y_f = _fnet(seqs(u_f))
            flat = lambda a: a.reshape(1, bc * lc, a.shape[-1])
            mix_c = (flat(g_s), flat(y_f))
            fin = fin.reshape(S5_GROUPS, bc, 2, 2, S5_STATE)
            new_s5.append(jnp.transpose(fin, (1, 2, 0, 4, 3)))

            u_s, u_f = _inproj(xl, mod, 1, norm1_g[i], w_in, splits, 2 * tm, (F32, BF16))
            s0 = jnp.transpose(state_s5[:, j].astype(F32), (2, 0, 1, 4, 3)).reshape(
                S5_GROUPS, bl, 4 * S5_STATE)
            g_s, _ = _s5(u_s, tables, od_d_skip[j], s0, False)
            y_f = _fnet(u_f)
            mix_l = (g_s, y_f)
        w_o = (w_out[:SSD_WIDTH], w_out[SSD_WIDTH:])
        w1, w3, w2 = ffn_w1[i].astype(BF16), ffn_w3[i].astype(BF16), ffn_w2[i].astype(BF16)
        fg = final_g if last else None
        xc = _ffn(xc, mod, 0, *mix_c, *w_o, glu_w, glu_b, norm2_g[i], w1, w3, w2, fg, tm)
        xl = _ffn(xl, mod, 1, *mix_l, *w_o, glu_w, glu_b, norm2_g[i], w1, w3, w2, fg, tm)
    return (xc.reshape(bc, lc, D_MODEL), xl,
            jnp.stack(new_k, axis=1), jnp.stack(new_v, axis=1),
            jnp.stack(new_ssd, axis=1), jnp.stack(new_s5, axis=1))
```
